```python
import math, functools
import jax, jax.numpy as jnp
from jax import lax
import numpy as np

D_MODEL = 1024
BATCH = 8
SEQ = 2048
DEPTH = 4

CHUNK = 64
N_MIXERS = 2
N_MLSTM_LAYERS = (DEPTH + 1) // 2
N_CONV_LAYERS = DEPTH // 2

M_HEADS = 4
M_DK = 128
M_DV = 256
M_QK = M_HEADS * M_DK
M_V = M_HEADS * M_DV
M_PROJ = 2 * M_QK + 2 * M_V + 2 * M_HEADS

CONV_K = 3

N_GROUPS = 4
E_PER_GROUP = 8
N_EXPERTS = N_GROUPS * E_PER_GROUP
TOP_K = 2
D_FF = 512
ROW_BLOCK = 128

EPS = 1e-6

kernel_name = "hybrid_mlstm_shortconv_hmoe_adaln"


def rmsnorm(x, g):
    xf = x.astype(jnp.float32)
    y = xf * lax.rsqrt(jnp.mean(xf * xf, axis=-1, keepdims=True) + EPS)
    return (y * g.astype(jnp.float32)).astype(x.dtype)


def adaln(c, w_ada, b_ada):
    mod = jax.nn.silu(c) @ w_ada + b_ada
    shift, scale, gate = jnp.split(mod, 3, axis=-1)
    return shift[:, None, :], scale[:, None, :], gate[:, None, :]


def mlstm_mixer(h, w_in, b_gates, norm_g, w_out):
    B, S, _ = h.shape
    nc = S // CHUNK
    proj = h @ w_in
    q, k, v, o, ig, fg = jnp.split(
        proj, [M_QK, 2 * M_QK, 2 * M_QK + M_V, 2 * M_QK + 2 * M_V,
               2 * M_QK + 2 * M_V + M_HEADS], axis=-1)
    f32 = jnp.float32

    def heads_to_chunks(t, d):
        return t.astype(f32).reshape(B, nc, CHUNK, M_HEADS, d).transpose(1, 0, 3, 2, 4)

    def gates_to_chunks(t):
        return t.reshape(B, nc, CHUNK, M_HEADS).transpose(1, 0, 3, 2)

    qc = heads_to_chunks(q, M_DK) * (M_DK ** -0.5)
    kc = heads_to_chunks(k, M_DK)
    vc = heads_to_chunks(v, M_DV)
    b_g = b_gates.astype(f32)
    i_pre = gates_to_chunks(ig.astype(f32) + b_g[:M_HEADS])
    log_f = gates_to_chunks(jax.nn.log_sigmoid(fg.astype(f32) + b_g[M_HEADS:]))
    causal = jnp.tril(jnp.ones((CHUNK, CHUNK), dtype=bool))

    def step(carry, xs):
        C, n, m = carry
        q_, k_, v_, i_, lf_ = xs
        bcum = jnp.cumsum(lf_, axis=-1)
        dlog = bcum[..., :, None] - bcum[..., None, :] + i_[..., None, :]
        dlog = jnp.where(causal, dlog, -jnp.inf)
        inter_log = bcum + m[..., None]
        m_t = jnp.maximum(inter_log, jnp.max(dlog, axis=-1))
        w_intra = jnp.exp(dlog - m_t[..., None])
        w_inter = jnp.exp(inter_log - m_t)
        scores = jnp.einsum('bhtd,bhsd->bhts', q_, k_) * w_intra
        num = (jnp.einsum('bhts,bhsv->bhtv', scores, v_)
               + w_inter[..., None] * jnp.einsum('bhvd,bhtd->bhtv', C, q_))
        den = scores.sum(-1) + w_inter * jnp.einsum('bhd,bhtd->bht', n, q_)
        h_out = num / jnp.maximum(jnp.abs(den), jnp.exp(-m_t))[..., None]
        b_last = bcum[..., -1]
        log_src = b_last[..., None] - bcum + i_
        m_new = jnp.maximum(b_last + m, jnp.max(log_src, axis=-1))
        w_src = jnp.exp(log_src - m_new[..., None])
        decay = jnp.exp(b_last + m - m_new)
        C_new = decay[..., None, None] * C + jnp.einsum('bhsv,bhsd->bhvd', v_ * w_src[..., None], k_)
        n_new = decay[..., None] * n + jnp.einsum('bhs,bhsd->bhd', w_src, k_)
        return (C_new, n_new, m_new), h_out

    init = (jnp.zeros((B, M_HEADS, M_DV, M_DK), f32),
            jnp.zeros((B, M_HEADS, M_DK), f32),
            jnp.zeros((B, M_HEADS), f32))
    _, hc = lax.scan(step, init, (qc, kc, vc, i_pre, log_f))
    hs = hc.transpose(1, 0, 3, 2, 4).reshape(B, S, M_HEADS, M_DV)
    hs = hs * lax.rsqrt(jnp.mean(hs * hs, axis=-1, keepdims=True) + EPS)
    hs = hs.reshape(B, S, M_V) * norm_g.astype(f32) * jax.nn.sigmoid(o.astype(f32))
    return hs.astype(h.dtype) @ w_out


def short_conv_mixer(h, w_in, conv_w, w_out):
    b_gate, c_gate, xb = jnp.split(h @ w_in, 3, axis=-1)
    u = c_gate * xb
    y = lax.conv_general_dilated(
        u, conv_w[:, None, :].astype(u.dtype), window_strides=(1,),
        padding=[(CONV_K - 1, 0)], dimension_numbers=('NWC', 'WIO', 'NWC'),
        feature_group_count=D_MODEL)
    return (b_gate * y) @ w_out


def hier_moe(h, w_group, b_group, w_expert, b_expert, w_gate, w_up, w_down):
    B, S, D = h.shape
    N = B * S
    xt = h.reshape(N, D)
    f32 = jnp.float32
    g_logits = (xt @ w_group).astype(f32) + b_group.astype(f32)
    g_prob = jax.nn.softmax(g_logits, axis=-1)
    g_sel = jnp.argmax(g_logits, axis=-1).astype(jnp.int32)
    p_sel = jnp.take_along_axis(g_prob, g_sel[:, None], axis=-1)[:, 0]
    e_logits = ((xt @ w_expert).astype(f32) + b_expert.astype(f32)).reshape(N, N_GROUPS, E_PER_GROUP)
    e_sel = jnp.take_along_axis(e_logits, g_sel[:, None, None], axis=1)[:, 0]
    top_l, top_i = lax.top_k(e_sel, TOP_K)
    combine = jax.nn.softmax(top_l, axis=-1) * p_sel[:, None]
    eid = g_sel[:, None] * E_PER_GROUP + top_i.astype(jnp.int32)

    A = N * TOP_K
    flat_e = eid.reshape(A)
    flat_w = combine.reshape(A)
    flat_tok = jnp.repeat(jnp.arange(N, dtype=jnp.int32), TOP_K)
    order = jnp.argsort(flat_e)
    se, stok, sw = flat_e[order], flat_tok[order], flat_w[order]
    counts = jax.ops.segment_sum(jnp.ones((A,), jnp.int32), flat_e, num_segments=N_EXPERTS)
    starts = jnp.cumsum(counts) - counts
    pcounts = (counts + ROW_BLOCK - 1) // ROW_BLOCK * ROW_BLOCK
    pends = jnp.cumsum(pcounts)
    pstarts = pends - pcounts
    dest = pstarts[se] + (jnp.arange(A, dtype=jnp.int32) - starts[se])
    n_blocks = (A + N_EXPERTS * (ROW_BLOCK - 1) + ROW_BLOCK - 1) // ROW_BLOCK
    P = n_blocks * ROW_BLOCK
    row_tok = jnp.zeros((P,), jnp.int32).at[dest].set(stok)
    row_w = jnp.zeros((P,), f32).at[dest].set(sw)
    blk_e = jnp.clip(jnp.searchsorted(pends, jnp.arange(n_blocks, dtype=jnp.int32) * ROW_BLOCK,
                                      side='right'), 0, N_EXPERTS - 1).astype(jnp.int32)
    xr = xt[row_tok].reshape(n_blocks, ROW_BLOCK, D)

    def expert_block(args):
        xb, e = args
        return (jax.nn.silu(xb @ w_gate[e]) * (xb @ w_up[e])) @ w_down[e]

    yr = lax.map(expert_block, (xr, blk_e)).reshape(P, D)
    out = jax.ops.segment_sum(yr.astype(f32) * row_w[:, None], row_tok, num_segments=N)
    return out.astype(h.dtype).reshape(B, S, D)


def setup_inputs(seed: int = 0) -> dict:
    key = jax.random.key(seed)
    ks = jax.random.split(key, 24)
    f32 = jnp.float32

    def nrm(k, shape, fan_in, mult=1.0):
        return jax.random.normal(k, shape, f32) * (mult * fan_in ** -0.5)

    D = D_MODEL
    x = jax.random.normal(ks[0], (BATCH, SEQ, D), f32)
    c = jax.random.normal(ks[1], (BATCH, D), f32)
    ada_w = nrm(ks[2], (DEPTH, 2, D, 3 * D), D, 0.5)
    ada_b = 0.02 * jax.random.normal(ks[3], (DEPTH, 2, 3 * D), f32)
    norm_g = 1.0 + 0.02 * jax.random.normal(ks[4], (DEPTH, 2, D), f32)
    final_g = 1.0 + 0.02 * jax.random.normal(ks[5], (D,), f32)
    m_w_in = nrm(ks[6], (N_MLSTM_LAYERS, D, M_PROJ), D)
    m_b_gates = jnp.concatenate([
        0.1 * jax.random.normal(ks[7], (N_MLSTM_LAYERS, M_HEADS), f32),
        3.0 + 0.1 * jax.random.normal(ks[8], (N_MLSTM_LAYERS, M_HEADS), f32)], axis=-1)
    m_norm_g = 1.0 + 0.02 * jax.random.normal(ks[9], (N_MLSTM_LAYERS, M_V), f32)
    m_w_out = nrm(ks[10], (N_MLSTM_LAYERS, M_V, D), M_V)
    s_w_in = nrm(ks[11], (N_CONV_LAYERS, D, 3 * D), D)
    s_conv_w = nrm(ks[12], (N_CONV_LAYERS, CONV_K, D), CONV_K)
    s_w_out = nrm(ks[13], (N_CONV_LAYERS, D, D), D)
    r_w_group = nrm(ks[14], (DEPTH, D, N_GROUPS), D)
    r_b_group = 0.01 * jax.random.normal(ks[15], (DEPTH, N_GROUPS), f32)
    r_w_expert = nrm(ks[16], (DEPTH, D, N_EXPERTS), D)
    r_b_expert = 0.01 * jax.random.normal(ks[17], (DEPTH, N_EXPERTS), f32)
    e_w_gate = nrm(ks[18], (DEPTH, N_EXPERTS, D, D_FF), D)
    e_w_up = nrm(ks[19], (DEPTH, N_EXPERTS, D, D_FF), D)
    e_w_down = nrm(ks[20], (DEPTH, N_EXPERTS, D_FF, D), D_FF)
    return {"x": x, "c": c, "ada_w": ada_w, "ada_b": ada_b, "norm_g": norm_g,
            "final_g": final_g, "m_w_in": m_w_in, "m_b_gates": m_b_gates,
            "m_norm_g": m_norm_g, "m_w_out": m_w_out, "s_w_in": s_w_in,
            "s_conv_w": s_conv_w, "s_w_out": s_w_out, "r_w_group": r_w_group,
            "r_b_group": r_b_group, "r_w_expert": r_w_expert, "r_b_expert": r_b_expert,
            "e_w_gate": e_w_gate, "e_w_up": e_w_up, "e_w_down": e_w_down}


def reference(x, c, ada_w, ada_b, norm_g, final_g, m_w_in, m_b_gates, m_norm_g,
              m_w_out, s_w_in, s_conv_w, s_w_out, r_w_group, r_b_group,
              r_w_expert, r_b_expert, e_w_gate, e_w_up, e_w_down):
    for i in range(DEPTH):
        shift, scale, gate = adaln(c, ada_w[i, 0], ada_b[i, 0])
        h = rmsnorm(x, norm_g[i, 0]) * (1.0 + scale) + shift
        j = i // N_MIXERS
        if i % N_MIXERS == 0:
            y = mlstm_mixer(h, m_w_in[j], m_b_gates[j], m_norm_g[j], m_w_out[j])
        else:
            y = short_conv_mixer(h, s_w_in[j], s_conv_w[j], s_w_out[j])
        x = x + gate * y
        shift, scale, gate = adaln(c, ada_w[i, 1], ada_b[i, 1])
        h = rmsnorm(x, norm_g[i, 1]) * (1.0 + scale) + shift
        y = hier_moe(h, r_w_group[i], r_b_group[i], r_w_expert[i], r_b_expert[i],
                     e_w_gate[i], e_w_up[i], e_w_down[i])
        x = x + gate * y
    return rmsnorm(x, final_g)
```

```python
import functools

import jax
import jax.numpy as jnp
from jax import lax
from jax.experimental import pallas as pl
from jax.experimental.pallas import tpu as pltpu

F32 = jnp.float32
BF16 = jnp.bfloat16
I32 = jnp.int32

D_MODEL = 1024
BATCH = 8
SEQ = 2048
DEPTH = 4
N_TOK = BATCH * SEQ
M_HEADS = 4
M_DK = 128
M_DV = 256
M_QK = M_HEADS * M_DK
M_V = M_HEADS * M_DV
CONV_K = 3
N_GROUPS = 4
E_PER_GROUP = 8
N_EXPERTS = N_GROUPS * E_PER_GROUP
TOP_K = 2
D_FF = 512
EPS = 1e-6

SUBLANES = 8
LANES = 128

ROW_TILE = 512
MLSTM_CHUNK = 256
MOE_TILE = 256
EXPERT_BLOCK = 256
SORT_ROWS = -(-(MOE_TILE * TOP_K + N_EXPERTS * (SUBLANES - 1)) // LANES) * LANES
N_MOE_TILES = N_TOK // MOE_TILE
_MAX_SORTED = (N_MOE_TILES * (MOE_TILE * TOP_K + N_EXPERTS * (SUBLANES - 1))
               + N_EXPERTS * (EXPERT_BLOCK - 1))
N_EXPERT_BLOCKS = -(-_MAX_SORTED // EXPERT_BLOCK)
SORTED_ROWS = N_EXPERT_BLOCKS * EXPERT_BLOCK

VMEM_LIMIT = 48 * 1024 * 1024


def _cparams(sem):
    return pltpu.CompilerParams(dimension_semantics=sem,
                                vmem_limit_bytes=VMEM_LIMIT)


def _dot(a, b):
    return jnp.dot(a, b, preferred_element_type=F32)


def _dot_nt(a, b):
    return lax.dot_general(a, b, (((1,), (1,)), ((), ())),
                           preferred_element_type=F32)


def _split3(x):
    hi = x.astype(BF16)
    r1 = x - hi.astype(F32)
    mid = r1.astype(BF16)
    lo = (r1 - mid.astype(F32)).astype(BF16)
    return hi, mid, lo


def _dot_sel_left(sel, x):
    hi, mid, lo = _split3(x)
    return _dot(sel, hi) + _dot(sel, mid) + _dot(sel, lo)


def _dot_sel_right(x, sel):
    hi, mid, lo = _split3(x)
    return _dot(hi, sel) + _dot(mid, sel) + _dot(lo, sel)


def _sigmoid(x):
    return 1.0 / (1.0 + jnp.exp(-x))


def _rms_mod(x, g, mod):
    ms = jnp.mean(x * x, axis=-1, keepdims=True)
    y = (x * lax.rsqrt(ms + EPS)) * g
    return y * (1.0 + mod[:, D_MODEL:2 * D_MODEL]) + mod[:, 0:D_MODEL]


def _ada_kernel(c_ref, w_ref, b_ref, o_ref):
    c = c_ref[...]
    s = (c * _sigmoid(c)).astype(BF16)
    o_ref[0] = _dot(s, w_ref[0].astype(BF16)) + b_ref[0]


def _ada_mods(c, ada_w, ada_b):
    n_pairs = DEPTH * 2
    w = ada_w.reshape(n_pairs, D_MODEL, 3 * D_MODEL)
    b = ada_b.reshape(n_pairs, 1, 3 * D_MODEL)
    col = D_MODEL
    return pl.pallas_call(
        _ada_kernel,
        out_shape=jax.ShapeDtypeStruct((n_pairs, BATCH, 3 * D_MODEL), F32),
        grid=(n_pairs, 3 * D_MODEL // col),
        in_specs=[
            pl.BlockSpec((BATCH, D_MODEL), lambda p, j: (0, 0)),
            pl.BlockSpec((1, D_MODEL, col), lambda p, j: (p, 0, j)),
            pl.BlockSpec((1, 1, col), lambda p, j: (p, 0, j)),
        ],
        out_specs=pl.BlockSpec((1, BATCH, col), lambda p, j: (p, 0, j)),
        compiler_params=_cparams(("arbitrary", "arbitrary")),
        name="ada_mods",
    )(c, w, b)


def _mlstm_in_kernel(x_ref, g_ref, mod_ref, wq_ref, wkt_ref, wv_ref, wo_ref,
                     wg_ref, bg_ref, q_ref, kt_ref, v_ref, o_ref, gates_ref,
                     gatest_ref):
    h = _rms_mod(x_ref[...], g_ref[...], mod_ref[0])
    hb = h.astype(BF16)
    q_ref[...] = (_dot(hb, wq_ref[...]) * (M_DK ** -0.5)).astype(BF16)
    kt_ref[...] = _dot_nt(wkt_ref[...], hb).astype(BF16)
    v_ref[...] = _dot(hb, wv_ref[...]).astype(BF16)
    o_ref[...] = _dot(hb, wo_ref[...]).astype(BF16)
    g = _dot(hb, wg_ref[...]) + bg_ref[...]
    log_sig = jnp.minimum(g, 0.0) - jnp.log(1.0 + jnp.exp(-jnp.abs(g)))
    lane = lax.broadcasted_iota(I32, g.shape, 1)
    gg = jnp.where(lane < M_HEADS, g, log_sig)
    gates_ref[...] = gg
    gatest_ref[...] = gg.T[0:SUBLANES, :]


def _mlstm_in(x, norm_g, mod, w_in, b_gates):
    t = ROW_TILE
    tiles_per_seq = SEQ // t
    wq = w_in[:, 0:M_QK].astype(BF16)
    wkt = w_in[:, M_QK:2 * M_QK].T.astype(BF16)
    wv = w_in[:, 2 * M_QK:2 * M_QK + M_V].astype(BF16)
    wo = w_in[:, 2 * M_QK + M_V:2 * M_QK + 2 * M_V].astype(BF16)
    n_gate = 2 * M_HEADS
    wg = jnp.pad(w_in[:, 2 * M_QK + 2 * M_V:], ((0, 0), (0, LANES - n_gate))).astype(BF16)
    bg = jnp.pad(b_gates.astype(F32), (0, LANES - n_gate)).reshape(1, LANES)
    full = lambda shape: pl.BlockSpec(shape, lambda i: (0, 0))
    return pl.pallas_call(
        _mlstm_in_kernel,
        out_shape=(
            jax.ShapeDtypeStruct((N_TOK, M_QK), BF16),
            jax.ShapeDtypeStruct((M_QK, N_TOK), BF16),
            jax.ShapeDtypeStruct((N_TOK, M_V), BF16),
            jax.ShapeDtypeStruct((N_TOK, M_V), BF16),
            jax.ShapeDtypeStruct((N_TOK, LANES), F32),
            jax.ShapeDtypeStruct((SUBLANES, N_TOK), F32),
        ),
        grid=(N_TOK // t,),
        in_specs=[
            pl.BlockSpec((t, D_MODEL), lambda i: (i, 0)),
            full((1, D_MODEL)),
            pl.BlockSpec((1, 1, 3 * D_MODEL), lambda i: (i // tiles_per_seq, 0, 0)),
            full((D_MODEL, M_QK)),
            full((M_QK, D_MODEL)),
            full((D_MODEL, M_V)),
            full((D_MODEL, M_V)),
            full((D_MODEL, LANES)),
            full((1, LANES)),
        ],
        out_specs=(
            pl.BlockSpec((t, M_QK), lambda i: (i, 0)),
            pl.BlockSpec((M_QK, t), lambda i: (0, i)),
            pl.BlockSpec((t, M_V), lambda i: (i, 0)),
            pl.BlockSpec((t, M_V), lambda i: (i, 0)),
            pl.BlockSpec((t, LANES), lambda i: (i, 0)),
            pl.BlockSpec((SUBLANES, t), lambda i: (0, i)),
        ),
        compiler_params=_cparams(("arbitrary",)),
        name="mlstm_in",
    )(x, norm_g.reshape(1, D_MODEL), mod, wq, wkt, wv, wo, wg, bg)


def _mlstm_rec_kernel(q_ref, kt_ref, v_ref, o_ref, gates_ref, gatest_ref,
                      x_ref, mod_ref, ng_ref, wout_ref, out_ref, ct_ref, m_ref):
    L = MLSTM_CHUNK

    @pl.when(pl.program_id(1) == 0)
    def _():
        ct_ref[...] = jnp.zeros_like(ct_ref)
        m_ref[...] = jnp.zeros_like(m_ref)

    gates = gates_ref[...]
    gatest = gatest_ref[...]
    row = lax.broadcasted_iota(I32, (L, L), 0)
    col = lax.broadcasted_iota(I32, (L, L), 1)
    causal = row >= col
    tri_low = jnp.where(causal, 1.0, 0.0).astype(BF16)
    tri_up = jnp.where(row <= col, 1.0, 0.0).astype(BF16)
    cum_cols = _dot_sel_left(tri_low, gates)
    cum_rows = _dot_sel_right(gatest, tri_up)
    ones_cols = jnp.ones((L, LANES), BF16)

    hs = []
    for h in range(M_HEADS):
        ig_row = gatest[h:h + 1, :]
        bcum_col = cum_cols[:, M_HEADS + h:M_HEADS + h + 1]
        bcum_row = cum_rows[M_HEADS + h:M_HEADS + h + 1, :]
        m_prev = m_ref[h:h + 1, 0:1]
        dlog = jnp.where(causal, bcum_col - bcum_row + ig_row, -jnp.inf)
        inter_log = bcum_col + m_prev
        m_t = jnp.maximum(inter_log, jnp.max(dlog, axis=1, keepdims=True))
        w_intra = jnp.exp(dlog - m_t)
        w_inter = jnp.exp(inter_log - m_t)
        qh = q_ref[:, h * M_DK:(h + 1) * M_DK]
        kth = kt_ref[h * M_DK:(h + 1) * M_DK, :]
        vh = v_ref[:, h * M_DV:(h + 1) * M_DV]
        scores = _dot(qh, kth) * w_intra
        state = ct_ref[h]
        q_state = _dot(qh, state.astype(BF16))
        num = _dot(scores.astype(BF16), vh) + w_inter * q_state[:, 0:M_DV]
        den = (jnp.sum(scores, axis=1, keepdims=True)
               + w_inter * q_state[:, M_DV:M_DV + 1])
        h_out = num / jnp.maximum(jnp.abs(den), jnp.exp(-m_t))

        b_last = bcum_row[:, L - 1:L]
        log_src = b_last - bcum_row + ig_row
        m_new = jnp.maximum(b_last + m_prev,
                            jnp.max(log_src, axis=1, keepdims=True))
        w_src = jnp.exp(log_src - m_new)
        decay = jnp.exp(b_last + m_prev - m_new)
        kt_w = (kth.astype(F32) * w_src).astype(BF16)
        v_ext = jnp.concatenate([vh, ones_cols], axis=1)
        ct_ref[h] = decay * state + _dot(kt_w, v_ext)
        m_ref[h:h + 1, :] = jnp.broadcast_to(m_new, (1, LANES))

        hn = h_out * lax.rsqrt(jnp.mean(h_out * h_out, axis=1, keepdims=True) + EPS)
        og = o_ref[:, h * M_DV:(h + 1) * M_DV].astype(F32)
        hs.append((hn * ng_ref[:, h * M_DV:(h + 1) * M_DV] * _sigmoid(og)).astype(BF16))

    y = _dot(jnp.concatenate(hs, axis=1), wout_ref[...])
    gate = mod_ref[0][:, 2 * D_MODEL:3 * D_MODEL]
    out_ref[...] = x_ref[...] + gate * y


def _mlstm_rec(x, mod, q, kt, v, o, gates, gatest, m_norm_g, w_out):
    L = MLSTM_CHUNK
    nc = SEQ // L
    rows = lambda width: pl.BlockSpec((L, width), lambda b, j: (b * nc + j, 0))
    cols = lambda height: pl.BlockSpec((height, L), lambda b, j: (0, b * nc + j))
    return pl.pallas_call(
        _mlstm_rec_kernel,
        out_shape=jax.ShapeDtypeStruct((N_TOK, D_MODEL), F32),
        grid=(BATCH, nc),
        in_specs=[
            rows(M_QK), cols(M_QK), rows(M_V), rows(M_V), rows(LANES),
            cols(SUBLANES), rows(D_MODEL),
            pl.BlockSpec((1, 1, 3 * D_MODEL), lambda b, j: (b, 0, 0)),
            pl.BlockSpec((1, M_V), lambda b, j: (0, 0)),
            pl.BlockSpec((M_V, D_MODEL), lambda b, j: (0, 0)),
        ],
        out_specs=rows(D_MODEL),
        scratch_shapes=[
            pltpu.VMEM((M_HEADS, M_DK, M_DV + LANES), F32),
            pltpu.VMEM((SUBLANES, LANES), F32),
        ],
        compiler_params=_cparams(("arbitrary", "arbitrary")),
        name="mlstm_rec",
    )(q, kt, v, o, gates, gatest, x, mod, m_norm_g.reshape(1, M_V).astype(F32),
      w_out.astype(BF16))


_CONV_COLS = 256


def _conv_kernel(x_ref, g_ref, mod_ref, win_ref, cw_ref, wout_ref, out_ref,
                 carry_ref, z_ref):
    t = ROW_TILE
    tiles_per_seq = SEQ // t

    @pl.when(pl.program_id(0) % tiles_per_seq == 0)
    def _():
        carry_ref[...] = jnp.zeros_like(carry_ref)

    x = x_ref[...]
    mod = mod_ref[0]
    hb = _rms_mod(x, g_ref[...], mod).astype(BF16)
    row = lax.broadcasted_iota(I32, (t, _CONV_COLS), 0)
    for j in range(D_MODEL // _CONV_COLS):
        lo, hi = j * _CONV_COLS, (j + 1) * _CONV_COLS
        b_gate = _dot(hb, win_ref[:, lo:hi])
        c_gate = _dot(hb, win_ref[:, D_MODEL + lo:D_MODEL + hi])
        xb = _dot(hb, win_ref[:, 2 * D_MODEL + lo:2 * D_MODEL + hi])
        u = c_gate * xb
        prev1 = carry_ref[SUBLANES - 1:SUBLANES, lo:hi]
        prev2 = carry_ref[SUBLANES - 2:SUBLANES - 1, lo:hi]
        u1 = jnp.where(row == 0, prev1, pltpu.roll(u, 1, axis=0))
        u2 = jnp.where(row == 0, prev2,
                       jnp.where(row == 1, prev1, pltpu.roll(u, 2, axis=0)))
        y = (cw_ref[0:1, lo:hi] * u2 + cw_ref[1:2, lo:hi] * u1
             + cw_ref[2:3, lo:hi] * u)
        z_ref[:, lo:hi] = (b_gate * y).astype(BF16)
        carry_ref[:, lo:hi] = u[t - SUBLANES:t, :]
    gate = mod[:, 2 * D_MODEL:3 * D_MODEL]
    out_ref[...] = x + gate * _dot(z_ref[...], wout_ref[...])


def _conv_layer(x, norm_g, mod, w_in, conv_w, w_out):
    t = ROW_TILE
    tiles_per_seq = SEQ // t
    full = lambda shape: pl.BlockSpec(shape, lambda i: (0, 0))
    cw = jnp.pad(conv_w.astype(F32), ((0, SUBLANES - CONV_K), (0, 0)))
    return pl.pallas_call(
        _conv_kernel,
        out_shape=jax.ShapeDtypeStruct((N_TOK, D_MODEL), F32),
        grid=(N_TOK // t,),
        in_specs=[
            pl.BlockSpec((t, D_MODEL), lambda i: (i, 0)),
            full((1, D_MODEL)),
            pl.BlockSpec((1, 1, 3 * D_MODEL), lambda i: (i // tiles_per_seq, 0, 0)),
            full((D_MODEL, 3 * D_MODEL)),
            full((SUBLANES, D_MODEL)),
            full((D_MODEL, D_MODEL)),
        ],
        out_specs=pl.BlockSpec((t, D_MODEL), lambda i: (i, 0)),
        scratch_shapes=[
            pltpu.VMEM((SUBLANES, D_MODEL), F32),
            pltpu.VMEM((t, D_MODEL), BF16),
        ],
        compiler_params=_cparams(("arbitrary",)),
        name="conv_layer",
    )(x, norm_g.reshape(1, D_MODEL), mod, w_in.astype(BF16), cw, w_out.astype(BF16))


_ROUTE_ROWS = LANES
_EXPERT_ROW0 = SUBLANES


def _route_kernel(x_ref, g_ref, mod_ref, wr_hi_ref, wr_lo_ref, br_ref,
                  h_ref, route_ref, cnt_ref):
    t = MOE_TILE
    h = _rms_mod(x_ref[...], g_ref[...], mod_ref[0])
    h_hi = h.astype(BF16)
    h_lo = (h - h_hi.astype(F32)).astype(BF16)
    h_ref[...] = h_hi
    logits = (_dot_nt(wr_hi_ref[...], h_hi) + _dot_nt(wr_hi_ref[...], h_lo)
              + _dot_nt(wr_lo_ref[...], h_hi)) + br_ref[...]

    sub = lax.broadcasted_iota(I32, (SUBLANES, t), 0)
    neg_inf = -jnp.inf
    gl = jnp.where(sub < N_GROUPS, logits[0:SUBLANES, :], neg_inf)
    gmax = jnp.max(gl, axis=0, keepdims=True)
    g_sel = jnp.min(jnp.where(gl == gmax, sub, SUBLANES), axis=0, keepdims=True)
    p_sel = 1.0 / jnp.sum(jnp.exp(gl - gmax), axis=0, keepdims=True)

    e_sel = jnp.zeros((E_PER_GROUP, t), F32)
    for g in range(N_GROUPS):
        r0 = _EXPERT_ROW0 + g * E_PER_GROUP
        e_sel = jnp.where(g_sel == g, logits[r0:r0 + E_PER_GROUP, :], e_sel)
    v1 = jnp.max(e_sel, axis=0, keepdims=True)
    i1 = jnp.min(jnp.where(e_sel == v1, sub, SUBLANES), axis=0, keepdims=True)
    e_rest = jnp.where(sub == i1, neg_inf, e_sel)
    v2 = jnp.max(e_rest, axis=0, keepdims=True)
    i2 = jnp.min(jnp.where(e_rest == v2, sub, SUBLANES), axis=0, keepdims=True)
    ratio = jnp.exp(v2 - v1)
    w1 = p_sel / (1.0 + ratio)
    w2 = p_sel * ratio / (1.0 + ratio)
    eid1 = g_sel * E_PER_GROUP + i1
    eid2 = g_sel * E_PER_GROUP + i2

    erow = lax.broadcasted_iota(I32, (N_EXPERTS, t), 0)
    m1 = erow == eid1
    m2 = erow == eid2
    member = jnp.where(m1 | m2, 1.0, 0.0)
    r = lax.broadcasted_iota(I32, (t, t), 0)
    c = lax.broadcasted_iota(I32, (t, t), 1)
    earlier = jnp.where(r < c, 1.0, 0.0).astype(BF16)
    rank = _dot(member.astype(BF16), earlier)
    cnt = jnp.sum(member, axis=1, keepdims=True)
    cnt_pad = jnp.floor((cnt + (SUBLANES - 1.0)) * (1.0 / SUBLANES)) * SUBLANES
    er = lax.broadcasted_iota(I32, (N_EXPERTS, N_EXPERTS), 0)
    ec = lax.broadcasted_iota(I32, (N_EXPERTS, N_EXPERTS), 1)
    before = jnp.where(er > ec, 1.0, 0.0).astype(BF16)
    seg_start = _dot_sel_left(before, jnp.broadcast_to(cnt_pad, (N_EXPERTS, LANES)))[:, 0:1]
    pos = seg_start + rank
    pos1 = jnp.sum(jnp.where(m1, pos, 0.0), axis=0, keepdims=True)
    pos2 = jnp.sum(jnp.where(m2, pos, 0.0), axis=0, keepdims=True)

    out = jnp.zeros((SUBLANES, t), F32)
    for k, val in enumerate((pos1, pos2, w1, w2)):
        out = jnp.where(sub == k, val, out)
    route_ref[...] = out
    cnt_ref[0] = jnp.broadcast_to(cnt, (N_EXPERTS, LANES))


def _route(x, norm_g, mod, w_group, b_group, w_expert, b_expert):
    t = MOE_TILE
    tiles_per_seq = SEQ // t
    wr = jnp.zeros((_ROUTE_ROWS, D_MODEL), F32)
    wr = wr.at[0:N_GROUPS].set(w_group.T.astype(F32))
    wr = wr.at[_EXPERT_ROW0:_EXPERT_ROW0 + N_EXPERTS].set(w_expert.T.astype(F32))
    wr_hi = wr.astype(BF16)
    wr_lo = (wr - wr_hi.astype(F32)).astype(BF16)
    br = jnp.zeros((_ROUTE_ROWS,), F32)
    br = br.at[0:N_GROUPS].set(b_group.astype(F32))
    br = br.at[_EXPERT_ROW0:_EXPERT_ROW0 + N_EXPERTS].set(b_expert.astype(F32))
    full = lambda shape: pl.BlockSpec(shape, lambda i: (0, 0))
    return pl.pallas_call(
        _route_kernel,
        out_shape=(
            jax.ShapeDtypeStruct((N_TOK, D_MODEL), BF16),
            jax.ShapeDtypeStruct((SUBLANES, N_TOK), F32),
            jax.ShapeDtypeStruct((N_MOE_TILES, N_EXPERTS, LANES), F32),
        ),
        grid=(N_MOE_TILES,),
        in_specs=[
            pl.BlockSpec((t, D_MODEL), lambda i: (i, 0)),
            full((1, D_MODEL)),
            pl.BlockSpec((1, 1, 3 * D_MODEL), lambda i: (i // tiles_per_seq, 0, 0)),
            full((_ROUTE_ROWS, D_MODEL)),
            full((_ROUTE_ROWS, D_MODEL)),
            full((_ROUTE_ROWS, 1)),
        ],
        out_specs=(
            pl.BlockSpec((t, D_MODEL), lambda i: (i, 0)),
            pl.BlockSpec((SUBLANES, t), lambda i: (0, i)),
            pl.BlockSpec((1, N_EXPERTS, LANES), lambda i: (i, 0, 0)),
        ),
        compiler_params=_cparams(("arbitrary",)),
        name="moe_route",
    )(x, norm_g.reshape(1, D_MODEL), mod, wr_hi, wr_lo, br.reshape(_ROUTE_ROWS, 1))


def _dispatch_plan(cnt):
    cnt = cnt.astype(I32)
    cnt_pad = (cnt + SUBLANES - 1) // SUBLANES * SUBLANES
    seg = jnp.cumsum(cnt_pad, axis=1) - cnt_pad
    tot = jnp.sum(cnt_pad, axis=0)
    ptot = (tot + EXPERT_BLOCK - 1) // EXPERT_BLOCK * EXPERT_BLOCK
    pend = jnp.cumsum(ptot)
    gbase = pend - ptot
    dst = gbase[None, :] + jnp.cumsum(cnt_pad, axis=0) - cnt_pad
    nch = cnt_pad // SUBLANES
    n_used = (pend[-1] // EXPERT_BLOCK).astype(I32)
    blk = jnp.arange(N_EXPERT_BLOCKS, dtype=I32)
    blk_e = jnp.clip(jnp.searchsorted(pend, blk * EXPERT_BLOCK, side='right'),
                     0, N_EXPERTS - 1).astype(I32)
    blk_e = blk_e[jnp.minimum(blk, n_used - 1)]
    return (seg.reshape(-1).astype(I32), dst.reshape(-1).astype(I32),
            nch.reshape(-1).astype(I32), blk_e, n_used.reshape(1))


def _segment_copies(tile, seg_ref, dst_ref, nch_ref, make_copy):
    def per_expert(e, total):
        idx = tile * N_EXPERTS + e
        n = nch_ref[idx]
        s0 = seg_ref[idx]
        d0 = dst_ref[idx]

        def per_chunk(cidx, carry):
            s = pl.multiple_of(s0 + cidx * SUBLANES, SUBLANES)
            d = pl.multiple_of(d0 + cidx * SUBLANES, SUBLANES)
            make_copy(s, d).start()
            return carry

        lax.fori_loop(0, n, per_chunk, 0)
        return total + n

    return lax.fori_loop(0, N_EXPERTS, per_expert, jnp.int32(0))


def _wait_copies(n, make_copy):
    def body(_, carry):
        make_copy(0, 0).wait()
        return carry
    lax.fori_loop(0, n, body, 0)


def _dispatch_kernel(seg_ref, dst_ref, nch_ref, route_ref, h_ref, xs_in_ref,
                     xs_ref, buf_ref, sem):
    del xs_in_ref
    t = MOE_TILE
    route = route_ref[...]
    pos1 = route[0:1, :].astype(I32)
    pos2 = route[1:2, :].astype(I32)
    r = lax.broadcasted_iota(I32, (SORT_ROWS, t), 0)
    perm = jnp.where((r == pos1) | (r == pos2), 1.0, 0.0).astype(BF16)
    buf_ref[...] = _dot(perm, h_ref[...])

    def make_copy(s, d):
        return pltpu.make_async_copy(buf_ref.at[pl.ds(s, SUBLANES)],
                                     xs_ref.at[pl.ds(d, SUBLANES)], sem)

    n = _segment_copies(pl.program_id(0), seg_ref, dst_ref, nch_ref, make_copy)
    _wait_copies(n, make_copy)


def _dispatch(seg, dst, nch, route, h):
    t = MOE_TILE
    xs0 = jnp.zeros((SORTED_ROWS, D_MODEL), F32)
    return pl.pallas_call(
        _dispatch_kernel,
        out_shape=jax.ShapeDtypeStruct((SORTED_ROWS, D_MODEL), F32),
        grid_spec=pltpu.PrefetchScalarGridSpec(
            num_scalar_prefetch=3,
            grid=(N_MOE_TILES,),
            in_specs=[
                pl.BlockSpec((SUBLANES, t), lambda i, *_: (0, i)),
                pl.BlockSpec((t, D_MODEL), lambda i, *_: (i, 0)),
                pl.BlockSpec(memory_space=pl.ANY),
            ],
            out_specs=pl.BlockSpec(memory_space=pl.ANY),
            scratch_shapes=[
                pltpu.VMEM((SORT_ROWS, D_MODEL), F32),
                pltpu.SemaphoreType.DMA,
            ],
        ),
        input_output_aliases={5: 0},
        compiler_params=_cparams(("arbitrary",)),
        name="moe_dispatch",
    )(seg, dst, nch, route, h, xs0)


def _expert_kernel(blk_e_ref, n_used_ref, x_ref, wg_ref, wu_ref, wd_ref, y_ref,
                   wg_b, wu_b, wd_b):
    i = pl.program_id(0)

    @pl.when(i < n_used_ref[0])
    def _():
        e = blk_e_ref[i]
        e_prev = blk_e_ref[jnp.maximum(i - 1, 0)]

        @pl.when((i == 0) | (e != e_prev))
        def _():
            wg_b[...] = wg_ref[0].astype(BF16)
            wu_b[...] = wu_ref[0].astype(BF16)
            wd_b[...] = wd_ref[0].astype(BF16)

        x = x_ref[...].astype(BF16)
        g = _dot(x, wg_b[...])
        u = _dot(x, wu_b[...])
        a = (g * _sigmoid(g) * u).astype(BF16)
        y_ref[...] = _dot(a, wd_b[...])

    @pl.when(i >= n_used_ref[0])
    def _():
        y_ref[...] = jnp.zeros_like(y_ref)


def _experts(blk_e, n_used, xs, w_gate, w_up, w_down):
    bm = EXPERT_BLOCK
    row_map = lambda i, be, nu: (jnp.minimum(i, nu[0] - 1), 0)
    out_map = lambda i, be, nu: (i, 0)
    w_map = lambda i, be, nu: (be[i], 0, 0)
    return pl.pallas_call(
        _expert_kernel,
        out_shape=jax.ShapeDtypeStruct((SORTED_ROWS, D_MODEL), F32),
        grid_spec=pltpu.PrefetchScalarGridSpec(
            num_scalar_prefetch=2,
            grid=(N_EXPERT_BLOCKS,),
            in_specs=[
                pl.BlockSpec((bm, D_MODEL), row_map),
                pl.BlockSpec((1, D_MODEL, D_FF), w_map),
                pl.BlockSpec((1, D_MODEL, D_FF), w_map),
                pl.BlockSpec((1, D_FF, D_MODEL), w_map),
            ],
            out_specs=pl.BlockSpec((bm, D_MODEL), out_map),
            scratch_shapes=[
                pltpu.VMEM((D_MODEL, D_FF), BF16),
                pltpu.VMEM((D_MODEL, D_FF), BF16),
                pltpu.VMEM((D_FF, D_MODEL), BF16),
            ],
        ),
        compiler_params=_cparams(("arbitrary",)),
        name="moe_experts",
    )(blk_e, n_used, xs, w_gate, w_up, w_down)


def _combine_kernel(seg_ref, dst_ref, nch_ref, route_ref, x_ref, mod_ref, fg_ref,
                    ys_ref, out_ref, buf_ref, sem, *, final_norm):
    t = MOE_TILE
    buf_ref[...] = jnp.zeros_like(buf_ref)

    def make_copy(s, d):
        return pltpu.make_async_copy(ys_ref.at[pl.ds(d, SUBLANES)],
                                     buf_ref.at[pl.ds(s, SUBLANES)], sem)

    n = _segment_copies(pl.program_id(0), seg_ref, dst_ref, nch_ref, make_copy)

    route = route_ref[...]
    route_t = jnp.concatenate(
        [route, jnp.zeros((LANES - SUBLANES, t), F32)], axis=0).T
    pos1 = route_t[:, 0:1].astype(I32)
    pos2 = route_t[:, 1:2].astype(I32)
    w1 = route_t[:, 2:3]
    w2 = route_t[:, 3:4]
    c = lax.broadcasted_iota(I32, (t, SORT_ROWS), 1)
    unsort = (jnp.where(c == pos1, w1, 0.0) + jnp.where(c == pos2, w2, 0.0)).astype(BF16)

    _wait_copies(n, make_copy)
    moe = _dot(unsort, buf_ref[...].astype(BF16))
    gate = mod_ref[0][:, 2 * D_MODEL:3 * D_MODEL]
    x_new = x_ref[...] + gate * moe
    if final_norm:
        ms = jnp.mean(x_new * x_new, axis=-1, keepdims=True)
        x_new = (x_new * lax.rsqrt(ms + EPS)) * fg_ref[...]
    out_ref[...] = x_new


def _combine(seg, dst, nch, route, x, mod, final_g, ys, final_norm):
    t = MOE_TILE
    tiles_per_seq = SEQ // t
    return pl.pallas_call(
        functools.partial(_combine_kernel, final_norm=final_norm),
        out_shape=jax.ShapeDtypeStruct((N_TOK, D_MODEL), F32),
        grid_spec=pltpu.PrefetchScalarGridSpec(
            num_scalar_prefetch=3,
            grid=(N_MOE_TILES,),
            in_specs=[
                pl.BlockSpec((SUBLANES, t), lambda i, *_: (0, i)),
                pl.BlockSpec((t, D_MODEL), lambda i, *_: (i, 0)),
                pl.BlockSpec((1, 1, 3 * D_MODEL), lambda i, *_: (i // tiles_per_seq, 0, 0)),
                pl.BlockSpec((1, D_MODEL), lambda i, *_: (0, 0)),
                pl.BlockSpec(memory_space=pl.ANY),
            ],
            out_specs=pl.BlockSpec((t, D_MODEL), lambda i, *_: (i, 0)),
            scratch_shapes=[
                pltpu.VMEM((SORT_ROWS, D_MODEL), F32),
                pltpu.SemaphoreType.DMA,
            ],
        ),
        compiler_params=_cparams(("arbitrary",)),
        name="moe_combine",
    )(seg, dst, nch, route, x, mod, final_g.reshape(1, D_MODEL).astype(F32), ys)


def _moe_layer(x, norm_g, mod, final_g, w_group, b_group, w_expert, b_expert,
               w_gate, w_up, w_down, final_norm):
    h, route, cnt = _route(x, norm_g, mod, w_group, b_group, w_expert, b_expert)
    seg, dst, nch, blk_e, n_used = _dispatch_plan(cnt[:, :, 0])
    xs = _dispatch(seg, dst, nch, route, h)
    ys = _experts(blk_e, n_used, xs, w_gate, w_up, w_down)
    return _combine(seg, dst, nch, route, x, mod, final_g, ys, final_norm)


def kernel(x, c, ada_w, ada_b, norm_g, final_g, m_w_in, m_b_gates, m_norm_g, m_w_out, s_w_in, s_conv_w, s_w_out, r_w_group, r_b_group, r_w_expert, r_b_expert, e_w_gate, e_w_up, e_w_down):
    mods = _ada_mods(c, ada_w, ada_b)
    xt = x.reshape(N_TOK, D_MODEL)
    for i in range(DEPTH):
        mod_mix = mods[2 * i].reshape(BATCH, 1, 3 * D_MODEL)
        mod_ffn = mods[2 * i + 1].reshape(BATCH, 1, 3 * D_MODEL)
        j = i // 2
        if i % 2 == 0:
            q, kt, v, o, gates, gatest = _mlstm_in(xt, norm_g[i, 0], mod_mix,
                                                   m_w_in[j], m_b_gates[j])
            xt = _mlstm_rec(xt, mod_mix, q, kt, v, o, gates, gatest,
                            m_norm_g[j], m_w_out[j])
        else:
            xt = _conv_layer(xt, norm_g[i, 0], mod_mix, s_w_in[j], s_conv_w[j],
                             s_w_out[j])
        xt = _moe_layer(xt, norm_g[i, 1], mod_ffn, final_g, r_w_group[i],
                        r_b_group[i], r_w_expert[i], r_b_expert[i], e_w_gate[i],
                        e_w_up[i], e_w_down[i], final_norm=(i == DEPTH - 1))
    return xt.reshape(BATCH, SEQ, D_MODEL)
```

```python
import functools

import jax
import jax.numpy as jnp
from jax import lax
from jax.experimental import pallas as pl
from jax.experimental.pallas import tpu as pltpu

F32 = jnp.float32
BF16 = jnp.bfloat16
I32 = jnp.int32

D_MODEL = 1024
BATCH = 8
SEQ = 2048
DEPTH = 4
N_TOK = BATCH * SEQ
M_HEADS = 4
M_DK = 128
M_DV = 256
M_QK = M_HEADS * M_DK
M_V = M_HEADS * M_DV
CONV_K = 3
N_GROUPS = 4
E_PER_GROUP = 8
N_EXPERTS = N_GROUPS * E_PER_GROUP
TOP_K = 2
D_FF = 512
EPS = 1e-6

SUBLANES = 8
LANES = 128

ROW_TILE = 512
MLSTM_CHUNK = 256
MOE_TILE = 256
EXPERT_BLOCK = 256
SORT_ROWS = -(-(MOE_TILE * TOP_K + N_EXPERTS * (SUBLANES - 1)) // LANES) * LANES
N_MOE_TILES = N_TOK // MOE_TILE
_MAX_SORTED = (N_MOE_TILES * (MOE_TILE * TOP_K + N_EXPERTS * (SUBLANES - 1))
               + N_EXPERTS * (EXPERT_BLOCK - 1))
N_EXPERT_BLOCKS = -(-_MAX_SORTED // EXPERT_BLOCK)
SORTED_ROWS = N_EXPERT_BLOCKS * EXPERT_BLOCK

VMEM_LIMIT = 48 * 1024 * 1024


def _cparams(sem):
    return pltpu.CompilerParams(dimension_semantics=sem,
                                vmem_limit_bytes=VMEM_LIMIT)


def _dot(a, b):
    return jnp.dot(a, b, preferred_element_type=F32)


def _dot_nt(a, b):
    return lax.dot_general(a, b, (((1,), (1,)), ((), ())),
                           preferred_element_type=F32)


def _split3(x):
    hi = x.astype(BF16)
    r1 = x - hi.astype(F32)
    mid = r1.astype(BF16)
    lo = (r1 - mid.astype(F32)).astype(BF16)
    return hi, mid, lo


def _dot_sel_left(sel, x):
    hi, mid, lo = _split3(x)
    return _dot(sel, hi) + _dot(sel, mid) + _dot(sel, lo)


def _dot_sel_right(x, sel):
    hi, mid, lo = _split3(x)
    return _dot(hi, sel) + _dot(mid, sel) + _dot(lo, sel)


def _sigmoid(x):
    return 1.0 / (1.0 + jnp.exp(-x))


def _rms_mod(x, g, mod):
    ms = jnp.mean(x * x, axis=-1, keepdims=True)
    y = (x * lax.rsqrt(ms + EPS)) * g
    return y * (1.0 + mod[:, D_MODEL:2 * D_MODEL]) + mod[:, 0:D_MODEL]


def _ada_kernel(c_ref, w_ref, b_ref, o_ref):
    c = c_ref[...]
    s = (c * _sigmoid(c)).astype(BF16)
    o_ref[0] = _dot(s, w_ref[0].astype(BF16)) + b_ref[0]


def _ada_mods(c, ada_w, ada_b):
    n_pairs = DEPTH * 2
    w = ada_w.reshape(n_pairs, D_MODEL, 3 * D_MODEL)
    b = ada_b.reshape(n_pairs, 1, 3 * D_MODEL)
    col = D_MODEL
    return pl.pallas_call(
        _ada_kernel,
        out_shape=jax.ShapeDtypeStruct((n_pairs, BATCH, 3 * D_MODEL), F32),
        grid=(n_pairs, 3 * D_MODEL // col),
        in_specs=[
            pl.BlockSpec((BATCH, D_MODEL), lambda p, j: (0, 0)),
            pl.BlockSpec((1, D_MODEL, col), lambda p, j: (p, 0, j)),
            pl.BlockSpec((1, 1, col), lambda p, j: (p, 0, j)),
        ],
        out_specs=pl.BlockSpec((1, BATCH, col), lambda p, j: (p, 0, j)),
        compiler_params=_cparams(("arbitrary", "arbitrary")),
        name="ada_mods",
    )(c, w, b)


def _mlstm_in_kernel(x_ref, g_ref, mod_ref, wq_ref, wkt_ref, wv_ref, wo_ref,
                     wg_ref, bg_ref, q_ref, kt_ref, v_ref, o_ref, gates_ref,
                     gatest_ref):
    h = _rms_mod(x_ref[...], g_ref[...], mod_ref[0])
    hb = h.astype(BF16)
    q_ref[...] = (_dot(hb, wq_ref[...]) * (M_DK ** -0.5)).astype(BF16)
    kt_ref[...] = _dot_nt(wkt_ref[...], hb).astype(BF16)
    v_ref[...] = _dot(hb, wv_ref[...]).astype(BF16)
    o_ref[...] = _dot(hb, wo_ref[...]).astype(BF16)
    g = _dot(hb, wg_ref[...]) + bg_ref[...]
    log_sig = jnp.minimum(g, 0.0) - jnp.log(1.0 + jnp.exp(-jnp.abs(g)))
    lane = lax.broadcasted_iota(I32, g.shape, 1)
    gg = jnp.where(lane < M_HEADS, g, log_sig)
    gates_ref[...] = gg
    gatest_ref[...] = gg.T[0:SUBLANES, :]


def _mlstm_in(x, norm_g, mod, w_in, b_gates):
    t = ROW_TILE
    tiles_per_seq = SEQ // t
    wq = w_in[:, 0:M_QK].astype(BF16)
    wkt = w_in[:, M_QK:2 * M_QK].T.astype(BF16)
    wv = w_in[:, 2 * M_QK:2 * M_QK + M_V].astype(BF16)
    wo = w_in[:, 2 * M_QK + M_V:2 * M_QK + 2 * M_V].astype(BF16)
    n_gate = 2 * M_HEADS
    wg = jnp.pad(w_in[:, 2 * M_QK + 2 * M_V:], ((0, 0), (0, LANES - n_gate))).astype(BF16)
    bg = jnp.pad(b_gates.astype(F32), (0, LANES - n_gate)).reshape(1, LANES)
    full = lambda shape: pl.BlockSpec(shape, lambda i: (0, 0))
    return pl.pallas_call(
        _mlstm_in_kernel,
        out_shape=(
            jax.ShapeDtypeStruct((N_TOK, M_QK), BF16),
            jax.ShapeDtypeStruct((M_QK, N_TOK), BF16),
            jax.ShapeDtypeStruct((N_TOK, M_V), BF16),
            jax.ShapeDtypeStruct((N_TOK, M_V), BF16),
            jax.ShapeDtypeStruct((N_TOK, LANES), F32),
            jax.ShapeDtypeStruct((SUBLANES, N_TOK), F32),
        ),
        grid=(N_TOK // t,),
        in_specs=[
            pl.BlockSpec((t, D_MODEL), lambda i: (i, 0)),
            full((1, D_MODEL)),
            pl.BlockSpec((1, 1, 3 * D_MODEL), lambda i: (i // tiles_per_seq, 0, 0)),
            full((D_MODEL, M_QK)),
            full((M_QK, D_MODEL)),
            full((D_MODEL, M_V)),
            full((D_MODEL, M_V)),
            full((D_MODEL, LANES)),
            full((1, LANES)),
        ],
        out_specs=(
            pl.BlockSpec((t, M_QK), lambda i: (i, 0)),
            pl.BlockSpec((M_QK, t), lambda i: (0, i)),
            pl.BlockSpec((t, M_V), lambda i: (i, 0)),
            pl.BlockSpec((t, M_V), lambda i: (i, 0)),
            pl.BlockSpec((t, LANES), lambda i: (i, 0)),
            pl.BlockSpec((SUBLANES, t), lambda i: (0, i)),
        ),
        compiler_params=_cparams(("arbitrary",)),
        name="mlstm_in",
    )(x, norm_g.reshape(1, D_MODEL), mod, wq, wkt, wv, wo, wg, bg)


def _mlstm_rec_kernel(q_ref, kt_ref, v_ref, o_ref, gates_ref, gatest_ref,
                      x_ref, mod_ref, ng_ref, wout_ref, out_ref, ct_ref, m_ref):
    L = MLSTM_CHUNK

    @pl.when(pl.program_id(1) == 0)
    def _():
        ct_ref[...] = jnp.zeros_like(ct_ref)
        m_ref[...] = jnp.zeros_like(m_ref)

    gates = gates_ref[...]
    gatest = gatest_ref[...]
    row = lax.broadcasted_iota(I32, (L, L), 0)
    col = lax.broadcasted_iota(I32, (L, L), 1)
    causal = row >= col
    tri_low = jnp.where(causal, 1.0, 0.0).astype(BF16)
    tri_up = jnp.where(row <= col, 1.0, 0.0).astype(BF16)
    cum_cols = _dot_sel_left(tri_low, gates)
    cum_rows = _dot_sel_right(gatest, tri_up)
    ones_cols = jnp.ones((L, LANES), BF16)

    hs = []
    for h in range(M_HEADS):
        ig_row = gatest[h:h + 1, :]
        bcum_col = cum_cols[:, M_HEADS + h:M_HEADS + h + 1]
        bcum_row = cum_rows[M_HEADS + h:M_HEADS + h + 1, :]
        m_prev = m_ref[h:h + 1, 0:1]
        dlog = jnp.where(causal, bcum_col - bcum_row + ig_row, -jnp.inf)
        inter_log = bcum_col + m_prev
        m_t = jnp.maximum(inter_log, jnp.max(dlog, axis=1, keepdims=True))
        w_intra = jnp.exp(dlog - m_t)
        w_inter = jnp.exp(inter_log - m_t)
        qh = q_ref[:, h * M_DK:(h + 1) * M_DK]
        kth = kt_ref[h * M_DK:(h + 1) * M_DK, :]
        vh = v_ref[:, h * M_DV:(h + 1) * M_DV]
        scores = _dot(qh, kth) * w_intra
        state = ct_ref[h]
        q_state = _dot(qh, state.astype(BF16))
        num = _dot(scores.astype(BF16), vh) + w_inter * q_state[:, 0:M_DV]
        den = (jnp.sum(scores, axis=1, keepdims=True)
               + w_inter * q_state[:, M_DV:M_DV + 1])
        h_out = num / jnp.maximum(jnp.abs(den), jnp.exp(-m_t))

        b_last = bcum_row[:, L - 1:L]
        log_src = b_last - bcum_row + ig_row
        m_new = jnp.maximum(b_last + m_prev,
                            jnp.max(log_src, axis=1, keepdims=True))
        w_src = jnp.exp(log_src - m_new)
        decay = jnp.exp(b_last + m_prev - m_new)
        kt_w = (kth.astype(F32) * w_src).astype(BF16)
        v_ext = jnp.concatenate([vh, ones_cols], axis=1)
        ct_ref[h] = decay * state + _dot(kt_w, v_ext)
        m_ref[h:h + 1, :] = jnp.broadcast_to(m_new, (1, LANES))

        hn = h_out * lax.rsqrt(jnp.mean(h_out * h_out, axis=1, keepdims=True) + EPS)
        og = o_ref[:, h * M_DV:(h + 1) * M_DV].astype(F32)
        hs.append((hn * ng_ref[:, h * M_DV:(h + 1) * M_DV] * _sigmoid(og)).astype(BF16))

    y = _dot(jnp.concatenate(hs, axis=1), wout_ref[...])
    gate = mod_ref[0][:, 2 * D_MODEL:3 * D_MODEL]
    out_ref[...] = x_ref[...] + gate * y


def _mlstm_rec(x, mod, q, kt, v, o, gates, gatest, m_norm_g, w_out):
    L = MLSTM_CHUNK
    nc = SEQ // L
    rows = lambda width: pl.BlockSpec((L, width), lambda b, j: (b * nc + j, 0))
    cols = lambda height: pl.BlockSpec((height, L), lambda b, j: (0, b * nc + j))
    return pl.pallas_call(
        _mlstm_rec_kernel,
        out_shape=jax.ShapeDtypeStruct((N_TOK, D_MODEL), F32),
        grid=(BATCH, nc),
        in_specs=[
            rows(M_QK), cols(M_QK), rows(M_V), rows(M_V), rows(LANES),
            cols(SUBLANES), rows(D_MODEL),
            pl.BlockSpec((1, 1, 3 * D_MODEL), lambda b, j: (b, 0, 0)),
            pl.BlockSpec((1, M_V), lambda b, j: (0, 0)),
            pl.BlockSpec((M_V, D_MODEL), lambda b, j: (0, 0)),
        ],
        out_specs=rows(D_MODEL),
        scratch_shapes=[
            pltpu.VMEM((M_HEADS, M_DK, M_DV + LANES), F32),
            pltpu.VMEM((SUBLANES, LANES), F32),
        ],
        compiler_params=_cparams(("arbitrary", "arbitrary")),
        name="mlstm_rec",
    )(q, kt, v, o, gates, gatest, x, mod, m_norm_g.reshape(1, M_V).astype(F32),
      w_out.astype(BF16))


_CONV_COLS = 256


def _conv_kernel(x_ref, g_ref, mod_ref, win_ref, cw_ref, wout_ref, out_ref,
                 carry_ref, z_ref):
    t = ROW_TILE
    tiles_per_seq = SEQ // t

    @pl.when(pl.program_id(0) % tiles_per_seq == 0)
    def _():
        carry_ref[...] = jnp.zeros_like(carry_ref)

    x = x_ref[...]
    mod = mod_ref[0]
    hb = _rms_mod(x, g_ref[...], mod).astype(BF16)
    row = lax.broadcasted_iota(I32, (t, _CONV_COLS), 0)
    for j in range(D_MODEL // _CONV_COLS):
        lo, hi = j * _CONV_COLS, (j + 1) * _CONV_COLS
        b_gate = _dot(hb, win_ref[:, lo:hi])
        c_gate = _dot(hb, win_ref[:, D_MODEL + lo:D_MODEL + hi])
        xb = _dot(hb, win_ref[:, 2 * D_MODEL + lo:2 * D_MODEL + hi])
        u = c_gate * xb
        prev1 = carry_ref[SUBLANES - 1:SUBLANES, lo:hi]
        prev2 = carry_ref[SUBLANES - 2:SUBLANES - 1, lo:hi]
        u1 = jnp.where(row == 0, prev1, pltpu.roll(u, 1, axis=0))
        u2 = jnp.where(row == 0, prev2,
                       jnp.where(row == 1, prev1, pltpu.roll(u, 2, axis=0)))
        y = (cw_ref[0:1, lo:hi] * u2 + cw_ref[1:2, lo:hi] * u1
             + cw_ref[2:3, lo:hi] * u)
        z_ref[:, lo:hi] = (b_gate * y).astype(BF16)
        carry_ref[:, lo:hi] = u[t - SUBLANES:t, :]
    gate = mod[:, 2 * D_MODEL:3 * D_MODEL]
    out_ref[...] = x + gate * _dot(z_ref[...], wout_ref[...])


def _conv_layer(x, norm_g, mod, w_in, conv_w, w_out):
    t = ROW_TILE
    tiles_per_seq = SEQ // t
    full = lambda shape: pl.BlockSpec(shape, lambda i: (0, 0))
    cw = jnp.pad(conv_w.astype(F32), ((0, SUBLANES - CONV_K), (0, 0)))
    return pl.pallas_call(
        _conv_kernel,
        out_shape=jax.ShapeDtypeStruct((N_TOK, D_MODEL), F32),
        grid=(N_TOK // t,),
        in_specs=[
            pl.BlockSpec((t, D_MODEL), lambda i: (i, 0)),
            full((1, D_MODEL)),
            pl.BlockSpec((1, 1, 3 * D_MODEL), lambda i: (i // tiles_per_seq, 0, 0)),
            full((D_MODEL, 3 * D_MODEL)),
            full((SUBLANES, D_MODEL)),
            full((D_MODEL, D_MODEL)),
        ],
        out_specs=pl.BlockSpec((t, D_MODEL), lambda i: (i, 0)),
        scratch_shapes=[
            pltpu.VMEM((SUBLANES, D_MODEL), F32),
            pltpu.VMEM((t, D_MODEL), BF16),
        ],
        compiler_params=_cparams(("arbitrary",)),
        name="conv_layer",
    )(x, norm_g.reshape(1, D_MODEL), mod, w_in.astype(BF16), cw, w_out.astype(BF16))


_ROUTE_ROWS = LANES
_EXPERT_ROW0 = SUBLANES


def _route_kernel(x_ref, g_ref, mod_ref, wr_hi_ref, wr_lo_ref, br_ref,
                  h_ref, route_ref, cnt_ref):
    t = MOE_TILE
    h = _rms_mod(x_ref[...], g_ref[...], mod_ref[0])
    h_hi = h.astype(BF16)
    h_lo = (h - h_hi.astype(F32)).astype(BF16)
    h_ref[...] = h_hi
    logits = (_dot_nt(wr_hi_ref[...], h_hi) + _dot_nt(wr_hi_ref[...], h_lo)
              + _dot_nt(wr_lo_ref[...], h_hi)) + br_ref[...]

    sub = lax.broadcasted_iota(I32, (SUBLANES, t), 0)
    neg_inf = -jnp.inf
    gl = jnp.where(sub < N_GROUPS, logits[0:SUBLANES, :], neg_inf)
    gmax = jnp.max(gl, axis=0, keepdims=True)
    g_sel = jnp.min(jnp.where(gl == gmax, sub, SUBLANES), axis=0, keepdims=True)
    p_sel = 1.0 / jnp.sum(jnp.exp(gl - gmax), axis=0, keepdims=True)

    e_sel = jnp.zeros((E_PER_GROUP, t), F32)
    for g in range(N_GROUPS):
        r0 = _EXPERT_ROW0 + g * E_PER_GROUP
        e_sel = jnp.where(g_sel == g, logits[r0:r0 + E_PER_GROUP, :], e_sel)
    v1 = jnp.max(e_sel, axis=0, keepdims=True)
    i1 = jnp.min(jnp.where(e_sel == v1, sub, SUBLANES), axis=0, keepdims=True)
    e_rest = jnp.where(sub == i1, neg_inf, e_sel)
    v2 = jnp.max(e_rest, axis=0, keepdims=True)
    i2 = jnp.min(jnp.where(e_rest == v2, sub, SUBLANES), axis=0, keepdims=True)
    ratio = jnp.exp(v2 - v1)
    w1 = p_sel / (1.0 + ratio)
    w2 = p_sel * ratio / (1.0 + ratio)
    eid1 = g_sel * E_PER_GROUP + i1
    eid2 = g_sel * E_PER_GROUP + i2

    erow = lax.broadcasted_iota(I32, (N_EXPERTS, t), 0)
    m1 = erow == eid1
    m2 = erow == eid2
    member = jnp.where(m1 | m2, 1.0, 0.0)
    r = lax.broadcasted_iota(I32, (t, t), 0)
    c = lax.broadcasted_iota(I32, (t, t), 1)
    earlier = jnp.where(r < c, 1.0, 0.0).astype(BF16)
    rank = _dot(member.astype(BF16), earlier)
    cnt = jnp.sum(member, axis=1, keepdims=True)
    cnt_pad = jnp.floor((cnt + (SUBLANES - 1.0)) * (1.0 / SUBLANES)) * SUBLANES
    er = lax.broadcasted_iota(I32, (N_EXPERTS, N_EXPERTS), 0)
    ec = lax.broadcasted_iota(I32, (N_EXPERTS, N_EXPERTS), 1)
    before = jnp.where(er > ec, 1.0, 0.0).astype(BF16)
    seg_start = _dot_sel_left(before, jnp.broadcast_to(cnt_pad, (N_EXPERTS, LANES)))[:, 0:1]
    pos = seg_start + rank
    pos1 = jnp.sum(jnp.where(m1, pos, 0.0), axis=0, keepdims=True)
    pos2 = jnp.sum(jnp.where(m2, pos, 0.0), axis=0, keepdims=True)

    out = jnp.zeros((SUBLANES, t), F32)
    for k, val in enumerate((pos1, pos2, w1, w2)):
        out = jnp.where(sub == k, val, out)
    route_ref[...] = out
    cnt_ref[0] = jnp.broadcast_to(cnt, (N_EXPERTS, LANES))


def _route(x, norm_g, mod, w_group, b_group, w_expert, b_expert):
    t = MOE_TILE
    tiles_per_seq = SEQ // t
    wr = jnp.zeros((_ROUTE_ROWS, D_MODEL), F32)
    wr = wr.at[0:N_GROUPS].set(w_group.T.astype(F32))
    wr = wr.at[_EXPERT_ROW0:_EXPERT_ROW0 + N_EXPERTS].set(w_expert.T.astype(F32))
    wr_hi = wr.astype(BF16)
    wr_lo = (wr - wr_hi.astype(F32)).astype(BF16)
    br = jnp.zeros((_ROUTE_ROWS,), F32)
    br = br.at[0:N_GROUPS].set(b_group.astype(F32))
    br = br.at[_EXPERT_ROW0:_EXPERT_ROW0 + N_EXPERTS].set(b_expert.astype(F32))
    full = lambda shape: pl.BlockSpec(shape, lambda i: (0, 0))
    return pl.pallas_call(
        _route_kernel,
        out_shape=(
            jax.ShapeDtypeStruct((N_TOK, D_MODEL), BF16),
            jax.ShapeDtypeStruct((SUBLANES, N_TOK), F32),
            jax.ShapeDtypeStruct((N_MOE_TILES, N_EXPERTS, LANES), F32),
        ),
        grid=(N_MOE_TILES,),
        in_specs=[
            pl.BlockSpec((t, D_MODEL), lambda i: (i, 0)),
            full((1, D_MODEL)),
            pl.BlockSpec((1, 1, 3 * D_MODEL), lambda i: (i // tiles_per_seq, 0, 0)),
            full((_ROUTE_ROWS, D_MODEL)),
            full((_ROUTE_ROWS, D_MODEL)),
            full((_ROUTE_ROWS, 1)),
        ],
        out_specs=(
            pl.BlockSpec((t, D_MODEL), lambda i: (i, 0)),
            pl.BlockSpec((SUBLANES, t), lambda i: (0, i)),
            pl.BlockSpec((1, N_EXPERTS, LANES), lambda i: (i, 0, 0)),
        ),
        compiler_params=_cparams(("arbitrary",)),
        name="moe_route",
    )(x, norm_g.reshape(1, D_MODEL), mod, wr_hi, wr_lo, br.reshape(_ROUTE_ROWS, 1))


def _dispatch_plan(cnt):
    cnt = cnt.astype(I32)
    cnt_pad = (cnt + SUBLANES - 1) // SUBLANES * SUBLANES
    seg = jnp.cumsum(cnt_pad, axis=1) - cnt_pad
    tot = jnp.sum(cnt_pad, axis=0)
    ptot = (tot + EXPERT_BLOCK - 1) // EXPERT_BLOCK * EXPERT_BLOCK
    pend = jnp.cumsum(ptot)
    gbase = pend - ptot
    dst = gbase[None, :] + jnp.cumsum(cnt_pad, axis=0) - cnt_pad
    nch = cnt_pad // SUBLANES
    n_used = (pend[-1] // EXPERT_BLOCK).astype(I32)
    blk = jnp.arange(N_EXPERT_BLOCKS, dtype=I32)
    blk_start = jnp.minimum(blk, n_used - 1) * EXPERT_BLOCK
    blk_e = jnp.sum((pend[None, :] <= blk_start[:, None]).astype(I32), axis=1)
    blk_e = jnp.minimum(blk_e, N_EXPERTS - 1).astype(I32)
    gap_dst = gbase + tot
    gap_nch = (ptot - tot) // SUBLANES
    misc = jnp.stack([n_used, jnp.sum(gap_nch)]).astype(I32)
    return dict(seg=seg.reshape(-1).astype(I32), dst=dst.reshape(-1).astype(I32),
                nch=nch.reshape(-1).astype(I32),
                ntot=jnp.sum(nch, axis=1).astype(I32), blk_e=blk_e,
                gap_dst=gap_dst.astype(I32), gap_nch=gap_nch.astype(I32), misc=misc)


def _segment_copies(tile, seg_ref, dst_ref, nch_ref, make_copy):
    def per_expert(e, total):
        idx = tile * N_EXPERTS + e
        n = nch_ref[idx]
        s0 = seg_ref[idx]
        d0 = dst_ref[idx]

        def per_chunk(cidx, carry):
            s = pl.multiple_of(s0 + cidx * SUBLANES, SUBLANES)
            d = pl.multiple_of(d0 + cidx * SUBLANES, SUBLANES)
            make_copy(s, d).start()
            return carry

        lax.fori_loop(0, n, per_chunk, 0)
        return total + n

    return lax.fori_loop(0, N_EXPERTS, per_expert, jnp.int32(0))


def _wait_copies(n, make_copy):
    def body(_, carry):
        make_copy(0, 0).wait()
        return carry
    lax.fori_loop(0, n, body, 0)


_HALF = D_MODEL // 2
_HI_MASK = -65536


def _pack_pairs(x):
    lo = lax.shift_right_logical(lax.bitcast_convert_type(x[:, 0:_HALF], I32), 16)
    hi = lax.bitcast_convert_type(x[:, _HALF:D_MODEL], I32) & _HI_MASK
    return lo | hi


def _unpack_pairs(w):
    lo = lax.bitcast_convert_type(lax.shift_left(w, 16), F32).astype(BF16)
    hi = lax.bitcast_convert_type(w & _HI_MASK, F32).astype(BF16)
    return lo, hi


def _dispatch_kernel(seg_ref, dst_ref, nch_ref, ntot_ref, gap_dst_ref, gap_nch_ref,
                     misc_ref, route_ref, h_ref, xs_ref, buf_ref, zero_ref, sem, zsem):
    t = MOE_TILE
    i = pl.program_id(0)
    slot = i % 2
    n_used = misc_ref[0]

    def zero_small(d):
        return pltpu.make_async_copy(zero_ref.at[pl.ds(0, SUBLANES)],
                                     xs_ref.at[pl.ds(d, SUBLANES)], zsem)

    def zero_block(b):
        d = pl.multiple_of(b * EXPERT_BLOCK, EXPERT_BLOCK)
        return pltpu.make_async_copy(zero_ref, xs_ref.at[pl.ds(d, EXPERT_BLOCK)], zsem)

    @pl.when(i == 0)
    def _():
        zero_ref[...] = jnp.zeros_like(zero_ref)

        def per_expert(e, carry):
            d0 = gap_dst_ref[e]

            def per_chunk(cidx, c2):
                zero_small(pl.multiple_of(d0 + cidx * SUBLANES, SUBLANES)).start()
                return c2

            lax.fori_loop(0, gap_nch_ref[e], per_chunk, 0)
            return carry

        lax.fori_loop(0, N_EXPERTS, per_expert, 0)

        def per_block(b, carry):
            zero_block(b).start()
            return carry

        lax.fori_loop(n_used, N_EXPERT_BLOCKS, per_block, 0)

    route = route_ref[...]
    pos1 = route[0:1, :].astype(I32)
    pos2 = route[1:2, :].astype(I32)
    r = lax.broadcasted_iota(I32, (SORT_ROWS, t), 0)
    perm = jnp.where((r == pos1) | (r == pos2), 1.0, 0.0).astype(BF16)
    buf_ref[slot] = _pack_pairs(_dot(perm, h_ref[...]))

    def make_copy(which):
        def mk(s, d):
            return pltpu.make_async_copy(buf_ref.at[which, pl.ds(s, SUBLANES)],
                                         xs_ref.at[pl.ds(d, SUBLANES)], sem.at[which])
        return mk

    _segment_copies(i, seg_ref, dst_ref, nch_ref, make_copy(slot))

    @pl.when(i > 0)
    def _():
        _wait_copies(ntot_ref[i - 1], make_copy(1 - slot))

    @pl.when(i == N_MOE_TILES - 1)
    def _():
        _wait_copies(ntot_ref[i], make_copy(slot))
        _wait_copies(misc_ref[1], lambda s, d: zero_small(0))
        _wait_copies(N_EXPERT_BLOCKS - n_used, lambda s, d: zero_block(0))


def _dispatch(plan, route, h):
    t = MOE_TILE
    return pl.pallas_call(
        _dispatch_kernel,
        out_shape=jax.ShapeDtypeStruct((SORTED_ROWS, _HALF), I32),
        grid_spec=pltpu.PrefetchScalarGridSpec(
            num_scalar_prefetch=7,
            grid=(N_MOE_TILES,),
            in_specs=[
                pl.BlockSpec((SUBLANES, t), lambda i, *_: (0, i)),
                pl.BlockSpec((t, D_MODEL), lambda i, *_: (i, 0)),
            ],
            out_specs=pl.BlockSpec(memory_space=pl.ANY),
            scratch_shapes=[
                pltpu.VMEM((2, SORT_ROWS, _HALF), I32),
                pltpu.VMEM((EXPERT_BLOCK, _HALF), I32),
                pltpu.SemaphoreType.DMA((2,)),
                pltpu.SemaphoreType.DMA,
            ],
        ),
        compiler_params=_cparams(("arbitrary",)),
        name="moe_dispatch",
    )(plan["seg"], plan["dst"], plan["nch"], plan["ntot"], plan["gap_dst"],
      plan["gap_nch"], plan["misc"], route, h)


def _expert_kernel(blk_e_ref, misc_ref, x_ref, wg_ref, wu_ref, wd_ref, y_ref,
                   wg_b, wu_b, wd_b):
    i = pl.program_id(0)
    n_used = misc_ref[0]

    @pl.when(i < n_used)
    def _():
        e = blk_e_ref[i]
        e_prev = blk_e_ref[jnp.maximum(i - 1, 0)]

        @pl.when((i == 0) | (e != e_prev))
        def _():
            wg_b[...] = wg_ref[0, 0].astype(BF16)
            wu_b[...] = wu_ref[0, 0].astype(BF16)
            wd_b[...] = wd_ref[0, 0].astype(BF16)

        x_lo, x_hi = _unpack_pairs(x_ref[...])
        x = jnp.concatenate([x_lo, x_hi], axis=1)
        g = _dot(x, wg_b[...])
        u = _dot(x, wu_b[...])
        a = (g * _sigmoid(g) * u).astype(BF16)
        y = _dot(a, wd_b[...])
        y_ref[...] = _pack_pairs(y.astype(BF16).astype(F32))

    @pl.when(i >= n_used)
    def _():
        y_ref[...] = jnp.zeros_like(y_ref)


def _experts(plan, xs, layer, w_gate, w_up, w_down):
    bm = EXPERT_BLOCK
    row_map = lambda i, be, misc: (jnp.minimum(i, misc[0] - 1), 0)
    out_map = lambda i, be, misc: (i, 0)
    w_map = lambda i, be, misc: (layer, be[i], 0, 0)
    return pl.pallas_call(
        _expert_kernel,
        out_shape=jax.ShapeDtypeStruct((SORTED_ROWS, _HALF), I32),
        grid_spec=pltpu.PrefetchScalarGridSpec(
            num_scalar_prefetch=2,
            grid=(N_EXPERT_BLOCKS,),
            in_specs=[
                pl.BlockSpec((bm, _HALF), row_map),
                pl.BlockSpec((1, 1, D_MODEL, D_FF), w_map),
                pl.BlockSpec((1, 1, D_MODEL, D_FF), w_map),
                pl.BlockSpec((1, 1, D_FF, D_MODEL), w_map),
            ],
            out_specs=pl.BlockSpec((bm, _HALF), out_map),
            scratch_shapes=[
                pltpu.VMEM((D_MODEL, D_FF), BF16),
                pltpu.VMEM((D_MODEL, D_FF), BF16),
                pltpu.VMEM((D_FF, D_MODEL), BF16),
            ],
        ),
        compiler_params=_cparams(("arbitrary",)),
        name="moe_experts",
    )(plan["blk_e"], plan["misc"], xs, w_gate, w_up, w_down)


def _combine_kernel(seg_ref, dst_ref, nch_ref, ntot_ref, route_ref, x_ref, mod_ref,
                    fg_ref, ys_ref, out_ref, buf_ref, sem, *, final_norm):
    t = MOE_TILE
    i = pl.program_id(0)
    slot = i % 2

    def make_copy(which):
        def mk(s, d):
            return pltpu.make_async_copy(ys_ref.at[pl.ds(d, SUBLANES)],
                                         buf_ref.at[which, pl.ds(s, SUBLANES)],
                                         sem.at[which])
        return mk

    @pl.when(i == 0)
    def _():
        buf_ref[...] = jnp.zeros_like(buf_ref)
        _segment_copies(i, seg_ref, dst_ref, nch_ref, make_copy(slot))

    @pl.when(i + 1 < N_MOE_TILES)
    def _():
        _segment_copies(i + 1, seg_ref, dst_ref, nch_ref, make_copy(1 - slot))

    route = route_ref[...]
    route_t = jnp.concatenate(
        [route, jnp.zeros((LANES - SUBLANES, t), F32)], axis=0).T
    pos1 = route_t[:, 0:1].astype(I32)
    pos2 = route_t[:, 1:2].astype(I32)
    w1 = route_t[:, 2:3]
    w2 = route_t[:, 3:4]
    c = lax.broadcasted_iota(I32, (t, SORT_ROWS), 1)
    unsort = (jnp.where(c == pos1, w1, 0.0) + jnp.where(c == pos2, w2, 0.0)).astype(BF16)

    _wait_copies(ntot_ref[i], make_copy(slot))
    y_lo, y_hi = _unpack_pairs(buf_ref[slot])
    moe = jnp.concatenate([_dot(unsort, y_lo), _dot(unsort, y_hi)], axis=1)
    gate = mod_ref[0][:, 2 * D_MODEL:3 * D_MODEL]
    x_new = x_ref[...] + gate * moe
    if final_norm:
        ms = jnp.mean(x_new * x_new, axis=-1, keepdims=True)
        x_new = (x_new * lax.rsqrt(ms + EPS)) * fg_ref[...]
    out_ref[...] = x_new


def _combine(plan, route, x, mod, final_g, ys, final_norm):
    t = MOE_TILE
    tiles_per_seq = SEQ // t
    return pl.pallas_call(
        functools.partial(_combine_kernel, final_norm=final_norm),
        out_shape=jax.ShapeDtypeStruct((N_TOK, D_MODEL), F32),
        grid_spec=pltpu.PrefetchScalarGridSpec(
            num_scalar_prefetch=4,
            grid=(N_MOE_TILES,),
            in_specs=[
                pl.BlockSpec((SUBLANES, t), lambda i, *_: (0, i)),
                pl.BlockSpec((t, D_MODEL), lambda i, *_: (i, 0)),
                pl.BlockSpec((1, 1, 3 * D_MODEL), lambda i, *_: (i // tiles_per_seq, 0, 0)),
                pl.BlockSpec((1, D_MODEL), lambda i, *_: (0, 0)),
                pl.BlockSpec(memory_space=pl.ANY),
            ],
            out_specs=pl.BlockSpec((t, D_MODEL), lambda i, *_: (i, 0)),
            scratch_shapes=[
                pltpu.VMEM((2, SORT_ROWS, _HALF), I32),
                pltpu.SemaphoreType.DMA((2,)),
            ],
        ),
        compiler_params=_cparams(("arbitrary",)),
        name="moe_combine",
    )(plan["seg"], plan["dst"], plan["nch"], plan["ntot"], route, x, mod,
      final_g.reshape(1, D_MODEL).astype(F32), ys)


def _moe_layer(x, norm_g, mod, final_g, w_group, b_group, w_expert, b_expert,
               layer, w_gate, w_up, w_down, final_norm):
    h, route, cnt = _route(x, norm_g, mod, w_group, b_group, w_expert, b_expert)
    plan = _dispatch_plan(cnt[:, :, 0])
    xs = _dispatch(plan, route, h)
    ys = _experts(plan, xs, layer, w_gate, w_up, w_down)
    return _combine(plan, route, x, mod, final_g, ys, final_norm)


def kernel(x, c, ada_w, ada_b, norm_g, final_g, m_w_in, m_b_gates, m_norm_g, m_w_out, s_w_in, s_conv_w, s_w_out, r_w_group, r_b_group, r_w_expert, r_b_expert, e_w_gate, e_w_up, e_w_down):
    mods = _ada_mods(c, ada_w, ada_b)
    xt = x.reshape(N_TOK, D_MODEL)
    for i in range(DEPTH):
        mod_mix = mods[2 * i].reshape(BATCH, 1, 3 * D_MODEL)
        mod_ffn = mods[2 * i + 1].reshape(BATCH, 1, 3 * D_MODEL)
        j = i // 2
        if i % 2 == 0:
            q, kt, v, o, gates, gatest = _mlstm_in(xt, norm_g[i, 0], mod_mix,
                                                   m_w_in[j], m_b_gates[j])
            xt = _mlstm_rec(xt, mod_mix, q, kt, v, o, gates, gatest,
                            m_norm_g[j], m_w_out[j])
        else:
            xt = _conv_layer(xt, norm_g[i, 0], mod_mix, s_w_in[j], s_conv_w[j],
                             s_w_out[j])
        xt = _moe_layer(xt, norm_g[i, 1], mod_ffn, final_g, r_w_group[i],
                        r_b_group[i], r_w_expert[i], r_b_expert[i], i, e_w_gate,
                        e_w_up, e_w_down, final_norm=(i == DEPTH - 1))
    return xt.reshape(BATCH, SEQ, D_MODEL)
```

```python
import functools

import jax
import jax.numpy as jnp
from jax import lax
from jax.experimental import pallas as pl
from jax.experimental.pallas import tpu as pltpu

F32 = jnp.float32
BF16 = jnp.bfloat16
I32 = jnp.int32

D_MODEL = 1024
BATCH = 8
SEQ = 2048
DEPTH = 4
N_TOK = BATCH * SEQ
M_HEADS = 4
M_DK = 128
M_DV = 256
M_QK = M_HEADS * M_DK
M_V = M_HEADS * M_DV
CONV_K = 3
N_GROUPS = 4
E_PER_GROUP = 8
N_EXPERTS = N_GROUPS * E_PER_GROUP
TOP_K = 2
D_FF = 512
EPS = 1e-6

SUBLANES = 8
LANES = 128

ROW_TILE = 512
MLSTM_CHUNK = 256
MOE_TILE = 256
EXPERT_BLOCK = 256
SORT_ROWS = -(-(MOE_TILE * TOP_K + N_EXPERTS * (SUBLANES - 1)) // LANES) * LANES
N_MOE_TILES = N_TOK // MOE_TILE
_MAX_SORTED = (N_MOE_TILES * (MOE_TILE * TOP_K + N_EXPERTS * (SUBLANES - 1))
               + N_EXPERTS * (EXPERT_BLOCK - 1))
N_EXPERT_BLOCKS = -(-_MAX_SORTED // EXPERT_BLOCK)
SORTED_ROWS = N_EXPERT_BLOCKS * EXPERT_BLOCK

VMEM_LIMIT = 48 * 1024 * 1024


def _cparams(sem):
    return pltpu.CompilerParams(dimension_semantics=sem,
                                vmem_limit_bytes=VMEM_LIMIT)


def _dot(a, b):
    return jnp.dot(a, b, preferred_element_type=F32)


def _dot_nt(a, b):
    return lax.dot_general(a, b, (((1,), (1,)), ((), ())),
                           preferred_element_type=F32)


def _split3(x):
    hi = x.astype(BF16)
    r1 = x - hi.astype(F32)
    mid = r1.astype(BF16)
    lo = (r1 - mid.astype(F32)).astype(BF16)
    return hi, mid, lo


def _dot_sel_left(sel, x):
    hi, mid, lo = _split3(x)
    return _dot(sel, hi) + _dot(sel, mid) + _dot(sel, lo)


def _dot_sel_right(x, sel):
    hi, mid, lo = _split3(x)
    return _dot(hi, sel) + _dot(mid, sel) + _dot(lo, sel)


def _sigmoid(x):
    return 1.0 / (1.0 + jnp.exp(-x))


def _rms_mod(x, g, mod):
    ms = jnp.mean(x * x, axis=-1, keepdims=True)
    y = (x * lax.rsqrt(ms + EPS)) * g
    return y * (1.0 + mod[:, D_MODEL:2 * D_MODEL]) + mod[:, 0:D_MODEL]


def _ada_kernel(c_ref, w_ref, b_ref, o_ref):
    c = c_ref[...]
    s = (c * _sigmoid(c)).astype(BF16)
    o_ref[0] = _dot(s, w_ref[0].astype(BF16)) + b_ref[0]


def _ada_mods(c, ada_w, ada_b):
    n_pairs = DEPTH * 2
    w = ada_w.reshape(n_pairs, D_MODEL, 3 * D_MODEL)
    b = ada_b.reshape(n_pairs, 1, 3 * D_MODEL)
    col = D_MODEL
    return pl.pallas_call(
        _ada_kernel,
        out_shape=jax.ShapeDtypeStruct((n_pairs, BATCH, 3 * D_MODEL), F32),
        grid=(n_pairs, 3 * D_MODEL // col),
        in_specs=[
            pl.BlockSpec((BATCH, D_MODEL), lambda p, j: (0, 0)),
            pl.BlockSpec((1, D_MODEL, col), lambda p, j: (p, 0, j)),
            pl.BlockSpec((1, 1, col), lambda p, j: (p, 0, j)),
        ],
        out_specs=pl.BlockSpec((1, BATCH, col), lambda p, j: (p, 0, j)),
        compiler_params=_cparams(("arbitrary", "arbitrary")),
        name="ada_mods",
    )(c, w, b)


def _mlstm_in_kernel(x_ref, g_ref, mod_ref, wqt_ref, wk_ref, wvt_ref, wot_ref,
                     wg_ref, bg_ref, qt_ref, k_ref, vt_ref, ot_ref, gates_ref,
                     gatest_ref):
    h = _rms_mod(x_ref[...], g_ref[...], mod_ref[0])
    hb = h.astype(BF16)
    qt_ref[...] = (_dot_nt(wqt_ref[...], hb) * (M_DK ** -0.5)).astype(BF16)
    k_ref[...] = _dot(hb, wk_ref[...]).astype(BF16)
    vt_ref[...] = _dot_nt(wvt_ref[...], hb).astype(BF16)
    ot_ref[...] = _dot_nt(wot_ref[...], hb).astype(BF16)
    g = _dot(hb, wg_ref[...]) + bg_ref[...]
    log_sig = jnp.minimum(g, 0.0) - jnp.log(1.0 + jnp.exp(-jnp.abs(g)))
    lane = lax.broadcasted_iota(I32, g.shape, 1)
    gg = jnp.where(lane < M_HEADS, g, log_sig)
    gates_ref[...] = gg
    gatest_ref[...] = gg.T[0:SUBLANES, :]


def _mlstm_in(x, norm_g, mod, w_in, b_gates):
    t = ROW_TILE
    tiles_per_seq = SEQ // t
    wqt = w_in[:, 0:M_QK].T.astype(BF16)
    wk = w_in[:, M_QK:2 * M_QK].astype(BF16)
    wvt = w_in[:, 2 * M_QK:2 * M_QK + M_V].T.astype(BF16)
    wot = w_in[:, 2 * M_QK + M_V:2 * M_QK + 2 * M_V].T.astype(BF16)
    n_gate = 2 * M_HEADS
    wg = jnp.pad(w_in[:, 2 * M_QK + 2 * M_V:], ((0, 0), (0, LANES - n_gate))).astype(BF16)
    bg = jnp.pad(b_gates.astype(F32), (0, LANES - n_gate)).reshape(1, LANES)
    full = lambda shape: pl.BlockSpec(shape, lambda i: (0, 0))
    return pl.pallas_call(
        _mlstm_in_kernel,
        out_shape=(
            jax.ShapeDtypeStruct((M_QK, N_TOK), BF16),
            jax.ShapeDtypeStruct((N_TOK, M_QK), BF16),
            jax.ShapeDtypeStruct((M_V, N_TOK), BF16),
            jax.ShapeDtypeStruct((M_V, N_TOK), BF16),
            jax.ShapeDtypeStruct((N_TOK, LANES), F32),
            jax.ShapeDtypeStruct((SUBLANES, N_TOK), F32),
        ),
        grid=(N_TOK // t,),
        in_specs=[
            pl.BlockSpec((t, D_MODEL), lambda i: (i, 0)),
            full((1, D_MODEL)),
            pl.BlockSpec((1, 1, 3 * D_MODEL), lambda i: (i // tiles_per_seq, 0, 0)),
            full((M_QK, D_MODEL)),
            full((D_MODEL, M_QK)),
            full((M_V, D_MODEL)),
            full((M_V, D_MODEL)),
            full((D_MODEL, LANES)),
            full((1, LANES)),
        ],
        out_specs=(
            pl.BlockSpec((M_QK, t), lambda i: (0, i)),
            pl.BlockSpec((t, M_QK), lambda i: (i, 0)),
            pl.BlockSpec((M_V, t), lambda i: (0, i)),
            pl.BlockSpec((M_V, t), lambda i: (0, i)),
            pl.BlockSpec((t, LANES), lambda i: (i, 0)),
            pl.BlockSpec((SUBLANES, t), lambda i: (0, i)),
        ),
        compiler_params=_cparams(("arbitrary",)),
        name="mlstm_in",
    )(x, norm_g.reshape(1, D_MODEL), mod, wqt, wk, wvt, wot, wg, bg)


def _mlstm_rec_kernel(qt_ref, k_ref, vt_ref, ot_ref, gates_ref, gatest_ref,
                      x_ref, mod_ref, ng_ref, wout_ref, out_ref, ct_ref, m_ref):
    L = MLSTM_CHUNK

    @pl.when(pl.program_id(1) == 0)
    def _():
        ct_ref[...] = jnp.zeros_like(ct_ref)
        m_ref[...] = jnp.zeros_like(m_ref)

    gates = gates_ref[...]
    gatest = gatest_ref[...]
    row = lax.broadcasted_iota(I32, (L, L), 0)
    col = lax.broadcasted_iota(I32, (L, L), 1)
    tri_low = jnp.where(row >= col, 1.0, 0.0).astype(BF16)
    tri_up = jnp.where(row <= col, 1.0, 0.0).astype(BF16)
    cum_cols = _dot_sel_left(tri_low, gates)
    cum_rows = _dot_sel_right(gatest, tri_up)

    col_term = gates - pltpu.roll(cum_cols, LANES - M_HEADS, axis=1)
    col_pieces = jnp.concatenate(_split3(col_term), axis=1)
    sel_row = lax.broadcasted_iota(I32, (3 * LANES, L), 0)
    ng_wide = jnp.concatenate([ng_ref[...]] * (L // LANES), axis=1)

    hs = []
    for h in range(M_HEADS):
        qt = qt_ref[h * M_DK:(h + 1) * M_DK, :]
        kh = k_ref[:, h * M_DK:(h + 1) * M_DK]
        vt = vt_ref[h * M_DV:(h + 1) * M_DV, :]
        state = ct_ref[h]
        ig_row = gatest[h:h + 1, :]
        bcum_row = cum_rows[M_HEADS + h:M_HEADS + h + 1, :]
        m_prev = m_ref[h:h + 1, 0:1]

        pick = jnp.where((sel_row == h) | (sel_row == LANES + h)
                         | (sel_row == 2 * LANES + h), 1.0, 0.0).astype(BF16)
        dlog = jnp.where(row <= col, _dot(col_pieces, pick) + bcum_row, -jnp.inf)
        inter_log = bcum_row + m_prev
        m_t = jnp.maximum(inter_log, jnp.max(dlog, axis=0, keepdims=True))
        w_intra = jnp.exp(dlog - m_t)
        w_inter = jnp.exp(inter_log - m_t)
        scores = _dot(kh, qt) * w_intra
        q_state = _dot(state.astype(BF16), qt)
        num = _dot(vt, scores.astype(BF16)) + w_inter * q_state[0:M_DV, :]
        den = (jnp.sum(scores, axis=0, keepdims=True)
               + w_inter * q_state[M_DV:M_DV + 1, :])
        h_out = num / jnp.maximum(jnp.abs(den), jnp.exp(-m_t))

        b_last = bcum_row[:, L - 1:L]
        log_src = b_last - bcum_row + ig_row
        m_new = jnp.maximum(b_last + m_prev,
                            jnp.max(log_src, axis=1, keepdims=True))
        w_src = jnp.exp(log_src - m_new)
        decay = jnp.exp(b_last + m_prev - m_new)
        vt_w = jnp.concatenate(
            [(vt.astype(F32) * w_src).astype(BF16),
             jnp.broadcast_to(w_src, (SUBLANES, L)).astype(BF16)], axis=0)
        ct_ref[h] = decay * state + _dot(vt_w, kh)
        m_ref[h:h + 1, :] = jnp.broadcast_to(m_new, (1, LANES))

        hn = h_out * lax.rsqrt(jnp.mean(h_out * h_out, axis=0, keepdims=True) + EPS)
        og = ot_ref[h * M_DV:(h + 1) * M_DV, :].astype(F32)
        hs.append((hn * ng_wide[h * M_DV:(h + 1) * M_DV, :] * _sigmoid(og)).astype(BF16))

    hs_t = jnp.concatenate(hs, axis=0)
    y = lax.dot_general(hs_t, wout_ref[...], (((0,), (0,)), ((), ())),
                        preferred_element_type=F32)
    gate = mod_ref[0][:, 2 * D_MODEL:3 * D_MODEL]
    out_ref[...] = x_ref[...] + gate * y


def _mlstm_rec(x, mod, qt, k, vt, ot, gates, gatest, m_norm_g, w_out):
    L = MLSTM_CHUNK
    nc = SEQ // L
    rows = lambda width: pl.BlockSpec((L, width), lambda b, j: (b * nc + j, 0))
    cols = lambda height: pl.BlockSpec((height, L), lambda b, j: (0, b * nc + j))
    ng = jnp.broadcast_to(m_norm_g.astype(F32).reshape(M_V, 1), (M_V, LANES))
    return pl.pallas_call(
        _mlstm_rec_kernel,
        out_shape=jax.ShapeDtypeStruct((N_TOK, D_MODEL), F32),
        grid=(BATCH, nc),
        in_specs=[
            cols(M_QK), rows(M_QK), cols(M_V), cols(M_V), rows(LANES),
            cols(SUBLANES), rows(D_MODEL),
            pl.BlockSpec((1, 1, 3 * D_MODEL), lambda b, j: (b, 0, 0)),
            pl.BlockSpec((M_V, LANES), lambda b, j: (0, 0)),
            pl.BlockSpec((M_V, D_MODEL), lambda b, j: (0, 0)),
        ],
        out_specs=rows(D_MODEL),
        scratch_shapes=[
            pltpu.VMEM((M_HEADS, M_DV + SUBLANES, M_DK), F32),
            pltpu.VMEM((SUBLANES, LANES), F32),
        ],
        compiler_params=_cparams(("arbitrary", "arbitrary")),
        name="mlstm_rec",
    )(qt, k, vt, ot, gates, gatest, x, mod, ng, w_out.astype(BF16))


_CONV_COLS = 256


def _conv_kernel(x_ref, g_ref, mod_ref, win_ref, cw_ref, wout_ref, out_ref,
                 carry_ref, z_ref):
    t = ROW_TILE
    tiles_per_seq = SEQ // t

    @pl.when(pl.program_id(0) % tiles_per_seq == 0)
    def _():
        carry_ref[...] = jnp.zeros_like(carry_ref)

    x = x_ref[...]
    mod = mod_ref[0]
    hb = _rms_mod(x, g_ref[...], mod).astype(BF16)
    row = lax.broadcasted_iota(I32, (t, _CONV_COLS), 0)
    for j in range(D_MODEL // _CONV_COLS):
        lo, hi = j * _CONV_COLS, (j + 1) * _CONV_COLS
        b_gate = _dot(hb, win_ref[:, lo:hi])
        c_gate = _dot(hb, win_ref[:, D_MODEL + lo:D_MODEL + hi])
        xb = _dot(hb, win_ref[:, 2 * D_MODEL + lo:2 * D_MODEL + hi])
        u = c_gate * xb
        prev1 = carry_ref[SUBLANES - 1:SUBLANES, lo:hi]
        prev2 = carry_ref[SUBLANES - 2:SUBLANES - 1, lo:hi]
        u1 = jnp.where(row == 0, prev1, pltpu.roll(u, 1, axis=0))
        u2 = jnp.where(row == 0, prev2,
                       jnp.where(row == 1, prev1, pltpu.roll(u, 2, axis=0)))
        y = (cw_ref[0:1, lo:hi] * u2 + cw_ref[1:2, lo:hi] * u1
             + cw_ref[2:3, lo:hi] * u)
        z_ref[:, lo:hi] = (b_gate * y).astype(BF16)
        carry_ref[:, lo:hi] = u[t - SUBLANES:t, :]
    gate = mod[:, 2 * D_MODEL:3 * D_MODEL]
    out_ref[...] = x + gate * _dot(z_ref[...], wout_ref[...])


def _conv_layer(x, norm_g, mod, w_in, conv_w, w_out):
    t = ROW_TILE
    tiles_per_seq = SEQ // t
    full = lambda shape: pl.BlockSpec(shape, lambda i: (0, 0))
    cw = jnp.pad(conv_w.astype(F32), ((0, SUBLANES - CONV_K), (0, 0)))
    return pl.pallas_call(
        _conv_kernel,
        out_shape=jax.ShapeDtypeStruct((N_TOK, D_MODEL), F32),
        grid=(N_TOK // t,),
        in_specs=[
            pl.BlockSpec((t, D_MODEL), lambda i: (i, 0)),
            full((1, D_MODEL)),
            pl.BlockSpec((1, 1, 3 * D_MODEL), lambda i: (i // tiles_per_seq, 0, 0)),
            full((D_MODEL, 3 * D_MODEL)),
            full((SUBLANES, D_MODEL)),
            full((D_MODEL, D_MODEL)),
        ],
        out_specs=pl.BlockSpec((t, D_MODEL), lambda i: (i, 0)),
        scratch_shapes=[
            pltpu.VMEM((SUBLANES, D_MODEL), F32),
            pltpu.VMEM((t, D_MODEL), BF16),
        ],
        compiler_params=_cparams(("arbitrary",)),
        name="conv_layer",
    )(x, norm_g.reshape(1, D_MODEL), mod, w_in.astype(BF16), cw, w_out.astype(BF16))


_ROUTE_ROWS = LANES
_EXPERT_ROW0 = SUBLANES


def _route_kernel(x_ref, g_ref, mod_ref, wr_hi_ref, wr_lo_ref, br_ref,
                  h_ref, route_ref, cnt_ref):
    t = MOE_TILE
    h = _rms_mod(x_ref[...], g_ref[...], mod_ref[0])
    h_hi = h.astype(BF16)
    h_lo = (h - h_hi.astype(F32)).astype(BF16)
    h_ref[...] = h_hi
    logits = (_dot_nt(wr_hi_ref[...], h_hi) + _dot_nt(wr_hi_ref[...], h_lo)
              + _dot_nt(wr_lo_ref[...], h_hi)) + br_ref[...]

    sub = lax.broadcasted_iota(I32, (SUBLANES, t), 0)
    neg_inf = -jnp.inf
    gl = jnp.where(sub < N_GROUPS, logits[0:SUBLANES, :], neg_inf)
    gmax = jnp.max(gl, axis=0, keepdims=True)
    g_sel = jnp.min(jnp.where(gl == gmax, sub, SUBLANES), axis=0, keepdims=True)
    p_sel = 1.0 / jnp.sum(jnp.exp(gl - gmax), axis=0, keepdims=True)

    e_sel = jnp.zeros((E_PER_GROUP, t), F32)
    for g in range(N_GROUPS):
        r0 = _EXPERT_ROW0 + g * E_PER_GROUP
        e_sel = jnp.where(g_sel == g, logits[r0:r0 + E_PER_GROUP, :], e_sel)
    v1 = jnp.max(e_sel, axis=0, keepdims=True)
    i1 = jnp.min(jnp.where(e_sel == v1, sub, SUBLANES), axis=0, keepdims=True)
    e_rest = jnp.where(sub == i1, neg_inf, e_sel)
    v2 = jnp.max(e_rest, axis=0, keepdims=True)
    i2 = jnp.min(jnp.where(e_rest == v2, sub, SUBLANES), axis=0, keepdims=True)
    ratio = jnp.exp(v2 - v1)
    w1 = p_sel / (1.0 + ratio)
    w2 = p_sel * ratio / (1.0 + ratio)
    eid1 = g_sel * E_PER_GROUP + i1
    eid2 = g_sel * E_PER_GROUP + i2

    erow = lax.broadcasted_iota(I32, (N_EXPERTS, t), 0)
    m1 = erow == eid1
    m2 = erow == eid2
    member = jnp.where(m1 | m2, 1.0, 0.0)
    r = lax.broadcasted_iota(I32, (t, t), 0)
    c = lax.broadcasted_iota(I32, (t, t), 1)
    earlier = jnp.where(r < c, 1.0, 0.0).astype(BF16)
    rank = _dot(member.astype(BF16), earlier)
    cnt = jnp.sum(member, axis=1, keepdims=True)
    cnt_pad = jnp.floor((cnt + (SUBLANES - 1.0)) * (1.0 / SUBLANES)) * SUBLANES
    er = lax.broadcasted_iota(I32, (N_EXPERTS, N_EXPERTS), 0)
    ec = lax.broadcasted_iota(I32, (N_EXPERTS, N_EXPERTS), 1)
    before = jnp.where(er > ec, 1.0, 0.0).astype(BF16)
    seg_start = _dot_sel_left(before, jnp.broadcast_to(cnt_pad, (N_EXPERTS, LANES)))[:, 0:1]
    pos = seg_start + rank
    pos1 = jnp.sum(jnp.where(m1, pos, 0.0), axis=0, keepdims=True)
    pos2 = jnp.sum(jnp.where(m2, pos, 0.0), axis=0, keepdims=True)

    out = jnp.zeros((SUBLANES, t), F32)
    for k, val in enumerate((pos1, pos2, w1, w2)):
        out = jnp.where(sub == k, val, out)
    route_ref[...] = out
    cnt_ref[0] = jnp.broadcast_to(cnt, (N_EXPERTS, LANES))


def _route(x, norm_g, mod, w_group, b_group, w_expert, b_expert):
    t = MOE_TILE
    tiles_per_seq = SEQ // t
    wr = jnp.zeros((_ROUTE_ROWS, D_MODEL), F32)
    wr = wr.at[0:N_GROUPS].set(w_group.T.astype(F32))
    wr = wr.at[_EXPERT_ROW0:_EXPERT_ROW0 + N_EXPERTS].set(w_expert.T.astype(F32))
    wr_hi = wr.astype(BF16)
    wr_lo = (wr - wr_hi.astype(F32)).astype(BF16)
    br = jnp.zeros((_ROUTE_ROWS,), F32)
    br = br.at[0:N_GROUPS].set(b_group.astype(F32))
    br = br.at[_EXPERT_ROW0:_EXPERT_ROW0 + N_EXPERTS].set(b_expert.astype(F32))
    full = lambda shape: pl.BlockSpec(shape, lambda i: (0, 0))
    return pl.pallas_call(
        _route_kernel,
        out_shape=(
            jax.ShapeDtypeStruct((N_TOK, D_MODEL), BF16),
            jax.ShapeDtypeStruct((SUBLANES, N_TOK), F32),
            jax.ShapeDtypeStruct((N_MOE_TILES, N_EXPERTS, LANES), F32),
        ),
        grid=(N_MOE_TILES,),
        in_specs=[
            pl.BlockSpec((t, D_MODEL), lambda i: (i, 0)),
            full((1, D_MODEL)),
            pl.BlockSpec((1, 1, 3 * D_MODEL), lambda i: (i // tiles_per_seq, 0, 0)),
            full((_ROUTE_ROWS, D_MODEL)),
            full((_ROUTE_ROWS, D_MODEL)),
            full((_ROUTE_ROWS, 1)),
        ],
        out_specs=(
            pl.BlockSpec((t, D_MODEL), lambda i: (i, 0)),
            pl.BlockSpec((SUBLANES, t), lambda i: (0, i)),
            pl.BlockSpec((1, N_EXPERTS, LANES), lambda i: (i, 0, 0)),
        ),
        compiler_params=_cparams(("arbitrary",)),
        name="moe_route",
    )(x, norm_g.reshape(1, D_MODEL), mod, wr_hi, wr_lo, br.reshape(_ROUTE_ROWS, 1))


def _dispatch_plan(cnt):
    cnt = cnt.astype(I32)
    cnt_pad = (cnt + SUBLANES - 1) // SUBLANES * SUBLANES
    seg = jnp.cumsum(cnt_pad, axis=1) - cnt_pad
    tot = jnp.sum(cnt_pad, axis=0)
    ptot = (tot + EXPERT_BLOCK - 1) // EXPERT_BLOCK * EXPERT_BLOCK
    pend = jnp.cumsum(ptot)
    gbase = pend - ptot
    dst = gbase[None, :] + jnp.cumsum(cnt_pad, axis=0) - cnt_pad
    nch = cnt_pad // SUBLANES
    n_used = (pend[-1] // EXPERT_BLOCK).astype(I32)
    blk = jnp.arange(N_EXPERT_BLOCKS, dtype=I32)
    blk_start = jnp.minimum(blk, n_used - 1) * EXPERT_BLOCK
    blk_e = jnp.sum((pend[None, :] <= blk_start[:, None]).astype(I32), axis=1)
    blk_e = jnp.minimum(blk_e, N_EXPERTS - 1).astype(I32)
    gap_dst = gbase + tot
    gap_nch = (ptot - tot) // SUBLANES
    misc = jnp.stack([n_used, jnp.sum(gap_nch)]).astype(I32)
    blk_first = jnp.concatenate([jnp.ones((1,), I32),
                                 (blk_e[1:] != blk_e[:-1]).astype(I32)])
    end_blk = pend[blk_e] // EXPERT_BLOCK
    blk_next = jnp.where(end_blk < n_used,
                         blk_e[jnp.minimum(end_blk, N_EXPERT_BLOCKS - 1)], -1)
    return dict(seg=seg.reshape(-1).astype(I32), dst=dst.reshape(-1).astype(I32),
                nch=nch.reshape(-1).astype(I32),
                ntot=jnp.sum(nch, axis=1).astype(I32), blk_e=blk_e,
                blk_first=blk_first, blk_next=blk_next.astype(I32),
                gap_dst=gap_dst.astype(I32), gap_nch=gap_nch.astype(I32), misc=misc)


def _segment_copies(tile, seg_ref, dst_ref, nch_ref, make_copy):
    def per_expert(e, total):
        idx = tile * N_EXPERTS + e
        n = nch_ref[idx]
        s0 = seg_ref[idx]
        d0 = dst_ref[idx]

        def per_chunk(cidx, carry):
            s = pl.multiple_of(s0 + cidx * SUBLANES, SUBLANES)
            d = pl.multiple_of(d0 + cidx * SUBLANES, SUBLANES)
            make_copy(s, d, SUBLANES).start()
            return carry

        lax.fori_loop(0, n, per_chunk, 0)
        return total + n

    return lax.fori_loop(0, N_EXPERTS, per_expert, jnp.int32(0))


_WAIT_GROUP = 16


def _wait_each(n, make_wait):
    def body(_, carry):
        make_wait().wait()
        return carry
    lax.fori_loop(0, n, body, 0)


def _wait_copies(n, make_copy):
    _wait_each(n // _WAIT_GROUP, lambda: make_copy(0, 0, _WAIT_GROUP * SUBLANES))
    _wait_each(n % _WAIT_GROUP, lambda: make_copy(0, 0, SUBLANES))


_HALF = D_MODEL // 2
_HI_MASK = -65536


def _pack_pairs(x):
    lo = lax.shift_right_logical(lax.bitcast_convert_type(x[:, 0:_HALF], I32), 16)
    hi = lax.bitcast_convert_type(x[:, _HALF:D_MODEL], I32) & _HI_MASK
    return lo | hi


def _unpack_pairs(w):
    lo = lax.bitcast_convert_type(lax.shift_left(w, 16), F32).astype(BF16)
    hi = lax.bitcast_convert_type(w & _HI_MASK, F32).astype(BF16)
    return lo, hi


def _dispatch_kernel(seg_ref, dst_ref, nch_ref, ntot_ref, gap_dst_ref, gap_nch_ref,
                     misc_ref, route_ref, h_ref, xs_ref, buf_ref, zero_ref, sem, zsem):
    t = MOE_TILE
    i = pl.program_id(0)
    slot = i % 2
    n_used = misc_ref[0]

    def zero_small(d):
        return pltpu.make_async_copy(zero_ref.at[pl.ds(0, SUBLANES)],
                                     xs_ref.at[pl.ds(d, SUBLANES)], zsem)

    def zero_block(b):
        d = pl.multiple_of(b * EXPERT_BLOCK, EXPERT_BLOCK)
        return pltpu.make_async_copy(zero_ref, xs_ref.at[pl.ds(d, EXPERT_BLOCK)], zsem)

    @pl.when(i == 0)
    def _():
        zero_ref[...] = jnp.zeros_like(zero_ref)

        def per_expert(e, carry):
            d0 = gap_dst_ref[e]

            def per_chunk(cidx, c2):
                zero_small(pl.multiple_of(d0 + cidx * SUBLANES, SUBLANES)).start()
                return c2

            lax.fori_loop(0, gap_nch_ref[e], per_chunk, 0)
            return carry

        lax.fori_loop(0, N_EXPERTS, per_expert, 0)

        def per_block(b, carry):
            zero_block(b).start()
            return carry

        lax.fori_loop(n_used, N_EXPERT_BLOCKS, per_block, 0)

    route = route_ref[...]
    pos1 = route[0:1, :].astype(I32)
    pos2 = route[1:2, :].astype(I32)
    r = lax.broadcasted_iota(I32, (SORT_ROWS, t), 0)
    perm = jnp.where((r == pos1) | (r == pos2), 1.0, 0.0).astype(BF16)
    buf_ref[slot] = _pack_pairs(_dot(perm, h_ref[...]))

    def make_copy(which):
        def mk(s, d, rows):
            return pltpu.make_async_copy(buf_ref.at[which, pl.ds(s, rows)],
                                         xs_ref.at[pl.ds(d, rows)], sem.at[which])
        return mk

    _segment_copies(i, seg_ref, dst_ref, nch_ref, make_copy(slot))

    @pl.when(i > 0)
    def _():
        _wait_copies(ntot_ref[i - 1], make_copy(1 - slot))

    @pl.when(i == N_MOE_TILES - 1)
    def _():
        _wait_copies(ntot_ref[i], make_copy(slot))
        _wait_each(misc_ref[1], lambda: zero_small(0))
        _wait_each(N_EXPERT_BLOCKS - n_used, lambda: zero_block(0))


def _dispatch(plan, route, h):
    t = MOE_TILE
    return pl.pallas_call(
        _dispatch_kernel,
        out_shape=jax.ShapeDtypeStruct((SORTED_ROWS, _HALF), I32),
        grid_spec=pltpu.PrefetchScalarGridSpec(
            num_scalar_prefetch=7,
            grid=(N_MOE_TILES,),
            in_specs=[
                pl.BlockSpec((SUBLANES, t), lambda i, *_: (0, i)),
                pl.BlockSpec((t, D_MODEL), lambda i, *_: (i, 0)),
            ],
            out_specs=pl.BlockSpec(memory_space=pl.ANY),
            scratch_shapes=[
                pltpu.VMEM((2, SORT_ROWS, _HALF), I32),
                pltpu.VMEM((EXPERT_BLOCK, _HALF), I32),
                pltpu.SemaphoreType.DMA((2,)),
                pltpu.SemaphoreType.DMA,
            ],
        ),
        compiler_params=_cparams(("arbitrary",)),
        name="moe_dispatch",
    )(plan["seg"], plan["dst"], plan["nch"], plan["ntot"], plan["gap_dst"],
      plan["gap_nch"], plan["misc"], route, h)


def _expert_kernel(blk_e_ref, first_ref, next_ref, misc_ref, x_ref, wg_hbm, wu_hbm,
                   wd_hbm, y_ref, wg_f, wu_f, wd_f, wg_b, wu_b, wd_b, sem, *, layer):
    i = pl.program_id(0)
    n_used = misc_ref[0]

    def weight_copies(e):
        return (pltpu.make_async_copy(wg_hbm.at[layer, e], wg_f, sem.at[0]),
                pltpu.make_async_copy(wu_hbm.at[layer, e], wu_f, sem.at[1]),
                pltpu.make_async_copy(wd_hbm.at[layer, e], wd_f, sem.at[2]))

    @pl.when(i == 0)
    def _():
        for cp in weight_copies(blk_e_ref[0]):
            cp.start()

    @pl.when(i < n_used)
    def _():
        @pl.when(first_ref[i] == 1)
        def _():
            for cp in weight_copies(blk_e_ref[i]):
                cp.wait()
            wg_b[...] = wg_f[...].astype(BF16)
            wu_b[...] = wu_f[...].astype(BF16)
            wd_b[...] = wd_f[...].astype(BF16)

            @pl.when(next_ref[i] >= 0)
            def _():
                for cp in weight_copies(next_ref[i]):
                    cp.start()

        x_lo, x_hi = _unpack_pairs(x_ref[...])
        x = jnp.concatenate([x_lo, x_hi], axis=1)
        g = _dot(x, wg_b[...])
        u = _dot(x, wu_b[...])
        a = (g * _sigmoid(g) * u).astype(BF16)
        y = _dot(a, wd_b[...])
        y_ref[...] = _pack_pairs(y.astype(BF16).astype(F32))

    @pl.when(i >= n_used)
    def _():
        y_ref[...] = jnp.zeros_like(y_ref)


def _experts(plan, xs, layer, w_gate, w_up, w_down):
    bm = EXPERT_BLOCK
    row_map = lambda i, be, fi, nx, misc: (jnp.minimum(i, misc[0] - 1), 0)
    out_map = lambda i, be, fi, nx, misc: (i, 0)
    return pl.pallas_call(
        functools.partial(_expert_kernel, layer=layer),
        out_shape=jax.ShapeDtypeStruct((SORTED_ROWS, _HALF), I32),
        grid_spec=pltpu.PrefetchScalarGridSpec(
            num_scalar_prefetch=4,
            grid=(N_EXPERT_BLOCKS,),
            in_specs=[
                pl.BlockSpec((bm, _HALF), row_map),
                pl.BlockSpec(memory_space=pl.ANY),
                pl.BlockSpec(memory_space=pl.ANY),
                pl.BlockSpec(memory_space=pl.ANY),
            ],
            out_specs=pl.BlockSpec((bm, _HALF), out_map),
            scratch_shapes=[
                pltpu.VMEM((D_MODEL, D_FF), F32),
                pltpu.VMEM((D_MODEL, D_FF), F32),
                pltpu.VMEM((D_FF, D_MODEL), F32),
                pltpu.VMEM((D_MODEL, D_FF), BF16),
                pltpu.VMEM((D_MODEL, D_FF), BF16),
                pltpu.VMEM((D_FF, D_MODEL), BF16),
                pltpu.SemaphoreType.DMA((3,)),
            ],
        ),
        compiler_params=_cparams(("arbitrary",)),
        name="moe_experts",
    )(plan["blk_e"], plan["blk_first"], plan["blk_next"], plan["misc"], xs,
      w_gate, w_up, w_down)


def _combine_kernel(seg_ref, dst_ref, nch_ref, ntot_ref, route_ref, x_ref, mod_ref,
                    fg_ref, ys_ref, out_ref, buf_ref, sem, *, final_norm):
    t = MOE_TILE
    i = pl.program_id(0)
    slot = i % 2

    def make_copy(which):
        def mk(s, d, rows):
            return pltpu.make_async_copy(ys_ref.at[pl.ds(d, rows)],
                                         buf_ref.at[which, pl.ds(s, rows)],
                                         sem.at[which])
        return mk

    @pl.when(i == 0)
    def _():
        buf_ref[...] = jnp.zeros_like(buf_ref)
        _segment_copies(i, seg_ref, dst_ref, nch_ref, make_copy(slot))

    @pl.when(i + 1 < N_MOE_TILES)
    def _():
        _segment_copies(i + 1, seg_ref, dst_ref, nch_ref, make_copy(1 - slot))

    route = route_ref[...]
    route_t = jnp.concatenate(
        [route, jnp.zeros((LANES - SUBLANES, t), F32)], axis=0).T
    pos1 = route_t[:, 0:1].astype(I32)
    pos2 = route_t[:, 1:2].astype(I32)
    w1 = route_t[:, 2:3]
    w2 = route_t[:, 3:4]
    c = lax.broadcasted_iota(I32, (t, SORT_ROWS), 1)
    unsort = (jnp.where(c == pos1, w1, 0.0) + jnp.where(c == pos2, w2, 0.0)).astype(BF16)

    _wait_copies(ntot_ref[i], make_copy(slot))
    y_lo, y_hi = _unpack_pairs(buf_ref[slot])
    moe = jnp.concatenate([_dot(unsort, y_lo), _dot(unsort, y_hi)], axis=1)
    gate = mod_ref[0][:, 2 * D_MODEL:3 * D_MODEL]
    x_new = x_ref[...] + gate * moe
    if final_norm:
        ms = jnp.mean(x_new * x_new, axis=-1, keepdims=True)
        x_new = (x_new * lax.rsqrt(ms + EPS)) * fg_ref[...]
    out_ref[...] = x_new


def _combine(plan, route, x, mod, final_g, ys, final_norm):
    t = MOE_TILE
    tiles_per_seq = SEQ // t
    return pl.pallas_call(
        functools.partial(_combine_kernel, final_norm=final_norm),
        out_shape=jax.ShapeDtypeStruct((N_TOK, D_MODEL), F32),
        grid_spec=pltpu.PrefetchScalarGridSpec(
            num_scalar_prefetch=4,
            grid=(N_MOE_TILES,),
            in_specs=[
                pl.BlockSpec((SUBLANES, t), lambda i, *_: (0, i)),
                pl.BlockSpec((t, D_MODEL), lambda i, *_: (i, 0)),
                pl.BlockSpec((1, 1, 3 * D_MODEL), lambda i, *_: (i // tiles_per_seq, 0, 0)),
                pl.BlockSpec((1, D_MODEL), lambda i, *_: (0, 0)),
                pl.BlockSpec(memory_space=pl.ANY),
            ],
            out_specs=pl.BlockSpec((t, D_MODEL), lambda i, *_: (i, 0)),
            scratch_shapes=[
                pltpu.VMEM((2, SORT_ROWS, _HALF), I32),
                pltpu.SemaphoreType.DMA((2,)),
            ],
        ),
        compiler_params=_cparams(("arbitrary",)),
        name="moe_combine",
    )(plan["seg"], plan["dst"], plan["nch"], plan["ntot"], route, x, mod,
      final_g.reshape(1, D_MODEL).astype(F32), ys)


def _moe_layer(x, norm_g, mod, final_g, w_group, b_group, w_expert, b_expert,
               layer, w_gate, w_up, w_down, final_norm):
    h, route, cnt = _route(x, norm_g, mod, w_group, b_group, w_expert, b_expert)
    plan = _dispatch_plan(cnt[:, :, 0])
    xs = _dispatch(plan, route, h)
    ys = _experts(plan, xs, layer, w_gate, w_up, w_down)
    return _combine(plan, route, x, mod, final_g, ys, final_norm)


def kernel(x, c, ada_w, ada_b, norm_g, final_g, m_w_in, m_b_gates, m_norm_g, m_w_out, s_w_in, s_conv_w, s_w_out, r_w_group, r_b_group, r_w_expert, r_b_expert, e_w_gate, e_w_up, e_w_down):
    mods = _ada_mods(c, ada_w, ada_b)
    xt = x.reshape(N_TOK, D_MODEL)
    for i in range(DEPTH):
        mod_mix = mods[2 * i].reshape(BATCH, 1, 3 * D_MODEL)
        mod_ffn = mods[2 * i + 1].reshape(BATCH, 1, 3 * D_MODEL)
        j = i // 2
        if i % 2 == 0:
            qt, k, vt, ot, gates, gatest = _mlstm_in(xt, norm_g[i, 0], mod_mix,
                                                     m_w_in[j], m_b_gates[j])
            xt = _mlstm_rec(xt, mod_mix, qt, k, vt, ot, gates, gatest,
                            m_norm_g[j], m_w_out[j])
        else:
            xt = _conv_layer(xt, norm_g[i, 0], mod_mix, s_w_in[j], s_conv_w[j],
                             s_w_out[j])
        xt = _moe_layer(xt, norm_g[i, 1], mod_ffn, final_g, r_w_group[i],
                        r_b_group[i], r_w_expert[i], r_b_expert[i], i, e_w_gate,
                        e_w_up, e_w_down, final_norm=(i == DEPTH - 1))
    return xt.reshape(BATCH, SEQ, D_MODEL)
```

```python
import functools

import jax
import jax.numpy as jnp
from jax import lax
from jax.experimental import pallas as pl
from jax.experimental.pallas import tpu as pltpu

F32 = jnp.float32
BF16 = jnp.bfloat16
I32 = jnp.int32

D_MODEL = 1024
BATCH = 8
SEQ = 2048
DEPTH = 4
N_TOK = BATCH * SEQ
M_HEADS = 4
M_DK = 128
M_DV = 256
M_QK = M_HEADS * M_DK
M_V = M_HEADS * M_DV
CONV_K = 3
N_GROUPS = 4
E_PER_GROUP = 8
N_EXPERTS = N_GROUPS * E_PER_GROUP
TOP_K = 2
D_FF = 512
EPS = 1e-6

SUBLANES = 8
LANES = 128

ROW_TILE = 512
MLSTM_CHUNK = 256
MOE_TILE = 256
EXPERT_BLOCK = 256
WORD_ROWS = (D_MODEL // 2) // LANES
SEG_ALIGN = SUBLANES // WORD_ROWS
CHUNK = 8
SORT_ROWS = -(-(MOE_TILE * TOP_K + N_EXPERTS * (SEG_ALIGN - 1)) // LANES) * LANES
N_MOE_TILES = N_TOK // MOE_TILE
_MAX_SORTED = (N_MOE_TILES * (MOE_TILE * TOP_K + N_EXPERTS * (SEG_ALIGN - 1))
               + N_EXPERTS * (EXPERT_BLOCK - 1))
N_EXPERT_BLOCKS = -(-_MAX_SORTED // EXPERT_BLOCK)
SORTED_ROWS = N_EXPERT_BLOCKS * EXPERT_BLOCK

VMEM_LIMIT = 48 * 1024 * 1024


def _cparams(sem):
    return pltpu.CompilerParams(dimension_semantics=sem,
                                vmem_limit_bytes=VMEM_LIMIT)


def _dot(a, b):
    return jnp.dot(a, b, preferred_element_type=F32)


def _dot_nt(a, b):
    return lax.dot_general(a, b, (((1,), (1,)), ((), ())),
                           preferred_element_type=F32)


def _split3(x):
    hi = x.astype(BF16)
    r1 = x - hi.astype(F32)
    mid = r1.astype(BF16)
    lo = (r1 - mid.astype(F32)).astype(BF16)
    return hi, mid, lo


def _dot_sel_left(sel, x):
    hi, mid, lo = _split3(x)
    return _dot(sel, hi) + _dot(sel, mid) + _dot(sel, lo)


def _dot_sel_right(x, sel):
    hi, mid, lo = _split3(x)
    return _dot(hi, sel) + _dot(mid, sel) + _dot(lo, sel)


def _sigmoid(x):
    return 1.0 / (1.0 + jnp.exp(-x))


def _rms_mod(x, g, mod):
    ms = jnp.mean(x * x, axis=-1, keepdims=True)
    y = (x * lax.rsqrt(ms + EPS)) * g
    return y * (1.0 + mod[:, D_MODEL:2 * D_MODEL]) + mod[:, 0:D_MODEL]


def _ada_kernel(c_ref, w_ref, b_ref, o_ref):
    c = c_ref[...]
    s = (c * _sigmoid(c)).astype(BF16)
    o_ref[0] = _dot(s, w_ref[0].astype(BF16)) + b_ref[0]


def _ada_mods(c, ada_w, ada_b):
    n_pairs = DEPTH * 2
    w = ada_w.reshape(n_pairs, D_MODEL, 3 * D_MODEL)
    b = ada_b.reshape(n_pairs, 1, 3 * D_MODEL)
    col = D_MODEL
    return pl.pallas_call(
        _ada_kernel,
        out_shape=jax.ShapeDtypeStruct((n_pairs, BATCH, 3 * D_MODEL), F32),
        grid=(n_pairs, 3 * D_MODEL // col),
        in_specs=[
            pl.BlockSpec((BATCH, D_MODEL), lambda p, j: (0, 0)),
            pl.BlockSpec((1, D_MODEL, col), lambda p, j: (p, 0, j)),
            pl.BlockSpec((1, 1, col), lambda p, j: (p, 0, j)),
        ],
        out_specs=pl.BlockSpec((1, BATCH, col), lambda p, j: (p, 0, j)),
        compiler_params=_cparams(("arbitrary", "arbitrary")),
        name="ada_mods",
    )(c, w, b)


def _mlstm_in_kernel(x_ref, g_ref, mod_ref, wqt_ref, wk_ref, wvt_ref, wot_ref,
                     wg_ref, bg_ref, qt_ref, k_ref, vt_ref, ot_ref, gates_ref,
                     gatest_ref):
    h = _rms_mod(x_ref[...], g_ref[...], mod_ref[0])
    hb = h.astype(BF16)
    qt_ref[...] = (_dot_nt(wqt_ref[...], hb) * (M_DK ** -0.5)).astype(BF16)
    k_ref[...] = _dot(hb, wk_ref[...]).astype(BF16)
    vt_ref[...] = _dot_nt(wvt_ref[...], hb).astype(BF16)
    ot_ref[...] = _dot_nt(wot_ref[...], hb).astype(BF16)
    g = _dot(hb, wg_ref[...]) + bg_ref[...]
    log_sig = jnp.minimum(g, 0.0) - jnp.log(1.0 + jnp.exp(-jnp.abs(g)))
    lane = lax.broadcasted_iota(I32, g.shape, 1)
    gg = jnp.where(lane < M_HEADS, g, log_sig)
    gates_ref[...] = gg
    gatest_ref[...] = gg.T[0:SUBLANES, :]


def _mlstm_in(x, norm_g, mod, w_in, b_gates):
    t = ROW_TILE
    tiles_per_seq = SEQ // t
    wqt = w_in[:, 0:M_QK].T.astype(BF16)
    wk = w_in[:, M_QK:2 * M_QK].astype(BF16)
    wvt = w_in[:, 2 * M_QK:2 * M_QK + M_V].T.astype(BF16)
    wot = w_in[:, 2 * M_QK + M_V:2 * M_QK + 2 * M_V].T.astype(BF16)
    n_gate = 2 * M_HEADS
    wg = jnp.pad(w_in[:, 2 * M_QK + 2 * M_V:], ((0, 0), (0, LANES - n_gate))).astype(BF16)
    bg = jnp.pad(b_gates.astype(F32), (0, LANES - n_gate)).reshape(1, LANES)
    full = lambda shape: pl.BlockSpec(shape, lambda i: (0, 0))
    return pl.pallas_call(
        _mlstm_in_kernel,
        out_shape=(
            jax.ShapeDtypeStruct((M_QK, N_TOK), BF16),
            jax.ShapeDtypeStruct((N_TOK, M_QK), BF16),
            jax.ShapeDtypeStruct((M_V, N_TOK), BF16),
            jax.ShapeDtypeStruct((M_V, N_TOK), BF16),
            jax.ShapeDtypeStruct((N_TOK, LANES), F32),
            jax.ShapeDtypeStruct((SUBLANES, N_TOK), F32),
        ),
        grid=(N_TOK // t,),
        in_specs=[
            pl.BlockSpec((t, D_MODEL), lambda i: (i, 0)),
            full((1, D_MODEL)),
            pl.BlockSpec((1, 1, 3 * D_MODEL), lambda i: (i // tiles_per_seq, 0, 0)),
            full((M_QK, D_MODEL)),
            full((D_MODEL, M_QK)),
            full((M_V, D_MODEL)),
            full((M_V, D_MODEL)),
            full((D_MODEL, LANES)),
            full((1, LANES)),
        ],
        out_specs=(
            pl.BlockSpec((M_QK, t), lambda i: (0, i)),
            pl.BlockSpec((t, M_QK), lambda i: (i, 0)),
            pl.BlockSpec((M_V, t), lambda i: (0, i)),
            pl.BlockSpec((M_V, t), lambda i: (0, i)),
            pl.BlockSpec((t, LANES), lambda i: (i, 0)),
            pl.BlockSpec((SUBLANES, t), lambda i: (0, i)),
        ),
        compiler_params=_cparams(("arbitrary",)),
        name="mlstm_in",
    )(x, norm_g.reshape(1, D_MODEL), mod, wqt, wk, wvt, wot, wg, bg)


def _mlstm_rec_kernel(qt_ref, k_ref, vt_ref, ot_ref, gates_ref, gatest_ref,
                      x_ref, mod_ref, ng_ref, wout_ref, out_ref, ct_ref, m_ref):
    L = MLSTM_CHUNK

    @pl.when(pl.program_id(1) == 0)
    def _():
        ct_ref[...] = jnp.zeros_like(ct_ref)
        m_ref[...] = jnp.zeros_like(m_ref)

    gates = gates_ref[...]
    gatest = gatest_ref[...]
    row = lax.broadcasted_iota(I32, (L, L), 0)
    col = lax.broadcasted_iota(I32, (L, L), 1)
    tri_low = jnp.where(row >= col, 1.0, 0.0).astype(BF16)
    tri_up = jnp.where(row <= col, 1.0, 0.0).astype(BF16)
    cum_cols = _dot_sel_left(tri_low, gates)
    cum_rows = _dot_sel_right(gatest, tri_up)

    col_term = gates - pltpu.roll(cum_cols, LANES - M_HEADS, axis=1)
    col_pieces = jnp.concatenate(_split3(col_term), axis=1)
    sel_row = lax.broadcasted_iota(I32, (3 * LANES, L), 0)
    ng_wide = jnp.concatenate([ng_ref[...]] * (L // LANES), axis=1)

    hs = []
    for h in range(M_HEADS):
        qt = qt_ref[h * M_DK:(h + 1) * M_DK, :]
        kh = k_ref[:, h * M_DK:(h + 1) * M_DK]
        vt = vt_ref[h * M_DV:(h + 1) * M_DV, :]
        state = ct_ref[h]
        ig_row = gatest[h:h + 1, :]
        bcum_row = cum_rows[M_HEADS + h:M_HEADS + h + 1, :]
        m_prev = m_ref[h:h + 1, 0:1]

        pick = jnp.where((sel_row == h) | (sel_row == LANES + h)
                         | (sel_row == 2 * LANES + h), 1.0, 0.0).astype(BF16)
        dlog = jnp.where(row <= col, _dot(col_pieces, pick) + bcum_row, -jnp.inf)
        inter_log = bcum_row + m_prev
        m_t = jnp.maximum(inter_log, jnp.max(dlog, axis=0, keepdims=True))
        w_intra = jnp.exp(dlog - m_t)
        w_inter = jnp.exp(inter_log - m_t)
        scores = _dot(kh, qt) * w_intra
        q_state = _dot(state.astype(BF16), qt)
        num = _dot(vt, scores.astype(BF16)) + w_inter * q_state[0:M_DV, :]
        den = (jnp.sum(scores, axis=0, keepdims=True)
               + w_inter * q_state[M_DV:M_DV + 1, :])
        h_out = num / jnp.maximum(jnp.abs(den), jnp.exp(-m_t))

        b_last = bcum_row[:, L - 1:L]
        log_src = b_last - bcum_row + ig_row
        m_new = jnp.maximum(b_last + m_prev,
                            jnp.max(log_src, axis=1, keepdims=True))
        w_src = jnp.exp(log_src - m_new)
        decay = jnp.exp(b_last + m_prev - m_new)
        vt_w = jnp.concatenate(
            [(vt.astype(F32) * w_src).astype(BF16),
             jnp.broadcast_to(w_src, (SUBLANES, L)).astype(BF16)], axis=0)
        ct_ref[h] = decay * state + _dot(vt_w, kh)
        m_ref[h:h + 1, :] = jnp.broadcast_to(m_new, (1, LANES))

        hn = h_out * lax.rsqrt(jnp.mean(h_out * h_out, axis=0, keepdims=True) + EPS)
        og = ot_ref[h * M_DV:(h + 1) * M_DV, :].astype(F32)
        hs.append((hn * ng_wide[h * M_DV:(h + 1) * M_DV, :] * _sigmoid(og)).astype(BF16))

    hs_t = jnp.concatenate(hs, axis=0)
    y = lax.dot_general(hs_t, wout_ref[...], (((0,), (0,)), ((), ())),
                        preferred_element_type=F32)
    gate = mod_ref[0][:, 2 * D_MODEL:3 * D_MODEL]
    out_ref[...] = x_ref[...] + gate * y


def _mlstm_rec(x, mod, qt, k, vt, ot, gates, gatest, m_norm_g, w_out):
    L = MLSTM_CHUNK
    nc = SEQ // L
    rows = lambda width: pl.BlockSpec((L, width), lambda b, j: (b * nc + j, 0))
    cols = lambda height: pl.BlockSpec((height, L), lambda b, j: (0, b * nc + j))
    ng = jnp.broadcast_to(m_norm_g.astype(F32).reshape(M_V, 1), (M_V, LANES))
    return pl.pallas_call(
        _mlstm_rec_kernel,
        out_shape=jax.ShapeDtypeStruct((N_TOK, D_MODEL), F32),
        grid=(BATCH, nc),
        in_specs=[
            cols(M_QK), rows(M_QK), cols(M_V), cols(M_V), rows(LANES),
            cols(SUBLANES), rows(D_MODEL),
            pl.BlockSpec((1, 1, 3 * D_MODEL), lambda b, j: (b, 0, 0)),
            pl.BlockSpec((M_V, LANES), lambda b, j: (0, 0)),
            pl.BlockSpec((M_V, D_MODEL), lambda b, j: (0, 0)),
        ],
        out_specs=rows(D_MODEL),
        scratch_shapes=[
            pltpu.VMEM((M_HEADS, M_DV + SUBLANES, M_DK), F32),
            pltpu.VMEM((SUBLANES, LANES), F32),
        ],
        compiler_params=_cparams(("arbitrary", "arbitrary")),
        name="mlstm_rec",
    )(qt, k, vt, ot, gates, gatest, x, mod, ng, w_out.astype(BF16))


_CONV_COLS = 256


def _conv_kernel(x_ref, g_ref, mod_ref, win_ref, cw_ref, wout_ref, out_ref,
                 carry_ref, z_ref):
    t = ROW_TILE
    tiles_per_seq = SEQ // t

    @pl.when(pl.program_id(0) % tiles_per_seq == 0)
    def _():
        carry_ref[...] = jnp.zeros_like(carry_ref)

    x = x_ref[...]
    mod = mod_ref[0]
    hb = _rms_mod(x, g_ref[...], mod).astype(BF16)
    row = lax.broadcasted_iota(I32, (t, _CONV_COLS), 0)
    for j in range(D_MODEL // _CONV_COLS):
        lo, hi = j * _CONV_COLS, (j + 1) * _CONV_COLS
        b_gate = _dot(hb, win_ref[:, lo:hi])
        c_gate = _dot(hb, win_ref[:, D_MODEL + lo:D_MODEL + hi])
        xb = _dot(hb, win_ref[:, 2 * D_MODEL + lo:2 * D_MODEL + hi])
        u = c_gate * xb
        prev1 = carry_ref[SUBLANES - 1:SUBLANES, lo:hi]
        prev2 = carry_ref[SUBLANES - 2:SUBLANES - 1, lo:hi]
        u1 = jnp.where(row == 0, prev1, pltpu.roll(u, 1, axis=0))
        u2 = jnp.where(row == 0, prev2,
                       jnp.where(row == 1, prev1, pltpu.roll(u, 2, axis=0)))
        y = (cw_ref[0:1, lo:hi] * u2 + cw_ref[1:2, lo:hi] * u1
             + cw_ref[2:3, lo:hi] * u)
        z_ref[:, lo:hi] = (b_gate * y).astype(BF16)
        carry_ref[:, lo:hi] = u[t - SUBLANES:t, :]
    gate = mod[:, 2 * D_MODEL:3 * D_MODEL]
    out_ref[...] = x + gate * _dot(z_ref[...], wout_ref[...])


def _conv_layer(x, norm_g, mod, w_in, conv_w, w_out):
    t = ROW_TILE
    tiles_per_seq = SEQ // t
    full = lambda shape: pl.BlockSpec(shape, lambda i: (0, 0))
    cw = jnp.pad(conv_w.astype(F32), ((0, SUBLANES - CONV_K), (0, 0)))
    return pl.pallas_call(
        _conv_kernel,
        out_shape=jax.ShapeDtypeStruct((N_TOK, D_MODEL), F32),
        grid=(N_TOK // t,),
        in_specs=[
            pl.BlockSpec((t, D_MODEL), lambda i: (i, 0)),
            full((1, D_MODEL)),
            pl.BlockSpec((1, 1, 3 * D_MODEL), lambda i: (i // tiles_per_seq, 0, 0)),
            full((D_MODEL, 3 * D_MODEL)),
            full((SUBLANES, D_MODEL)),
            full((D_MODEL, D_MODEL)),
        ],
        out_specs=pl.BlockSpec((t, D_MODEL), lambda i: (i, 0)),
        scratch_shapes=[
            pltpu.VMEM((SUBLANES, D_MODEL), F32),
            pltpu.VMEM((t, D_MODEL), BF16),
        ],
        compiler_params=_cparams(("arbitrary",)),
        name="conv_layer",
    )(x, norm_g.reshape(1, D_MODEL), mod, w_in.astype(BF16), cw, w_out.astype(BF16))


_ROUTE_ROWS = LANES
_EXPERT_ROW0 = SUBLANES


_ROUTE_TILES = 1


def _route_kernel(x_ref, g_ref, mod_ref, wr_hi_ref, wr_lo_ref, br_ref,
                  h_ref, route_ref, cnt_ref):
    t = MOE_TILE
    for j in range(_ROUTE_TILES):
        h, route, cnt = _route_tile(x_ref[j * t:(j + 1) * t, :], g_ref[...], mod_ref[0],
                                    wr_hi_ref[...], wr_lo_ref[...], br_ref[...])
        h_ref[j * t:(j + 1) * t, :] = h
        route_ref[:, j * t:(j + 1) * t] = route
        cnt_ref[j] = cnt


def _route_tile(x, g, mod, wr_hi, wr_lo, br):
    t = MOE_TILE
    h = _rms_mod(x, g, mod)
    h_hi = h.astype(BF16)
    h_lo = (h - h_hi.astype(F32)).astype(BF16)
    logits = (_dot_nt(wr_hi, h_hi) + _dot_nt(wr_hi, h_lo)
              + _dot_nt(wr_lo, h_hi)) + br

    sub = lax.broadcasted_iota(I32, (SUBLANES, t), 0)
    neg_inf = -jnp.inf
    gl = jnp.where(sub < N_GROUPS, logits[0:SUBLANES, :], neg_inf)
    gmax = jnp.max(gl, axis=0, keepdims=True)
    g_sel = jnp.min(jnp.where(gl == gmax, sub, SUBLANES), axis=0, keepdims=True)
    p_sel = 1.0 / jnp.sum(jnp.exp(gl - gmax), axis=0, keepdims=True)

    e_sel = jnp.zeros((E_PER_GROUP, t), F32)
    for g in range(N_GROUPS):
        r0 = _EXPERT_ROW0 + g * E_PER_GROUP
        e_sel = jnp.where(g_sel == g, logits[r0:r0 + E_PER_GROUP, :], e_sel)
    v1 = jnp.max(e_sel, axis=0, keepdims=True)
    i1 = jnp.min(jnp.where(e_sel == v1, sub, SUBLANES), axis=0, keepdims=True)
    e_rest = jnp.where(sub == i1, neg_inf, e_sel)
    v2 = jnp.max(e_rest, axis=0, keepdims=True)
    i2 = jnp.min(jnp.where(e_rest == v2, sub, SUBLANES), axis=0, keepdims=True)
    ratio = jnp.exp(v2 - v1)
    w1 = p_sel / (1.0 + ratio)
    w2 = p_sel * ratio / (1.0 + ratio)
    eid1 = g_sel * E_PER_GROUP + i1
    eid2 = g_sel * E_PER_GROUP + i2

    erow = lax.broadcasted_iota(I32, (N_EXPERTS, t), 0)
    m1 = erow == eid1
    m2 = erow == eid2
    member = jnp.where(m1 | m2, 1.0, 0.0)
    r = lax.broadcasted_iota(I32, (t, t), 0)
    c = lax.broadcasted_iota(I32, (t, t), 1)
    earlier = jnp.where(r < c, 1.0, 0.0).astype(BF16)
    member_b = member.astype(BF16)
    rank = _dot(member_b, earlier)
    cnt = _dot(member_b, jnp.ones((t, LANES), BF16))
    cnt_pad = jnp.floor((cnt + (SEG_ALIGN - 1.0)) * (1.0 / SEG_ALIGN)) * SEG_ALIGN
    er = lax.broadcasted_iota(I32, (N_EXPERTS, N_EXPERTS), 0)
    ec = lax.broadcasted_iota(I32, (N_EXPERTS, N_EXPERTS), 1)
    before = jnp.where(er > ec, 1.0, 0.0).astype(BF16)
    seg_start = _dot(before, cnt_pad.astype(BF16))[:, 0:1]
    pos = seg_start + rank
    pos1 = jnp.sum(jnp.where(m1, pos, 0.0), axis=0, keepdims=True)
    pos2 = jnp.sum(jnp.where(m2, pos, 0.0), axis=0, keepdims=True)

    out = jnp.zeros((SUBLANES, t), F32)
    for k, val in enumerate((pos1, pos2, w1, w2)):
        out = jnp.where(sub == k, val, out)
    return h_hi, out, cnt


def _route(x, norm_g, mod, w_group, b_group, w_expert, b_expert):
    t = MOE_TILE * _ROUTE_TILES
    tiles_per_seq = SEQ // t
    wr = jnp.zeros((_ROUTE_ROWS, D_MODEL), F32)
    wr = wr.at[0:N_GROUPS].set(w_group.T.astype(F32))
    wr = wr.at[_EXPERT_ROW0:_EXPERT_ROW0 + N_EXPERTS].set(w_expert.T.astype(F32))
    wr_hi = wr.astype(BF16)
    wr_lo = (wr - wr_hi.astype(F32)).astype(BF16)
    br = jnp.zeros((_ROUTE_ROWS,), F32)
    br = br.at[0:N_GROUPS].set(b_group.astype(F32))
    br = br.at[_EXPERT_ROW0:_EXPERT_ROW0 + N_EXPERTS].set(b_expert.astype(F32))
    full = lambda shape: pl.BlockSpec(shape, lambda i: (0, 0))
    return pl.pallas_call(
        _route_kernel,
        out_shape=(
            jax.ShapeDtypeStruct((N_TOK, D_MODEL), BF16),
            jax.ShapeDtypeStruct((SUBLANES, N_TOK), F32),
            jax.ShapeDtypeStruct((N_MOE_TILES, N_EXPERTS, LANES), F32),
        ),
        grid=(N_MOE_TILES // _ROUTE_TILES,),
        in_specs=[
            pl.BlockSpec((t, D_MODEL), lambda i: (i, 0)),
            full((1, D_MODEL)),
            pl.BlockSpec((1, 1, 3 * D_MODEL), lambda i: (i // tiles_per_seq, 0, 0)),
            full((_ROUTE_ROWS, D_MODEL)),
            full((_ROUTE_ROWS, D_MODEL)),
            full((_ROUTE_ROWS, 1)),
        ],
        out_specs=(
            pl.BlockSpec((t, D_MODEL), lambda i: (i, 0)),
            pl.BlockSpec((SUBLANES, t), lambda i: (0, i)),
            pl.BlockSpec((_ROUTE_TILES, N_EXPERTS, LANES), lambda i: (i, 0, 0)),
        ),
        compiler_params=_cparams(("arbitrary",)),
        name="moe_route",
    )(x, norm_g.reshape(1, D_MODEL), mod, wr_hi, wr_lo, br.reshape(_ROUTE_ROWS, 1))


_REM_UNITS = CHUNK // SEG_ALIGN
_REM_SHIFT = _REM_UNITS.bit_length() - 1


def _chunk_code(tokens):
    return (tokens // CHUNK) * _REM_UNITS + (tokens % CHUNK) // SEG_ALIGN


def _dispatch_plan(cnt):
    cnt = cnt.astype(I32)
    cnt_pad = (cnt + SEG_ALIGN - 1) // SEG_ALIGN * SEG_ALIGN
    seg = jnp.cumsum(cnt_pad, axis=1) - cnt_pad
    tot = jnp.sum(cnt_pad, axis=0)
    ptot = (tot + EXPERT_BLOCK - 1) // EXPERT_BLOCK * EXPERT_BLOCK
    pend = jnp.cumsum(ptot)
    gbase = pend - ptot
    dst = gbase[None, :] + jnp.cumsum(cnt_pad, axis=0) - cnt_pad
    nch = _chunk_code(cnt_pad)
    n_used = (pend[-1] // EXPERT_BLOCK).astype(I32)
    blk = jnp.arange(N_EXPERT_BLOCKS, dtype=I32)
    blk_start = jnp.minimum(blk, n_used - 1) * EXPERT_BLOCK
    blk_e = jnp.sum((pend[None, :] <= blk_start[:, None]).astype(I32), axis=1)
    blk_e = jnp.minimum(blk_e, N_EXPERTS - 1).astype(I32)
    gap_dst = gbase + tot
    gap_nch = _chunk_code(ptot - tot)
    misc = jnp.stack([n_used, jnp.sum(ptot - tot) // SEG_ALIGN]).astype(I32)
    first_blk = gbase // EXPERT_BLOCK
    ids = jnp.arange(N_EXPERTS, dtype=I32)
    later = (ids[None, :] > ids[:, None]) & (ptot[None, :] > 0)
    next_e = jnp.min(jnp.where(later, ids[None, :], N_EXPERTS), axis=1)
    next_e = jnp.where(next_e == N_EXPERTS, -1, next_e)
    return dict(seg=seg.reshape(-1).astype(I32), dst=dst.reshape(-1).astype(I32),
                nch=nch.reshape(-1).astype(I32),
                ntot=(jnp.sum(cnt_pad, axis=1) // SEG_ALIGN).astype(I32),
                nmax=(jnp.max(cnt_pad, axis=1) // CHUNK).astype(I32), blk_e=blk_e,
                first_blk=first_blk.astype(I32), next_e=next_e.astype(I32),
                gap_dst=gap_dst.astype(I32), gap_nch=gap_nch.astype(I32), misc=misc)


_INLINE_CHUNKS = 3


def _start_copy(make_copy, s0, d0, offset, tokens, priority=0):
    s = pl.multiple_of(s0 + offset, SEG_ALIGN)
    d = pl.multiple_of(d0 + offset, SEG_ALIGN)
    make_copy(s, d, tokens).start(priority=priority)


def _start_tail(make_copy, s0, d0, n_full, rem, priority=0):
    half, quarter = CHUNK // 2, CHUNK // 4
    tail = n_full * CHUNK
    has_half = (rem & 2) != 0
    pl.when(has_half)(functools.partial(_start_copy, make_copy, s0, d0, tail, half,
                                        priority))
    tail2 = tail + jnp.where(has_half, half, 0)
    pl.when((rem & 1) != 0)(functools.partial(_start_copy, make_copy, s0, d0, tail2,
                                              quarter, priority))


def _segment_copies_inline(tile, live, seg_ref, dst_ref, nch_ref, make_copy):
    for e in range(N_EXPERTS):
        idx = tile * N_EXPERTS + e
        code = jnp.where(live, nch_ref[idx], 0)
        n = lax.shift_right_logical(code, _REM_SHIFT)
        s0 = seg_ref[idx]
        d0 = dst_ref[idx]
        for cidx in range(_INLINE_CHUNKS):
            pl.when(cidx < n)(functools.partial(_start_copy, make_copy, s0, d0,
                                                cidx * CHUNK, CHUNK, (e + cidx) % 2))
        _start_tail(make_copy, s0, d0, n, code & (_REM_UNITS - 1), e % 2)


def _segment_copies_loop(tile, first, with_tail, seg_ref, dst_ref, nch_ref, make_copy):
    def per_expert(e, carry):
        idx = tile * N_EXPERTS + e
        code = nch_ref[idx]
        n = lax.shift_right_logical(code, _REM_SHIFT)
        s0 = seg_ref[idx]
        d0 = dst_ref[idx]

        def per_chunk(cidx, c2):
            _start_copy(make_copy, s0, d0, cidx * CHUNK, CHUNK)
            return c2

        lax.fori_loop(first, jnp.maximum(n, first), per_chunk, 0)
        if with_tail:
            _start_tail(make_copy, s0, d0, n, code & (_REM_UNITS - 1))
        return carry

    lax.fori_loop(0, N_EXPERTS, per_expert, 0)


_WAIT_GROUP = 64


def _wait_each(n, make_wait):
    def body(_, carry):
        make_wait().wait()
        return carry
    lax.fori_loop(0, n, body, 0)


def _wait_copies(units, make_copy):
    _wait_each(units // _WAIT_GROUP, lambda: make_copy(0, 0, _WAIT_GROUP * SEG_ALIGN))
    _wait_each(units % _WAIT_GROUP, lambda: make_copy(0, 0, SEG_ALIGN))


_HALF = D_MODEL // 2
_HI_MASK = -65536


def _pack_pairs(x):
    lo = lax.shift_right_logical(lax.bitcast_convert_type(x[:, 0:_HALF], I32), 16)
    hi = lax.bitcast_convert_type(x[:, _HALF:D_MODEL], I32) & _HI_MASK
    return lo | hi


def _unpack_pairs(w):
    lo = lax.bitcast_convert_type(lax.shift_left(w, 16), F32).astype(BF16)
    hi = lax.bitcast_convert_type(w & _HI_MASK, F32).astype(BF16)
    return lo, hi


def _token_rows(ref, tok, tokens):
    start = pl.multiple_of(tok * WORD_ROWS, SUBLANES)
    return ref.at[pl.ds(start, tokens * WORD_ROWS)]


def _store_words(ref, words):
    rows = words.shape[0]
    for q in range(WORD_ROWS):
        ref[pl.ds(q, rows, stride=WORD_ROWS), :] = words[:, q * LANES:(q + 1) * LANES]


def _load_words(ref, rows):
    return jnp.concatenate(
        [ref[pl.ds(q, rows, stride=WORD_ROWS), :] for q in range(WORD_ROWS)], axis=1)


_DISPATCH_BUFS = 3


def _dispatch_kernel(seg_ref, dst_ref, nch_ref, ntot_ref, nmax_ref, gap_dst_ref,
                     gap_nch_ref, misc_ref, route_ref, h_ref, xs_ref, buf_ref,
                     zero_ref, sem, zsem):
    t = MOE_TILE
    i = pl.program_id(0)
    slot = i % _DISPATCH_BUFS
    prev = (i + _DISPATCH_BUFS - 1) % _DISPATCH_BUFS
    prev2 = (i + _DISPATCH_BUFS - 2) % _DISPATCH_BUFS
    last = N_MOE_TILES - 1
    n_used = misc_ref[0]

    def make_copy(which):
        def mk(s, d, tokens):
            return pltpu.make_async_copy(_token_rows(buf_ref.at[which], s, tokens),
                                         _token_rows(xs_ref, d, tokens), sem.at[which])
        return mk

    def zero_copy(s, d, tokens):
        del s
        return pltpu.make_async_copy(_token_rows(zero_ref, 0, tokens),
                                     _token_rows(xs_ref, d, tokens), zsem)

    @pl.when(i == 0)
    def _():
        zero_ref[...] = jnp.zeros_like(zero_ref)

        def per_expert(e, carry):
            d0 = gap_dst_ref[e]
            code = gap_nch_ref[e]
            n = lax.shift_right_logical(code, _REM_SHIFT)

            def per_chunk(cidx, c2):
                _start_copy(zero_copy, 0, d0, cidx * CHUNK, CHUNK)
                return c2

            lax.fori_loop(0, n, per_chunk, 0)
            _start_tail(zero_copy, 0, d0, n, code & (_REM_UNITS - 1))
            return carry

        lax.fori_loop(0, N_EXPERTS, per_expert, 0)

        def per_block(b, carry):
            zero_copy(0, b * EXPERT_BLOCK, EXPERT_BLOCK).start()
            return carry

        lax.fori_loop(n_used, N_EXPERT_BLOCKS, per_block, 0)

    tile_prev = jnp.maximum(i - 1, 0)
    _segment_copies_inline(tile_prev, i > 0, seg_ref, dst_ref, nch_ref, make_copy(prev))

    route = route_ref[...]
    pos1 = route[0:1, :].astype(I32)
    pos2 = route[1:2, :].astype(I32)
    r = lax.broadcasted_iota(I32, (SORT_ROWS, t), 0)
    perm = jnp.where((r == pos1) | (r == pos2), 1.0, 0.0).astype(BF16)
    _store_words(buf_ref.at[slot], _pack_pairs(_dot(perm, h_ref[...])))

    @pl.when((i > 0) & (nmax_ref[tile_prev] > _INLINE_CHUNKS))
    def _():
        _segment_copies_loop(tile_prev, _INLINE_CHUNKS, False, seg_ref, dst_ref,
                             nch_ref, make_copy(prev))

    @pl.when(i > 1)
    def _():
        _wait_copies(ntot_ref[jnp.maximum(i - 2, 0)], make_copy(prev2))

    @pl.when(i == last)
    def _():
        _segment_copies_loop(i, 0, True, seg_ref, dst_ref, nch_ref, make_copy(slot))
        _wait_copies(ntot_ref[last - 1], make_copy(prev))
        _wait_copies(ntot_ref[last], make_copy(slot))
        _wait_copies(misc_ref[1], zero_copy)
        _wait_each(N_EXPERT_BLOCKS - n_used, lambda: zero_copy(0, 0, EXPERT_BLOCK))


def _dispatch(plan, route, h):
    t = MOE_TILE
    return pl.pallas_call(
        _dispatch_kernel,
        out_shape=jax.ShapeDtypeStruct((SORTED_ROWS * WORD_ROWS, LANES), I32),
        grid_spec=pltpu.PrefetchScalarGridSpec(
            num_scalar_prefetch=8,
            grid=(N_MOE_TILES,),
            in_specs=[
                pl.BlockSpec((SUBLANES, t), lambda i, *_: (0, i)),
                pl.BlockSpec((t, D_MODEL), lambda i, *_: (i, 0)),
            ],
            out_specs=pl.BlockSpec(memory_space=pl.ANY),
            scratch_shapes=[
                pltpu.VMEM((_DISPATCH_BUFS, SORT_ROWS * WORD_ROWS, LANES), I32),
                pltpu.VMEM((EXPERT_BLOCK * WORD_ROWS, LANES), I32),
                pltpu.SemaphoreType.DMA((_DISPATCH_BUFS,)),
                pltpu.SemaphoreType.DMA,
            ],
        ),
        compiler_params=_cparams(("arbitrary",)),
        name="moe_dispatch",
    )(plan["seg"], plan["dst"], plan["nch"], plan["ntot"], plan["nmax"],
      plan["gap_dst"], plan["gap_nch"], plan["misc"], route, h)


_WEIGHT_DMA_PRIORITY = 1


def _expert_kernel(blk_e_ref, first_ref, next_ref, misc_ref, x_ref, wg_hbm, wu_hbm,
                   wd_hbm, y_ref, wg_f, wu_f, wd_f, wg_b, wu_b, wd_b, sem, *, layer):
    i = pl.program_id(0)
    n_used = misc_ref[0]

    def weight_copies(e):
        return (pltpu.make_async_copy(wg_hbm.at[layer, e], wg_f, sem.at[0]),
                pltpu.make_async_copy(wu_hbm.at[layer, e], wu_f, sem.at[1]),
                pltpu.make_async_copy(wd_hbm.at[layer, e], wd_f, sem.at[2]))

    @pl.when(i == 0)
    def _():
        for cp in weight_copies(blk_e_ref[0]):
            cp.start(priority=_WEIGHT_DMA_PRIORITY)

    @pl.when(i < n_used)
    def _():
        e = blk_e_ref[i]

        @pl.when(i == first_ref[e])
        def _():
            for cp in weight_copies(e):
                cp.wait()
            wg_b[...] = wg_f[...].astype(BF16)
            wu_b[...] = wu_f[...].astype(BF16)
            wd_b[...] = wd_f[...].astype(BF16)

            @pl.when(next_ref[e] >= 0)
            def _():
                for cp in weight_copies(next_ref[e]):
                    cp.start(priority=_WEIGHT_DMA_PRIORITY)

        x_lo, x_hi = _unpack_pairs(_load_words(x_ref, EXPERT_BLOCK))
        x = jnp.concatenate([x_lo, x_hi], axis=1)
        g = _dot(x, wg_b[...])
        u = _dot(x, wu_b[...])
        a = (g * _sigmoid(g) * u).astype(BF16)
        y = _dot(a, wd_b[...])
        _store_words(y_ref, _pack_pairs(y.astype(BF16).astype(F32)))

    @pl.when(i >= n_used)
    def _():
        y_ref[...] = jnp.zeros_like(y_ref)


def _experts(plan, xs, layer, w_gate, w_up, w_down):
    bm = EXPERT_BLOCK
    row_map = lambda i, be, fi, nx, misc: (jnp.minimum(i, misc[0] - 1), 0)
    out_map = lambda i, be, fi, nx, misc: (i, 0)
    return pl.pallas_call(
        functools.partial(_expert_kernel, layer=layer),
        out_shape=jax.ShapeDtypeStruct((SORTED_ROWS * WORD_ROWS, LANES), I32),
        grid_spec=pltpu.PrefetchScalarGridSpec(
            num_scalar_prefetch=4,
            grid=(N_EXPERT_BLOCKS,),
            in_specs=[
                pl.BlockSpec((bm * WORD_ROWS, LANES), row_map),
                pl.BlockSpec(memory_space=pl.ANY),
                pl.BlockSpec(memory_space=pl.ANY),
                pl.BlockSpec(memory_space=pl.ANY),
            ],
            out_specs=pl.BlockSpec((bm * WORD_ROWS, LANES), out_map),
            scratch_shapes=[
                pltpu.VMEM((D_MODEL, D_FF), F32),
                pltpu.VMEM((D_MODEL, D_FF), F32),
                pltpu.VMEM((D_FF, D_MODEL), F32),
                pltpu.VMEM((D_MODEL, D_FF), BF16),
                pltpu.VMEM((D_MODEL, D_FF), BF16),
                pltpu.VMEM((D_FF, D_MODEL), BF16),
                pltpu.SemaphoreType.DMA((3,)),
            ],
        ),
        compiler_params=_cparams(("arbitrary",)),
        name="moe_experts",
    )(plan["blk_e"], plan["first_blk"], plan["next_e"], plan["misc"], xs,
      w_gate, w_up, w_down)


_COMBINE_BUFS = 3


def _combine_kernel(seg_ref, dst_ref, nch_ref, ntot_ref, nmax_ref, route_ref, x_ref,
                    mod_ref, fg_ref, ys_ref, out_ref, buf_ref, sem, *, final_norm):
    t = MOE_TILE
    i = pl.program_id(0)
    ahead = _COMBINE_BUFS - 1
    slot = i % _COMBINE_BUFS
    slot_next = (i + ahead) % _COMBINE_BUFS
    tile_next = jnp.minimum(i + ahead, N_MOE_TILES - 1)
    has_next = i + ahead < N_MOE_TILES

    def make_copy(which):
        def mk(s, d, tokens):
            return pltpu.make_async_copy(_token_rows(ys_ref, d, tokens),
                                         _token_rows(buf_ref.at[which], s, tokens),
                                         sem.at[which])
        return mk

    @pl.when(i == 0)
    def _():
        buf_ref[...] = jnp.zeros_like(buf_ref)
        for k in range(_COMBINE_BUFS - 1):
            _segment_copies_loop(k, 0, True, seg_ref, dst_ref, nch_ref, make_copy(k))

    _segment_copies_inline(tile_next, has_next, seg_ref, dst_ref, nch_ref,
                           make_copy(slot_next))

    route = route_ref[...]
    route_t = jnp.concatenate(
        [route, jnp.zeros((LANES - SUBLANES, t), F32)], axis=0).T
    pos1 = route_t[:, 0:1].astype(I32)
    pos2 = route_t[:, 1:2].astype(I32)
    w1 = route_t[:, 2:3]
    w2 = route_t[:, 3:4]
    c = lax.broadcasted_iota(I32, (t, SORT_ROWS), 1)
    unsort = (jnp.where(c == pos1, w1, 0.0) + jnp.where(c == pos2, w2, 0.0)).astype(BF16)

    @pl.when(has_next & (nmax_ref[tile_next] > _INLINE_CHUNKS))
    def _():
        _segment_copies_loop(tile_next, _INLINE_CHUNKS, False, seg_ref, dst_ref,
                             nch_ref, make_copy(slot_next))

    _wait_copies(ntot_ref[i], make_copy(slot))
    y_lo, y_hi = _unpack_pairs(_load_words(buf_ref.at[slot], SORT_ROWS))
    moe = jnp.concatenate([_dot(unsort, y_lo), _dot(unsort, y_hi)], axis=1)
    gate = mod_ref[0][:, 2 * D_MODEL:3 * D_MODEL]
    x_new = x_ref[...] + gate * moe
    if final_norm:
        ms = jnp.mean(x_new * x_new, axis=-1, keepdims=True)
        x_new = (x_new * lax.rsqrt(ms + EPS)) * fg_ref[...]
    out_ref[...] = x_new


def _combine(plan, route, x, mod, final_g, ys, final_norm):
    t = MOE_TILE
    tiles_per_seq = SEQ // t
    return pl.pallas_call(
        functools.partial(_combine_kernel, final_norm=final_norm),
        out_shape=jax.ShapeDtypeStruct((N_TOK, D_MODEL), F32),
        grid_spec=pltpu.PrefetchScalarGridSpec(
            num_scalar_prefetch=5,
            grid=(N_MOE_TILES,),
            in_specs=[
                pl.BlockSpec((SUBLANES, t), lambda i, *_: (0, i)),
                pl.BlockSpec((t, D_MODEL), lambda i, *_: (i, 0)),
                pl.BlockSpec((1, 1, 3 * D_MODEL), lambda i, *_: (i // tiles_per_seq, 0, 0)),
                pl.BlockSpec((1, D_MODEL), lambda i, *_: (0, 0)),
                pl.BlockSpec(memory_space=pl.ANY),
            ],
            out_specs=pl.BlockSpec((t, D_MODEL), lambda i, *_: (i, 0)),
            scratch_shapes=[
                pltpu.VMEM((_COMBINE_BUFS, SORT_ROWS * WORD_ROWS, LANES), I32),
                pltpu.SemaphoreType.DMA((_COMBINE_BUFS,)),
            ],
        ),
        compiler_params=_cparams(("arbitrary",)),
        name="moe_combine",
    )(plan["seg"], plan["dst"], plan["nch"], plan["ntot"], plan["nmax"], route, x,
      mod, final_g.reshape(1, D_MODEL).astype(F32), ys)


def _moe_layer(x, norm_g, mod, final_g, w_group, b_group, w_expert, b_expert,
               layer, w_gate, w_up, w_down, final_norm):
    h, route, cnt = _route(x, norm_g, mod, w_group, b_group, w_expert, b_expert)
    plan = _dispatch_plan(cnt[:, :, 0])
    xs = _dispatch(plan, route, h)
    ys = _experts(plan, xs, layer, w_gate, w_up, w_down)
    return _combine(plan, route, x, mod, final_g, ys, final_norm)


def kernel(x, c, ada_w, ada_b, norm_g, final_g, m_w_in, m_b_gates, m_norm_g, m_w_out, s_w_in, s_conv_w, s_w_out, r_w_group, r_b_group, r_w_expert, r_b_expert, e_w_gate, e_w_up, e_w_down):
    mods = _ada_mods(c, ada_w, ada_b)
    xt = x.reshape(N_TOK, D_MODEL)
    for i in range(DEPTH):
        mod_mix = mods[2 * i].reshape(BATCH, 1, 3 * D_MODEL)
        mod_ffn = mods[2 * i + 1].reshape(BATCH, 1, 3 * D_MODEL)
        j = i // 2
        if i % 2 == 0:
            qt, k, vt, ot, gates, gatest = _mlstm_in(xt, norm_g[i, 0], mod_mix,
                                                     m_w_in[j], m_b_gates[j])
            xt = _mlstm_rec(xt, mod_mix, qt, k, vt, ot, gates, gatest,
                            m_norm_g[j], m_w_out[j])
        else:
            xt = _conv_layer(xt, norm_g[i, 0], mod_mix, s_w_in[j], s_conv_w[j],
                             s_w_out[j])
        xt = _moe_layer(xt, norm_g[i, 1], mod_ffn, final_g, r_w_group[i],
                        r_b_group[i], r_w_expert[i], r_b_expert[i], i, e_w_gate,
                        e_w_up, e_w_down, final_norm=(i == DEPTH - 1))
    return xt.reshape(BATCH, SEQ, D_MODEL)
```

```python
import functools

import jax
import jax.numpy as jnp
from jax import lax
from jax.experimental import pallas as pl
from jax.experimental.pallas import tpu as pltpu

F32 = jnp.float32
BF16 = jnp.bfloat16
I32 = jnp.int32

D_MODEL = 1024
BATCH = 8
SEQ = 2048
DEPTH = 4
N_TOK = BATCH * SEQ
M_HEADS = 4
M_DK = 128
M_DV = 256
M_QK = M_HEADS * M_DK
M_V = M_HEADS * M_DV
CONV_K = 3
N_GROUPS = 4
E_PER_GROUP = 8
N_EXPERTS = N_GROUPS * E_PER_GROUP
TOP_K = 2
D_FF = 512
EPS = 1e-6

SUBLANES = 8
LANES = 128

ROW_TILE = 512
MLSTM_CHUNK = 256
MOE_TILE = 256
EXPERT_BLOCK = 256
WORD_ROWS = (D_MODEL // 2) // LANES
SEG_ALIGN = SUBLANES // WORD_ROWS
CHUNK = 8
SORT_ROWS = -(-(MOE_TILE * TOP_K + N_EXPERTS * (SEG_ALIGN - 1)) // LANES) * LANES
N_MOE_TILES = N_TOK // MOE_TILE
_MAX_SORTED = (N_MOE_TILES * (MOE_TILE * TOP_K + N_EXPERTS * (SEG_ALIGN - 1))
               + N_EXPERTS * (EXPERT_BLOCK - 1))
_BLOCKS_PER_STEP = 2
N_EXPERT_BLOCKS = (-(-_MAX_SORTED // (EXPERT_BLOCK * _BLOCKS_PER_STEP))
                   * _BLOCKS_PER_STEP)
SORTED_ROWS = N_EXPERT_BLOCKS * EXPERT_BLOCK

VMEM_LIMIT = 48 * 1024 * 1024


def _cparams(sem):
    return pltpu.CompilerParams(dimension_semantics=sem,
                                vmem_limit_bytes=VMEM_LIMIT)


def _dot(a, b):
    return jnp.dot(a, b, preferred_element_type=F32)


def _dot_nt(a, b):
    return lax.dot_general(a, b, (((1,), (1,)), ((), ())),
                           preferred_element_type=F32)


def _split3(x):
    hi = x.astype(BF16)
    r1 = x - hi.astype(F32)
    mid = r1.astype(BF16)
    lo = (r1 - mid.astype(F32)).astype(BF16)
    return hi, mid, lo


def _dot_sel_left(sel, x):
    hi, mid, lo = _split3(x)
    return _dot(sel, hi) + _dot(sel, mid) + _dot(sel, lo)


def _dot_sel_right(x, sel):
    hi, mid, lo = _split3(x)
    return _dot(hi, sel) + _dot(mid, sel) + _dot(lo, sel)


def _sigmoid(x):
    return 1.0 / (1.0 + jnp.exp(-x))


def _rms_mod(x, g, mod):
    ms = jnp.mean(x * x, axis=-1, keepdims=True)
    y = (x * lax.rsqrt(ms + EPS)) * g
    return y * (1.0 + mod[:, D_MODEL:2 * D_MODEL]) + mod[:, 0:D_MODEL]


def _ada_kernel(c_ref, w_ref, b_ref, o_ref):
    c = c_ref[...]
    s = (c * _sigmoid(c)).astype(BF16)
    o_ref[0] = _dot(s, w_ref[0].astype(BF16)) + b_ref[0]


def _ada_mods(c, ada_w, ada_b):
    n_pairs = DEPTH * 2
    w = ada_w.reshape(n_pairs, D_MODEL, 3 * D_MODEL)
    b = ada_b.reshape(n_pairs, 1, 3 * D_MODEL)
    col = D_MODEL
    return pl.pallas_call(
        _ada_kernel,
        out_shape=jax.ShapeDtypeStruct((n_pairs, BATCH, 3 * D_MODEL), F32),
        grid=(n_pairs, 3 * D_MODEL // col),
        in_specs=[
            pl.BlockSpec((BATCH, D_MODEL), lambda p, j: (0, 0)),
            pl.BlockSpec((1, D_MODEL, col), lambda p, j: (p, 0, j)),
            pl.BlockSpec((1, 1, col), lambda p, j: (p, 0, j)),
        ],
        out_specs=pl.BlockSpec((1, BATCH, col), lambda p, j: (p, 0, j)),
        compiler_params=_cparams(("arbitrary", "arbitrary")),
        name="ada_mods",
    )(c, w, b)


def _mlstm_in_kernel(x_ref, g_ref, mod_ref, wqt_ref, wk_ref, wvt_ref, wot_ref,
                     wg_ref, bg_ref, qt_ref, k_ref, vt_ref, ot_ref, gates_ref,
                     gatest_ref):
    h = _rms_mod(x_ref[...], g_ref[...], mod_ref[0])
    hb = h.astype(BF16)
    qt_ref[...] = (_dot_nt(wqt_ref[...], hb) * (M_DK ** -0.5)).astype(BF16)
    k_ref[...] = _dot(hb, wk_ref[...]).astype(BF16)
    vt_ref[...] = _dot_nt(wvt_ref[...], hb).astype(BF16)
    ot_ref[...] = _dot_nt(wot_ref[...], hb).astype(BF16)
    g = _dot(hb, wg_ref[...]) + bg_ref[...]
    log_sig = jnp.minimum(g, 0.0) - jnp.log(1.0 + jnp.exp(-jnp.abs(g)))
    lane = lax.broadcasted_iota(I32, g.shape, 1)
    gg = jnp.where(lane < M_HEADS, g, log_sig)
    gates_ref[...] = gg
    gatest_ref[...] = gg.T[0:SUBLANES, :]


def _mlstm_in(x, norm_g, mod, w_in, b_gates):
    t = ROW_TILE
    tiles_per_seq = SEQ // t
    wqt = w_in[:, 0:M_QK].T.astype(BF16)
    wk = w_in[:, M_QK:2 * M_QK].astype(BF16)
    wvt = w_in[:, 2 * M_QK:2 * M_QK + M_V].T.astype(BF16)
    wot = w_in[:, 2 * M_QK + M_V:2 * M_QK + 2 * M_V].T.astype(BF16)
    n_gate = 2 * M_HEADS
    wg = jnp.pad(w_in[:, 2 * M_QK + 2 * M_V:], ((0, 0), (0, LANES - n_gate))).astype(BF16)
    bg = jnp.pad(b_gates.astype(F32), (0, LANES - n_gate)).reshape(1, LANES)
    full = lambda shape: pl.BlockSpec(shape, lambda i: (0, 0))
    return pl.pallas_call(
        _mlstm_in_kernel,
        out_shape=(
            jax.ShapeDtypeStruct((M_QK, N_TOK), BF16),
            jax.ShapeDtypeStruct((N_TOK, M_QK), BF16),
            jax.ShapeDtypeStruct((M_V, N_TOK), BF16),
            jax.ShapeDtypeStruct((M_V, N_TOK), BF16),
            jax.ShapeDtypeStruct((N_TOK, LANES), F32),
            jax.ShapeDtypeStruct((SUBLANES, N_TOK), F32),
        ),
        grid=(N_TOK // t,),
        in_specs=[
            pl.BlockSpec((t, D_MODEL), lambda i: (i, 0)),
            full((1, D_MODEL)),
            pl.BlockSpec((1, 1, 3 * D_MODEL), lambda i: (i // tiles_per_seq, 0, 0)),
            full((M_QK, D_MODEL)),
            full((D_MODEL, M_QK)),
            full((M_V, D_MODEL)),
            full((M_V, D_MODEL)),
            full((D_MODEL, LANES)),
            full((1, LANES)),
        ],
        out_specs=(
            pl.BlockSpec((M_QK, t), lambda i: (0, i)),
            pl.BlockSpec((t, M_QK), lambda i: (i, 0)),
            pl.BlockSpec((M_V, t), lambda i: (0, i)),
            pl.BlockSpec((M_V, t), lambda i: (0, i)),
            pl.BlockSpec((t, LANES), lambda i: (i, 0)),
            pl.BlockSpec((SUBLANES, t), lambda i: (0, i)),
        ),
        compiler_params=_cparams(("arbitrary",)),
        name="mlstm_in",
    )(x, norm_g.reshape(1, D_MODEL), mod, wqt, wk, wvt, wot, wg, bg)


def _mlstm_rec_kernel(qt_ref, k_ref, vt_ref, ot_ref, gates_ref, gatest_ref,
                      x_ref, mod_ref, ng_ref, wout_ref, out_ref, ct_ref, m_ref):
    L = MLSTM_CHUNK

    @pl.when(pl.program_id(1) == 0)
    def _():
        ct_ref[...] = jnp.zeros_like(ct_ref)
        m_ref[...] = jnp.zeros_like(m_ref)

    gates = gates_ref[...]
    gatest = gatest_ref[...]
    row = lax.broadcasted_iota(I32, (L, L), 0)
    col = lax.broadcasted_iota(I32, (L, L), 1)
    tri_low = jnp.where(row >= col, 1.0, 0.0).astype(BF16)
    tri_up = jnp.where(row <= col, 1.0, 0.0).astype(BF16)
    cum_cols = _dot_sel_left(tri_low, gates)
    cum_rows = _dot_sel_right(gatest, tri_up)

    col_term = gates - pltpu.roll(cum_cols, LANES - M_HEADS, axis=1)
    col_pieces = jnp.concatenate(_split3(col_term), axis=1)
    sel_row = lax.broadcasted_iota(I32, (3 * LANES, L), 0)
    ng_wide = jnp.concatenate([ng_ref[...]] * (L // LANES), axis=1)

    hs = []
    for h in range(M_HEADS):
        qt = qt_ref[h * M_DK:(h + 1) * M_DK, :]
        kh = k_ref[:, h * M_DK:(h + 1) * M_DK]
        vt = vt_ref[h * M_DV:(h + 1) * M_DV, :]
        state = ct_ref[h]
        ig_row = gatest[h:h + 1, :]
        bcum_row = cum_rows[M_HEADS + h:M_HEADS + h + 1, :]
        m_prev = m_ref[h:h + 1, 0:1]

        pick = jnp.where((sel_row == h) | (sel_row == LANES + h)
                         | (sel_row == 2 * LANES + h), 1.0, 0.0).astype(BF16)
        dlog = jnp.where(row <= col, _dot(col_pieces, pick) + bcum_row, -jnp.inf)
        inter_log = bcum_row + m_prev
        m_t = jnp.maximum(inter_log, jnp.max(dlog, axis=0, keepdims=True))
        w_intra = jnp.exp(dlog - m_t)
        w_inter = jnp.exp(inter_log - m_t)
        scores = _dot(kh, qt) * w_intra
        q_state = _dot(state.astype(BF16), qt)
        num = _dot(vt, scores.astype(BF16)) + w_inter * q_state[0:M_DV, :]
        den = (jnp.sum(scores, axis=0, keepdims=True)
               + w_inter * q_state[M_DV:M_DV + 1, :])
        h_out = num / jnp.maximum(jnp.abs(den), jnp.exp(-m_t))

        b_last = bcum_row[:, L - 1:L]
        log_src = b_last - bcum_row + ig_row
        m_new = jnp.maximum(b_last + m_prev,
                            jnp.max(log_src, axis=1, keepdims=True))
        w_src = jnp.exp(log_src - m_new)
        decay = jnp.exp(b_last + m_prev - m_new)
        vt_w = jnp.concatenate(
            [(vt.astype(F32) * w_src).astype(BF16),
             jnp.broadcast_to(w_src, (SUBLANES, L)).astype(BF16)], axis=0)
        ct_ref[h] = decay * state + _dot(vt_w, kh)
        m_ref[h:h + 1, :] = jnp.broadcast_to(m_new, (1, LANES))

        hn = h_out * lax.rsqrt(jnp.mean(h_out * h_out, axis=0, keepdims=True) + EPS)
        og = ot_ref[h * M_DV:(h + 1) * M_DV, :].astype(F32)
        hs.append((hn * ng_wide[h * M_DV:(h + 1) * M_DV, :] * _sigmoid(og)).astype(BF16))

    hs_t = jnp.concatenate(hs, axis=0)
    y = lax.dot_general(hs_t, wout_ref[...], (((0,), (0,)), ((), ())),
                        preferred_element_type=F32)
    gate = mod_ref[0][:, 2 * D_MODEL:3 * D_MODEL]
    out_ref[...] = x_ref[...] + gate * y


def _mlstm_rec(x, mod, qt, k, vt, ot, gates, gatest, m_norm_g, w_out):
    L = MLSTM_CHUNK
    nc = SEQ // L
    rows = lambda width: pl.BlockSpec((L, width), lambda b, j: (b * nc + j, 0))
    cols = lambda height: pl.BlockSpec((height, L), lambda b, j: (0, b * nc + j))
    ng = jnp.broadcast_to(m_norm_g.astype(F32).reshape(M_V, 1), (M_V, LANES))
    return pl.pallas_call(
        _mlstm_rec_kernel,
        out_shape=jax.ShapeDtypeStruct((N_TOK, D_MODEL), F32),
        grid=(BATCH, nc),
        in_specs=[
            cols(M_QK), rows(M_QK), cols(M_V), cols(M_V), rows(LANES),
            cols(SUBLANES), rows(D_MODEL),
            pl.BlockSpec((1, 1, 3 * D_MODEL), lambda b, j: (b, 0, 0)),
            pl.BlockSpec((M_V, LANES), lambda b, j: (0, 0)),
            pl.BlockSpec((M_V, D_MODEL), lambda b, j: (0, 0)),
        ],
        out_specs=rows(D_MODEL),
        scratch_shapes=[
            pltpu.VMEM((M_HEADS, M_DV + SUBLANES, M_DK), F32),
            pltpu.VMEM((SUBLANES, LANES), F32),
        ],
        compiler_params=_cparams(("arbitrary", "arbitrary")),
        name="mlstm_rec",
    )(qt, k, vt, ot, gates, gatest, x, mod, ng, w_out.astype(BF16))


_CONV_COLS = 256


def _conv_kernel(x_ref, g_ref, mod_ref, win_ref, cw_ref, wout_ref, out_ref,
                 carry_ref, z_ref):
    t = ROW_TILE
    tiles_per_seq = SEQ // t

    @pl.when(pl.program_id(0) % tiles_per_seq == 0)
    def _():
        carry_ref[...] = jnp.zeros_like(carry_ref)

    x = x_ref[...]
    mod = mod_ref[0]
    hb = _rms_mod(x, g_ref[...], mod).astype(BF16)
    row = lax.broadcasted_iota(I32, (t, _CONV_COLS), 0)
    for j in range(D_MODEL // _CONV_COLS):
        lo, hi = j * _CONV_COLS, (j + 1) * _CONV_COLS
        b_gate = _dot(hb, win_ref[:, lo:hi])
        c_gate = _dot(hb, win_ref[:, D_MODEL + lo:D_MODEL + hi])
        xb = _dot(hb, win_ref[:, 2 * D_MODEL + lo:2 * D_MODEL + hi])
        u = c_gate * xb
        prev1 = carry_ref[SUBLANES - 1:SUBLANES, lo:hi]
        prev2 = carry_ref[SUBLANES - 2:SUBLANES - 1, lo:hi]
        u1 = jnp.where(row == 0, prev1, pltpu.roll(u, 1, axis=0))
        u2 = jnp.where(row == 0, prev2,
                       jnp.where(row == 1, prev1, pltpu.roll(u, 2, axis=0)))
        y = (cw_ref[0:1, lo:hi] * u2 + cw_ref[1:2, lo:hi] * u1
             + cw_ref[2:3, lo:hi] * u)
        z_ref[:, lo:hi] = (b_gate * y).astype(BF16)
        carry_ref[:, lo:hi] = u[t - SUBLANES:t, :]
    gate = mod[:, 2 * D_MODEL:3 * D_MODEL]
    out_ref[...] = x + gate * _dot(z_ref[...], wout_ref[...])


def _conv_layer(x, norm_g, mod, w_in, conv_w, w_out):
    t = ROW_TILE
    tiles_per_seq = SEQ // t
    full = lambda shape: pl.BlockSpec(shape, lambda i: (0, 0))
    cw = jnp.pad(conv_w.astype(F32), ((0, SUBLANES - CONV_K), (0, 0)))
    return pl.pallas_call(
        _conv_kernel,
        out_shape=jax.ShapeDtypeStruct((N_TOK, D_MODEL), F32),
        grid=(N_TOK // t,),
        in_specs=[
            pl.BlockSpec((t, D_MODEL), lambda i: (i, 0)),
            full((1, D_MODEL)),
            pl.BlockSpec((1, 1, 3 * D_MODEL), lambda i: (i // tiles_per_seq, 0, 0)),
            full((D_MODEL, 3 * D_MODEL)),
            full((SUBLANES, D_MODEL)),
            full((D_MODEL, D_MODEL)),
        ],
        out_specs=pl.BlockSpec((t, D_MODEL), lambda i: (i, 0)),
        scratch_shapes=[
            pltpu.VMEM((SUBLANES, D_MODEL), F32),
            pltpu.VMEM((t, D_MODEL), BF16),
        ],
        compiler_params=_cparams(("arbitrary",)),
        name="conv_layer",
    )(x, norm_g.reshape(1, D_MODEL), mod, w_in.astype(BF16), cw, w_out.astype(BF16))


_ROUTE_ROWS = LANES
_EXPERT_ROW0 = SUBLANES


_ROUTE_TILES = 1


def _route_kernel(x_ref, g_ref, mod_ref, wr_hi_ref, wr_lo_ref, br_ref,
                  h_ref, route_ref, cnt_ref):
    t = MOE_TILE
    for j in range(_ROUTE_TILES):
        h, route, cnt = _route_tile(x_ref[j * t:(j + 1) * t, :], g_ref[...], mod_ref[0],
                                    wr_hi_ref[...], wr_lo_ref[...], br_ref[...])
        h_ref[j * t:(j + 1) * t, :] = h
        route_ref[:, j * t:(j + 1) * t] = route
        cnt_ref[j] = cnt


def _route_tile(x, g, mod, wr_hi, wr_lo, br):
    t = MOE_TILE
    h = _rms_mod(x, g, mod)
    h_hi = h.astype(BF16)
    h_lo = (h - h_hi.astype(F32)).astype(BF16)
    logits = (_dot_nt(wr_hi, h_hi) + _dot_nt(wr_hi, h_lo)
              + _dot_nt(wr_lo, h_hi)) + br

    sub = lax.broadcasted_iota(I32, (SUBLANES, t), 0)
    neg_inf = -jnp.inf
    gl = jnp.where(sub < N_GROUPS, logits[0:SUBLANES, :], neg_inf)
    gmax = jnp.max(gl, axis=0, keepdims=True)
    g_sel = jnp.min(jnp.where(gl == gmax, sub, SUBLANES), axis=0, keepdims=True)
    p_sel = 1.0 / jnp.sum(jnp.exp(gl - gmax), axis=0, keepdims=True)

    e_sel = jnp.zeros((E_PER_GROUP, t), F32)
    for g in range(N_GROUPS):
        r0 = _EXPERT_ROW0 + g * E_PER_GROUP
        e_sel = jnp.where(g_sel == g, logits[r0:r0 + E_PER_GROUP, :], e_sel)
    v1 = jnp.max(e_sel, axis=0, keepdims=True)
    i1 = jnp.min(jnp.where(e_sel == v1, sub, SUBLANES), axis=0, keepdims=True)
    e_rest = jnp.where(sub == i1, neg_inf, e_sel)
    v2 = jnp.max(e_rest, axis=0, keepdims=True)
    i2 = jnp.min(jnp.where(e_rest == v2, sub, SUBLANES), axis=0, keepdims=True)
    ratio = jnp.exp(v2 - v1)
    w1 = p_sel / (1.0 + ratio)
    w2 = p_sel * ratio / (1.0 + ratio)
    eid1 = g_sel * E_PER_GROUP + i1
    eid2 = g_sel * E_PER_GROUP + i2

    erow = lax.broadcasted_iota(I32, (N_EXPERTS, t), 0)
    m1 = erow == eid1
    m2 = erow == eid2
    member = jnp.where(m1 | m2, 1.0, 0.0)
    r = lax.broadcasted_iota(I32, (t, t), 0)
    c = lax.broadcasted_iota(I32, (t, t), 1)
    earlier = jnp.where(r < c, 1.0, 0.0).astype(BF16)
    member_b = member.astype(BF16)
    rank = _dot(member_b, earlier)
    cnt = _dot(member_b, jnp.ones((t, LANES), BF16))
    cnt_pad = jnp.floor((cnt + (SEG_ALIGN - 1.0)) * (1.0 / SEG_ALIGN)) * SEG_ALIGN
    er = lax.broadcasted_iota(I32, (N_EXPERTS, N_EXPERTS), 0)
    ec = lax.broadcasted_iota(I32, (N_EXPERTS, N_EXPERTS), 1)
    before = jnp.where(er > ec, 1.0, 0.0).astype(BF16)
    seg_start = _dot(before, cnt_pad.astype(BF16))[:, 0:1]
    pos = seg_start + rank
    pos1 = jnp.sum(jnp.where(m1, pos, 0.0), axis=0, keepdims=True)
    pos2 = jnp.sum(jnp.where(m2, pos, 0.0), axis=0, keepdims=True)

    out = jnp.zeros((SUBLANES, t), F32)
    for k, val in enumerate((pos1, pos2, w1, w2)):
        out = jnp.where(sub == k, val, out)
    return h_hi, out, cnt


def _route(x, norm_g, mod, w_group, b_group, w_expert, b_expert):
    t = MOE_TILE * _ROUTE_TILES
    tiles_per_seq = SEQ // t
    wr = jnp.zeros((_ROUTE_ROWS, D_MODEL), F32)
    wr = wr.at[0:N_GROUPS].set(w_group.T.astype(F32))
    wr = wr.at[_EXPERT_ROW0:_EXPERT_ROW0 + N_EXPERTS].set(w_expert.T.astype(F32))
    wr_hi = wr.astype(BF16)
    wr_lo = (wr - wr_hi.astype(F32)).astype(BF16)
    br = jnp.zeros((_ROUTE_ROWS,), F32)
    br = br.at[0:N_GROUPS].set(b_group.astype(F32))
    br = br.at[_EXPERT_ROW0:_EXPERT_ROW0 + N_EXPERTS].set(b_expert.astype(F32))
    full = lambda shape: pl.BlockSpec(shape, lambda i: (0, 0))
    return pl.pallas_call(
        _route_kernel,
        out_shape=(
            jax.ShapeDtypeStruct((N_TOK, D_MODEL), BF16),
            jax.ShapeDtypeStruct((SUBLANES, N_TOK), F32),
            jax.ShapeDtypeStruct((N_MOE_TILES, N_EXPERTS, LANES), F32),
        ),
        grid=(N_MOE_TILES // _ROUTE_TILES,),
        in_specs=[
            pl.BlockSpec((t, D_MODEL), lambda i: (i, 0)),
            full((1, D_MODEL)),
            pl.BlockSpec((1, 1, 3 * D_MODEL), lambda i: (i // tiles_per_seq, 0, 0)),
            full((_ROUTE_ROWS, D_MODEL)),
            full((_ROUTE_ROWS, D_MODEL)),
            full((_ROUTE_ROWS, 1)),
        ],
        out_specs=(
            pl.BlockSpec((t, D_MODEL), lambda i: (i, 0)),
            pl.BlockSpec((SUBLANES, t), lambda i: (0, i)),
            pl.BlockSpec((_ROUTE_TILES, N_EXPERTS, LANES), lambda i: (i, 0, 0)),
        ),
        compiler_params=_cparams(("arbitrary",)),
        name="moe_route",
    )(x, norm_g.reshape(1, D_MODEL), mod, wr_hi, wr_lo, br.reshape(_ROUTE_ROWS, 1))


_REM_UNITS = CHUNK // SEG_ALIGN
_REM_SHIFT = _REM_UNITS.bit_length() - 1


def _chunk_code(tokens):
    return (tokens // CHUNK) * _REM_UNITS + (tokens % CHUNK) // SEG_ALIGN


def _dispatch_plan(cnt):
    cnt = cnt.astype(I32)
    cnt_pad = (cnt + SEG_ALIGN - 1) // SEG_ALIGN * SEG_ALIGN
    seg = jnp.cumsum(cnt_pad, axis=1) - cnt_pad
    tot = jnp.sum(cnt_pad, axis=0)
    ptot = (tot + EXPERT_BLOCK - 1) // EXPERT_BLOCK * EXPERT_BLOCK
    pend = jnp.cumsum(ptot)
    gbase = pend - ptot
    dst = gbase[None, :] + jnp.cumsum(cnt_pad, axis=0) - cnt_pad
    nch = _chunk_code(cnt_pad)
    n_used = (pend[-1] // EXPERT_BLOCK).astype(I32)
    blk = jnp.arange(N_EXPERT_BLOCKS, dtype=I32)
    blk_start = jnp.minimum(blk, n_used - 1) * EXPERT_BLOCK
    blk_e = jnp.sum((pend[None, :] <= blk_start[:, None]).astype(I32), axis=1)
    blk_e = jnp.minimum(blk_e, N_EXPERTS - 1).astype(I32)
    gap_dst = gbase + tot
    gap_nch = _chunk_code(ptot - tot)
    misc = jnp.stack([n_used, jnp.sum(ptot - tot) // SEG_ALIGN]).astype(I32)
    first_blk = gbase // EXPERT_BLOCK
    ids = jnp.arange(N_EXPERTS, dtype=I32)
    later = (ids[None, :] > ids[:, None]) & (ptot[None, :] > 0)
    next_e = jnp.min(jnp.where(later, ids[None, :], N_EXPERTS), axis=1)
    next_e = jnp.where(next_e == N_EXPERTS, -1, next_e)
    return dict(seg=seg.reshape(-1).astype(I32), dst=dst.reshape(-1).astype(I32),
                nch=nch.reshape(-1).astype(I32),
                ntot=(jnp.sum(cnt_pad, axis=1) // SEG_ALIGN).astype(I32),
                nmax=(jnp.max(cnt_pad, axis=1) // CHUNK).astype(I32), blk_e=blk_e,
                first_blk=first_blk.astype(I32), next_e=next_e.astype(I32),
                gap_dst=gap_dst.astype(I32), gap_nch=gap_nch.astype(I32), misc=misc)


_INLINE_CHUNKS = 3


def _start_copy(make_copy, s0, d0, offset, tokens, priority=0):
    s = pl.multiple_of(s0 + offset, SEG_ALIGN)
    d = pl.multiple_of(d0 + offset, SEG_ALIGN)
    make_copy(s, d, tokens).start(priority=priority)


def _start_tail(make_copy, s0, d0, n_full, rem, priority=0):
    half, quarter = CHUNK // 2, CHUNK // 4
    tail = n_full * CHUNK
    has_half = (rem & 2) != 0
    pl.when(has_half)(functools.partial(_start_copy, make_copy, s0, d0, tail, half,
                                        priority))
    tail2 = tail + jnp.where(has_half, half, 0)
    pl.when((rem & 1) != 0)(functools.partial(_start_copy, make_copy, s0, d0, tail2,
                                              quarter, priority))


def _segment_copies_inline(tile, live, seg_ref, dst_ref, nch_ref, make_copy):
    for e in range(N_EXPERTS):
        idx = tile * N_EXPERTS + e
        code = jnp.where(live, nch_ref[idx], 0)
        n = lax.shift_right_logical(code, _REM_SHIFT)
        s0 = seg_ref[idx]
        d0 = dst_ref[idx]
        for cidx in range(_INLINE_CHUNKS):
            pl.when(cidx < n)(functools.partial(_start_copy, make_copy, s0, d0,
                                                cidx * CHUNK, CHUNK, (e + cidx) % 2))
        _start_tail(make_copy, s0, d0, n, code & (_REM_UNITS - 1), e % 2)


def _segment_copies_loop(tile, first, with_tail, seg_ref, dst_ref, nch_ref, make_copy):
    def per_expert(e, carry):
        idx = tile * N_EXPERTS + e
        code = nch_ref[idx]
        n = lax.shift_right_logical(code, _REM_SHIFT)
        s0 = seg_ref[idx]
        d0 = dst_ref[idx]

        def per_chunk(cidx, c2):
            _start_copy(make_copy, s0, d0, cidx * CHUNK, CHUNK)
            return c2

        lax.fori_loop(first, jnp.maximum(n, first), per_chunk, 0)
        if with_tail:
            _start_tail(make_copy, s0, d0, n, code & (_REM_UNITS - 1))
        return carry

    lax.fori_loop(0, N_EXPERTS, per_expert, 0)


_WAIT_GROUP = 64


def _wait_each(n, make_wait):
    def body(_, carry):
        make_wait().wait()
        return carry
    lax.fori_loop(0, n, body, 0)


def _wait_copies(units, make_copy):
    _wait_each(units // _WAIT_GROUP, lambda: make_copy(0, 0, _WAIT_GROUP * SEG_ALIGN))
    _wait_each(units % _WAIT_GROUP, lambda: make_copy(0, 0, SEG_ALIGN))


_HALF = D_MODEL // 2
_HI_MASK = -65536


def _pack_pairs(x):
    lo = lax.shift_right_logical(lax.bitcast_convert_type(x[:, 0:_HALF], I32), 16)
    hi = lax.bitcast_convert_type(x[:, _HALF:D_MODEL], I32) & _HI_MASK
    return lo | hi


def _unpack_pairs(w):
    lo = lax.bitcast_convert_type(lax.shift_left(w, 16), F32).astype(BF16)
    hi = lax.bitcast_convert_type(w & _HI_MASK, F32).astype(BF16)
    return lo, hi


def _token_rows(ref, tok, tokens):
    start = pl.multiple_of(tok * WORD_ROWS, SUBLANES)
    return ref.at[pl.ds(start, tokens * WORD_ROWS)]


def _store_words(ref, words, first=0):
    rows = words.shape[0]
    for q in range(WORD_ROWS):
        ref[pl.ds(first * WORD_ROWS + q, rows, stride=WORD_ROWS), :] = (
            words[:, q * LANES:(q + 1) * LANES])


def _load_words(ref, rows, first=0):
    return jnp.concatenate(
        [ref[pl.ds(first * WORD_ROWS + q, rows, stride=WORD_ROWS), :]
         for q in range(WORD_ROWS)], axis=1)


_DISPATCH_BUFS = 3


def _dispatch_kernel(seg_ref, dst_ref, nch_ref, ntot_ref, nmax_ref, gap_dst_ref,
                     gap_nch_ref, misc_ref, route_ref, h_ref, xs_ref, buf_ref,
                     zero_ref, sem, zsem):
    t = MOE_TILE
    i = pl.program_id(0)
    slot = i % _DISPATCH_BUFS
    prev = (i + _DISPATCH_BUFS - 1) % _DISPATCH_BUFS
    prev2 = (i + _DISPATCH_BUFS - 2) % _DISPATCH_BUFS
    last = N_MOE_TILES - 1
    n_used = misc_ref[0]

    def make_copy(which):
        def mk(s, d, tokens):
            return pltpu.make_async_copy(_token_rows(buf_ref.at[which], s, tokens),
                                         _token_rows(xs_ref, d, tokens), sem.at[which])
        return mk

    def zero_copy(s, d, tokens):
        del s
        return pltpu.make_async_copy(_token_rows(zero_ref, 0, tokens),
                                     _token_rows(xs_ref, d, tokens), zsem)

    @pl.when(i == 0)
    def _():
        zero_ref[...] = jnp.zeros_like(zero_ref)

        def per_expert(e, carry):
            d0 = gap_dst_ref[e]
            code = gap_nch_ref[e]
            n = lax.shift_right_logical(code, _REM_SHIFT)

            def per_chunk(cidx, c2):
                _start_copy(zero_copy, 0, d0, cidx * CHUNK, CHUNK)
                return c2

            lax.fori_loop(0, n, per_chunk, 0)
            _start_tail(zero_copy, 0, d0, n, code & (_REM_UNITS - 1))
            return carry

        lax.fori_loop(0, N_EXPERTS, per_expert, 0)

        def per_block(b, carry):
            zero_copy(0, b * EXPERT_BLOCK, EXPERT_BLOCK).start()
            return carry

        lax.fori_loop(n_used, N_EXPERT_BLOCKS, per_block, 0)

    tile_prev = jnp.maximum(i - 1, 0)
    _segment_copies_inline(tile_prev, i > 0, seg_ref, dst_ref, nch_ref, make_copy(prev))

    route = route_ref[...]
    pos1 = route[0:1, :].astype(I32)
    pos2 = route[1:2, :].astype(I32)
    r = lax.broadcasted_iota(I32, (SORT_ROWS, t), 0)
    perm = jnp.where((r == pos1) | (r == pos2), 1.0, 0.0).astype(BF16)
    _store_words(buf_ref.at[slot], _pack_pairs(_dot(perm, h_ref[...])))

    @pl.when((i > 0) & (nmax_ref[tile_prev] > _INLINE_CHUNKS))
    def _():
        _segment_copies_loop(tile_prev, _INLINE_CHUNKS, False, seg_ref, dst_ref,
                             nch_ref, make_copy(prev))

    @pl.when(i > 1)
    def _():
        _wait_copies(ntot_ref[jnp.maximum(i - 2, 0)], make_copy(prev2))

    @pl.when(i == last)
    def _():
        _segment_copies_loop(i, 0, True, seg_ref, dst_ref, nch_ref, make_copy(slot))
        _wait_copies(ntot_ref[last - 1], make_copy(prev))
        _wait_copies(ntot_ref[last], make_copy(slot))
        _wait_copies(misc_ref[1], zero_copy)
        _wait_each(N_EXPERT_BLOCKS - n_used, lambda: zero_copy(0, 0, EXPERT_BLOCK))


def _dispatch(plan, route, h):
    t = MOE_TILE
    return pl.pallas_call(
        _dispatch_kernel,
        out_shape=jax.ShapeDtypeStruct((SORTED_ROWS * WORD_ROWS, LANES), I32),
        grid_spec=pltpu.PrefetchScalarGridSpec(
            num_scalar_prefetch=8,
            grid=(N_MOE_TILES,),
            in_specs=[
                pl.BlockSpec((SUBLANES, t), lambda i, *_: (0, i)),
                pl.BlockSpec((t, D_MODEL), lambda i, *_: (i, 0)),
            ],
            out_specs=pl.BlockSpec(memory_space=pl.ANY),
            scratch_shapes=[
                pltpu.VMEM((_DISPATCH_BUFS, SORT_ROWS * WORD_ROWS, LANES), I32),
                pltpu.VMEM((EXPERT_BLOCK * WORD_ROWS, LANES), I32),
                pltpu.SemaphoreType.DMA((_DISPATCH_BUFS,)),
                pltpu.SemaphoreType.DMA,
            ],
        ),
        compiler_params=_cparams(("arbitrary",)),
        name="moe_dispatch",
    )(plan["seg"], plan["dst"], plan["nch"], plan["ntot"], plan["nmax"],
      plan["gap_dst"], plan["gap_nch"], plan["misc"], route, h)


_WEIGHT_DMA_PRIORITY = 1


def _expert_kernel(blk_e_ref, first_ref, next_ref, misc_ref, x_ref, wg_hbm, wu_hbm,
                   wd_hbm, y_ref, wg_f, wu_f, wd_f, wg_b, wu_b, wd_b, sem, *, layer):
    blk0 = pl.program_id(0) * _BLOCKS_PER_STEP
    blk1 = blk0 + 1
    n_used = misc_ref[0]

    def weight_copies(e):
        return (pltpu.make_async_copy(wg_hbm.at[layer, e], wg_f, sem.at[0]),
                pltpu.make_async_copy(wu_hbm.at[layer, e], wu_f, sem.at[1]),
                pltpu.make_async_copy(wd_hbm.at[layer, e], wd_f, sem.at[2]))

    def take_weights(e):
        for cp in weight_copies(e):
            cp.wait()
        wg_b[...] = wg_f[...].astype(BF16)
        wu_b[...] = wu_f[...].astype(BF16)
        wd_b[...] = wd_f[...].astype(BF16)

        @pl.when(next_ref[e] >= 0)
        def _():
            for cp in weight_copies(next_ref[e]):
                cp.start(priority=_WEIGHT_DMA_PRIORITY)

    def run_rows(first, rows):
        x_lo, x_hi = _unpack_pairs(_load_words(x_ref, rows, first))
        x = jnp.concatenate([x_lo, x_hi], axis=1)
        g = _dot(x, wg_b[...])
        u = _dot(x, wu_b[...])
        a = (g * _sigmoid(g) * u).astype(BF16)
        y = _dot(a, wd_b[...])
        _store_words(y_ref, _pack_pairs(y.astype(BF16).astype(F32)), first)

    @pl.when(blk0 == 0)
    def _():
        for cp in weight_copies(blk_e_ref[0]):
            cp.start(priority=_WEIGHT_DMA_PRIORITY)

    @pl.when(blk0 < n_used)
    def _():
        e0 = blk_e_ref[blk0]
        e1 = blk_e_ref[blk1]
        pl.when(blk0 == first_ref[e0])(functools.partial(take_weights, e0))

        @pl.when(e0 == e1)
        def _():
            run_rows(0, _BLOCKS_PER_STEP * EXPERT_BLOCK)

        @pl.when(e0 != e1)
        def _():
            run_rows(0, EXPERT_BLOCK)
            take_weights(e1)
            run_rows(EXPERT_BLOCK, EXPERT_BLOCK)

    @pl.when(blk0 >= n_used)
    def _():
        y_ref[...] = jnp.zeros_like(y_ref)


def _experts(plan, xs, layer, w_gate, w_up, w_down):
    step_rows = _BLOCKS_PER_STEP * EXPERT_BLOCK * WORD_ROWS
    row_map = lambda i, be, fi, nx, misc: (
        jnp.minimum(i, (misc[0] - 1) // _BLOCKS_PER_STEP), 0)
    out_map = lambda i, be, fi, nx, misc: (i, 0)
    return pl.pallas_call(
        functools.partial(_expert_kernel, layer=layer),
        out_shape=jax.ShapeDtypeStruct((SORTED_ROWS * WORD_ROWS, LANES), I32),
        grid_spec=pltpu.PrefetchScalarGridSpec(
            num_scalar_prefetch=4,
            grid=(N_EXPERT_BLOCKS // _BLOCKS_PER_STEP,),
            in_specs=[
                pl.BlockSpec((step_rows, LANES), row_map),
                pl.BlockSpec(memory_space=pl.ANY),
                pl.BlockSpec(memory_space=pl.ANY),
                pl.BlockSpec(memory_space=pl.ANY),
            ],
            out_specs=pl.BlockSpec((step_rows, LANES), out_map),
            scratch_shapes=[
                pltpu.VMEM((D_MODEL, D_FF), F32),
                pltpu.VMEM((D_MODEL, D_FF), F32),
                pltpu.VMEM((D_FF, D_MODEL), F32),
                pltpu.VMEM((D_MODEL, D_FF), BF16),
                pltpu.VMEM((D_MODEL, D_FF), BF16),
                pltpu.VMEM((D_FF, D_MODEL), BF16),
                pltpu.SemaphoreType.DMA((3,)),
            ],
        ),
        compiler_params=_cparams(("arbitrary",)),
        name="moe_experts",
    )(plan["blk_e"], plan["first_blk"], plan["next_e"], plan["misc"], xs,
      w_gate, w_up, w_down)


_COMBINE_BUFS = 3


def _combine_kernel(seg_ref, dst_ref, nch_ref, ntot_ref, nmax_ref, route_ref, x_ref,
                    mod_ref, fg_ref, ys_ref, out_ref, buf_ref, sem, *, final_norm):
    t = MOE_TILE
    i = pl.program_id(0)
    ahead = _COMBINE_BUFS - 1
    slot = i % _COMBINE_BUFS
    slot_next = (i + ahead) % _COMBINE_BUFS
    tile_next = jnp.minimum(i + ahead, N_MOE_TILES - 1)
    has_next = i + ahead < N_MOE_TILES

    def make_copy(which):
        def mk(s, d, tokens):
            return pltpu.make_async_copy(_token_rows(ys_ref, d, tokens),
                                         _token_rows(buf_ref.at[which], s, tokens),
                                         sem.at[which])
        return mk

    @pl.when(i == 0)
    def _():
        buf_ref[...] = jnp.zeros_like(buf_ref)
        for k in range(_COMBINE_BUFS - 1):
            _segment_copies_loop(k, 0, True, seg_ref, dst_ref, nch_ref, make_copy(k))

    _segment_copies_inline(tile_next, has_next, seg_ref, dst_ref, nch_ref,
                           make_copy(slot_next))

    route = route_ref[...]
    route_t = jnp.concatenate(
        [route, jnp.zeros((LANES - SUBLANES, t), F32)], axis=0).T
    pos1 = route_t[:, 0:1].astype(I32)
    pos2 = route_t[:, 1:2].astype(I32)
    w1 = route_t[:, 2:3]
    w2 = route_t[:, 3:4]
    c = lax.broadcasted_iota(I32, (t, SORT_ROWS), 1)
    unsort = (jnp.where(c == pos1, w1, 0.0) + jnp.where(c == pos2, w2, 0.0)).astype(BF16)

    @pl.when(has_next & (nmax_ref[tile_next] > _INLINE_CHUNKS))
    def _():
        _segment_copies_loop(tile_next, _INLINE_CHUNKS, False, seg_ref, dst_ref,
                             nch_ref, make_copy(slot_next))

    _wait_copies(ntot_ref[i], make_copy(slot))
    y_lo, y_hi = _unpack_pairs(_load_words(buf_ref.at[slot], SORT_ROWS))
    moe = jnp.concatenate([_dot(unsort, y_lo), _dot(unsort, y_hi)], axis=1)
    gate = mod_ref[0][:, 2 * D_MODEL:3 * D_MODEL]
    x_new = x_ref[...] + gate * moe
    if final_norm:
        ms = jnp.mean(x_new * x_new, axis=-1, keepdims=True)
        x_new = (x_new * lax.rsqrt(ms + EPS)) * fg_ref[...]
    out_ref[...] = x_new


def _combine(plan, route, x, mod, final_g, ys, final_norm):
    t = MOE_TILE
    tiles_per_seq = SEQ // t
    return pl.pallas_call(
        functools.partial(_combine_kernel, final_norm=final_norm),
        out_shape=jax.ShapeDtypeStruct((N_TOK, D_MODEL), F32),
        grid_spec=pltpu.PrefetchScalarGridSpec(
            num_scalar_prefetch=5,
            grid=(N_MOE_TILES,),
            in_specs=[
                pl.BlockSpec((SUBLANES, t), lambda i, *_: (0, i)),
                pl.BlockSpec((t, D_MODEL), lambda i, *_: (i, 0)),
                pl.BlockSpec((1, 1, 3 * D_MODEL), lambda i, *_: (i // tiles_per_seq, 0, 0)),
                pl.BlockSpec((1, D_MODEL), lambda i, *_: (0, 0)),
                pl.BlockSpec(memory_space=pl.ANY),
            ],
            out_specs=pl.BlockSpec((t, D_MODEL), lambda i, *_: (i, 0)),
            scratch_shapes=[
                pltpu.VMEM((_COMBINE_BUFS, SORT_ROWS * WORD_ROWS, LANES), I32),
                pltpu.SemaphoreType.DMA((_COMBINE_BUFS,)),
            ],
        ),
        compiler_params=_cparams(("arbitrary",)),
        name="moe_combine",
    )(plan["seg"], plan["dst"], plan["nch"], plan["ntot"], plan["nmax"], route, x,
      mod, final_g.reshape(1, D_MODEL).astype(F32), ys)


def _moe_layer(x, norm_g, mod, final_g, w_group, b_group, w_expert, b_expert,
               layer, w_gate, w_up, w_down, final_norm):
    h, route, cnt = _route(x, norm_g, mod, w_group, b_group, w_expert, b_expert)
    plan = _dispatch_plan(cnt[:, :, 0])
    xs = _dispatch(plan, route, h)
    ys = _experts(plan, xs, layer, w_gate, w_up, w_down)
    return _combine(plan, route, x, mod, final_g, ys, final_norm)


def kernel(x, c, ada_w, ada_b, norm_g, final_g, m_w_in, m_b_gates, m_norm_g, m_w_out, s_w_in, s_conv_w, s_w_out, r_w_group, r_b_group, r_w_expert, r_b_expert, e_w_gate, e_w_up, e_w_down):
    mods = _ada_mods(c, ada_w, ada_b)
    xt = x.reshape(N_TOK, D_MODEL)
    for i in range(DEPTH):
        mod_mix = mods[2 * i].reshape(BATCH, 1, 3 * D_MODEL)
        mod_ffn = mods[2 * i + 1].reshape(BATCH, 1, 3 * D_MODEL)
        j = i // 2
        if i % 2 == 0:
            qt, k, vt, ot, gates, gatest = _mlstm_in(xt, norm_g[i, 0], mod_mix,
                                                     m_w_in[j], m_b_gates[j])
            xt = _mlstm_rec(xt, mod_mix, qt, k, vt, ot, gates, gatest,
                            m_norm_g[j], m_w_out[j])
        else:
            xt = _conv_layer(xt, norm_g[i, 0], mod_mix, s_w_in[j], s_conv_w[j],
                             s_w_out[j])
        xt = _moe_layer(xt, norm_g[i, 1], mod_ffn, final_g, r_w_group[i],
                        r_b_group[i], r_w_expert[i], r_b_expert[i], i, e_w_gate,
                        e_w_up, e_w_down, final_norm=(i == DEPTH - 1))
    return xt.reshape(BATCH, SEQ, D_MODEL)
```

```python
import functools

import jax
import jax.numpy as jnp
from jax import lax
from jax.experimental import pallas as pl
from jax.experimental.pallas import tpu as pltpu

F32 = jnp.float32
BF16 = jnp.bfloat16
I32 = jnp.int32

D_MODEL = 1024
BATCH = 8
SEQ = 2048
DEPTH = 4
N_TOK = BATCH * SEQ
M_HEADS = 4
M_DK = 128
M_DV = 256
M_QK = M_HEADS * M_DK
M_V = M_HEADS * M_DV
CONV_K = 3
N_GROUPS = 4
E_PER_GROUP = 8
N_EXPERTS = N_GROUPS * E_PER_GROUP
TOP_K = 2
D_FF = 512
EPS = 1e-6

SUBLANES = 8
LANES = 128

ROW_TILE = 512
MLSTM_CHUNK = 256
MOE_TILE = 256
EXPERT_BLOCK = 256
WORD_ROWS = (D_MODEL // 2) // LANES
SEG_ALIGN = SUBLANES // WORD_ROWS
CHUNK = 8
SORT_ROWS = -(-(MOE_TILE * TOP_K + N_EXPERTS * (SEG_ALIGN - 1)) // LANES) * LANES
N_MOE_TILES = N_TOK // MOE_TILE
_MAX_SORTED = (N_MOE_TILES * (MOE_TILE * TOP_K + N_EXPERTS * (SEG_ALIGN - 1))
               + N_EXPERTS * (EXPERT_BLOCK - 1))
_BLOCKS_PER_STEP = 2
N_EXPERT_BLOCKS = (-(-_MAX_SORTED // (EXPERT_BLOCK * _BLOCKS_PER_STEP))
                   * _BLOCKS_PER_STEP)
SORTED_ROWS = N_EXPERT_BLOCKS * EXPERT_BLOCK

VMEM_LIMIT = 48 * 1024 * 1024


def _cparams(sem):
    return pltpu.CompilerParams(dimension_semantics=sem,
                                vmem_limit_bytes=VMEM_LIMIT)


def _dot(a, b):
    return jnp.dot(a, b, preferred_element_type=F32)


def _dot_nt(a, b):
    return lax.dot_general(a, b, (((1,), (1,)), ((), ())),
                           preferred_element_type=F32)


def _split3(x):
    hi = x.astype(BF16)
    r1 = x - hi.astype(F32)
    mid = r1.astype(BF16)
    lo = (r1 - mid.astype(F32)).astype(BF16)
    return hi, mid, lo


def _dot_sel_left(sel, x):
    hi, mid, lo = _split3(x)
    return _dot(sel, hi) + _dot(sel, mid) + _dot(sel, lo)


def _dot_sel_right(x, sel):
    hi, mid, lo = _split3(x)
    return _dot(hi, sel) + _dot(mid, sel) + _dot(lo, sel)


def _sigmoid(x):
    return 1.0 / (1.0 + jnp.exp(-x))


def _rms_mod(x, g, mod):
    ms = jnp.mean(x * x, axis=-1, keepdims=True)
    y = (x * lax.rsqrt(ms + EPS)) * g
    return y * (1.0 + mod[:, D_MODEL:2 * D_MODEL]) + mod[:, 0:D_MODEL]


def _ada_kernel(c_ref, w_ref, b_ref, o_ref):
    c = c_ref[...]
    s = (c * _sigmoid(c)).astype(BF16)
    o_ref[0] = _dot(s, w_ref[0].astype(BF16)) + b_ref[0]


def _ada_mods(c, ada_w, ada_b):
    n_pairs = DEPTH * 2
    w = ada_w.reshape(n_pairs, D_MODEL, 3 * D_MODEL)
    b = ada_b.reshape(n_pairs, 1, 3 * D_MODEL)
    col = D_MODEL
    return pl.pallas_call(
        _ada_kernel,
        out_shape=jax.ShapeDtypeStruct((n_pairs, BATCH, 3 * D_MODEL), F32),
        grid=(n_pairs, 3 * D_MODEL // col),
        in_specs=[
            pl.BlockSpec((BATCH, D_MODEL), lambda p, j: (0, 0)),
            pl.BlockSpec((1, D_MODEL, col), lambda p, j: (p, 0, j)),
            pl.BlockSpec((1, 1, col), lambda p, j: (p, 0, j)),
        ],
        out_specs=pl.BlockSpec((1, BATCH, col), lambda p, j: (p, 0, j)),
        compiler_params=_cparams(("arbitrary", "arbitrary")),
        name="ada_mods",
    )(c, w, b)


def _mlstm_in_kernel(x_ref, g_ref, mod_ref, wqt_ref, wk_ref, wvt_ref, wot_ref,
                     wg_ref, bg_ref, qt_ref, k_ref, vt_ref, ot_ref, gates_ref,
                     gatest_ref):
    h = _rms_mod(x_ref[...], g_ref[...], mod_ref[0])
    hb = h.astype(BF16)
    qt_ref[...] = (_dot_nt(wqt_ref[...], hb) * (M_DK ** -0.5)).astype(BF16)
    k_ref[...] = _dot(hb, wk_ref[...]).astype(BF16)
    vt_ref[...] = _dot_nt(wvt_ref[...], hb).astype(BF16)
    ot_ref[...] = _dot_nt(wot_ref[...], hb).astype(BF16)
    g = _dot(hb, wg_ref[...]) + bg_ref[...]
    log_sig = jnp.minimum(g, 0.0) - jnp.log(1.0 + jnp.exp(-jnp.abs(g)))
    lane = lax.broadcasted_iota(I32, g.shape, 1)
    gg = jnp.where(lane < M_HEADS, g, log_sig)
    gates_ref[...] = gg
    gatest_ref[...] = gg.T[0:SUBLANES, :]


def _mlstm_in(x, norm_g, mod, w_in, b_gates):
    t = ROW_TILE
    tiles_per_seq = SEQ // t
    wqt = w_in[:, 0:M_QK].T.astype(BF16)
    wk = w_in[:, M_QK:2 * M_QK].astype(BF16)
    wvt = w_in[:, 2 * M_QK:2 * M_QK + M_V].T.astype(BF16)
    wot = w_in[:, 2 * M_QK + M_V:2 * M_QK + 2 * M_V].T.astype(BF16)
    n_gate = 2 * M_HEADS
    wg = jnp.pad(w_in[:, 2 * M_QK + 2 * M_V:], ((0, 0), (0, LANES - n_gate))).astype(BF16)
    bg = jnp.pad(b_gates.astype(F32), (0, LANES - n_gate)).reshape(1, LANES)
    full = lambda shape: pl.BlockSpec(shape, lambda i: (0, 0))
    return pl.pallas_call(
        _mlstm_in_kernel,
        out_shape=(
            jax.ShapeDtypeStruct((M_QK, N_TOK), BF16),
            jax.ShapeDtypeStruct((N_TOK, M_QK), BF16),
            jax.ShapeDtypeStruct((M_V, N_TOK), BF16),
            jax.ShapeDtypeStruct((M_V, N_TOK), BF16),
            jax.ShapeDtypeStruct((N_TOK, LANES), F32),
            jax.ShapeDtypeStruct((SUBLANES, N_TOK), F32),
        ),
        grid=(N_TOK // t,),
        in_specs=[
            pl.BlockSpec((t, D_MODEL), lambda i: (i, 0)),
            full((1, D_MODEL)),
            pl.BlockSpec((1, 1, 3 * D_MODEL), lambda i: (i // tiles_per_seq, 0, 0)),
            full((M_QK, D_MODEL)),
            full((D_MODEL, M_QK)),
            full((M_V, D_MODEL)),
            full((M_V, D_MODEL)),
            full((D_MODEL, LANES)),
            full((1, LANES)),
        ],
        out_specs=(
            pl.BlockSpec((M_QK, t), lambda i: (0, i)),
            pl.BlockSpec((t, M_QK), lambda i: (i, 0)),
            pl.BlockSpec((M_V, t), lambda i: (0, i)),
            pl.BlockSpec((M_V, t), lambda i: (0, i)),
            pl.BlockSpec((t, LANES), lambda i: (i, 0)),
            pl.BlockSpec((SUBLANES, t), lambda i: (0, i)),
        ),
        compiler_params=_cparams(("arbitrary",)),
        name="mlstm_in",
    )(x, norm_g.reshape(1, D_MODEL), mod, wqt, wk, wvt, wot, wg, bg)


def _mlstm_rec_kernel(qt_ref, k_ref, vt_ref, ot_ref, gates_ref, gatest_ref,
                      x_ref, mod_ref, ng_ref, wout_ref, out_ref, ct_ref, m_ref):
    L = MLSTM_CHUNK

    @pl.when(pl.program_id(1) == 0)
    def _():
        ct_ref[...] = jnp.zeros_like(ct_ref)
        m_ref[...] = jnp.zeros_like(m_ref)

    gates = gates_ref[...]
    gatest = gatest_ref[...]
    row = lax.broadcasted_iota(I32, (L, L), 0)
    col = lax.broadcasted_iota(I32, (L, L), 1)
    tri_low = jnp.where(row >= col, 1.0, 0.0).astype(BF16)
    tri_up = jnp.where(row <= col, 1.0, 0.0).astype(BF16)
    cum_cols = _dot_sel_left(tri_low, gates)
    cum_rows = _dot_sel_right(gatest, tri_up)

    col_term = gates - pltpu.roll(cum_cols, LANES - M_HEADS, axis=1)
    col_pieces = jnp.concatenate(_split3(col_term), axis=1)
    sel_row = lax.broadcasted_iota(I32, (3 * LANES, L), 0)
    ng_wide = jnp.concatenate([ng_ref[...]] * (L // LANES), axis=1)

    hs = []
    for h in range(M_HEADS):
        qt = qt_ref[h * M_DK:(h + 1) * M_DK, :]
        kh = k_ref[:, h * M_DK:(h + 1) * M_DK]
        vt = vt_ref[h * M_DV:(h + 1) * M_DV, :]
        state = ct_ref[h]
        ig_row = gatest[h:h + 1, :]
        bcum_row = cum_rows[M_HEADS + h:M_HEADS + h + 1, :]
        m_prev = m_ref[h:h + 1, 0:1]

        pick = jnp.where((sel_row == h) | (sel_row == LANES + h)
                         | (sel_row == 2 * LANES + h), 1.0, 0.0).astype(BF16)
        dlog = jnp.where(row <= col, _dot(col_pieces, pick) + bcum_row, -jnp.inf)
        inter_log = bcum_row + m_prev
        m_t = jnp.maximum(inter_log, jnp.max(dlog, axis=0, keepdims=True))
        w_intra = jnp.exp(dlog - m_t)
        w_inter = jnp.exp(inter_log - m_t)
        scores = _dot(kh, qt) * w_intra
        q_state = _dot(state.astype(BF16), qt)
        num = _dot(vt, scores.astype(BF16)) + w_inter * q_state[0:M_DV, :]
        den = (jnp.sum(scores, axis=0, keepdims=True)
               + w_inter * q_state[M_DV:M_DV + 1, :])
        h_out = num / jnp.maximum(jnp.abs(den), jnp.exp(-m_t))

        b_last = bcum_row[:, L - 1:L]
        log_src = b_last - bcum_row + ig_row
        m_new = jnp.maximum(b_last + m_prev,
                            jnp.max(log_src, axis=1, keepdims=True))
        w_src = jnp.exp(log_src - m_new)
        decay = jnp.exp(b_last + m_prev - m_new)
        vt_w = jnp.concatenate(
            [(vt.astype(F32) * w_src).astype(BF16),
             jnp.broadcast_to(w_src, (SUBLANES, L)).astype(BF16)], axis=0)
        ct_ref[h] = decay * state + _dot(vt_w, kh)
        m_ref[h:h + 1, :] = jnp.broadcast_to(m_new, (1, LANES))

        hn = h_out * lax.rsqrt(jnp.mean(h_out * h_out, axis=0, keepdims=True) + EPS)
        og = ot_ref[h * M_DV:(h + 1) * M_DV, :].astype(F32)
        hs.append((hn * ng_wide[h * M_DV:(h + 1) * M_DV, :] * _sigmoid(og)).astype(BF16))

    hs_t = jnp.concatenate(hs, axis=0)
    y = lax.dot_general(hs_t, wout_ref[...], (((0,), (0,)), ((), ())),
                        preferred_element_type=F32)
    gate = mod_ref[0][:, 2 * D_MODEL:3 * D_MODEL]
    out_ref[...] = x_ref[...] + gate * y


def _mlstm_rec(x, mod, qt, k, vt, ot, gates, gatest, m_norm_g, w_out):
    L = MLSTM_CHUNK
    nc = SEQ // L
    rows = lambda width: pl.BlockSpec((L, width), lambda b, j: (b * nc + j, 0))
    cols = lambda height: pl.BlockSpec((height, L), lambda b, j: (0, b * nc + j))
    ng = jnp.broadcast_to(m_norm_g.astype(F32).reshape(M_V, 1), (M_V, LANES))
    return pl.pallas_call(
        _mlstm_rec_kernel,
        out_shape=jax.ShapeDtypeStruct((N_TOK, D_MODEL), F32),
        grid=(BATCH, nc),
        in_specs=[
            cols(M_QK), rows(M_QK), cols(M_V), cols(M_V), rows(LANES),
            cols(SUBLANES), rows(D_MODEL),
            pl.BlockSpec((1, 1, 3 * D_MODEL), lambda b, j: (b, 0, 0)),
            pl.BlockSpec((M_V, LANES), lambda b, j: (0, 0)),
            pl.BlockSpec((M_V, D_MODEL), lambda b, j: (0, 0)),
        ],
        out_specs=rows(D_MODEL),
        scratch_shapes=[
            pltpu.VMEM((M_HEADS, M_DV + SUBLANES, M_DK), F32),
            pltpu.VMEM((SUBLANES, LANES), F32),
        ],
        compiler_params=_cparams(("arbitrary", "arbitrary")),
        name="mlstm_rec",
    )(qt, k, vt, ot, gates, gatest, x, mod, ng, w_out.astype(BF16))


_CONV_COLS = 256


def _conv_kernel(x_ref, g_ref, mod_ref, win_ref, cw_ref, wout_ref, out_ref,
                 carry_ref, z_ref):
    t = ROW_TILE
    tiles_per_seq = SEQ // t

    @pl.when(pl.program_id(0) % tiles_per_seq == 0)
    def _():
        carry_ref[...] = jnp.zeros_like(carry_ref)

    x = x_ref[...]
    mod = mod_ref[0]
    hb = _rms_mod(x, g_ref[...], mod).astype(BF16)
    row = lax.broadcasted_iota(I32, (t, _CONV_COLS), 0)
    for j in range(D_MODEL // _CONV_COLS):
        lo, hi = j * _CONV_COLS, (j + 1) * _CONV_COLS
        b_gate = _dot(hb, win_ref[:, lo:hi])
        c_gate = _dot(hb, win_ref[:, D_MODEL + lo:D_MODEL + hi])
        xb = _dot(hb, win_ref[:, 2 * D_MODEL + lo:2 * D_MODEL + hi])
        u = c_gate * xb
        prev1 = carry_ref[SUBLANES - 1:SUBLANES, lo:hi]
        prev2 = carry_ref[SUBLANES - 2:SUBLANES - 1, lo:hi]
        u1 = jnp.where(row == 0, prev1, pltpu.roll(u, 1, axis=0))
        u2 = jnp.where(row == 0, prev2,
                       jnp.where(row == 1, prev1, pltpu.roll(u, 2, axis=0)))
        y = (cw_ref[0:1, lo:hi] * u2 + cw_ref[1:2, lo:hi] * u1
             + cw_ref[2:3, lo:hi] * u)
        z_ref[:, lo:hi] = (b_gate * y).astype(BF16)
        carry_ref[:, lo:hi] = u[t - SUBLANES:t, :]
    gate = mod[:, 2 * D_MODEL:3 * D_MODEL]
    out_ref[...] = x + gate * _dot(z_ref[...], wout_ref[...])


def _conv_layer(x, norm_g, mod, w_in, conv_w, w_out):
    t = ROW_TILE
    tiles_per_seq = SEQ // t
    full = lambda shape: pl.BlockSpec(shape, lambda i: (0, 0))
    cw = jnp.pad(conv_w.astype(F32), ((0, SUBLANES - CONV_K), (0, 0)))
    return pl.pallas_call(
        _conv_kernel,
        out_shape=jax.ShapeDtypeStruct((N_TOK, D_MODEL), F32),
        grid=(N_TOK // t,),
        in_specs=[
            pl.BlockSpec((t, D_MODEL), lambda i: (i, 0)),
            full((1, D_MODEL)),
            pl.BlockSpec((1, 1, 3 * D_MODEL), lambda i: (i // tiles_per_seq, 0, 0)),
            full((D_MODEL, 3 * D_MODEL)),
            full((SUBLANES, D_MODEL)),
            full((D_MODEL, D_MODEL)),
        ],
        out_specs=pl.BlockSpec((t, D_MODEL), lambda i: (i, 0)),
        scratch_shapes=[
            pltpu.VMEM((SUBLANES, D_MODEL), F32),
            pltpu.VMEM((t, D_MODEL), BF16),
        ],
        compiler_params=_cparams(("arbitrary",)),
        name="conv_layer",
    )(x, norm_g.reshape(1, D_MODEL), mod, w_in.astype(BF16), cw, w_out.astype(BF16))


_ROUTE_ROWS = LANES
_EXPERT_ROW0 = SUBLANES


_ROUTE_TILES = 1


def _route_kernel(x_ref, g_ref, mod_ref, wr_hi_ref, wr_lo_ref, br_ref,
                  h_ref, route_ref, cnt_ref):
    t = MOE_TILE
    for j in range(_ROUTE_TILES):
        h, route, cnt = _route_tile(x_ref[j * t:(j + 1) * t, :], g_ref[...], mod_ref[0],
                                    wr_hi_ref[...], wr_lo_ref[...], br_ref[...])
        h_ref[j * t:(j + 1) * t, :] = h
        route_ref[:, j * t:(j + 1) * t] = route
        cnt_ref[j] = cnt


def _route_tile(x, g, mod, wr_hi, wr_lo, br):
    t = MOE_TILE
    h = _rms_mod(x, g, mod)
    h_hi = h.astype(BF16)
    h_lo = (h - h_hi.astype(F32)).astype(BF16)
    logits = (_dot_nt(wr_hi, h_hi) + _dot_nt(wr_hi, h_lo)
              + _dot_nt(wr_lo, h_hi)) + br

    sub = lax.broadcasted_iota(I32, (SUBLANES, t), 0)
    neg_inf = -jnp.inf
    gl = jnp.where(sub < N_GROUPS, logits[0:SUBLANES, :], neg_inf)
    gmax = jnp.max(gl, axis=0, keepdims=True)
    g_sel = jnp.min(jnp.where(gl == gmax, sub, SUBLANES), axis=0, keepdims=True)
    p_sel = 1.0 / jnp.sum(jnp.exp(gl - gmax), axis=0, keepdims=True)

    e_sel = jnp.zeros((E_PER_GROUP, t), F32)
    for g in range(N_GROUPS):
        r0 = _EXPERT_ROW0 + g * E_PER_GROUP
        e_sel = jnp.where(g_sel == g, logits[r0:r0 + E_PER_GROUP, :], e_sel)
    v1 = jnp.max(e_sel, axis=0, keepdims=True)
    i1 = jnp.min(jnp.where(e_sel == v1, sub, SUBLANES), axis=0, keepdims=True)
    e_rest = jnp.where(sub == i1, neg_inf, e_sel)
    v2 = jnp.max(e_rest, axis=0, keepdims=True)
    i2 = jnp.min(jnp.where(e_rest == v2, sub, SUBLANES), axis=0, keepdims=True)
    ratio = jnp.exp(v2 - v1)
    w1 = p_sel / (1.0 + ratio)
    w2 = p_sel * ratio / (1.0 + ratio)
    eid1 = g_sel * E_PER_GROUP + i1
    eid2 = g_sel * E_PER_GROUP + i2

    erow = lax.broadcasted_iota(I32, (N_EXPERTS, t), 0)
    m1 = erow == eid1
    m2 = erow == eid2
    member = jnp.where(m1 | m2, 1.0, 0.0)
    r = lax.broadcasted_iota(I32, (t, t), 0)
    c = lax.broadcasted_iota(I32, (t, t), 1)
    earlier = jnp.where(r < c, 1.0, 0.0).astype(BF16)
    member_b = member.astype(BF16)
    rank = _dot(member_b, earlier)
    cnt = _dot(member_b, jnp.ones((t, LANES), BF16))
    cnt_pad = jnp.floor((cnt + (SEG_ALIGN - 1.0)) * (1.0 / SEG_ALIGN)) * SEG_ALIGN
    er = lax.broadcasted_iota(I32, (N_EXPERTS, N_EXPERTS), 0)
    ec = lax.broadcasted_iota(I32, (N_EXPERTS, N_EXPERTS), 1)
    before = jnp.where(er > ec, 1.0, 0.0).astype(BF16)
    seg_start = _dot(before, cnt_pad.astype(BF16))[:, 0:1]
    pos = seg_start + rank
    pos1 = jnp.sum(jnp.where(m1, pos, 0.0), axis=0, keepdims=True)
    pos2 = jnp.sum(jnp.where(m2, pos, 0.0), axis=0, keepdims=True)

    out = jnp.zeros((SUBLANES, t), F32)
    for k, val in enumerate((pos1, pos2, w1, w2)):
        out = jnp.where(sub == k, val, out)
    return h_hi, out, cnt


def _route(x, norm_g, mod, w_group, b_group, w_expert, b_expert):
    t = MOE_TILE * _ROUTE_TILES
    tiles_per_seq = SEQ // t
    wr = jnp.zeros((_ROUTE_ROWS, D_MODEL), F32)
    wr = wr.at[0:N_GROUPS].set(w_group.T.astype(F32))
    wr = wr.at[_EXPERT_ROW0:_EXPERT_ROW0 + N_EXPERTS].set(w_expert.T.astype(F32))
    wr_hi = wr.astype(BF16)
    wr_lo = (wr - wr_hi.astype(F32)).astype(BF16)
    br = jnp.zeros((_ROUTE_ROWS,), F32)
    br = br.at[0:N_GROUPS].set(b_group.astype(F32))
    br = br.at[_EXPERT_ROW0:_EXPERT_ROW0 + N_EXPERTS].set(b_expert.astype(F32))
    full = lambda shape: pl.BlockSpec(shape, lambda i: (0, 0))
    return pl.pallas_call(
        _route_kernel,
        out_shape=(
            jax.ShapeDtypeStruct((N_TOK, D_MODEL), BF16),
            jax.ShapeDtypeStruct((SUBLANES, N_TOK), F32),
            jax.ShapeDtypeStruct((N_MOE_TILES, N_EXPERTS, LANES), F32),
        ),
        grid=(N_MOE_TILES // _ROUTE_TILES,),
        in_specs=[
            pl.BlockSpec((t, D_MODEL), lambda i: (i, 0)),
            full((1, D_MODEL)),
            pl.BlockSpec((1, 1, 3 * D_MODEL), lambda i: (i // tiles_per_seq, 0, 0)),
            full((_ROUTE_ROWS, D_MODEL)),
            full((_ROUTE_ROWS, D_MODEL)),
            full((_ROUTE_ROWS, 1)),
        ],
        out_specs=(
            pl.BlockSpec((t, D_MODEL), lambda i: (i, 0)),
            pl.BlockSpec((SUBLANES, t), lambda i: (0, i)),
            pl.BlockSpec((_ROUTE_TILES, N_EXPERTS, LANES), lambda i: (i, 0, 0)),
        ),
        compiler_params=_cparams(("arbitrary",)),
        name="moe_route",
    )(x, norm_g.reshape(1, D_MODEL), mod, wr_hi, wr_lo, br.reshape(_ROUTE_ROWS, 1))


_REM_UNITS = CHUNK // SEG_ALIGN
_REM_SHIFT = _REM_UNITS.bit_length() - 1


def _chunk_code(tokens):
    return (tokens // CHUNK) * _REM_UNITS + (tokens % CHUNK) // SEG_ALIGN


def _dispatch_plan(cnt):
    cnt = cnt.astype(I32)
    cnt_pad = (cnt + SEG_ALIGN - 1) // SEG_ALIGN * SEG_ALIGN
    seg = jnp.cumsum(cnt_pad, axis=1) - cnt_pad
    tot = jnp.sum(cnt_pad, axis=0)
    ptot = (tot + EXPERT_BLOCK - 1) // EXPERT_BLOCK * EXPERT_BLOCK
    pend = jnp.cumsum(ptot)
    gbase = pend - ptot
    dst = gbase[None, :] + jnp.cumsum(cnt_pad, axis=0) - cnt_pad
    nch = _chunk_code(cnt_pad)
    n_used = (pend[-1] // EXPERT_BLOCK).astype(I32)
    blk = jnp.arange(N_EXPERT_BLOCKS, dtype=I32)
    blk_start = jnp.minimum(blk, n_used - 1) * EXPERT_BLOCK
    blk_e = jnp.sum((pend[None, :] <= blk_start[:, None]).astype(I32), axis=1)
    blk_e = jnp.minimum(blk_e, N_EXPERTS - 1).astype(I32)
    gap_dst = gbase + tot
    gap_nch = _chunk_code(ptot - tot)
    misc = jnp.stack([n_used, jnp.sum(ptot - tot) // SEG_ALIGN]).astype(I32)
    first_blk = gbase // EXPERT_BLOCK
    ids = jnp.arange(N_EXPERTS, dtype=I32)
    later = (ids[None, :] > ids[:, None]) & (ptot[None, :] > 0)
    next_e = jnp.min(jnp.where(later, ids[None, :], N_EXPERTS), axis=1)
    next_e = jnp.where(next_e == N_EXPERTS, -1, next_e)
    return dict(seg=seg.reshape(-1).astype(I32), dst=dst.reshape(-1).astype(I32),
                nch=nch.reshape(-1).astype(I32),
                ntot=(jnp.sum(cnt_pad, axis=1) // SEG_ALIGN).astype(I32),
                nmax=(jnp.max(cnt_pad, axis=1) // CHUNK).astype(I32), blk_e=blk_e,
                first_blk=first_blk.astype(I32), next_e=next_e.astype(I32),
                gap_dst=gap_dst.astype(I32), gap_nch=gap_nch.astype(I32), misc=misc)


_INLINE_CHUNKS = 3


def _start_copy(make_copy, s0, d0, offset, tokens, priority=0):
    s = pl.multiple_of(s0 + offset, SEG_ALIGN)
    d = pl.multiple_of(d0 + offset, SEG_ALIGN)
    make_copy(s, d, tokens).start(priority=priority)


def _start_tail(make_copy, s0, d0, n_full, rem, priority=0):
    half, quarter = CHUNK // 2, CHUNK // 4
    tail = n_full * CHUNK
    has_half = (rem & 2) != 0
    pl.when(has_half)(functools.partial(_start_copy, make_copy, s0, d0, tail, half,
                                        priority))
    tail2 = tail + jnp.where(has_half, half, 0)
    pl.when((rem & 1) != 0)(functools.partial(_start_copy, make_copy, s0, d0, tail2,
                                              quarter, priority))


def _segment_copies_inline(tile, live, seg_ref, dst_ref, nch_ref, make_copy):
    for e in range(N_EXPERTS):
        idx = tile * N_EXPERTS + e
        code = jnp.where(live, nch_ref[idx], 0)
        n = lax.shift_right_logical(code, _REM_SHIFT)
        s0 = seg_ref[idx]
        d0 = dst_ref[idx]
        for cidx in range(_INLINE_CHUNKS):
            pl.when(cidx < n)(functools.partial(_start_copy, make_copy, s0, d0,
                                                cidx * CHUNK, CHUNK, (e + cidx) % 2))
        _start_tail(make_copy, s0, d0, n, code & (_REM_UNITS - 1), e % 2)


def _segment_copies_loop(tile, first, with_tail, seg_ref, dst_ref, nch_ref, make_copy):
    def per_expert(e, carry):
        idx = tile * N_EXPERTS + e
        code = nch_ref[idx]
        n = lax.shift_right_logical(code, _REM_SHIFT)
        s0 = seg_ref[idx]
        d0 = dst_ref[idx]

        def per_chunk(cidx, c2):
            _start_copy(make_copy, s0, d0, cidx * CHUNK, CHUNK)
            return c2

        lax.fori_loop(first, jnp.maximum(n, first), per_chunk, 0)
        if with_tail:
            _start_tail(make_copy, s0, d0, n, code & (_REM_UNITS - 1))
        return carry

    lax.fori_loop(0, N_EXPERTS, per_expert, 0)


_WAIT_GROUP = 64


def _wait_each(n, make_wait):
    def body(_, carry):
        make_wait().wait()
        return carry
    lax.fori_loop(0, n, body, 0)


def _wait_copies(units, make_copy):
    _wait_each(units // _WAIT_GROUP, lambda: make_copy(0, 0, _WAIT_GROUP * SEG_ALIGN))
    _wait_each(units % _WAIT_GROUP, lambda: make_copy(0, 0, SEG_ALIGN))


_HALF = D_MODEL // 2
_HI_MASK = -65536


def _pack_pairs(x):
    lo = lax.shift_right_logical(lax.bitcast_convert_type(x[:, 0:_HALF], I32), 16)
    hi = lax.bitcast_convert_type(x[:, _HALF:D_MODEL], I32) & _HI_MASK
    return lo | hi


def _unpack_pairs(w):
    lo = lax.bitcast_convert_type(lax.shift_left(w, 16), F32).astype(BF16)
    hi = lax.bitcast_convert_type(w & _HI_MASK, F32).astype(BF16)
    return lo, hi


def _token_rows(ref, tok, tokens):
    start = pl.multiple_of(tok * WORD_ROWS, SUBLANES)
    return ref.at[pl.ds(start, tokens * WORD_ROWS)]


def _store_words(ref, words, first=0):
    rows = words.shape[0]
    for q in range(WORD_ROWS):
        ref[pl.ds(first * WORD_ROWS + q, rows, stride=WORD_ROWS), :] = (
            words[:, q * LANES:(q + 1) * LANES])


def _load_words(ref, rows, first=0):
    return jnp.concatenate(
        [ref[pl.ds(first * WORD_ROWS + q, rows, stride=WORD_ROWS), :]
         for q in range(WORD_ROWS)], axis=1)


_DISPATCH_BUFS = 3


def _dispatch_kernel(seg_ref, dst_ref, nch_ref, ntot_ref, nmax_ref, gap_dst_ref,
                     gap_nch_ref, misc_ref, route_ref, h_ref, xs_ref, buf_ref,
                     zero_ref, sem, zsem):
    t = MOE_TILE
    i = pl.program_id(0)
    slot = i % _DISPATCH_BUFS
    prev = (i + _DISPATCH_BUFS - 1) % _DISPATCH_BUFS
    prev2 = (i + _DISPATCH_BUFS - 2) % _DISPATCH_BUFS
    last = N_MOE_TILES - 1
    n_used = misc_ref[0]

    def make_copy(which):
        def mk(s, d, tokens):
            return pltpu.make_async_copy(_token_rows(buf_ref.at[which], s, tokens),
                                         _token_rows(xs_ref, d, tokens), sem.at[which])
        return mk

    def zero_copy(s, d, tokens):
        del s
        return pltpu.make_async_copy(_token_rows(zero_ref, 0, tokens),
                                     _token_rows(xs_ref, d, tokens), zsem)

    @pl.when(i == 0)
    def _():
        zero_ref[...] = jnp.zeros_like(zero_ref)

        def per_expert(e, carry):
            d0 = gap_dst_ref[e]
            code = gap_nch_ref[e]
            n = lax.shift_right_logical(code, _REM_SHIFT)

            def per_chunk(cidx, c2):
                _start_copy(zero_copy, 0, d0, cidx * CHUNK, CHUNK)
                return c2

            lax.fori_loop(0, n, per_chunk, 0)
            _start_tail(zero_copy, 0, d0, n, code & (_REM_UNITS - 1))
            return carry

        lax.fori_loop(0, N_EXPERTS, per_expert, 0)

        def per_block(b, carry):
            zero_copy(0, b * EXPERT_BLOCK, EXPERT_BLOCK).start()
            return carry

        lax.fori_loop(n_used, N_EXPERT_BLOCKS, per_block, 0)

    tile_prev = jnp.maximum(i - 1, 0)
    _segment_copies_inline(tile_prev, i > 0, seg_ref, dst_ref, nch_ref, make_copy(prev))

    route = route_ref[...]
    pos1 = route[0:1, :].astype(I32)
    pos2 = route[1:2, :].astype(I32)
    r = lax.broadcasted_iota(I32, (SORT_ROWS, t), 0)
    perm = jnp.where((r == pos1) | (r == pos2), 1.0, 0.0).astype(BF16)
    _store_words(buf_ref.at[slot], _pack_pairs(_dot(perm, h_ref[...])))

    @pl.when((i > 0) & (nmax_ref[tile_prev] > _INLINE_CHUNKS))
    def _():
        _segment_copies_loop(tile_prev, _INLINE_CHUNKS, False, seg_ref, dst_ref,
                             nch_ref, make_copy(prev))

    @pl.when(i > 1)
    def _():
        _wait_copies(ntot_ref[jnp.maximum(i - 2, 0)], make_copy(prev2))

    @pl.when(i == last)
    def _():
        _segment_copies_loop(i, 0, True, seg_ref, dst_ref, nch_ref, make_copy(slot))
        _wait_copies(ntot_ref[last - 1], make_copy(prev))
        _wait_copies(ntot_ref[last], make_copy(slot))
        _wait_copies(misc_ref[1], zero_copy)
        _wait_each(N_EXPERT_BLOCKS - n_used, lambda: zero_copy(0, 0, EXPERT_BLOCK))


def _dispatch(plan, route, h):
    t = MOE_TILE
    return pl.pallas_call(
        _dispatch_kernel,
        out_shape=jax.ShapeDtypeStruct((SORTED_ROWS * WORD_ROWS, LANES), I32),
        grid_spec=pltpu.PrefetchScalarGridSpec(
            num_scalar_prefetch=8,
            grid=(N_MOE_TILES,),
            in_specs=[
                pl.BlockSpec((SUBLANES, t), lambda i, *_: (0, i)),
                pl.BlockSpec((t, D_MODEL), lambda i, *_: (i, 0)),
            ],
            out_specs=pl.BlockSpec(memory_space=pl.ANY),
            scratch_shapes=[
                pltpu.VMEM((_DISPATCH_BUFS, SORT_ROWS * WORD_ROWS, LANES), I32),
                pltpu.VMEM((EXPERT_BLOCK * WORD_ROWS, LANES), I32),
                pltpu.SemaphoreType.DMA((_DISPATCH_BUFS,)),
                pltpu.SemaphoreType.DMA,
            ],
        ),
        compiler_params=_cparams(("arbitrary",)),
        name="moe_dispatch",
    )(plan["seg"], plan["dst"], plan["nch"], plan["ntot"], plan["nmax"],
      plan["gap_dst"], plan["gap_nch"], plan["misc"], route, h)


_WEIGHT_DMA_PRIORITY = 1


def _expert_kernel(blk_e_ref, first_ref, next_ref, misc_ref, x_ref, wg_hbm, wu_hbm,
                   wd_hbm, y_ref, wg_f, wu_f, wd_f, wg_b, wu_b, wd_b, sem, *, layer):
    blk0 = pl.program_id(0) * _BLOCKS_PER_STEP
    blk1 = blk0 + 1
    n_used = misc_ref[0]

    def weight_copies(e):
        return (pltpu.make_async_copy(wg_hbm.at[layer, e], wg_f, sem.at[0]),
                pltpu.make_async_copy(wu_hbm.at[layer, e], wu_f, sem.at[1]),
                pltpu.make_async_copy(wd_hbm.at[layer, e], wd_f, sem.at[2]))

    def take_weights(e):
        for cp in weight_copies(e):
            cp.wait()
        wg_b[...] = wg_f[...].astype(BF16)
        wu_b[...] = wu_f[...].astype(BF16)
        wd_b[...] = wd_f[...].astype(BF16)

        @pl.when(next_ref[e] >= 0)
        def _():
            for cp in weight_copies(next_ref[e]):
                cp.start(priority=_WEIGHT_DMA_PRIORITY)

    def run_rows(first, rows):
        x_lo, x_hi = _unpack_pairs(_load_words(x_ref, rows, first))
        x = jnp.concatenate([x_lo, x_hi], axis=1)
        g = _dot(x, wg_b[...])
        u = _dot(x, wu_b[...])
        a = (g * _sigmoid(g) * u).astype(BF16)
        y = _dot(a, wd_b[...])
        _store_words(y_ref, _pack_pairs(y.astype(BF16).astype(F32)), first)

    @pl.when(blk0 == 0)
    def _():
        for cp in weight_copies(blk_e_ref[0]):
            cp.start(priority=_WEIGHT_DMA_PRIORITY)

    @pl.when(blk0 < n_used)
    def _():
        e0 = blk_e_ref[blk0]
        e1 = blk_e_ref[blk1]
        pl.when(blk0 == first_ref[e0])(functools.partial(take_weights, e0))

        @pl.when(e0 == e1)
        def _():
            run_rows(0, _BLOCKS_PER_STEP * EXPERT_BLOCK)

        @pl.when(e0 != e1)
        def _():
            run_rows(0, EXPERT_BLOCK)
            take_weights(e1)
            run_rows(EXPERT_BLOCK, EXPERT_BLOCK)

    @pl.when(blk0 >= n_used)
    def _():
        y_ref[...] = jnp.zeros_like(y_ref)


def _experts(plan, xs, layer, w_gate, w_up, w_down):
    step_rows = _BLOCKS_PER_STEP * EXPERT_BLOCK * WORD_ROWS
    row_map = lambda i, be, fi, nx, misc: (
        jnp.minimum(i, (misc[0] - 1) // _BLOCKS_PER_STEP), 0)
    out_map = lambda i, be, fi, nx, misc: (i, 0)
    return pl.pallas_call(
        functools.partial(_expert_kernel, layer=layer),
        out_shape=jax.ShapeDtypeStruct((SORTED_ROWS * WORD_ROWS, LANES), I32),
        grid_spec=pltpu.PrefetchScalarGridSpec(
            num_scalar_prefetch=4,
            grid=(N_EXPERT_BLOCKS // _BLOCKS_PER_STEP,),
            in_specs=[
                pl.BlockSpec((step_rows, LANES), row_map),
                pl.BlockSpec(memory_space=pl.ANY),
                pl.BlockSpec(memory_space=pl.ANY),
                pl.BlockSpec(memory_space=pl.ANY),
            ],
            out_specs=pl.BlockSpec((step_rows, LANES), out_map),
            scratch_shapes=[
                pltpu.VMEM((D_MODEL, D_FF), F32),
                pltpu.VMEM((D_MODEL, D_FF), F32),
                pltpu.VMEM((D_FF, D_MODEL), F32),
                pltpu.VMEM((D_MODEL, D_FF), BF16),
                pltpu.VMEM((D_MODEL, D_FF), BF16),
                pltpu.VMEM((D_FF, D_MODEL), BF16),
                pltpu.SemaphoreType.DMA((3,)),
            ],
        ),
        compiler_params=_cparams(("arbitrary",)),
        name="moe_experts",
    )(plan["blk_e"], plan["first_blk"], plan["next_e"], plan["misc"], xs,
      w_gate, w_up, w_down)


_COMBINE_BUFS = 3


def _combine_kernel(seg_ref, dst_ref, nch_ref, ntot_ref, nmax_ref, route_ref, x_ref,
                    mod_ref, fg_ref, ys_ref, out_ref, *scratch, final_norm):
    bufs, sem = scratch[:_COMBINE_BUFS], scratch[_COMBINE_BUFS]
    t = MOE_TILE
    i = pl.program_id(0)
    ahead = _COMBINE_BUFS - 1
    slot = i % _COMBINE_BUFS
    tile_next = jnp.minimum(i + ahead, N_MOE_TILES - 1)
    has_next = i + ahead < N_MOE_TILES

    def make_copy(which):
        def mk(s, d, tokens):
            return pltpu.make_async_copy(_token_rows(ys_ref, d, tokens),
                                         _token_rows(bufs[which], s, tokens),
                                         sem.at[which])
        return mk

    @pl.when(i == 0)
    def _():
        for buf in bufs:
            buf[...] = jnp.zeros_like(buf)
        for k in range(_COMBINE_BUFS - 1):
            _segment_copies_loop(k, 0, True, seg_ref, dst_ref, nch_ref, make_copy(k))

    def unsort_tile(cur):
        nxt = (cur + ahead) % _COMBINE_BUFS
        _wait_copies(ntot_ref[i], make_copy(cur))
        _segment_copies_inline(tile_next, has_next, seg_ref, dst_ref, nch_ref,
                               make_copy(nxt))
        route = route_ref[...]
        route_t = jnp.concatenate(
            [route, jnp.zeros((LANES - SUBLANES, t), F32)], axis=0).T
        pos1 = route_t[:, 0:1].astype(I32)
        pos2 = route_t[:, 1:2].astype(I32)
        w1 = route_t[:, 2:3]
        w2 = route_t[:, 3:4]
        c = lax.broadcasted_iota(I32, (t, SORT_ROWS), 1)
        unsort = (jnp.where(c == pos1, w1, 0.0)
                  + jnp.where(c == pos2, w2, 0.0)).astype(BF16)
        y_lo, y_hi = _unpack_pairs(_load_words(bufs[cur], SORT_ROWS))
        moe = jnp.concatenate([_dot(unsort, y_lo), _dot(unsort, y_hi)], axis=1)
        gate = mod_ref[0][:, 2 * D_MODEL:3 * D_MODEL]
        x_new = x_ref[...] + gate * moe
        if final_norm:
            ms = jnp.mean(x_new * x_new, axis=-1, keepdims=True)
            x_new = (x_new * lax.rsqrt(ms + EPS)) * fg_ref[...]
        out_ref[...] = x_new

        @pl.when(has_next & (nmax_ref[tile_next] > _INLINE_CHUNKS))
        def _():
            _segment_copies_loop(tile_next, _INLINE_CHUNKS, False, seg_ref, dst_ref,
                                 nch_ref, make_copy(nxt))

    for cur in range(_COMBINE_BUFS):
        pl.when(slot == cur)(functools.partial(unsort_tile, cur))


def _combine(plan, route, x, mod, final_g, ys, final_norm):
    t = MOE_TILE
    tiles_per_seq = SEQ // t
    return pl.pallas_call(
        functools.partial(_combine_kernel, final_norm=final_norm),
        out_shape=jax.ShapeDtypeStruct((N_TOK, D_MODEL), F32),
        grid_spec=pltpu.PrefetchScalarGridSpec(
            num_scalar_prefetch=5,
            grid=(N_MOE_TILES,),
            in_specs=[
                pl.BlockSpec((SUBLANES, t), lambda i, *_: (0, i)),
                pl.BlockSpec((t, D_MODEL), lambda i, *_: (i, 0)),
                pl.BlockSpec((1, 1, 3 * D_MODEL), lambda i, *_: (i // tiles_per_seq, 0, 0)),
                pl.BlockSpec((1, D_MODEL), lambda i, *_: (0, 0)),
                pl.BlockSpec(memory_space=pl.ANY),
            ],
            out_specs=pl.BlockSpec((t, D_MODEL), lambda i, *_: (i, 0)),
            scratch_shapes=(
                [pltpu.VMEM((SORT_ROWS * WORD_ROWS, LANES), I32)] * _COMBINE_BUFS
                + [pltpu.SemaphoreType.DMA((_COMBINE_BUFS,))]),
        ),
        compiler_params=_cparams(("arbitrary",)),
        name="moe_combine",
    )(plan["seg"], plan["dst"], plan["nch"], plan["ntot"], plan["nmax"], route, x,
      mod, final_g.reshape(1, D_MODEL).astype(F32), ys)


def _moe_layer(x, norm_g, mod, final_g, w_group, b_group, w_expert, b_expert,
               layer, w_gate, w_up, w_down, final_norm):
    h, route, cnt = _route(x, norm_g, mod, w_group, b_group, w_expert, b_expert)
    plan = _dispatch_plan(cnt[:, :, 0])
    xs = _dispatch(plan, route, h)
    ys = _experts(plan, xs, layer, w_gate, w_up, w_down)
    return _combine(plan, route, x, mod, final_g, ys, final_norm)


def kernel(x, c, ada_w, ada_b, norm_g, final_g, m_w_in, m_b_gates, m_norm_g, m_w_out, s_w_in, s_conv_w, s_w_out, r_w_group, r_b_group, r_w_expert, r_b_expert, e_w_gate, e_w_up, e_w_down):
    mods = _ada_mods(c, ada_w, ada_b)
    xt = x.reshape(N_TOK, D_MODEL)
    for i in range(DEPTH):
        mod_mix = mods[2 * i].reshape(BATCH, 1, 3 * D_MODEL)
        mod_ffn = mods[2 * i + 1].reshape(BATCH, 1, 3 * D_MODEL)
        j = i // 2
        if i % 2 == 0:
            qt, k, vt, ot, gates, gatest = _mlstm_in(xt, norm_g[i, 0], mod_mix,
                                                     m_w_in[j], m_b_gates[j])
            xt = _mlstm_rec(xt, mod_mix, qt, k, vt, ot, gates, gatest,
                            m_norm_g[j], m_w_out[j])
        else:
            xt = _conv_layer(xt, norm_g[i, 0], mod_mix, s_w_in[j], s_conv_w[j],
                             s_w_out[j])
        xt = _moe_layer(xt, norm_g[i, 1], mod_ffn, final_g, r_w_group[i],
                        r_b_group[i], r_w_expert[i], r_b_expert[i], i, e_w_gate,
                        e_w_up, e_w_down, final_norm=(i == DEPTH - 1))
    return xt.reshape(BATCH, SEQ, D_MODEL)
```

```python
import functools

import jax
import jax.numpy as jnp
from jax import lax
from jax.experimental import pallas as pl
from jax.experimental.pallas import tpu as pltpu

F32 = jnp.float32
BF16 = jnp.bfloat16
I32 = jnp.int32

D_MODEL = 1024
BATCH = 8
SEQ = 2048
DEPTH = 4
N_TOK = BATCH * SEQ
M_HEADS = 4
M_DK = 128
M_DV = 256
M_QK = M_HEADS * M_DK
M_V = M_HEADS * M_DV
CONV_K = 3
N_GROUPS = 4
E_PER_GROUP = 8
N_EXPERTS = N_GROUPS * E_PER_GROUP
TOP_K = 2
D_FF = 512
EPS = 1e-6

SUBLANES = 8
LANES = 128

ROW_TILE = 1024
MLSTM_CHUNK = 256
MOE_TILE = 256
EXPERT_BLOCK = 256
WORD_ROWS = (D_MODEL // 2) // LANES
SEG_ALIGN = SUBLANES // WORD_ROWS
CHUNK = 8
SORT_ROWS = -(-(MOE_TILE * TOP_K + N_EXPERTS * (SEG_ALIGN - 1)) // LANES) * LANES
N_MOE_TILES = N_TOK // MOE_TILE
_MAX_SORTED = (N_MOE_TILES * (MOE_TILE * TOP_K + N_EXPERTS * (SEG_ALIGN - 1))
               + N_EXPERTS * (EXPERT_BLOCK - 1))
_BLOCKS_PER_STEP = 2
N_EXPERT_BLOCKS = (-(-_MAX_SORTED // (EXPERT_BLOCK * _BLOCKS_PER_STEP))
                   * _BLOCKS_PER_STEP)
SORTED_ROWS = N_EXPERT_BLOCKS * EXPERT_BLOCK

VMEM_LIMIT = 48 * 1024 * 1024


def _cparams(sem):
    return pltpu.CompilerParams(dimension_semantics=sem,
                                vmem_limit_bytes=VMEM_LIMIT)


def _dot(a, b):
    return jnp.dot(a, b, preferred_element_type=F32)


def _dot_nt(a, b):
    return lax.dot_general(a, b, (((1,), (1,)), ((), ())),
                           preferred_element_type=F32)


def _split3(x):
    hi = x.astype(BF16)
    r1 = x - hi.astype(F32)
    mid = r1.astype(BF16)
    lo = (r1 - mid.astype(F32)).astype(BF16)
    return hi, mid, lo


def _dot_sel_left(sel, x):
    hi, mid, lo = _split3(x)
    return _dot(sel, hi) + _dot(sel, mid) + _dot(sel, lo)


def _dot_sel_right(x, sel):
    hi, mid, lo = _split3(x)
    return _dot(hi, sel) + _dot(mid, sel) + _dot(lo, sel)


def _sigmoid(x):
    return 1.0 / (1.0 + jnp.exp(-x))


def _rms_mod(x, g, mod):
    ms = jnp.mean(x * x, axis=-1, keepdims=True)
    y = (x * lax.rsqrt(ms + EPS)) * g
    return y * (1.0 + mod[:, D_MODEL:2 * D_MODEL]) + mod[:, 0:D_MODEL]


def _ada_kernel(c_ref, w_ref, b_ref, o_ref):
    c = c_ref[...]
    s = (c * _sigmoid(c)).astype(BF16)
    o_ref[0] = _dot(s, w_ref[0].astype(BF16)) + b_ref[0]


def _ada_mods(c, ada_w, ada_b):
    n_pairs = DEPTH * 2
    w = ada_w.reshape(n_pairs, D_MODEL, 3 * D_MODEL)
    b = ada_b.reshape(n_pairs, 1, 3 * D_MODEL)
    col = D_MODEL
    return pl.pallas_call(
        _ada_kernel,
        out_shape=jax.ShapeDtypeStruct((n_pairs, BATCH, 3 * D_MODEL), F32),
        grid=(n_pairs, 3 * D_MODEL // col),
        in_specs=[
            pl.BlockSpec((BATCH, D_MODEL), lambda p, j: (0, 0)),
            pl.BlockSpec((1, D_MODEL, col), lambda p, j: (p, 0, j)),
            pl.BlockSpec((1, 1, col), lambda p, j: (p, 0, j)),
        ],
        out_specs=pl.BlockSpec((1, BATCH, col), lambda p, j: (p, 0, j)),
        compiler_params=_cparams(("arbitrary", "arbitrary")),
        name="ada_mods",
    )(c, w, b)


def _mlstm_in_kernel(x_ref, g_ref, mod_ref, wqt_ref, wk_ref, wvt_ref, wot_ref,
                     wg_ref, bg_ref, qt_ref, k_ref, vt_ref, ot_ref, gates_ref,
                     gatest_ref):
    h = _rms_mod(x_ref[...], g_ref[...], mod_ref[0])
    hb = h.astype(BF16)
    qt_ref[...] = (_dot_nt(wqt_ref[...], hb) * (M_DK ** -0.5)).astype(BF16)
    k_ref[...] = _dot(hb, wk_ref[...]).astype(BF16)
    vt_ref[...] = _dot_nt(wvt_ref[...], hb).astype(BF16)
    ot_ref[...] = _dot_nt(wot_ref[...], hb).astype(BF16)
    g = _dot(hb, wg_ref[...]) + bg_ref[...]
    log_sig = jnp.minimum(g, 0.0) - jnp.log(1.0 + jnp.exp(-jnp.abs(g)))
    lane = lax.broadcasted_iota(I32, g.shape, 1)
    gg = jnp.where(lane < M_HEADS, g, log_sig)
    gates_ref[...] = gg
    gatest_ref[...] = gg.T[0:SUBLANES, :]


def _mlstm_in(x, norm_g, mod, w_in, b_gates):
    t = ROW_TILE
    tiles_per_seq = SEQ // t
    wqt = w_in[:, 0:M_QK].T.astype(BF16)
    wk = w_in[:, M_QK:2 * M_QK].astype(BF16)
    wvt = w_in[:, 2 * M_QK:2 * M_QK + M_V].T.astype(BF16)
    wot = w_in[:, 2 * M_QK + M_V:2 * M_QK + 2 * M_V].T.astype(BF16)
    n_gate = 2 * M_HEADS
    wg = jnp.pad(w_in[:, 2 * M_QK + 2 * M_V:], ((0, 0), (0, LANES - n_gate))).astype(BF16)
    bg = jnp.pad(b_gates.astype(F32), (0, LANES - n_gate)).reshape(1, LANES)
    full = lambda shape: pl.BlockSpec(shape, lambda i: (0, 0))
    return pl.pallas_call(
        _mlstm_in_kernel,
        out_shape=(
            jax.ShapeDtypeStruct((M_QK, N_TOK), BF16),
            jax.ShapeDtypeStruct((N_TOK, M_QK), BF16),
            jax.ShapeDtypeStruct((M_V, N_TOK), BF16),
            jax.ShapeDtypeStruct((M_V, N_TOK), BF16),
            jax.ShapeDtypeStruct((N_TOK, LANES), F32),
            jax.ShapeDtypeStruct((SUBLANES, N_TOK), F32),
        ),
        grid=(N_TOK // t,),
        in_specs=[
            pl.BlockSpec((t, D_MODEL), lambda i: (i, 0)),
            full((1, D_MODEL)),
            pl.BlockSpec((1, 1, 3 * D_MODEL), lambda i: (i // tiles_per_seq, 0, 0)),
            full((M_QK, D_MODEL)),
            full((D_MODEL, M_QK)),
            full((M_V, D_MODEL)),
            full((M_V, D_MODEL)),
            full((D_MODEL, LANES)),
            full((1, LANES)),
        ],
        out_specs=(
            pl.BlockSpec((M_QK, t), lambda i: (0, i)),
            pl.BlockSpec((t, M_QK), lambda i: (i, 0)),
            pl.BlockSpec((M_V, t), lambda i: (0, i)),
            pl.BlockSpec((M_V, t), lambda i: (0, i)),
            pl.BlockSpec((t, LANES), lambda i: (i, 0)),
            pl.BlockSpec((SUBLANES, t), lambda i: (0, i)),
        ),
        compiler_params=_cparams(("arbitrary",)),
        name="mlstm_in",
    )(x, norm_g.reshape(1, D_MODEL), mod, wqt, wk, wvt, wot, wg, bg)


def _mlstm_rec_kernel(qt_ref, k_ref, vt_ref, ot_ref, gates_ref, gatest_ref,
                      x_ref, mod_ref, ng_ref, wout_ref, out_ref, ct_ref, m_ref):
    L = MLSTM_CHUNK

    @pl.when(pl.program_id(1) == 0)
    def _():
        ct_ref[...] = jnp.zeros_like(ct_ref)
        m_ref[...] = jnp.zeros_like(m_ref)

    gates = gates_ref[...]
    gatest = gatest_ref[...]
    row = lax.broadcasted_iota(I32, (L, L), 0)
    col = lax.broadcasted_iota(I32, (L, L), 1)
    tri_low = jnp.where(row >= col, 1.0, 0.0).astype(BF16)
    tri_up = jnp.where(row <= col, 1.0, 0.0).astype(BF16)
    cum_cols = _dot_sel_left(tri_low, gates)
    cum_rows = _dot_sel_right(gatest, tri_up)

    col_term = gates - pltpu.roll(cum_cols, LANES - M_HEADS, axis=1)
    col_pieces = jnp.concatenate(_split3(col_term), axis=1)
    sel_row = lax.broadcasted_iota(I32, (3 * LANES, L), 0)
    ng_wide = jnp.concatenate([ng_ref[...]] * (L // LANES), axis=1)

    hs = []
    for h in range(M_HEADS):
        qt = qt_ref[h * M_DK:(h + 1) * M_DK, :]
        kh = k_ref[:, h * M_DK:(h + 1) * M_DK]
        vt = vt_ref[h * M_DV:(h + 1) * M_DV, :]
        state = ct_ref[h]
        ig_row = gatest[h:h + 1, :]
        bcum_row = cum_rows[M_HEADS + h:M_HEADS + h + 1, :]
        m_prev = m_ref[h:h + 1, 0:1]

        pick = jnp.where((sel_row == h) | (sel_row == LANES + h)
                         | (sel_row == 2 * LANES + h), 1.0, 0.0).astype(BF16)
        dlog = jnp.where(row <= col, _dot(col_pieces, pick) + bcum_row, -jnp.inf)
        inter_log = bcum_row + m_prev
        m_t = jnp.maximum(inter_log, jnp.max(dlog, axis=0, keepdims=True))
        w_intra = jnp.exp(dlog - m_t)
        w_inter = jnp.exp(inter_log - m_t)
        scores = _dot(kh, qt) * w_intra
        q_state = _dot(state.astype(BF16), qt)
        num = _dot(vt, scores.astype(BF16)) + w_inter * q_state[0:M_DV, :]
        den = (jnp.sum(scores, axis=0, keepdims=True)
               + w_inter * q_state[M_DV:M_DV + 1, :])
        h_out = num / jnp.maximum(jnp.abs(den), jnp.exp(-m_t))

        b_last = bcum_row[:, L - 1:L]
        log_src = b_last - bcum_row + ig_row
        m_new = jnp.maximum(b_last + m_prev,
                            jnp.max(log_src, axis=1, keepdims=True))
        w_src = jnp.exp(log_src - m_new)
        decay = jnp.exp(b_last + m_prev - m_new)
        vt_w = jnp.concatenate(
            [(vt.astype(F32) * w_src).astype(BF16),
             jnp.broadcast_to(w_src, (SUBLANES, L)).astype(BF16)], axis=0)
        ct_ref[h] = decay * state + _dot(vt_w, kh)
        m_ref[h:h + 1, :] = jnp.broadcast_to(m_new, (1, LANES))

        hn = h_out * lax.rsqrt(jnp.mean(h_out * h_out, axis=0, keepdims=True) + EPS)
        og = ot_ref[h * M_DV:(h + 1) * M_DV, :].astype(F32)
        hs.append((hn * ng_wide[h * M_DV:(h + 1) * M_DV, :] * _sigmoid(og)).astype(BF16))

    hs_t = jnp.concatenate(hs, axis=0)
    y = lax.dot_general(hs_t, wout_ref[...], (((0,), (0,)), ((), ())),
                        preferred_element_type=F32)
    gate = mod_ref[0][:, 2 * D_MODEL:3 * D_MODEL]
    out_ref[...] = x_ref[...] + gate * y


def _mlstm_rec(x, mod, qt, k, vt, ot, gates, gatest, m_norm_g, w_out):
    L = MLSTM_CHUNK
    nc = SEQ // L
    rows = lambda width: pl.BlockSpec((L, width), lambda b, j: (b * nc + j, 0))
    cols = lambda height: pl.BlockSpec((height, L), lambda b, j: (0, b * nc + j))
    ng = jnp.broadcast_to(m_norm_g.astype(F32).reshape(M_V, 1), (M_V, LANES))
    return pl.pallas_call(
        _mlstm_rec_kernel,
        out_shape=jax.ShapeDtypeStruct((N_TOK, D_MODEL), F32),
        grid=(BATCH, nc),
        in_specs=[
            cols(M_QK), rows(M_QK), cols(M_V), cols(M_V), rows(LANES),
            cols(SUBLANES), rows(D_MODEL),
            pl.BlockSpec((1, 1, 3 * D_MODEL), lambda b, j: (b, 0, 0)),
            pl.BlockSpec((M_V, LANES), lambda b, j: (0, 0)),
            pl.BlockSpec((M_V, D_MODEL), lambda b, j: (0, 0)),
        ],
        out_specs=rows(D_MODEL),
        scratch_shapes=[
            pltpu.VMEM((M_HEADS, M_DV + SUBLANES, M_DK), F32),
            pltpu.VMEM((SUBLANES, LANES), F32),
        ],
        compiler_params=_cparams(("arbitrary", "arbitrary")),
        name="mlstm_rec",
    )(qt, k, vt, ot, gates, gatest, x, mod, ng, w_out.astype(BF16))


_CONV_COLS = 256


def _conv_kernel(x_ref, g_ref, mod_ref, win_ref, cw_ref, wout_ref, out_ref,
                 carry_ref, z_ref):
    t = ROW_TILE
    tiles_per_seq = SEQ // t

    @pl.when(pl.program_id(0) % tiles_per_seq == 0)
    def _():
        carry_ref[...] = jnp.zeros_like(carry_ref)

    x = x_ref[...]
    mod = mod_ref[0]
    hb = _rms_mod(x, g_ref[...], mod).astype(BF16)
    row = lax.broadcasted_iota(I32, (t, _CONV_COLS), 0)
    for j in range(D_MODEL // _CONV_COLS):
        lo, hi = j * _CONV_COLS, (j + 1) * _CONV_COLS
        b_gate = _dot(hb, win_ref[:, lo:hi])
        c_gate = _dot(hb, win_ref[:, D_MODEL + lo:D_MODEL + hi])
        xb = _dot(hb, win_ref[:, 2 * D_MODEL + lo:2 * D_MODEL + hi])
        u = c_gate * xb
        prev1 = carry_ref[SUBLANES - 1:SUBLANES, lo:hi]
        prev2 = carry_ref[SUBLANES - 2:SUBLANES - 1, lo:hi]
        u1 = jnp.where(row == 0, prev1, pltpu.roll(u, 1, axis=0))
        u2 = jnp.where(row == 0, prev2,
                       jnp.where(row == 1, prev1, pltpu.roll(u, 2, axis=0)))
        y = (cw_ref[0:1, lo:hi] * u2 + cw_ref[1:2, lo:hi] * u1
             + cw_ref[2:3, lo:hi] * u)
        z_ref[:, lo:hi] = (b_gate * y).astype(BF16)
        carry_ref[:, lo:hi] = u[t - SUBLANES:t, :]
    gate = mod[:, 2 * D_MODEL:3 * D_MODEL]
    out_ref[...] = x + gate * _dot(z_ref[...], wout_ref[...])


def _conv_layer(x, norm_g, mod, w_in, conv_w, w_out):
    t = ROW_TILE
    tiles_per_seq = SEQ // t
    full = lambda shape: pl.BlockSpec(shape, lambda i: (0, 0))
    cw = jnp.pad(conv_w.astype(F32), ((0, SUBLANES - CONV_K), (0, 0)))
    return pl.pallas_call(
        _conv_kernel,
        out_shape=jax.ShapeDtypeStruct((N_TOK, D_MODEL), F32),
        grid=(N_TOK // t,),
        in_specs=[
            pl.BlockSpec((t, D_MODEL), lambda i: (i, 0)),
            full((1, D_MODEL)),
            pl.BlockSpec((1, 1, 3 * D_MODEL), lambda i: (i // tiles_per_seq, 0, 0)),
            full((D_MODEL, 3 * D_MODEL)),
            full((SUBLANES, D_MODEL)),
            full((D_MODEL, D_MODEL)),
        ],
        out_specs=pl.BlockSpec((t, D_MODEL), lambda i: (i, 0)),
        scratch_shapes=[
            pltpu.VMEM((SUBLANES, D_MODEL), F32),
            pltpu.VMEM((t, D_MODEL), BF16),
        ],
        compiler_params=_cparams(("arbitrary",)),
        name="conv_layer",
    )(x, norm_g.reshape(1, D_MODEL), mod, w_in.astype(BF16), cw, w_out.astype(BF16))


_ROUTE_ROWS = LANES
_EXPERT_ROW0 = SUBLANES


_ROUTE_TILES = 4
_TILE_SHIFT = MOE_TILE.bit_length() - 1
_LANE_SHIFT = LANES.bit_length() - 1


def _route_kernel(x_ref, g_ref, mod_ref, wr_hi_ref, wr_lo_ref, br_ref,
                  h_ref, route_ref, cnt_ref):
    h, route, cnt = _route_tiles(x_ref[...], g_ref[...], mod_ref[0],
                                 wr_hi_ref[...], wr_lo_ref[...], br_ref[...])
    h_ref[...] = h
    route_ref[...] = route
    for j in range(_ROUTE_TILES):
        cnt_ref[j] = cnt[:, j * LANES:(j + 1) * LANES]


def _route_tiles(x, g, mod, wr_hi, wr_lo, br):
    t = MOE_TILE * _ROUTE_TILES
    h = _rms_mod(x, g, mod)
    h_hi = h.astype(BF16)
    h_lo = (h - h_hi.astype(F32)).astype(BF16)
    logits = (_dot_nt(wr_hi, h_hi) + _dot_nt(wr_hi, h_lo)
              + _dot_nt(wr_lo, h_hi)) + br

    sub = lax.broadcasted_iota(I32, (SUBLANES, t), 0)
    neg_inf = -jnp.inf
    gl = jnp.where(sub < N_GROUPS, logits[0:SUBLANES, :], neg_inf)
    gmax = jnp.max(gl, axis=0, keepdims=True)
    g_sel = jnp.min(jnp.where(gl == gmax, sub, SUBLANES), axis=0, keepdims=True)
    p_sel = 1.0 / jnp.sum(jnp.exp(gl - gmax), axis=0, keepdims=True)

    e_sel = jnp.zeros((E_PER_GROUP, t), F32)
    for g in range(N_GROUPS):
        r0 = _EXPERT_ROW0 + g * E_PER_GROUP
        e_sel = jnp.where(g_sel == g, logits[r0:r0 + E_PER_GROUP, :], e_sel)
    v1 = jnp.max(e_sel, axis=0, keepdims=True)
    i1 = jnp.min(jnp.where(e_sel == v1, sub, SUBLANES), axis=0, keepdims=True)
    e_rest = jnp.where(sub == i1, neg_inf, e_sel)
    v2 = jnp.max(e_rest, axis=0, keepdims=True)
    i2 = jnp.min(jnp.where(e_rest == v2, sub, SUBLANES), axis=0, keepdims=True)
    ratio = jnp.exp(v2 - v1)
    w1 = p_sel / (1.0 + ratio)
    w2 = p_sel * ratio / (1.0 + ratio)
    eid1 = g_sel * E_PER_GROUP + i1
    eid2 = g_sel * E_PER_GROUP + i2

    erow = lax.broadcasted_iota(I32, (N_EXPERTS, t), 0)
    m1 = erow == eid1
    m2 = erow == eid2
    member = jnp.where(m1 | m2, 1.0, 0.0)
    r = lax.broadcasted_iota(I32, (t, t), 0)
    c = lax.broadcasted_iota(I32, (t, t), 1)
    same_tile = (lax.shift_right_logical(r, _TILE_SHIFT)
                 == lax.shift_right_logical(c, _TILE_SHIFT))
    earlier = jnp.where((r < c) & same_tile, 1.0, 0.0).astype(BF16)
    member_b = member.astype(BF16)
    rank = _dot(member_b, earlier)
    tr = lax.broadcasted_iota(I32, (t, _ROUTE_TILES * LANES), 0)
    tc = lax.broadcasted_iota(I32, (t, _ROUTE_TILES * LANES), 1)
    in_tile = jnp.where(lax.shift_right_logical(tr, _TILE_SHIFT)
                        == lax.shift_right_logical(tc, _LANE_SHIFT), 1.0, 0.0)
    cnt = _dot(member_b, in_tile.astype(BF16))
    cnt_pad = jnp.floor((cnt + (SEG_ALIGN - 1.0)) * (1.0 / SEG_ALIGN)) * SEG_ALIGN
    er = lax.broadcasted_iota(I32, (N_EXPERTS, N_EXPERTS), 0)
    ec = lax.broadcasted_iota(I32, (N_EXPERTS, N_EXPERTS), 1)
    before = jnp.where(er > ec, 1.0, 0.0).astype(BF16)
    seg_start = _dot(before, cnt_pad.astype(BF16))
    seg_start_tok = jnp.concatenate(
        [seg_start[:, j * LANES:(j + 1) * LANES]
         for j in range(_ROUTE_TILES) for _ in range(MOE_TILE // LANES)], axis=1)
    pos = seg_start_tok + rank
    pos1 = jnp.sum(jnp.where(m1, pos, 0.0), axis=0, keepdims=True)
    pos2 = jnp.sum(jnp.where(m2, pos, 0.0), axis=0, keepdims=True)

    out = jnp.zeros((SUBLANES, t), F32)
    for k, val in enumerate((pos1, pos2, w1, w2)):
        out = jnp.where(sub == k, val, out)
    return h_hi, out, cnt


def _route(x, norm_g, mod, w_group, b_group, w_expert, b_expert):
    t = MOE_TILE * _ROUTE_TILES
    tiles_per_seq = SEQ // t
    wr = jnp.zeros((_ROUTE_ROWS, D_MODEL), F32)
    wr = wr.at[0:N_GROUPS].set(w_group.T.astype(F32))
    wr = wr.at[_EXPERT_ROW0:_EXPERT_ROW0 + N_EXPERTS].set(w_expert.T.astype(F32))
    wr_hi = wr.astype(BF16)
    wr_lo = (wr - wr_hi.astype(F32)).astype(BF16)
    br = jnp.zeros((_ROUTE_ROWS,), F32)
    br = br.at[0:N_GROUPS].set(b_group.astype(F32))
    br = br.at[_EXPERT_ROW0:_EXPERT_ROW0 + N_EXPERTS].set(b_expert.astype(F32))
    full = lambda shape: pl.BlockSpec(shape, lambda i: (0, 0))
    return pl.pallas_call(
        _route_kernel,
        out_shape=(
            jax.ShapeDtypeStruct((N_TOK, D_MODEL), BF16),
            jax.ShapeDtypeStruct((SUBLANES, N_TOK), F32),
            jax.ShapeDtypeStruct((N_MOE_TILES, N_EXPERTS, LANES), F32),
        ),
        grid=(N_MOE_TILES // _ROUTE_TILES,),
        in_specs=[
            pl.BlockSpec((t, D_MODEL), lambda i: (i, 0)),
            full((1, D_MODEL)),
            pl.BlockSpec((1, 1, 3 * D_MODEL), lambda i: (i // tiles_per_seq, 0, 0)),
            full((_ROUTE_ROWS, D_MODEL)),
            full((_ROUTE_ROWS, D_MODEL)),
            full((_ROUTE_ROWS, 1)),
        ],
        out_specs=(
            pl.BlockSpec((t, D_MODEL), lambda i: (i, 0)),
            pl.BlockSpec((SUBLANES, t), lambda i: (0, i)),
            pl.BlockSpec((_ROUTE_TILES, N_EXPERTS, LANES), lambda i: (i, 0, 0)),
        ),
        compiler_params=_cparams(("arbitrary",)),
        name="moe_route",
    )(x, norm_g.reshape(1, D_MODEL), mod, wr_hi, wr_lo, br.reshape(_ROUTE_ROWS, 1))


_REM_UNITS = CHUNK // SEG_ALIGN
_REM_SHIFT = _REM_UNITS.bit_length() - 1


def _chunk_code(tokens):
    return (tokens // CHUNK) * _REM_UNITS + (tokens % CHUNK) // SEG_ALIGN


def _dispatch_plan(cnt):
    cnt = cnt.astype(I32)
    cnt_pad = (cnt + SEG_ALIGN - 1) // SEG_ALIGN * SEG_ALIGN
    seg = jnp.cumsum(cnt_pad, axis=1) - cnt_pad
    tot = jnp.sum(cnt_pad, axis=0)
    ptot = (tot + EXPERT_BLOCK - 1) // EXPERT_BLOCK * EXPERT_BLOCK
    pend = jnp.cumsum(ptot)
    gbase = pend - ptot
    dst = gbase[None, :] + jnp.cumsum(cnt_pad, axis=0) - cnt_pad
    nch = _chunk_code(cnt_pad)
    n_used = (pend[-1] // EXPERT_BLOCK).astype(I32)
    blk = jnp.arange(N_EXPERT_BLOCKS, dtype=I32)
    blk_start = jnp.minimum(blk, n_used - 1) * EXPERT_BLOCK
    blk_e = jnp.sum((pend[None, :] <= blk_start[:, None]).astype(I32), axis=1)
    blk_e = jnp.minimum(blk_e, N_EXPERTS - 1).astype(I32)
    gap_dst = gbase + tot
    gap_nch = _chunk_code(ptot - tot)
    misc = jnp.stack([n_used, jnp.sum(ptot - tot) // SEG_ALIGN]).astype(I32)
    first_blk = gbase // EXPERT_BLOCK
    ids = jnp.arange(N_EXPERTS, dtype=I32)
    later = (ids[None, :] > ids[:, None]) & (ptot[None, :] > 0)
    next_e = jnp.min(jnp.where(later, ids[None, :], N_EXPERTS), axis=1)
    next_e = jnp.where(next_e == N_EXPERTS, -1, next_e)
    return dict(seg=seg.reshape(-1).astype(I32), dst=dst.reshape(-1).astype(I32),
                nch=nch.reshape(-1).astype(I32),
                ntot=(jnp.sum(cnt_pad, axis=1) // SEG_ALIGN).astype(I32),
                nmax=(jnp.max(cnt_pad, axis=1) // CHUNK).astype(I32), blk_e=blk_e,
                first_blk=first_blk.astype(I32), next_e=next_e.astype(I32),
                gap_dst=gap_dst.astype(I32), gap_nch=gap_nch.astype(I32), misc=misc)


_INLINE_CHUNKS = 3


def _start_copy(make_copy, s0, d0, offset, tokens, priority=0):
    s = pl.multiple_of(s0 + offset, SEG_ALIGN)
    d = pl.multiple_of(d0 + offset, SEG_ALIGN)
    make_copy(s, d, tokens).start(priority=priority)


def _start_tail(make_copy, s0, d0, n_full, rem, priority=0):
    half, quarter = CHUNK // 2, CHUNK // 4
    tail = n_full * CHUNK
    has_half = (rem & 2) != 0
    pl.when(has_half)(functools.partial(_start_copy, make_copy, s0, d0, tail, half,
                                        priority))
    tail2 = tail + jnp.where(has_half, half, 0)
    pl.when((rem & 1) != 0)(functools.partial(_start_copy, make_copy, s0, d0, tail2,
                                              quarter, priority))


def _segment_copies_inline(tile, live, seg_ref, dst_ref, nch_ref, make_copy):
    for e in range(N_EXPERTS):
        idx = tile * N_EXPERTS + e
        code = jnp.where(live, nch_ref[idx], 0)
        n = lax.shift_right_logical(code, _REM_SHIFT)
        s0 = seg_ref[idx]
        d0 = dst_ref[idx]
        for cidx in range(_INLINE_CHUNKS):
            pl.when(cidx < n)(functools.partial(_start_copy, make_copy, s0, d0,
                                                cidx * CHUNK, CHUNK, (e + cidx) % 2))
        _start_tail(make_copy, s0, d0, n, code & (_REM_UNITS - 1), e % 2)


def _segment_copies_loop(tile, first, with_tail, seg_ref, dst_ref, nch_ref, make_copy):
    def per_expert(e, carry):
        idx = tile * N_EXPERTS + e
        code = nch_ref[idx]
        n = lax.shift_right_logical(code, _REM_SHIFT)
        s0 = seg_ref[idx]
        d0 = dst_ref[idx]

        def per_chunk(cidx, c2):
            _start_copy(make_copy, s0, d0, cidx * CHUNK, CHUNK)
            return c2

        lax.fori_loop(first, jnp.maximum(n, first), per_chunk, 0)
        if with_tail:
            _start_tail(make_copy, s0, d0, n, code & (_REM_UNITS - 1))
        return carry

    lax.fori_loop(0, N_EXPERTS, per_expert, 0)


_WAIT_GROUP = 64


def _wait_each(n, make_wait):
    def body(_, carry):
        make_wait().wait()
        return carry
    lax.fori_loop(0, n, body, 0)


def _wait_copies(units, make_copy):
    _wait_each(units // _WAIT_GROUP, lambda: make_copy(0, 0, _WAIT_GROUP * SEG_ALIGN))
    _wait_each(units % _WAIT_GROUP, lambda: make_copy(0, 0, SEG_ALIGN))


_HALF = D_MODEL // 2
_HI_MASK = -65536


def _pack_pairs(x):
    lo = lax.shift_right_logical(lax.bitcast_convert_type(x[:, 0:_HALF], I32), 16)
    hi = lax.bitcast_convert_type(x[:, _HALF:D_MODEL], I32) & _HI_MASK
    return lo | hi


def _unpack_pairs(w):
    lo = lax.bitcast_convert_type(lax.shift_left(w, 16), F32).astype(BF16)
    hi = lax.bitcast_convert_type(w & _HI_MASK, F32).astype(BF16)
    return lo, hi


def _token_rows(ref, tok, tokens):
    start = pl.multiple_of(tok * WORD_ROWS, SUBLANES)
    return ref.at[pl.ds(start, tokens * WORD_ROWS)]


def _store_words(ref, words, first=0):
    rows = words.shape[0]
    for q in range(WORD_ROWS):
        ref[pl.ds(first * WORD_ROWS + q, rows, stride=WORD_ROWS), :] = (
            words[:, q * LANES:(q + 1) * LANES])


def _load_words(ref, rows, first=0):
    return jnp.concatenate(
        [ref[pl.ds(first * WORD_ROWS + q, rows, stride=WORD_ROWS), :]
         for q in range(WORD_ROWS)], axis=1)


_DISPATCH_BUFS = 3


def _dispatch_kernel(seg_ref, dst_ref, nch_ref, ntot_ref, nmax_ref, gap_dst_ref,
                     gap_nch_ref, misc_ref, route_ref, h_ref, xs_ref, buf_ref,
                     zero_ref, sem, zsem):
    t = MOE_TILE
    i = pl.program_id(0)
    slot = i % _DISPATCH_BUFS
    prev = (i + _DISPATCH_BUFS - 1) % _DISPATCH_BUFS
    prev2 = (i + _DISPATCH_BUFS - 2) % _DISPATCH_BUFS
    last = N_MOE_TILES - 1
    n_used = misc_ref[0]

    def make_copy(which):
        def mk(s, d, tokens):
            return pltpu.make_async_copy(_token_rows(buf_ref.at[which], s, tokens),
                                         _token_rows(xs_ref, d, tokens), sem.at[which])
        return mk

    def zero_copy(s, d, tokens):
        del s
        return pltpu.make_async_copy(_token_rows(zero_ref, 0, tokens),
                                     _token_rows(xs_ref, d, tokens), zsem)

    @pl.when(i == 0)
    def _():
        zero_ref[...] = jnp.zeros_like(zero_ref)

        def per_expert(e, carry):
            d0 = gap_dst_ref[e]
            code = gap_nch_ref[e]
            n = lax.shift_right_logical(code, _REM_SHIFT)

            def per_chunk(cidx, c2):
                _start_copy(zero_copy, 0, d0, cidx * CHUNK, CHUNK)
                return c2

            lax.fori_loop(0, n, per_chunk, 0)
            _start_tail(zero_copy, 0, d0, n, code & (_REM_UNITS - 1))
            return carry

        lax.fori_loop(0, N_EXPERTS, per_expert, 0)

        def per_block(b, carry):
            zero_copy(0, b * EXPERT_BLOCK, EXPERT_BLOCK).start()
            return carry

        lax.fori_loop(n_used, N_EXPERT_BLOCKS, per_block, 0)

    tile_prev = jnp.maximum(i - 1, 0)
    _segment_copies_inline(tile_prev, i > 0, seg_ref, dst_ref, nch_ref, make_copy(prev))

    route = route_ref[...]
    pos1 = route[0:1, :].astype(I32)
    pos2 = route[1:2, :].astype(I32)
    r = lax.broadcasted_iota(I32, (SORT_ROWS, t), 0)
    perm = jnp.where((r == pos1) | (r == pos2), 1.0, 0.0).astype(BF16)
    _store_words(buf_ref.at[slot], _pack_pairs(_dot(perm, h_ref[...])))

    @pl.when((i > 0) & (nmax_ref[tile_prev] > _INLINE_CHUNKS))
    def _():
        _segment_copies_loop(tile_prev, _INLINE_CHUNKS, False, seg_ref, dst_ref,
                             nch_ref, make_copy(prev))

    @pl.when(i > 1)
    def _():
        _wait_copies(ntot_ref[jnp.maximum(i - 2, 0)], make_copy(prev2))

    @pl.when(i == last)
    def _():
        _segment_copies_loop(i, 0, True, seg_ref, dst_ref, nch_ref, make_copy(slot))
        _wait_copies(ntot_ref[last - 1], make_copy(prev))
        _wait_copies(ntot_ref[last], make_copy(slot))
        _wait_copies(misc_ref[1], zero_copy)
        _wait_each(N_EXPERT_BLOCKS - n_used, lambda: zero_copy(0, 0, EXPERT_BLOCK))


def _dispatch(plan, route, h):
    t = MOE_TILE
    return pl.pallas_call(
        _dispatch_kernel,
        out_shape=jax.ShapeDtypeStruct((SORTED_ROWS * WORD_ROWS, LANES), I32),
        grid_spec=pltpu.PrefetchScalarGridSpec(
            num_scalar_prefetch=8,
            grid=(N_MOE_TILES,),
            in_specs=[
                pl.BlockSpec((SUBLANES, t), lambda i, *_: (0, i)),
                pl.BlockSpec((t, D_MODEL), lambda i, *_: (i, 0)),
            ],
            out_specs=pl.BlockSpec(memory_space=pl.ANY),
            scratch_shapes=[
                pltpu.VMEM((_DISPATCH_BUFS, SORT_ROWS * WORD_ROWS, LANES), I32),
                pltpu.VMEM((EXPERT_BLOCK * WORD_ROWS, LANES), I32),
                pltpu.SemaphoreType.DMA((_DISPATCH_BUFS,)),
                pltpu.SemaphoreType.DMA,
            ],
        ),
        compiler_params=_cparams(("arbitrary",)),
        name="moe_dispatch",
    )(plan["seg"], plan["dst"], plan["nch"], plan["ntot"], plan["nmax"],
      plan["gap_dst"], plan["gap_nch"], plan["misc"], route, h)


_WEIGHT_DMA_PRIORITY = 1


def _expert_kernel(blk_e_ref, first_ref, next_ref, misc_ref, x_ref, wg_hbm, wu_hbm,
                   wd_hbm, y_ref, wg_f, wu_f, wd_f, wg_b, wu_b, wd_b, sem, *, layer):
    blk0 = pl.program_id(0) * _BLOCKS_PER_STEP
    blk1 = blk0 + 1
    n_used = misc_ref[0]

    def weight_copies(e):
        return (pltpu.make_async_copy(wg_hbm.at[layer, e], wg_f, sem.at[0]),
                pltpu.make_async_copy(wu_hbm.at[layer, e], wu_f, sem.at[1]),
                pltpu.make_async_copy(wd_hbm.at[layer, e], wd_f, sem.at[2]))

    def take_weights(e):
        for cp in weight_copies(e):
            cp.wait()
        wg_b[...] = wg_f[...].astype(BF16)
        wu_b[...] = wu_f[...].astype(BF16)
        wd_b[...] = wd_f[...].astype(BF16)

        @pl.when(next_ref[e] >= 0)
        def _():
            for cp in weight_copies(next_ref[e]):
                cp.start(priority=_WEIGHT_DMA_PRIORITY)

    def run_rows(first, rows):
        x_lo, x_hi = _unpack_pairs(_load_words(x_ref, rows, first))
        x = jnp.concatenate([x_lo, x_hi], axis=1)
        g = _dot(x, wg_b[...])
        u = _dot(x, wu_b[...])
        a = (g * _sigmoid(g) * u).astype(BF16)
        y = _dot(a, wd_b[...])
        _store_words(y_ref, _pack_pairs(y.astype(BF16).astype(F32)), first)

    @pl.when(blk0 == 0)
    def _():
        for cp in weight_copies(blk_e_ref[0]):
            cp.start(priority=_WEIGHT_DMA_PRIORITY)

    @pl.when(blk0 < n_used)
    def _():
        e0 = blk_e_ref[blk0]
        e1 = blk_e_ref[blk1]
        pl.when(blk0 == first_ref[e0])(functools.partial(take_weights, e0))

        @pl.when(e0 == e1)
        def _():
            run_rows(0, _BLOCKS_PER_STEP * EXPERT_BLOCK)

        @pl.when(e0 != e1)
        def _():
            run_rows(0, EXPERT_BLOCK)
            take_weights(e1)
            run_rows(EXPERT_BLOCK, EXPERT_BLOCK)

    @pl.when(blk0 >= n_used)
    def _():
        y_ref[...] = jnp.zeros_like(y_ref)


def _experts(plan, xs, layer, w_gate, w_up, w_down):
    step_rows = _BLOCKS_PER_STEP * EXPERT_BLOCK * WORD_ROWS
    row_map = lambda i, be, fi, nx, misc: (
        jnp.minimum(i, (misc[0] - 1) // _BLOCKS_PER_STEP), 0)
    out_map = lambda i, be, fi, nx, misc: (i, 0)
    return pl.pallas_call(
        functools.partial(_expert_kernel, layer=layer),
        out_shape=jax.ShapeDtypeStruct((SORTED_ROWS * WORD_ROWS, LANES), I32),
        grid_spec=pltpu.PrefetchScalarGridSpec(
            num_scalar_prefetch=4,
            grid=(N_EXPERT_BLOCKS // _BLOCKS_PER_STEP,),
            in_specs=[
                pl.BlockSpec((step_rows, LANES), row_map),
                pl.BlockSpec(memory_space=pl.ANY),
                pl.BlockSpec(memory_space=pl.ANY),
                pl.BlockSpec(memory_space=pl.ANY),
            ],
            out_specs=pl.BlockSpec((step_rows, LANES), out_map),
            scratch_shapes=[
                pltpu.VMEM((D_MODEL, D_FF), F32),
                pltpu.VMEM((D_MODEL, D_FF), F32),
                pltpu.VMEM((D_FF, D_MODEL), F32),
                pltpu.VMEM((D_MODEL, D_FF), BF16),
                pltpu.VMEM((D_MODEL, D_FF), BF16),
                pltpu.VMEM((D_FF, D_MODEL), BF16),
                pltpu.SemaphoreType.DMA((3,)),
            ],
        ),
        compiler_params=_cparams(("arbitrary",)),
        name="moe_experts",
    )(plan["blk_e"], plan["first_blk"], plan["next_e"], plan["misc"], xs,
      w_gate, w_up, w_down)


_COMBINE_BUFS = 3


def _combine_kernel(seg_ref, dst_ref, nch_ref, ntot_ref, nmax_ref, route_ref, x_ref,
                    mod_ref, fg_ref, ys_ref, out_ref, *scratch, final_norm):
    bufs, sem = scratch[:_COMBINE_BUFS], scratch[_COMBINE_BUFS]
    t = MOE_TILE
    i = pl.program_id(0)
    ahead = _COMBINE_BUFS - 1
    slot = i % _COMBINE_BUFS
    tile_next = jnp.minimum(i + ahead, N_MOE_TILES - 1)
    has_next = i + ahead < N_MOE_TILES

    def make_copy(which):
        def mk(s, d, tokens):
            return pltpu.make_async_copy(_token_rows(ys_ref, d, tokens),
                                         _token_rows(bufs[which], s, tokens),
                                         sem.at[which])
        return mk

    @pl.when(i == 0)
    def _():
        for buf in bufs:
            buf[...] = jnp.zeros_like(buf)
        for k in range(_COMBINE_BUFS - 1):
            _segment_copies_loop(k, 0, True, seg_ref, dst_ref, nch_ref, make_copy(k))

    def unsort_tile(cur):
        nxt = (cur + ahead) % _COMBINE_BUFS
        _wait_copies(ntot_ref[i], make_copy(cur))
        _segment_copies_inline(tile_next, has_next, seg_ref, dst_ref, nch_ref,
                               make_copy(nxt))
        route = route_ref[...]
        route_t = jnp.concatenate(
            [route, jnp.zeros((LANES - SUBLANES, t), F32)], axis=0).T
        pos1 = route_t[:, 0:1].astype(I32)
        pos2 = route_t[:, 1:2].astype(I32)
        w1 = route_t[:, 2:3]
        w2 = route_t[:, 3:4]
        c = lax.broadcasted_iota(I32, (t, SORT_ROWS), 1)
        unsort = (jnp.where(c == pos1, w1, 0.0)
                  + jnp.where(c == pos2, w2, 0.0)).astype(BF16)
        y_lo, y_hi = _unpack_pairs(_load_words(bufs[cur], SORT_ROWS))
        moe = jnp.concatenate([_dot(unsort, y_lo), _dot(unsort, y_hi)], axis=1)
        gate = mod_ref[0][:, 2 * D_MODEL:3 * D_MODEL]
        x_new = x_ref[...] + gate * moe
        if final_norm:
            ms = jnp.mean(x_new * x_new, axis=-1, keepdims=True)
            x_new = (x_new * lax.rsqrt(ms + EPS)) * fg_ref[...]
        out_ref[...] = x_new

        @pl.when(has_next & (nmax_ref[tile_next] > _INLINE_CHUNKS))
        def _():
            _segment_copies_loop(tile_next, _INLINE_CHUNKS, False, seg_ref, dst_ref,
                                 nch_ref, make_copy(nxt))

    for cur in range(_COMBINE_BUFS):
        pl.when(slot == cur)(functools.partial(unsort_tile, cur))


def _combine(plan, route, x, mod, final_g, ys, final_norm):
    t = MOE_TILE
    tiles_per_seq = SEQ // t
    return pl.pallas_call(
        functools.partial(_combine_kernel, final_norm=final_norm),
        out_shape=jax.ShapeDtypeStruct((N_TOK, D_MODEL), F32),
        grid_spec=pltpu.PrefetchScalarGridSpec(
            num_scalar_prefetch=5,
            grid=(N_MOE_TILES,),
            in_specs=[
                pl.BlockSpec((SUBLANES, t), lambda i, *_: (0, i)),
                pl.BlockSpec((t, D_MODEL), lambda i, *_: (i, 0)),
                pl.BlockSpec((1, 1, 3 * D_MODEL), lambda i, *_: (i // tiles_per_seq, 0, 0)),
                pl.BlockSpec((1, D_MODEL), lambda i, *_: (0, 0)),
                pl.BlockSpec(memory_space=pl.ANY),
            ],
            out_specs=pl.BlockSpec((t, D_MODEL), lambda i, *_: (i, 0)),
            scratch_shapes=(
                [pltpu.VMEM((SORT_ROWS * WORD_ROWS, LANES), I32)] * _COMBINE_BUFS
                + [pltpu.SemaphoreType.DMA((_COMBINE_BUFS,))]),
        ),
        compiler_params=_cparams(("arbitrary",)),
        name="moe_combine",
    )(plan["seg"], plan["dst"], plan["nch"], plan["ntot"], plan["nmax"], route, x,
      mod, final_g.reshape(1, D_MODEL).astype(F32), ys)


def _moe_layer(x, norm_g, mod, final_g, w_group, b_group, w_expert, b_expert,
               layer, w_gate, w_up, w_down, final_norm):
    h, route, cnt = _route(x, norm_g, mod, w_group, b_group, w_expert, b_expert)
    plan = _dispatch_plan(cnt[:, :, 0])
    xs = _dispatch(plan, route, h)
    ys = _experts(plan, xs, layer, w_gate, w_up, w_down)
    return _combine(plan, route, x, mod, final_g, ys, final_norm)


def kernel(x, c, ada_w, ada_b, norm_g, final_g, m_w_in, m_b_gates, m_norm_g, m_w_out, s_w_in, s_conv_w, s_w_out, r_w_group, r_b_group, r_w_expert, r_b_expert, e_w_gate, e_w_up, e_w_down):
    mods = _ada_mods(c, ada_w, ada_b)
    xt = x.reshape(N_TOK, D_MODEL)
    for i in range(DEPTH):
        mod_mix = mods[2 * i].reshape(BATCH, 1, 3 * D_MODEL)
        mod_ffn = mods[2 * i + 1].reshape(BATCH, 1, 3 * D_MODEL)
        j = i // 2
        if i % 2 == 0:
            qt, k, vt, ot, gates, gatest = _mlstm_in(xt, norm_g[i, 0], mod_mix,
                                                     m_w_in[j], m_b_gates[j])
            xt = _mlstm_rec(xt, mod_mix, qt, k, vt, ot, gates, gatest,
                            m_norm_g[j], m_w_out[j])
        else:
            xt = _conv_layer(xt, norm_g[i, 0], mod_mix, s_w_in[j], s_conv_w[j],
                             s_w_out[j])
        xt = _moe_layer(xt, norm_g[i, 1], mod_ffn, final_g, r_w_group[i],
                        r_b_group[i], r_w_expert[i], r_b_expert[i], i, e_w_gate,
                        e_w_up, e_w_down, final_norm=(i == DEPTH - 1))
    return xt.reshape(BATCH, SEQ, D_MODEL)
```

```python
import functools

import jax
import jax.numpy as jnp
from jax import lax
from jax.experimental import pallas as pl
from jax.experimental.pallas import tpu as pltpu

F32 = jnp.float32
BF16 = jnp.bfloat16
I32 = jnp.int32

D_MODEL = 1024
BATCH = 8
SEQ = 2048
DEPTH = 4
N_TOK = BATCH * SEQ
M_HEADS = 4
M_DK = 128
M_DV = 256
M_QK = M_HEADS * M_DK
M_V = M_HEADS * M_DV
CONV_K = 3
N_GROUPS = 4
E_PER_GROUP = 8
N_EXPERTS = N_GROUPS * E_PER_GROUP
TOP_K = 2
D_FF = 512
EPS = 1e-6

SUBLANES = 8
LANES = 128

ROW_TILE = 1024
MLSTM_CHUNK = 256
MOE_TILE = 256
EXPERT_BLOCK = 256
WORD_ROWS = (D_MODEL // 2) // LANES
SEG_ALIGN = SUBLANES // WORD_ROWS
CHUNK = 8
SORT_ROWS = -(-(MOE_TILE * TOP_K + N_EXPERTS * (SEG_ALIGN - 1)) // LANES) * LANES
N_MOE_TILES = N_TOK // MOE_TILE
_MAX_SORTED = (N_MOE_TILES * (MOE_TILE * TOP_K + N_EXPERTS * (SEG_ALIGN - 1))
               + N_EXPERTS * (EXPERT_BLOCK - 1))
_BLOCKS_PER_STEP = 2
N_EXPERT_BLOCKS = (-(-_MAX_SORTED // (EXPERT_BLOCK * _BLOCKS_PER_STEP))
                   * _BLOCKS_PER_STEP)
SORTED_ROWS = N_EXPERT_BLOCKS * EXPERT_BLOCK

VMEM_LIMIT = 48 * 1024 * 1024


def _cparams(sem):
    return pltpu.CompilerParams(dimension_semantics=sem,
                                vmem_limit_bytes=VMEM_LIMIT)


def _dot(a, b):
    return jnp.dot(a, b, preferred_element_type=F32)


def _dot_nt(a, b):
    return lax.dot_general(a, b, (((1,), (1,)), ((), ())),
                           preferred_element_type=F32)


def _split3(x):
    hi = x.astype(BF16)
    r1 = x - hi.astype(F32)
    mid = r1.astype(BF16)
    lo = (r1 - mid.astype(F32)).astype(BF16)
    return hi, mid, lo


def _dot_sel_left(sel, x):
    hi, mid, lo = _split3(x)
    return _dot(sel, hi) + _dot(sel, mid) + _dot(sel, lo)


def _dot_sel_right(x, sel):
    hi, mid, lo = _split3(x)
    return _dot(hi, sel) + _dot(mid, sel) + _dot(lo, sel)


def _sigmoid(x):
    return 1.0 / (1.0 + jnp.exp(-x))


def _rms_mod(x, g, mod):
    ms = jnp.mean(x * x, axis=-1, keepdims=True)
    y = (x * lax.rsqrt(ms + EPS)) * g
    return y * (1.0 + mod[:, D_MODEL:2 * D_MODEL]) + mod[:, 0:D_MODEL]


def _ada_kernel(c_ref, w_ref, b_ref, o_ref):
    c = c_ref[...]
    s = (c * _sigmoid(c)).astype(BF16)
    o_ref[0] = _dot(s, w_ref[0].astype(BF16)) + b_ref[0]


def _ada_mods(c, ada_w, ada_b):
    n_pairs = DEPTH * 2
    w = ada_w.reshape(n_pairs, D_MODEL, 3 * D_MODEL)
    b = ada_b.reshape(n_pairs, 1, 3 * D_MODEL)
    col = D_MODEL
    return pl.pallas_call(
        _ada_kernel,
        out_shape=jax.ShapeDtypeStruct((n_pairs, BATCH, 3 * D_MODEL), F32),
        grid=(n_pairs, 3 * D_MODEL // col),
        in_specs=[
            pl.BlockSpec((BATCH, D_MODEL), lambda p, j: (0, 0)),
            pl.BlockSpec((1, D_MODEL, col), lambda p, j: (p, 0, j)),
            pl.BlockSpec((1, 1, col), lambda p, j: (p, 0, j)),
        ],
        out_specs=pl.BlockSpec((1, BATCH, col), lambda p, j: (p, 0, j)),
        compiler_params=_cparams(("arbitrary", "arbitrary")),
        name="ada_mods",
    )(c, w, b)


def _mlstm_in_kernel(x_ref, g_ref, mod_ref, wqt_ref, wk_ref, wvt_ref, wot_ref,
                     wg_ref, bg_ref, qt_ref, k_ref, vt_ref, ot_ref, gates_ref,
                     gatest_ref):
    h = _rms_mod(x_ref[...], g_ref[...], mod_ref[0])
    hb = h.astype(BF16)
    qt_ref[...] = (_dot_nt(wqt_ref[...], hb) * (M_DK ** -0.5)).astype(BF16)
    k_ref[...] = _dot(hb, wk_ref[...]).astype(BF16)
    vt_ref[...] = _dot_nt(wvt_ref[...], hb).astype(BF16)
    ot_ref[...] = _dot_nt(wot_ref[...], hb).astype(BF16)
    g = _dot(hb, wg_ref[...]) + bg_ref[...]
    log_sig = jnp.minimum(g, 0.0) - jnp.log(1.0 + jnp.exp(-jnp.abs(g)))
    lane = lax.broadcasted_iota(I32, g.shape, 1)
    gg = jnp.where(lane < M_HEADS, g, log_sig)
    gates_ref[...] = gg
    gatest_ref[...] = gg.T[0:SUBLANES, :]


def _mlstm_in(x, norm_g, mod, w_in, b_gates):
    t = ROW_TILE
    tiles_per_seq = SEQ // t
    wqt = w_in[:, 0:M_QK].T.astype(BF16)
    wk = w_in[:, M_QK:2 * M_QK].astype(BF16)
    wvt = w_in[:, 2 * M_QK:2 * M_QK + M_V].T.astype(BF16)
    wot = w_in[:, 2 * M_QK + M_V:2 * M_QK + 2 * M_V].T.astype(BF16)
    n_gate = 2 * M_HEADS
    wg = jnp.pad(w_in[:, 2 * M_QK + 2 * M_V:], ((0, 0), (0, LANES - n_gate))).astype(BF16)
    bg = jnp.pad(b_gates.astype(F32), (0, LANES - n_gate)).reshape(1, LANES)
    full = lambda shape: pl.BlockSpec(shape, lambda i: (0, 0))
    return pl.pallas_call(
        _mlstm_in_kernel,
        out_shape=(
            jax.ShapeDtypeStruct((M_QK, N_TOK), BF16),
            jax.ShapeDtypeStruct((N_TOK, M_QK), BF16),
            jax.ShapeDtypeStruct((M_V, N_TOK), BF16),
            jax.ShapeDtypeStruct((M_V, N_TOK), BF16),
            jax.ShapeDtypeStruct((N_TOK, LANES), F32),
            jax.ShapeDtypeStruct((SUBLANES, N_TOK), F32),
        ),
        grid=(N_TOK // t,),
        in_specs=[
            pl.BlockSpec((t, D_MODEL), lambda i: (i, 0)),
            full((1, D_MODEL)),
            pl.BlockSpec((1, 1, 3 * D_MODEL), lambda i: (i // tiles_per_seq, 0, 0)),
            full((M_QK, D_MODEL)),
            full((D_MODEL, M_QK)),
            full((M_V, D_MODEL)),
            full((M_V, D_MODEL)),
            full((D_MODEL, LANES)),
            full((1, LANES)),
        ],
        out_specs=(
            pl.BlockSpec((M_QK, t), lambda i: (0, i)),
            pl.BlockSpec((t, M_QK), lambda i: (i, 0)),
            pl.BlockSpec((M_V, t), lambda i: (0, i)),
            pl.BlockSpec((M_V, t), lambda i: (0, i)),
            pl.BlockSpec((t, LANES), lambda i: (i, 0)),
            pl.BlockSpec((SUBLANES, t), lambda i: (0, i)),
        ),
        compiler_params=_cparams(("arbitrary",)),
        name="mlstm_in",
    )(x, norm_g.reshape(1, D_MODEL), mod, wqt, wk, wvt, wot, wg, bg)


def _mlstm_rec_kernel(qt_ref, k_ref, vt_ref, ot_ref, gates_ref, gatest_ref,
                      x_ref, mod_ref, ng_ref, wout_ref, out_ref, ct_ref, m_ref):
    L = MLSTM_CHUNK

    @pl.when(pl.program_id(1) == 0)
    def _():
        ct_ref[...] = jnp.zeros_like(ct_ref)
        m_ref[...] = jnp.zeros_like(m_ref)

    gates = gates_ref[...]
    gatest = gatest_ref[...]
    row = lax.broadcasted_iota(I32, (L, L), 0)
    col = lax.broadcasted_iota(I32, (L, L), 1)
    tri_low = jnp.where(row >= col, 1.0, 0.0).astype(BF16)
    tri_up = jnp.where(row <= col, 1.0, 0.0).astype(BF16)
    cum_cols = _dot_sel_left(tri_low, gates)
    cum_rows = _dot_sel_right(gatest, tri_up)

    col_term = gates - pltpu.roll(cum_cols, LANES - M_HEADS, axis=1)
    col_pieces = jnp.concatenate(_split3(col_term), axis=1)
    sel_row = lax.broadcasted_iota(I32, (3 * LANES, L), 0)
    ng_wide = jnp.concatenate([ng_ref[...]] * (L // LANES), axis=1)

    hs = []
    for h in range(M_HEADS):
        qt = qt_ref[h * M_DK:(h + 1) * M_DK, :]
        kh = k_ref[:, h * M_DK:(h + 1) * M_DK]
        vt = vt_ref[h * M_DV:(h + 1) * M_DV, :]
        state = ct_ref[h]
        ig_row = gatest[h:h + 1, :]
        bcum_row = cum_rows[M_HEADS + h:M_HEADS + h + 1, :]
        m_prev = m_ref[h:h + 1, 0:1]

        pick = jnp.where((sel_row == h) | (sel_row == LANES + h)
                         | (sel_row == 2 * LANES + h), 1.0, 0.0).astype(BF16)
        dlog = jnp.where(row <= col, _dot(col_pieces, pick) + bcum_row, -jnp.inf)
        inter_log = bcum_row + m_prev
        m_t = jnp.maximum(inter_log, jnp.max(dlog, axis=0, keepdims=True))
        w_intra = jnp.exp(dlog - m_t)
        w_inter = jnp.exp(inter_log - m_t)
        scores = _dot(kh, qt) * w_intra
        q_state = _dot(state.astype(BF16), qt)
        num = _dot(vt, scores.astype(BF16)) + w_inter * q_state[0:M_DV, :]
        den = (jnp.sum(scores, axis=0, keepdims=True)
               + w_inter * q_state[M_DV:M_DV + 1, :])
        h_out = num / jnp.maximum(jnp.abs(den), jnp.exp(-m_t))

        b_last = bcum_row[:, L - 1:L]
        log_src = b_last - bcum_row + ig_row
        m_new = jnp.maximum(b_last + m_prev,
                            jnp.max(log_src, axis=1, keepdims=True))
        w_src = jnp.exp(log_src - m_new)
        decay = jnp.exp(b_last + m_prev - m_new)
        vt_w = jnp.concatenate(
            [(vt.astype(F32) * w_src).astype(BF16),
             jnp.broadcast_to(w_src, (SUBLANES, L)).astype(BF16)], axis=0)
        ct_ref[h] = decay * state + _dot(vt_w, kh)
        m_ref[h:h + 1, :] = jnp.broadcast_to(m_new, (1, LANES))

        hn = h_out * lax.rsqrt(jnp.mean(h_out * h_out, axis=0, keepdims=True) + EPS)
        og = ot_ref[h * M_DV:(h + 1) * M_DV, :].astype(F32)
        hs.append((hn * ng_wide[h * M_DV:(h + 1) * M_DV, :] * _sigmoid(og)).astype(BF16))

    hs_t = jnp.concatenate(hs, axis=0)
    y = lax.dot_general(hs_t, wout_ref[...], (((0,), (0,)), ((), ())),
                        preferred_element_type=F32)
    gate = mod_ref[0][:, 2 * D_MODEL:3 * D_MODEL]
    out_ref[...] = x_ref[...] + gate * y


def _mlstm_rec(x, mod, qt, k, vt, ot, gates, gatest, m_norm_g, w_out):
    L = MLSTM_CHUNK
    nc = SEQ // L
    rows = lambda width: pl.BlockSpec((L, width), lambda b, j: (b * nc + j, 0))
    cols = lambda height: pl.BlockSpec((height, L), lambda b, j: (0, b * nc + j))
    ng = jnp.broadcast_to(m_norm_g.astype(F32).reshape(M_V, 1), (M_V, LANES))
    return pl.pallas_call(
        _mlstm_rec_kernel,
        out_shape=jax.ShapeDtypeStruct((N_TOK, D_MODEL), F32),
        grid=(BATCH, nc),
        in_specs=[
            cols(M_QK), rows(M_QK), cols(M_V), cols(M_V), rows(LANES),
            cols(SUBLANES), rows(D_MODEL),
            pl.BlockSpec((1, 1, 3 * D_MODEL), lambda b, j: (b, 0, 0)),
            pl.BlockSpec((M_V, LANES), lambda b, j: (0, 0)),
            pl.BlockSpec((M_V, D_MODEL), lambda b, j: (0, 0)),
        ],
        out_specs=rows(D_MODEL),
        scratch_shapes=[
            pltpu.VMEM((M_HEADS, M_DV + SUBLANES, M_DK), F32),
            pltpu.VMEM((SUBLANES, LANES), F32),
        ],
        compiler_params=_cparams(("arbitrary", "arbitrary")),
        name="mlstm_rec",
    )(qt, k, vt, ot, gates, gatest, x, mod, ng, w_out.astype(BF16))


_CONV_COLS = 256


def _conv_kernel(x_ref, g_ref, mod_ref, win_ref, cw_ref, wout_ref, out_ref,
                 carry_ref, z_ref):
    t = ROW_TILE
    tiles_per_seq = SEQ // t

    @pl.when(pl.program_id(0) % tiles_per_seq == 0)
    def _():
        carry_ref[...] = jnp.zeros_like(carry_ref)

    x = x_ref[...]
    mod = mod_ref[0]
    hb = _rms_mod(x, g_ref[...], mod).astype(BF16)
    row = lax.broadcasted_iota(I32, (t, _CONV_COLS), 0)
    for j in range(D_MODEL // _CONV_COLS):
        lo, hi = j * _CONV_COLS, (j + 1) * _CONV_COLS
        b_gate = _dot(hb, win_ref[:, lo:hi])
        c_gate = _dot(hb, win_ref[:, D_MODEL + lo:D_MODEL + hi])
        xb = _dot(hb, win_ref[:, 2 * D_MODEL + lo:2 * D_MODEL + hi])
        u = c_gate * xb
        prev1 = carry_ref[SUBLANES - 1:SUBLANES, lo:hi]
        prev2 = carry_ref[SUBLANES - 2:SUBLANES - 1, lo:hi]
        u1 = jnp.where(row == 0, prev1, pltpu.roll(u, 1, axis=0))
        u2 = jnp.where(row == 0, prev2,
                       jnp.where(row == 1, prev1, pltpu.roll(u, 2, axis=0)))
        y = (cw_ref[0:1, lo:hi] * u2 + cw_ref[1:2, lo:hi] * u1
             + cw_ref[2:3, lo:hi] * u)
        z_ref[:, lo:hi] = (b_gate * y).astype(BF16)
        carry_ref[:, lo:hi] = u[t - SUBLANES:t, :]
    gate = mod[:, 2 * D_MODEL:3 * D_MODEL]
    out_ref[...] = x + gate * _dot(z_ref[...], wout_ref[...])


def _conv_layer(x, norm_g, mod, w_in, conv_w, w_out):
    t = ROW_TILE
    tiles_per_seq = SEQ // t
    full = lambda shape: pl.BlockSpec(shape, lambda i: (0, 0))
    cw = jnp.pad(conv_w.astype(F32), ((0, SUBLANES - CONV_K), (0, 0)))
    return pl.pallas_call(
        _conv_kernel,
        out_shape=jax.ShapeDtypeStruct((N_TOK, D_MODEL), F32),
        grid=(N_TOK // t,),
        in_specs=[
            pl.BlockSpec((t, D_MODEL), lambda i: (i, 0)),
            full((1, D_MODEL)),
            pl.BlockSpec((1, 1, 3 * D_MODEL), lambda i: (i // tiles_per_seq, 0, 0)),
            full((D_MODEL, 3 * D_MODEL)),
            full((SUBLANES, D_MODEL)),
            full((D_MODEL, D_MODEL)),
        ],
        out_specs=pl.BlockSpec((t, D_MODEL), lambda i: (i, 0)),
        scratch_shapes=[
            pltpu.VMEM((SUBLANES, D_MODEL), F32),
            pltpu.VMEM((t, D_MODEL), BF16),
        ],
        compiler_params=_cparams(("arbitrary",)),
        name="conv_layer",
    )(x, norm_g.reshape(1, D_MODEL), mod, w_in.astype(BF16), cw, w_out.astype(BF16))


_ROUTE_ROWS = LANES
_EXPERT_ROW0 = SUBLANES


_ROUTE_TILES = 4
_TILE_SHIFT = MOE_TILE.bit_length() - 1
_LANE_SHIFT = LANES.bit_length() - 1


def _route_kernel(x_ref, g_ref, mod_ref, wr_hi_ref, wr_lo_ref, br_ref,
                  h_ref, route_ref, cnt_ref):
    h, route, cnt = _route_tiles(x_ref[...], g_ref[...], mod_ref[0],
                                 wr_hi_ref[...], wr_lo_ref[...], br_ref[...])
    h_ref[...] = h
    route_ref[...] = route
    for j in range(_ROUTE_TILES):
        cnt_ref[j] = cnt[:, j * LANES:(j + 1) * LANES]


def _route_tiles(x, g, mod, wr_hi, wr_lo, br):
    t = MOE_TILE * _ROUTE_TILES
    h = _rms_mod(x, g, mod)
    h_hi = h.astype(BF16)
    h_lo = (h - h_hi.astype(F32)).astype(BF16)
    logits = (_dot_nt(wr_hi, h_hi) + _dot_nt(wr_hi, h_lo)
              + _dot_nt(wr_lo, h_hi)) + br

    sub = lax.broadcasted_iota(I32, (SUBLANES, t), 0)
    neg_inf = -jnp.inf
    gl = jnp.where(sub < N_GROUPS, logits[0:SUBLANES, :], neg_inf)
    gmax = jnp.max(gl, axis=0, keepdims=True)
    g_sel = jnp.min(jnp.where(gl == gmax, sub, SUBLANES), axis=0, keepdims=True)
    p_sel = 1.0 / jnp.sum(jnp.exp(gl - gmax), axis=0, keepdims=True)

    e_sel = jnp.zeros((E_PER_GROUP, t), F32)
    for g in range(N_GROUPS):
        r0 = _EXPERT_ROW0 + g * E_PER_GROUP
        e_sel = jnp.where(g_sel == g, logits[r0:r0 + E_PER_GROUP, :], e_sel)
    v1 = jnp.max(e_sel, axis=0, keepdims=True)
    i1 = jnp.min(jnp.where(e_sel == v1, sub, SUBLANES), axis=0, keepdims=True)
    e_rest = jnp.where(sub == i1, neg_inf, e_sel)
    v2 = jnp.max(e_rest, axis=0, keepdims=True)
    i2 = jnp.min(jnp.where(e_rest == v2, sub, SUBLANES), axis=0, keepdims=True)
    ratio = jnp.exp(v2 - v1)
    w1 = p_sel / (1.0 + ratio)
    w2 = p_sel * ratio / (1.0 + ratio)
    eid1 = g_sel * E_PER_GROUP + i1
    eid2 = g_sel * E_PER_GROUP + i2

    erow = lax.broadcasted_iota(I32, (N_EXPERTS, t), 0)
    m1 = erow == eid1
    m2 = erow == eid2
    member = jnp.where(m1 | m2, 1.0, 0.0)
    r = lax.broadcasted_iota(I32, (t, t), 0)
    c = lax.broadcasted_iota(I32, (t, t), 1)
    same_tile = (lax.shift_right_logical(r, _TILE_SHIFT)
                 == lax.shift_right_logical(c, _TILE_SHIFT))
    earlier = jnp.where((r < c) & same_tile, 1.0, 0.0).astype(BF16)
    member_b = member.astype(BF16)
    rank = _dot(member_b, earlier)
    tr = lax.broadcasted_iota(I32, (t, _ROUTE_TILES * LANES), 0)
    tc = lax.broadcasted_iota(I32, (t, _ROUTE_TILES * LANES), 1)
    in_tile = jnp.where(lax.shift_right_logical(tr, _TILE_SHIFT)
                        == lax.shift_right_logical(tc, _LANE_SHIFT), 1.0, 0.0)
    cnt = _dot(member_b, in_tile.astype(BF16))
    cnt_pad = jnp.floor((cnt + (SEG_ALIGN - 1.0)) * (1.0 / SEG_ALIGN)) * SEG_ALIGN
    er = lax.broadcasted_iota(I32, (N_EXPERTS, N_EXPERTS), 0)
    ec = lax.broadcasted_iota(I32, (N_EXPERTS, N_EXPERTS), 1)
    before = jnp.where(er > ec, 1.0, 0.0).astype(BF16)
    seg_start = _dot(before, cnt_pad.astype(BF16))
    seg_start_tok = jnp.concatenate(
        [seg_start[:, j * LANES:(j + 1) * LANES]
         for j in range(_ROUTE_TILES) for _ in range(MOE_TILE // LANES)], axis=1)
    pos = seg_start_tok + rank
    pos1 = jnp.sum(jnp.where(m1, pos, 0.0), axis=0, keepdims=True)
    pos2 = jnp.sum(jnp.where(m2, pos, 0.0), axis=0, keepdims=True)

    out = jnp.zeros((SUBLANES, t), F32)
    for k, val in enumerate((pos1, pos2, w1, w2)):
        out = jnp.where(sub == k, val, out)
    return h_hi, out, cnt


def _route(x, norm_g, mod, w_group, b_group, w_expert, b_expert):
    t = MOE_TILE * _ROUTE_TILES
    tiles_per_seq = SEQ // t
    wr = jnp.zeros((_ROUTE_ROWS, D_MODEL), F32)
    wr = wr.at[0:N_GROUPS].set(w_group.T.astype(F32))
    wr = wr.at[_EXPERT_ROW0:_EXPERT_ROW0 + N_EXPERTS].set(w_expert.T.astype(F32))
    wr_hi = wr.astype(BF16)
    wr_lo = (wr - wr_hi.astype(F32)).astype(BF16)
    br = jnp.zeros((_ROUTE_ROWS,), F32)
    br = br.at[0:N_GROUPS].set(b_group.astype(F32))
    br = br.at[_EXPERT_ROW0:_EXPERT_ROW0 + N_EXPERTS].set(b_expert.astype(F32))
    full = lambda shape: pl.BlockSpec(shape, lambda i: (0, 0))
    return pl.pallas_call(
        _route_kernel,
        out_shape=(
            jax.ShapeDtypeStruct((N_TOK, D_MODEL), BF16),
            jax.ShapeDtypeStruct((SUBLANES, N_TOK), F32),
            jax.ShapeDtypeStruct((N_MOE_TILES, N_EXPERTS, LANES), F32),
        ),
        grid=(N_MOE_TILES // _ROUTE_TILES,),
        in_specs=[
            pl.BlockSpec((t, D_MODEL), lambda i: (i, 0)),
            full((1, D_MODEL)),
            pl.BlockSpec((1, 1, 3 * D_MODEL), lambda i: (i // tiles_per_seq, 0, 0)),
            full((_ROUTE_ROWS, D_MODEL)),
            full((_ROUTE_ROWS, D_MODEL)),
            full((_ROUTE_ROWS, 1)),
        ],
        out_specs=(
            pl.BlockSpec((t, D_MODEL), lambda i: (i, 0)),
            pl.BlockSpec((SUBLANES, t), lambda i: (0, i)),
            pl.BlockSpec((_ROUTE_TILES, N_EXPERTS, LANES), lambda i: (i, 0, 0)),
        ),
        compiler_params=_cparams(("arbitrary",)),
        name="moe_route",
    )(x, norm_g.reshape(1, D_MODEL), mod, wr_hi, wr_lo, br.reshape(_ROUTE_ROWS, 1))


_REM_UNITS = CHUNK // SEG_ALIGN
_REM_SHIFT = _REM_UNITS.bit_length() - 1


def _chunk_code(tokens):
    return (tokens // CHUNK) * _REM_UNITS + (tokens % CHUNK) // SEG_ALIGN


def _dispatch_plan(cnt):
    cnt = cnt.astype(I32)
    cnt_pad = (cnt + SEG_ALIGN - 1) // SEG_ALIGN * SEG_ALIGN
    seg = jnp.cumsum(cnt_pad, axis=1) - cnt_pad
    tot = jnp.sum(cnt_pad, axis=0)
    ptot = (tot + EXPERT_BLOCK - 1) // EXPERT_BLOCK * EXPERT_BLOCK
    pend = jnp.cumsum(ptot)
    gbase = pend - ptot
    dst = gbase[None, :] + jnp.cumsum(cnt_pad, axis=0) - cnt_pad
    nch = _chunk_code(cnt_pad)
    n_used = (pend[-1] // EXPERT_BLOCK).astype(I32)
    blk = jnp.arange(N_EXPERT_BLOCKS, dtype=I32)
    blk_start = jnp.minimum(blk, n_used - 1) * EXPERT_BLOCK
    blk_e = jnp.sum((pend[None, :] <= blk_start[:, None]).astype(I32), axis=1)
    blk_e = jnp.minimum(blk_e, N_EXPERTS - 1).astype(I32)
    gap_dst = gbase + tot
    gap_nch = _chunk_code(ptot - tot)
    misc = jnp.stack([n_used, jnp.sum(ptot - tot) // SEG_ALIGN]).astype(I32)
    first_blk = gbase // EXPERT_BLOCK
    ids = jnp.arange(N_EXPERTS, dtype=I32)
    later = (ids[None, :] > ids[:, None]) & (ptot[None, :] > 0)
    next_e = jnp.min(jnp.where(later, ids[None, :], N_EXPERTS), axis=1)
    next_e = jnp.where(next_e == N_EXPERTS, -1, next_e)
    return dict(seg=seg.reshape(-1).astype(I32), dst=dst.reshape(-1).astype(I32),
                nch=nch.reshape(-1).astype(I32),
                ntot=(jnp.sum(cnt_pad, axis=1) // SEG_ALIGN).astype(I32),
                nmax=(jnp.max(cnt_pad, axis=1) // CHUNK).astype(I32), blk_e=blk_e,
                first_blk=first_blk.astype(I32), next_e=next_e.astype(I32),
                gap_dst=gap_dst.astype(I32), gap_nch=gap_nch.astype(I32), misc=misc)


_INLINE_CHUNKS = 3


def _start_copy(make_copy, s0, d0, offset, tokens, priority=0):
    s = pl.multiple_of(s0 + offset, SEG_ALIGN)
    d = pl.multiple_of(d0 + offset, SEG_ALIGN)
    make_copy(s, d, tokens).start(priority=priority)


def _start_tail(make_copy, s0, d0, n_full, rem, priority=0):
    half, quarter = CHUNK // 2, CHUNK // 4
    tail = n_full * CHUNK
    has_half = (rem & 2) != 0
    pl.when(has_half)(functools.partial(_start_copy, make_copy, s0, d0, tail, half,
                                        priority))
    tail2 = tail + jnp.where(has_half, half, 0)
    pl.when((rem & 1) != 0)(functools.partial(_start_copy, make_copy, s0, d0, tail2,
                                              quarter, priority))


def _segment_copies_inline(tile, live, seg_ref, dst_ref, nch_ref, make_copy):
    for e in range(N_EXPERTS):
        idx = tile * N_EXPERTS + e
        code = jnp.where(live, nch_ref[idx], 0)
        n = lax.shift_right_logical(code, _REM_SHIFT)
        s0 = seg_ref[idx]
        d0 = dst_ref[idx]
        for cidx in range(_INLINE_CHUNKS):
            pl.when(cidx < n)(functools.partial(_start_copy, make_copy, s0, d0,
                                                cidx * CHUNK, CHUNK, (e + cidx) % 2))
        _start_tail(make_copy, s0, d0, n, code & (_REM_UNITS - 1), e % 2)


def _segment_copies_loop(tile, first, with_tail, seg_ref, dst_ref, nch_ref, make_copy):
    def per_expert(e, carry):
        idx = tile * N_EXPERTS + e
        code = nch_ref[idx]
        n = lax.shift_right_logical(code, _REM_SHIFT)
        s0 = seg_ref[idx]
        d0 = dst_ref[idx]

        def per_chunk(cidx, c2):
            _start_copy(make_copy, s0, d0, cidx * CHUNK, CHUNK)
            return c2

        lax.fori_loop(first, jnp.maximum(n, first), per_chunk, 0)
        if with_tail:
            _start_tail(make_copy, s0, d0, n, code & (_REM_UNITS - 1))
        return carry

    lax.fori_loop(0, N_EXPERTS, per_expert, 0)


_WAIT_GROUP = 64


def _wait_each(n, make_wait):
    def body(_, carry):
        make_wait().wait()
        return carry
    lax.fori_loop(0, n, body, 0)


def _wait_copies(units, make_copy):
    _wait_each(units // _WAIT_GROUP, lambda: make_copy(0, 0, _WAIT_GROUP * SEG_ALIGN))
    _wait_each(units % _WAIT_GROUP, lambda: make_copy(0, 0, SEG_ALIGN))


_HALF = D_MODEL // 2
_HI_MASK = -65536


def _pack_pairs(x):
    lo = lax.shift_right_logical(lax.bitcast_convert_type(x[:, 0:_HALF], I32), 16)
    hi = lax.bitcast_convert_type(x[:, _HALF:D_MODEL], I32) & _HI_MASK
    return lo | hi


def _unpack_pairs(w):
    lo = lax.bitcast_convert_type(lax.shift_left(w, 16), F32).astype(BF16)
    hi = lax.bitcast_convert_type(w & _HI_MASK, F32).astype(BF16)
    return lo, hi


def _token_rows(ref, tok, tokens):
    start = pl.multiple_of(tok * WORD_ROWS, SUBLANES)
    return ref.at[pl.ds(start, tokens * WORD_ROWS)]


def _store_words(ref, words, first=0):
    rows = words.shape[0]
    for q in range(WORD_ROWS):
        ref[pl.ds(first * WORD_ROWS + q, rows, stride=WORD_ROWS), :] = (
            words[:, q * LANES:(q + 1) * LANES])


def _load_words(ref, rows, first=0):
    return jnp.concatenate(
        [ref[pl.ds(first * WORD_ROWS + q, rows, stride=WORD_ROWS), :]
         for q in range(WORD_ROWS)], axis=1)


_DISPATCH_BUFS = 3
_TILES_PER_STEP = 2


def _dispatch_kernel(seg_ref, dst_ref, nch_ref, ntot_ref, nmax_ref, gap_dst_ref,
                     gap_nch_ref, misc_ref, route_ref, h_ref, xs_ref, buf_ref,
                     zero_ref, sem, zsem):
    t = MOE_TILE
    last = N_MOE_TILES - 1
    n_used = misc_ref[0]

    def make_copy(which):
        def mk(s, d, tokens):
            return pltpu.make_async_copy(_token_rows(buf_ref.at[which], s, tokens),
                                         _token_rows(xs_ref, d, tokens), sem.at[which])
        return mk

    def zero_copy(s, d, tokens):
        del s
        return pltpu.make_async_copy(_token_rows(zero_ref, 0, tokens),
                                     _token_rows(xs_ref, d, tokens), zsem)

    @pl.when(pl.program_id(0) == 0)
    def _():
        zero_ref[...] = jnp.zeros_like(zero_ref)

        def per_expert(e, carry):
            d0 = gap_dst_ref[e]
            code = gap_nch_ref[e]
            n = lax.shift_right_logical(code, _REM_SHIFT)

            def per_chunk(cidx, c2):
                _start_copy(zero_copy, 0, d0, cidx * CHUNK, CHUNK)
                return c2

            lax.fori_loop(0, n, per_chunk, 0)
            _start_tail(zero_copy, 0, d0, n, code & (_REM_UNITS - 1))
            return carry

        lax.fori_loop(0, N_EXPERTS, per_expert, 0)

        def per_block(b, carry):
            zero_copy(0, b * EXPERT_BLOCK, EXPERT_BLOCK).start()
            return carry

        lax.fori_loop(n_used, N_EXPERT_BLOCKS, per_block, 0)

    def sort_tile(j):
        k = pl.program_id(0) * _TILES_PER_STEP + j
        slot = k % _DISPATCH_BUFS
        prev = (k + _DISPATCH_BUFS - 1) % _DISPATCH_BUFS
        prev2 = (k + _DISPATCH_BUFS - 2) % _DISPATCH_BUFS
        tile_prev = jnp.maximum(k - 1, 0)
        _segment_copies_inline(tile_prev, k > 0, seg_ref, dst_ref, nch_ref,
                               make_copy(prev))

        route = route_ref[:, j * t:(j + 1) * t]
        pos1 = route[0:1, :].astype(I32)
        pos2 = route[1:2, :].astype(I32)
        r = lax.broadcasted_iota(I32, (SORT_ROWS, t), 0)
        perm = jnp.where((r == pos1) | (r == pos2), 1.0, 0.0).astype(BF16)
        _store_words(buf_ref.at[slot],
                     _pack_pairs(_dot(perm, h_ref[j * t:(j + 1) * t, :])))

        @pl.when((k > 0) & (nmax_ref[tile_prev] > _INLINE_CHUNKS))
        def _():
            _segment_copies_loop(tile_prev, _INLINE_CHUNKS, False, seg_ref, dst_ref,
                                 nch_ref, make_copy(prev))

        @pl.when(k > 1)
        def _():
            _wait_copies(ntot_ref[jnp.maximum(k - 2, 0)], make_copy(prev2))

        @pl.when(k == last)
        def _():
            _segment_copies_loop(k, 0, True, seg_ref, dst_ref, nch_ref, make_copy(slot))
            _wait_copies(ntot_ref[last - 1], make_copy(prev))
            _wait_copies(ntot_ref[last], make_copy(slot))
            _wait_copies(misc_ref[1], zero_copy)
            _wait_each(N_EXPERT_BLOCKS - n_used, lambda: zero_copy(0, 0, EXPERT_BLOCK))

    for j in range(_TILES_PER_STEP):
        sort_tile(j)


def _dispatch(plan, route, h):
    t = MOE_TILE * _TILES_PER_STEP
    return pl.pallas_call(
        _dispatch_kernel,
        out_shape=jax.ShapeDtypeStruct((SORTED_ROWS * WORD_ROWS, LANES), I32),
        grid_spec=pltpu.PrefetchScalarGridSpec(
            num_scalar_prefetch=8,
            grid=(N_MOE_TILES // _TILES_PER_STEP,),
            in_specs=[
                pl.BlockSpec((SUBLANES, t), lambda i, *_: (0, i)),
                pl.BlockSpec((t, D_MODEL), lambda i, *_: (i, 0)),
            ],
            out_specs=pl.BlockSpec(memory_space=pl.ANY),
            scratch_shapes=[
                pltpu.VMEM((_DISPATCH_BUFS, SORT_ROWS * WORD_ROWS, LANES), I32),
                pltpu.VMEM((EXPERT_BLOCK * WORD_ROWS, LANES), I32),
                pltpu.SemaphoreType.DMA((_DISPATCH_BUFS,)),
                pltpu.SemaphoreType.DMA,
            ],
        ),
        compiler_params=_cparams(("arbitrary",)),
        name="moe_dispatch",
    )(plan["seg"], plan["dst"], plan["nch"], plan["ntot"], plan["nmax"],
      plan["gap_dst"], plan["gap_nch"], plan["misc"], route, h)


_WEIGHT_DMA_PRIORITY = 1


def _expert_kernel(blk_e_ref, first_ref, next_ref, misc_ref, x_ref, wg_hbm, wu_hbm,
                   wd_hbm, y_ref, wg_f, wu_f, wd_f, wg_b, wu_b, wd_b, sem, *, layer):
    blk0 = pl.program_id(0) * _BLOCKS_PER_STEP
    blk1 = blk0 + 1
    n_used = misc_ref[0]

    def weight_copies(e):
        return (pltpu.make_async_copy(wg_hbm.at[layer, e], wg_f, sem.at[0]),
                pltpu.make_async_copy(wu_hbm.at[layer, e], wu_f, sem.at[1]),
                pltpu.make_async_copy(wd_hbm.at[layer, e], wd_f, sem.at[2]))

    def take_weights(e):
        for cp in weight_copies(e):
            cp.wait()
        wg_b[...] = wg_f[...].astype(BF16)
        wu_b[...] = wu_f[...].astype(BF16)
        wd_b[...] = wd_f[...].astype(BF16)

        @pl.when(next_ref[e] >= 0)
        def _():
            for cp in weight_copies(next_ref[e]):
                cp.start(priority=_WEIGHT_DMA_PRIORITY)

    def run_rows(first, rows):
        x_lo, x_hi = _unpack_pairs(_load_words(x_ref, rows, first))
        x = jnp.concatenate([x_lo, x_hi], axis=1)
        g = _dot(x, wg_b[...])
        u = _dot(x, wu_b[...])
        a = (g * _sigmoid(g) * u).astype(BF16)
        y = _dot(a, wd_b[...])
        _store_words(y_ref, _pack_pairs(y.astype(BF16).astype(F32)), first)

    @pl.when(blk0 == 0)
    def _():
        for cp in weight_copies(blk_e_ref[0]):
            cp.start(priority=_WEIGHT_DMA_PRIORITY)

    @pl.when(blk0 < n_used)
    def _():
        e0 = blk_e_ref[blk0]
        e1 = blk_e_ref[blk1]
        pl.when(blk0 == first_ref[e0])(functools.partial(take_weights, e0))

        @pl.when(e0 == e1)
        def _():
            run_rows(0, _BLOCKS_PER_STEP * EXPERT_BLOCK)

        @pl.when(e0 != e1)
        def _():
            run_rows(0, EXPERT_BLOCK)
            take_weights(e1)
            run_rows(EXPERT_BLOCK, EXPERT_BLOCK)

    @pl.when(blk0 >= n_used)
    def _():
        y_ref[...] = jnp.zeros_like(y_ref)


def _experts(plan, xs, layer, w_gate, w_up, w_down):
    step_rows = _BLOCKS_PER_STEP * EXPERT_BLOCK * WORD_ROWS
    row_map = lambda i, be, fi, nx, misc: (
        jnp.minimum(i, (misc[0] - 1) // _BLOCKS_PER_STEP), 0)
    out_map = lambda i, be, fi, nx, misc: (i, 0)
    return pl.pallas_call(
        functools.partial(_expert_kernel, layer=layer),
        out_shape=jax.ShapeDtypeStruct((SORTED_ROWS * WORD_ROWS, LANES), I32),
        grid_spec=pltpu.PrefetchScalarGridSpec(
            num_scalar_prefetch=4,
            grid=(N_EXPERT_BLOCKS // _BLOCKS_PER_STEP,),
            in_specs=[
                pl.BlockSpec((step_rows, LANES), row_map),
                pl.BlockSpec(memory_space=pl.ANY),
                pl.BlockSpec(memory_space=pl.ANY),
                pl.BlockSpec(memory_space=pl.ANY),
            ],
            out_specs=pl.BlockSpec((step_rows, LANES), out_map),
            scratch_shapes=[
                pltpu.VMEM((D_MODEL, D_FF), F32),
                pltpu.VMEM((D_MODEL, D_FF), F32),
                pltpu.VMEM((D_FF, D_MODEL), F32),
                pltpu.VMEM((D_MODEL, D_FF), BF16),
                pltpu.VMEM((D_MODEL, D_FF), BF16),
                pltpu.VMEM((D_FF, D_MODEL), BF16),
                pltpu.SemaphoreType.DMA((3,)),
            ],
        ),
        compiler_params=_cparams(("arbitrary",)),
        name="moe_experts",
    )(plan["blk_e"], plan["first_blk"], plan["next_e"], plan["misc"], xs,
      w_gate, w_up, w_down)


_COMBINE_BUFS = 3


def _combine_kernel(seg_ref, dst_ref, nch_ref, ntot_ref, nmax_ref, route_ref, x_ref,
                    mod_ref, fg_ref, ys_ref, out_ref, *scratch, final_norm):
    bufs, sem = scratch[:_COMBINE_BUFS], scratch[_COMBINE_BUFS]
    t = MOE_TILE
    ahead = _COMBINE_BUFS - 1

    def make_copy(which):
        def mk(s, d, tokens):
            return pltpu.make_async_copy(_token_rows(ys_ref, d, tokens),
                                         _token_rows(bufs[which], s, tokens),
                                         sem.at[which])
        return mk

    @pl.when(pl.program_id(0) == 0)
    def _():
        for buf in bufs:
            buf[...] = jnp.zeros_like(buf)
        for k in range(_COMBINE_BUFS - 1):
            _segment_copies_loop(k, 0, True, seg_ref, dst_ref, nch_ref, make_copy(k))

    def unsort_tile(j, k, cur):
        nxt = (cur + ahead) % _COMBINE_BUFS
        tile_next = jnp.minimum(k + ahead, N_MOE_TILES - 1)
        has_next = k + ahead < N_MOE_TILES
        rows = pl.ds(j * t, t)
        _wait_copies(ntot_ref[k], make_copy(cur))
        _segment_copies_inline(tile_next, has_next, seg_ref, dst_ref, nch_ref,
                               make_copy(nxt))
        route = route_ref[:, j * t:(j + 1) * t]
        route_t = jnp.concatenate(
            [route, jnp.zeros((LANES - SUBLANES, t), F32)], axis=0).T
        pos1 = route_t[:, 0:1].astype(I32)
        pos2 = route_t[:, 1:2].astype(I32)
        w1 = route_t[:, 2:3]
        w2 = route_t[:, 3:4]
        c = lax.broadcasted_iota(I32, (t, SORT_ROWS), 1)
        unsort = (jnp.where(c == pos1, w1, 0.0)
                  + jnp.where(c == pos2, w2, 0.0)).astype(BF16)
        y_lo, y_hi = _unpack_pairs(_load_words(bufs[cur], SORT_ROWS))
        moe = jnp.concatenate([_dot(unsort, y_lo), _dot(unsort, y_hi)], axis=1)
        gate = mod_ref[0][:, 2 * D_MODEL:3 * D_MODEL]
        x_new = x_ref[rows, :] + gate * moe
        if final_norm:
            ms = jnp.mean(x_new * x_new, axis=-1, keepdims=True)
            x_new = (x_new * lax.rsqrt(ms + EPS)) * fg_ref[...]
        out_ref[rows, :] = x_new

        @pl.when(has_next & (nmax_ref[tile_next] > _INLINE_CHUNKS))
        def _():
            _segment_copies_loop(tile_next, _INLINE_CHUNKS, False, seg_ref, dst_ref,
                                 nch_ref, make_copy(nxt))

    for j in range(_TILES_PER_STEP):
        k = pl.program_id(0) * _TILES_PER_STEP + j
        for cur in range(_COMBINE_BUFS):
            pl.when(k % _COMBINE_BUFS == cur)(functools.partial(unsort_tile, j, k, cur))


def _combine(plan, route, x, mod, final_g, ys, final_norm):
    t = MOE_TILE * _TILES_PER_STEP
    tiles_per_seq = SEQ // t
    return pl.pallas_call(
        functools.partial(_combine_kernel, final_norm=final_norm),
        out_shape=jax.ShapeDtypeStruct((N_TOK, D_MODEL), F32),
        grid_spec=pltpu.PrefetchScalarGridSpec(
            num_scalar_prefetch=5,
            grid=(N_MOE_TILES // _TILES_PER_STEP,),
            in_specs=[
                pl.BlockSpec((SUBLANES, t), lambda i, *_: (0, i)),
                pl.BlockSpec((t, D_MODEL), lambda i, *_: (i, 0)),
                pl.BlockSpec((1, 1, 3 * D_MODEL), lambda i, *_: (i // tiles_per_seq, 0, 0)),
                pl.BlockSpec((1, D_MODEL), lambda i, *_: (0, 0)),
                pl.BlockSpec(memory_space=pl.ANY),
            ],
            out_specs=pl.BlockSpec((t, D_MODEL), lambda i, *_: (i, 0)),
            scratch_shapes=(
                [pltpu.VMEM((SORT_ROWS * WORD_ROWS, LANES), I32)] * _COMBINE_BUFS
                + [pltpu.SemaphoreType.DMA((_COMBINE_BUFS,))]),
        ),
        compiler_params=_cparams(("arbitrary",)),
        name="moe_combine",
    )(plan["seg"], plan["dst"], plan["nch"], plan["ntot"], plan["nmax"], route, x,
      mod, final_g.reshape(1, D_MODEL).astype(F32), ys)


def _moe_layer(x, norm_g, mod, final_g, w_group, b_group, w_expert, b_expert,
               layer, w_gate, w_up, w_down, final_norm):
    h, route, cnt = _route(x, norm_g, mod, w_group, b_group, w_expert, b_expert)
    plan = _dispatch_plan(cnt[:, :, 0])
    xs = _dispatch(plan, route, h)
    ys = _experts(plan, xs, layer, w_gate, w_up, w_down)
    return _combine(plan, route, x, mod, final_g, ys, final_norm)


def kernel(x, c, ada_w, ada_b, norm_g, final_g, m_w_in, m_b_gates, m_norm_g, m_w_out, s_w_in, s_conv_w, s_w_out, r_w_group, r_b_group, r_w_expert, r_b_expert, e_w_gate, e_w_up, e_w_down):
    mods = _ada_mods(c, ada_w, ada_b)
    xt = x.reshape(N_TOK, D_MODEL)
    for i in range(DEPTH):
        mod_mix = mods[2 * i].reshape(BATCH, 1, 3 * D_MODEL)
        mod_ffn = mods[2 * i + 1].reshape(BATCH, 1, 3 * D_MODEL)
        j = i // 2
        if i % 2 == 0:
            qt, k, vt, ot, gates, gatest = _mlstm_in(xt, norm_g[i, 0], mod_mix,
                                                     m_w_in[j], m_b_gates[j])
            xt = _mlstm_rec(xt, mod_mix, qt, k, vt, ot, gates, gatest,
                            m_norm_g[j], m_w_out[j])
        else:
            xt = _conv_layer(xt, norm_g[i, 0], mod_mix, s_w_in[j], s_conv_w[j],
                             s_w_out[j])
        xt = _moe_layer(xt, norm_g[i, 1], mod_ffn, final_g, r_w_group[i],
                        r_b_group[i], r_w_expert[i], r_b_expert[i], i, e_w_gate,
                        e_w_up, e_w_down, final_norm=(i == DEPTH - 1))
    return xt.reshape(BATCH, SEQ, D_MODEL)
```

```python
import functools

import jax
import jax.numpy as jnp
from jax import lax
from jax.experimental import pallas as pl
from jax.experimental.pallas import tpu as pltpu

F32 = jnp.float32
BF16 = jnp.bfloat16
I32 = jnp.int32

D_MODEL = 1024
BATCH = 8
SEQ = 2048
DEPTH = 4
N_TOK = BATCH * SEQ
M_HEADS = 4
M_DK = 128
M_DV = 256
M_QK = M_HEADS * M_DK
M_V = M_HEADS * M_DV
CONV_K = 3
N_GROUPS = 4
E_PER_GROUP = 8
N_EXPERTS = N_GROUPS * E_PER_GROUP
TOP_K = 2
D_FF = 512
EPS = 1e-6

SUBLANES = 8
LANES = 128

ROW_TILE = 1024
MLSTM_CHUNK = 256
MOE_TILE = 256
EXPERT_BLOCK = 256
WORD_ROWS = (D_MODEL // 2) // LANES
SEG_ALIGN = SUBLANES // WORD_ROWS
CHUNK = 8
SORT_ROWS = -(-(MOE_TILE * TOP_K + N_EXPERTS * (SEG_ALIGN - 1)) // LANES) * LANES
N_MOE_TILES = N_TOK // MOE_TILE
_MAX_SORTED = (N_MOE_TILES * (MOE_TILE * TOP_K + N_EXPERTS * (SEG_ALIGN - 1))
               + N_EXPERTS * (EXPERT_BLOCK - 1))
_BLOCKS_PER_STEP = 2
N_EXPERT_BLOCKS = (-(-_MAX_SORTED // (EXPERT_BLOCK * _BLOCKS_PER_STEP))
                   * _BLOCKS_PER_STEP)
SORTED_ROWS = N_EXPERT_BLOCKS * EXPERT_BLOCK

VMEM_LIMIT = 48 * 1024 * 1024


def _cparams(sem):
    return pltpu.CompilerParams(dimension_semantics=sem,
                                vmem_limit_bytes=VMEM_LIMIT)


def _dot(a, b):
    return jnp.dot(a, b, preferred_element_type=F32)


def _dot_nt(a, b):
    return lax.dot_general(a, b, (((1,), (1,)), ((), ())),
                           preferred_element_type=F32)


def _split3(x):
    hi = x.astype(BF16)
    r1 = x - hi.astype(F32)
    mid = r1.astype(BF16)
    lo = (r1 - mid.astype(F32)).astype(BF16)
    return hi, mid, lo


def _dot_sel_left(sel, x):
    hi, mid, lo = _split3(x)
    return _dot(sel, hi) + _dot(sel, mid) + _dot(sel, lo)


def _dot_sel_right(x, sel):
    hi, mid, lo = _split3(x)
    return _dot(hi, sel) + _dot(mid, sel) + _dot(lo, sel)


def _sigmoid(x):
    return 1.0 / (1.0 + jnp.exp(-x))


def _rms_mod(x, g, mod):
    ms = jnp.mean(x * x, axis=-1, keepdims=True)
    y = (x * lax.rsqrt(ms + EPS)) * g
    return y * (1.0 + mod[:, D_MODEL:2 * D_MODEL]) + mod[:, 0:D_MODEL]


def _ada_kernel(c_ref, w_ref, b_ref, o_ref):
    c = c_ref[...]
    s = (c * _sigmoid(c)).astype(BF16)
    o_ref[0] = _dot(s, w_ref[0].astype(BF16)) + b_ref[0]


def _ada_mods(c, ada_w, ada_b):
    n_pairs = DEPTH * 2
    w = ada_w.reshape(n_pairs, D_MODEL, 3 * D_MODEL)
    b = ada_b.reshape(n_pairs, 1, 3 * D_MODEL)
    col = D_MODEL
    return pl.pallas_call(
        _ada_kernel,
        out_shape=jax.ShapeDtypeStruct((n_pairs, BATCH, 3 * D_MODEL), F32),
        grid=(n_pairs, 3 * D_MODEL // col),
        in_specs=[
            pl.BlockSpec((BATCH, D_MODEL), lambda p, j: (0, 0)),
            pl.BlockSpec((1, D_MODEL, col), lambda p, j: (p, 0, j)),
            pl.BlockSpec((1, 1, col), lambda p, j: (p, 0, j)),
        ],
        out_specs=pl.BlockSpec((1, BATCH, col), lambda p, j: (p, 0, j)),
        compiler_params=_cparams(("arbitrary", "arbitrary")),
        name="ada_mods",
    )(c, w, b)


def _mlstm_in_kernel(x_ref, g_ref, mod_ref, wqt_ref, wk_ref, wvt_ref, wot_ref,
                     wg_ref, bg_ref, qt_ref, k_ref, vt_ref, ot_ref, gates_ref,
                     gatest_ref):
    h = _rms_mod(x_ref[...], g_ref[...], mod_ref[0])
    hb = h.astype(BF16)
    qt_ref[...] = (_dot_nt(wqt_ref[...], hb) * (M_DK ** -0.5)).astype(BF16)
    k_ref[...] = _dot(hb, wk_ref[...]).astype(BF16)
    vt_ref[...] = _dot_nt(wvt_ref[...], hb).astype(BF16)
    ot_ref[...] = _dot_nt(wot_ref[...], hb).astype(BF16)
    g = _dot(hb, wg_ref[...]) + bg_ref[...]
    log_sig = jnp.minimum(g, 0.0) - jnp.log(1.0 + jnp.exp(-jnp.abs(g)))
    lane = lax.broadcasted_iota(I32, g.shape, 1)
    gg = jnp.where(lane < M_HEADS, g, log_sig)
    gates_ref[...] = gg
    gatest_ref[...] = gg.T[0:SUBLANES, :]


def _mlstm_in(x, norm_g, mod, w_in, b_gates):
    t = ROW_TILE
    tiles_per_seq = SEQ // t
    wqt = w_in[:, 0:M_QK].T.astype(BF16)
    wk = w_in[:, M_QK:2 * M_QK].astype(BF16)
    wvt = w_in[:, 2 * M_QK:2 * M_QK + M_V].T.astype(BF16)
    wot = w_in[:, 2 * M_QK + M_V:2 * M_QK + 2 * M_V].T.astype(BF16)
    n_gate = 2 * M_HEADS
    wg = jnp.pad(w_in[:, 2 * M_QK + 2 * M_V:], ((0, 0), (0, LANES - n_gate))).astype(BF16)
    bg = jnp.pad(b_gates.astype(F32), (0, LANES - n_gate)).reshape(1, LANES)
    full = lambda shape: pl.BlockSpec(shape, lambda i: (0, 0))
    return pl.pallas_call(
        _mlstm_in_kernel,
        out_shape=(
            jax.ShapeDtypeStruct((M_QK, N_TOK), BF16),
            jax.ShapeDtypeStruct((N_TOK, M_QK), BF16),
            jax.ShapeDtypeStruct((M_V, N_TOK), BF16),
            jax.ShapeDtypeStruct((M_V, N_TOK), BF16),
            jax.ShapeDtypeStruct((N_TOK, LANES), F32),
            jax.ShapeDtypeStruct((SUBLANES, N_TOK), F32),
        ),
        grid=(N_TOK // t,),
        in_specs=[
            pl.BlockSpec((t, D_MODEL), lambda i: (i, 0)),
            full((1, D_MODEL)),
            pl.BlockSpec((1, 1, 3 * D_MODEL), lambda i: (i // tiles_per_seq, 0, 0)),
            full((M_QK, D_MODEL)),
            full((D_MODEL, M_QK)),
            full((M_V, D_MODEL)),
            full((M_V, D_MODEL)),
            full((D_MODEL, LANES)),
            full((1, LANES)),
        ],
        out_specs=(
            pl.BlockSpec((M_QK, t), lambda i: (0, i)),
            pl.BlockSpec((t, M_QK), lambda i: (i, 0)),
            pl.BlockSpec((M_V, t), lambda i: (0, i)),
            pl.BlockSpec((M_V, t), lambda i: (0, i)),
            pl.BlockSpec((t, LANES), lambda i: (i, 0)),
            pl.BlockSpec((SUBLANES, t), lambda i: (0, i)),
        ),
        compiler_params=_cparams(("arbitrary",)),
        name="mlstm_in",
    )(x, norm_g.reshape(1, D_MODEL), mod, wqt, wk, wvt, wot, wg, bg)


def _mlstm_rec_kernel(qt_ref, k_ref, vt_ref, ot_ref, gates_ref, gatest_ref,
                      x_ref, mod_ref, ng_ref, wout_ref, out_ref, ct_ref, m_ref):
    L = MLSTM_CHUNK

    @pl.when(pl.program_id(1) == 0)
    def _():
        ct_ref[...] = jnp.zeros_like(ct_ref)
        m_ref[...] = jnp.zeros_like(m_ref)

    gates = gates_ref[...]
    gatest = gatest_ref[...]
    row = lax.broadcasted_iota(I32, (L, L), 0)
    col = lax.broadcasted_iota(I32, (L, L), 1)
    tri_low = jnp.where(row >= col, 1.0, 0.0).astype(BF16)
    tri_up = jnp.where(row <= col, 1.0, 0.0).astype(BF16)
    cum_cols = _dot_sel_left(tri_low, gates)
    cum_rows = _dot_sel_right(gatest, tri_up)

    col_term = gates - pltpu.roll(cum_cols, LANES - M_HEADS, axis=1)
    head_lane = lax.broadcasted_iota(I32, col_term.shape, 1) < M_HEADS
    pieces = [jnp.where(head_lane, p.astype(F32), 0.0) for p in _split3(col_term)]
    col_pieces = (pieces[0] + pltpu.roll(pieces[1], M_HEADS, axis=1)
                  + pltpu.roll(pieces[2], 2 * M_HEADS, axis=1)).astype(BF16)
    ng_wide = jnp.concatenate([ng_ref[...]] * (L // LANES), axis=1)

    heads = range(M_HEADS)
    hl = M_HEADS * L
    qts = [qt_ref[h * M_DK:(h + 1) * M_DK, :] for h in heads]
    khs = [k_ref[:, h * M_DK:(h + 1) * M_DK] for h in heads]
    m_old = m_ref[...]
    ig_all = jnp.concatenate([gatest[h:h + 1, :] for h in heads], axis=1)
    bcum_all = jnp.concatenate(
        [cum_rows[M_HEADS + h:M_HEADS + h + 1, :] for h in heads], axis=1)
    m_prev_all = jnp.concatenate(
        [jnp.broadcast_to(m_old[h:h + 1, 0:1], (1, L)) for h in heads], axis=1)

    sel_row = lax.broadcasted_iota(I32, (LANES, hl), 0)
    sel_col = lax.broadcasted_iota(I32, (LANES, hl), 1)
    sel_head = lax.shift_right_logical(sel_col, L.bit_length() - 1)
    pick = jnp.where((sel_row == sel_head) | (sel_row == M_HEADS + sel_head)
                     | (sel_row == 2 * M_HEADS + sel_head), 1.0, 0.0).astype(BF16)
    src = lax.broadcasted_iota(I32, (L, hl), 0)
    tgt = lax.broadcasted_iota(I32, (L, hl), 1) & (L - 1)
    dlog = jnp.where(src <= tgt, _dot(col_pieces, pick) + bcum_all, -jnp.inf)
    inter_log = bcum_all + m_prev_all
    m_t_all = jnp.maximum(inter_log, jnp.max(dlog, axis=0, keepdims=True))
    w_intra = jnp.exp(dlog - m_t_all)
    w_inter_all = jnp.exp(inter_log - m_t_all)
    floor_all = jnp.exp(-m_t_all)
    scores_all = jnp.concatenate([_dot(khs[h], qts[h]) for h in heads], axis=1) * w_intra
    score_sum = jnp.sum(scores_all, axis=0, keepdims=True)
    scores_b = scores_all.astype(BF16)

    hs = []
    for h in heads:
        lanes = slice(h * L, (h + 1) * L)
        qt, kh = qts[h], khs[h]
        vt = vt_ref[h * M_DV:(h + 1) * M_DV, :]
        state = ct_ref[h]
        ig_row = ig_all[:, lanes]
        bcum_row = bcum_all[:, lanes]
        m_prev = m_old[h:h + 1, 0:1]
        w_inter = w_inter_all[:, lanes]
        q_state = _dot(state.astype(BF16), qt)
        num = _dot(vt, scores_b[:, lanes]) + w_inter * q_state[0:M_DV, :]
        den = score_sum[:, lanes] + w_inter * q_state[M_DV:M_DV + 1, :]
        h_out = num / jnp.maximum(jnp.abs(den), floor_all[:, lanes])

        b_last = bcum_row[:, L - 1:L]
        log_src = b_last - bcum_row + ig_row
        m_new = jnp.maximum(b_last + m_prev,
                            jnp.max(log_src, axis=1, keepdims=True))
        w_src = jnp.exp(log_src - m_new)
        decay = jnp.exp(b_last + m_prev - m_new)
        vt_w = jnp.concatenate(
            [(vt.astype(F32) * w_src).astype(BF16),
             jnp.broadcast_to(w_src, (SUBLANES, L)).astype(BF16)], axis=0)
        ct_ref[h] = decay * state + _dot(vt_w, kh)
        m_ref[h:h + 1, :] = jnp.broadcast_to(m_new, (1, LANES))

        hn = h_out * lax.rsqrt(jnp.mean(h_out * h_out, axis=0, keepdims=True) + EPS)
        og = ot_ref[h * M_DV:(h + 1) * M_DV, :].astype(F32)
        hs.append((hn * ng_wide[h * M_DV:(h + 1) * M_DV, :] * _sigmoid(og)).astype(BF16))

    hs_t = jnp.concatenate(hs, axis=0)
    y = lax.dot_general(hs_t, wout_ref[...], (((0,), (0,)), ((), ())),
                        preferred_element_type=F32)
    gate = mod_ref[0][:, 2 * D_MODEL:3 * D_MODEL]
    out_ref[...] = x_ref[...] + gate * y


def _mlstm_rec(x, mod, qt, k, vt, ot, gates, gatest, m_norm_g, w_out):
    L = MLSTM_CHUNK
    nc = SEQ // L
    rows = lambda width: pl.BlockSpec((L, width), lambda b, j: (b * nc + j, 0))
    cols = lambda height: pl.BlockSpec((height, L), lambda b, j: (0, b * nc + j))
    ng = jnp.broadcast_to(m_norm_g.astype(F32).reshape(M_V, 1), (M_V, LANES))
    return pl.pallas_call(
        _mlstm_rec_kernel,
        out_shape=jax.ShapeDtypeStruct((N_TOK, D_MODEL), F32),
        grid=(BATCH, nc),
        in_specs=[
            cols(M_QK), rows(M_QK), cols(M_V), cols(M_V), rows(LANES),
            cols(SUBLANES), rows(D_MODEL),
            pl.BlockSpec((1, 1, 3 * D_MODEL), lambda b, j: (b, 0, 0)),
            pl.BlockSpec((M_V, LANES), lambda b, j: (0, 0)),
            pl.BlockSpec((M_V, D_MODEL), lambda b, j: (0, 0)),
        ],
        out_specs=rows(D_MODEL),
        scratch_shapes=[
            pltpu.VMEM((M_HEADS, M_DV + SUBLANES, M_DK), F32),
            pltpu.VMEM((SUBLANES, LANES), F32),
        ],
        compiler_params=_cparams(("arbitrary", "arbitrary")),
        name="mlstm_rec",
    )(qt, k, vt, ot, gates, gatest, x, mod, ng, w_out.astype(BF16))


_CONV_COLS = 256


def _conv_kernel(x_ref, g_ref, mod_ref, win_ref, cw_ref, wout_ref, out_ref,
                 carry_ref, z_ref):
    t = ROW_TILE
    tiles_per_seq = SEQ // t

    @pl.when(pl.program_id(0) % tiles_per_seq == 0)
    def _():
        carry_ref[...] = jnp.zeros_like(carry_ref)

    x = x_ref[...]
    mod = mod_ref[0]
    hb = _rms_mod(x, g_ref[...], mod).astype(BF16)
    row = lax.broadcasted_iota(I32, (t, _CONV_COLS), 0)
    for j in range(D_MODEL // _CONV_COLS):
        lo, hi = j * _CONV_COLS, (j + 1) * _CONV_COLS
        b_gate = _dot(hb, win_ref[:, lo:hi])
        c_gate = _dot(hb, win_ref[:, D_MODEL + lo:D_MODEL + hi])
        xb = _dot(hb, win_ref[:, 2 * D_MODEL + lo:2 * D_MODEL + hi])
        u = c_gate * xb
        prev1 = carry_ref[SUBLANES - 1:SUBLANES, lo:hi]
        prev2 = carry_ref[SUBLANES - 2:SUBLANES - 1, lo:hi]
        u1 = jnp.where(row == 0, prev1, pltpu.roll(u, 1, axis=0))
        u2 = jnp.where(row == 0, prev2,
                       jnp.where(row == 1, prev1, pltpu.roll(u, 2, axis=0)))
        y = (cw_ref[0:1, lo:hi] * u2 + cw_ref[1:2, lo:hi] * u1
             + cw_ref[2:3, lo:hi] * u)
        z_ref[:, lo:hi] = (b_gate * y).astype(BF16)
        carry_ref[:, lo:hi] = u[t - SUBLANES:t, :]
    gate = mod[:, 2 * D_MODEL:3 * D_MODEL]
    out_ref[...] = x + gate * _dot(z_ref[...], wout_ref[...])


def _conv_layer(x, norm_g, mod, w_in, conv_w, w_out):
    t = ROW_TILE
    tiles_per_seq = SEQ // t
    full = lambda shape: pl.BlockSpec(shape, lambda i: (0, 0))
    cw = jnp.pad(conv_w.astype(F32), ((0, SUBLANES - CONV_K), (0, 0)))
    return pl.pallas_call(
        _conv_kernel,
        out_shape=jax.ShapeDtypeStruct((N_TOK, D_MODEL), F32),
        grid=(N_TOK // t,),
        in_specs=[
            pl.BlockSpec((t, D_MODEL), lambda i: (i, 0)),
            full((1, D_MODEL)),
            pl.BlockSpec((1, 1, 3 * D_MODEL), lambda i: (i // tiles_per_seq, 0, 0)),
            full((D_MODEL, 3 * D_MODEL)),
            full((SUBLANES, D_MODEL)),
            full((D_MODEL, D_MODEL)),
        ],
        out_specs=pl.BlockSpec((t, D_MODEL), lambda i: (i, 0)),
        scratch_shapes=[
            pltpu.VMEM((SUBLANES, D_MODEL), F32),
            pltpu.VMEM((t, D_MODEL), BF16),
        ],
        compiler_params=_cparams(("arbitrary",)),
        name="conv_layer",
    )(x, norm_g.reshape(1, D_MODEL), mod, w_in.astype(BF16), cw, w_out.astype(BF16))


_ROUTE_ROWS = LANES
_EXPERT_ROW0 = SUBLANES


_ROUTE_TILES = 4
_TILE_SHIFT = MOE_TILE.bit_length() - 1
_LANE_SHIFT = LANES.bit_length() - 1


def _route_kernel(x_ref, g_ref, mod_ref, wr_hi_ref, wr_lo_ref, br_ref,
                  h_ref, route_ref, cnt_ref):
    h, route, cnt = _route_tiles(x_ref[...], g_ref[...], mod_ref[0],
                                 wr_hi_ref[...], wr_lo_ref[...], br_ref[...])
    h_ref[...] = h
    route_ref[...] = route
    for j in range(_ROUTE_TILES):
        cnt_ref[j] = cnt[:, j * LANES:(j + 1) * LANES]


def _route_tiles(x, g, mod, wr_hi, wr_lo, br):
    t = MOE_TILE * _ROUTE_TILES
    h = _rms_mod(x, g, mod)
    h_hi = h.astype(BF16)
    h_lo = (h - h_hi.astype(F32)).astype(BF16)
    logits = (_dot_nt(wr_hi, h_hi) + _dot_nt(wr_hi, h_lo)
              + _dot_nt(wr_lo, h_hi)) + br

    sub = lax.broadcasted_iota(I32, (SUBLANES, t), 0)
    neg_inf = -jnp.inf
    gl = jnp.where(sub < N_GROUPS, logits[0:SUBLANES, :], neg_inf)
    gmax = jnp.max(gl, axis=0, keepdims=True)
    g_sel = jnp.min(jnp.where(gl == gmax, sub, SUBLANES), axis=0, keepdims=True)
    p_sel = 1.0 / jnp.sum(jnp.exp(gl - gmax), axis=0, keepdims=True)

    e_sel = jnp.zeros((E_PER_GROUP, t), F32)
    for g in range(N_GROUPS):
        r0 = _EXPERT_ROW0 + g * E_PER_GROUP
        e_sel = jnp.where(g_sel == g, logits[r0:r0 + E_PER_GROUP, :], e_sel)
    v1 = jnp.max(e_sel, axis=0, keepdims=True)
    i1 = jnp.min(jnp.where(e_sel == v1, sub, SUBLANES), axis=0, keepdims=True)
    e_rest = jnp.where(sub == i1, neg_inf, e_sel)
    v2 = jnp.max(e_rest, axis=0, keepdims=True)
    i2 = jnp.min(jnp.where(e_rest == v2, sub, SUBLANES), axis=0, keepdims=True)
    ratio = jnp.exp(v2 - v1)
    w1 = p_sel / (1.0 + ratio)
    w2 = p_sel * ratio / (1.0 + ratio)
    eid1 = g_sel * E_PER_GROUP + i1
    eid2 = g_sel * E_PER_GROUP + i2

    erow = lax.broadcasted_iota(I32, (N_EXPERTS, t), 0)
    m1 = erow == eid1
    m2 = erow == eid2
    member = jnp.where(m1 | m2, 1.0, 0.0)
    r = lax.broadcasted_iota(I32, (t, t), 0)
    c = lax.broadcasted_iota(I32, (t, t), 1)
    same_tile = (lax.shift_right_logical(r, _TILE_SHIFT)
                 == lax.shift_right_logical(c, _TILE_SHIFT))
    earlier = jnp.where((r < c) & same_tile, 1.0, 0.0).astype(BF16)
    member_b = member.astype(BF16)
    rank = _dot(member_b, earlier)
    tr = lax.broadcasted_iota(I32, (t, _ROUTE_TILES * LANES), 0)
    tc = lax.broadcasted_iota(I32, (t, _ROUTE_TILES * LANES), 1)
    in_tile = jnp.where(lax.shift_right_logical(tr, _TILE_SHIFT)
                        == lax.shift_right_logical(tc, _LANE_SHIFT), 1.0, 0.0)
    cnt = _dot(member_b, in_tile.astype(BF16))
    cnt_pad = jnp.floor((cnt + (SEG_ALIGN - 1.0)) * (1.0 / SEG_ALIGN)) * SEG_ALIGN
    er = lax.broadcasted_iota(I32, (N_EXPERTS, N_EXPERTS), 0)
    ec = lax.broadcasted_iota(I32, (N_EXPERTS, N_EXPERTS), 1)
    before = jnp.where(er > ec, 1.0, 0.0).astype(BF16)
    seg_start = _dot(before, cnt_pad.astype(BF16))
    seg_start_tok = jnp.concatenate(
        [seg_start[:, j * LANES:(j + 1) * LANES]
         for j in range(_ROUTE_TILES) for _ in range(MOE_TILE // LANES)], axis=1)
    pos = seg_start_tok + rank
    pos1 = jnp.sum(jnp.where(m1, pos, 0.0), axis=0, keepdims=True)
    pos2 = jnp.sum(jnp.where(m2, pos, 0.0), axis=0, keepdims=True)

    out = jnp.zeros((SUBLANES, t), F32)
    for k, val in enumerate((pos1, pos2, w1, w2)):
        out = jnp.where(sub == k, val, out)
    return h_hi, out, cnt


def _route(x, norm_g, mod, w_group, b_group, w_expert, b_expert):
    t = MOE_TILE * _ROUTE_TILES
    tiles_per_seq = SEQ // t
    wr = jnp.zeros((_ROUTE_ROWS, D_MODEL), F32)
    wr = wr.at[0:N_GROUPS].set(w_group.T.astype(F32))
    wr = wr.at[_EXPERT_ROW0:_EXPERT_ROW0 + N_EXPERTS].set(w_expert.T.astype(F32))
    wr_hi = wr.astype(BF16)
    wr_lo = (wr - wr_hi.astype(F32)).astype(BF16)
    br = jnp.zeros((_ROUTE_ROWS,), F32)
    br = br.at[0:N_GROUPS].set(b_group.astype(F32))
    br = br.at[_EXPERT_ROW0:_EXPERT_ROW0 + N_EXPERTS].set(b_expert.astype(F32))
    full = lambda shape: pl.BlockSpec(shape, lambda i: (0, 0))
    return pl.pallas_call(
        _route_kernel,
        out_shape=(
            jax.ShapeDtypeStruct((N_TOK, D_MODEL), BF16),
            jax.ShapeDtypeStruct((SUBLANES, N_TOK), F32),
            jax.ShapeDtypeStruct((N_MOE_TILES, N_EXPERTS, LANES), F32),
        ),
        grid=(N_MOE_TILES // _ROUTE_TILES,),
        in_specs=[
            pl.BlockSpec((t, D_MODEL), lambda i: (i, 0)),
            full((1, D_MODEL)),
            pl.BlockSpec((1, 1, 3 * D_MODEL), lambda i: (i // tiles_per_seq, 0, 0)),
            full((_ROUTE_ROWS, D_MODEL)),
            full((_ROUTE_ROWS, D_MODEL)),
            full((_ROUTE_ROWS, 1)),
        ],
        out_specs=(
            pl.BlockSpec((t, D_MODEL), lambda i: (i, 0)),
            pl.BlockSpec((SUBLANES, t), lambda i: (0, i)),
            pl.BlockSpec((_ROUTE_TILES, N_EXPERTS, LANES), lambda i: (i, 0, 0)),
        ),
        compiler_params=_cparams(("arbitrary",)),
        name="moe_route",
    )(x, norm_g.reshape(1, D_MODEL), mod, wr_hi, wr_lo, br.reshape(_ROUTE_ROWS, 1))


_REM_UNITS = CHUNK // SEG_ALIGN
_REM_SHIFT = _REM_UNITS.bit_length() - 1


def _chunk_code(tokens):
    return (tokens // CHUNK) * _REM_UNITS + (tokens % CHUNK) // SEG_ALIGN


def _dispatch_plan(cnt):
    cnt = cnt.astype(I32)
    cnt_pad = (cnt + SEG_ALIGN - 1) // SEG_ALIGN * SEG_ALIGN
    seg = jnp.cumsum(cnt_pad, axis=1) - cnt_pad
    tot = jnp.sum(cnt_pad, axis=0)
    ptot = (tot + EXPERT_BLOCK - 1) // EXPERT_BLOCK * EXPERT_BLOCK
    pend = jnp.cumsum(ptot)
    gbase = pend - ptot
    dst = gbase[None, :] + jnp.cumsum(cnt_pad, axis=0) - cnt_pad
    nch = _chunk_code(cnt_pad)
    n_used = (pend[-1] // EXPERT_BLOCK).astype(I32)
    blk = jnp.arange(N_EXPERT_BLOCKS, dtype=I32)
    blk_start = jnp.minimum(blk, n_used - 1) * EXPERT_BLOCK
    blk_e = jnp.sum((pend[None, :] <= blk_start[:, None]).astype(I32), axis=1)
    blk_e = jnp.minimum(blk_e, N_EXPERTS - 1).astype(I32)
    gap_dst = gbase + tot
    gap_nch = _chunk_code(ptot - tot)
    misc = jnp.stack([n_used, jnp.sum(ptot - tot) // SEG_ALIGN]).astype(I32)
    first_blk = gbase // EXPERT_BLOCK
    ids = jnp.arange(N_EXPERTS, dtype=I32)
    later = (ids[None, :] > ids[:, None]) & (ptot[None, :] > 0)
    next_e = jnp.min(jnp.where(later, ids[None, :], N_EXPERTS), axis=1)
    next_e = jnp.where(next_e == N_EXPERTS, -1, next_e)
    return dict(seg=seg.reshape(-1).astype(I32), dst=dst.reshape(-1).astype(I32),
                nch=nch.reshape(-1).astype(I32),
                ntot=(jnp.sum(cnt_pad, axis=1) // SEG_ALIGN).astype(I32),
                nmax=(jnp.max(cnt_pad, axis=1) // CHUNK).astype(I32), blk_e=blk_e,
                first_blk=first_blk.astype(I32), next_e=next_e.astype(I32),
                gap_dst=gap_dst.astype(I32), gap_nch=gap_nch.astype(I32), misc=misc)


_INLINE_CHUNKS = 3


def _start_copy(make_copy, s0, d0, offset, tokens, priority=0):
    s = pl.multiple_of(s0 + offset, SEG_ALIGN)
    d = pl.multiple_of(d0 + offset, SEG_ALIGN)
    make_copy(s, d, tokens).start(priority=priority)


def _start_tail(make_copy, s0, d0, n_full, rem, priority=0):
    half, quarter = CHUNK // 2, CHUNK // 4
    tail = n_full * CHUNK
    has_half = (rem & 2) != 0
    pl.when(has_half)(functools.partial(_start_copy, make_copy, s0, d0, tail, half,
                                        priority))
    tail2 = tail + jnp.where(has_half, half, 0)
    pl.when((rem & 1) != 0)(functools.partial(_start_copy, make_copy, s0, d0, tail2,
                                              quarter, priority))


def _segment_copies_inline(tile, live, seg_ref, dst_ref, nch_ref, make_copy):
    for e in range(N_EXPERTS):
        idx = tile * N_EXPERTS + e
        code = jnp.where(live, nch_ref[idx], 0)
        n = lax.shift_right_logical(code, _REM_SHIFT)
        s0 = seg_ref[idx]
        d0 = dst_ref[idx]
        for cidx in range(_INLINE_CHUNKS):
            pl.when(cidx < n)(functools.partial(_start_copy, make_copy, s0, d0,
                                                cidx * CHUNK, CHUNK, (e + cidx) % 2))
        _start_tail(make_copy, s0, d0, n, code & (_REM_UNITS - 1), e % 2)


def _segment_copies_loop(tile, first, with_tail, seg_ref, dst_ref, nch_ref, make_copy):
    def per_expert(e, carry):
        idx = tile * N_EXPERTS + e
        code = nch_ref[idx]
        n = lax.shift_right_logical(code, _REM_SHIFT)
        s0 = seg_ref[idx]
        d0 = dst_ref[idx]

        def per_chunk(cidx, c2):
            _start_copy(make_copy, s0, d0, cidx * CHUNK, CHUNK)
            return c2

        lax.fori_loop(first, jnp.maximum(n, first), per_chunk, 0)
        if with_tail:
            _start_tail(make_copy, s0, d0, n, code & (_REM_UNITS - 1))
        return carry

    lax.fori_loop(0, N_EXPERTS, per_expert, 0)


_WAIT_GROUP = 64


def _wait_each(n, make_wait):
    def body(_, carry):
        make_wait().wait()
        return carry
    lax.fori_loop(0, n, body, 0)


def _wait_copies(units, make_copy):
    _wait_each(units // _WAIT_GROUP, lambda: make_copy(0, 0, _WAIT_GROUP * SEG_ALIGN))
    _wait_each(units % _WAIT_GROUP, lambda: make_copy(0, 0, SEG_ALIGN))


_HALF = D_MODEL // 2
_BF16_BITS = 16
_HI_MASK = -(1 << _BF16_BITS)


def _pack_pairs(x):
    lo = lax.shift_right_logical(lax.bitcast_convert_type(x[:, 0:_HALF], I32),
                                 _BF16_BITS)
    hi = lax.bitcast_convert_type(x[:, _HALF:D_MODEL], I32) & _HI_MASK
    return lo | hi


def _unpack_pairs(w):
    lo = lax.bitcast_convert_type(lax.shift_left(w, _BF16_BITS), F32).astype(BF16)
    hi = lax.bitcast_convert_type(w & _HI_MASK, F32).astype(BF16)
    return lo, hi


def _token_rows(ref, tok, tokens):
    start = pl.multiple_of(tok * WORD_ROWS, SUBLANES)
    return ref.at[pl.ds(start, tokens * WORD_ROWS)]


def _store_words(ref, words, first=0):
    rows = words.shape[0]
    for q in range(WORD_ROWS):
        ref[pl.ds(first * WORD_ROWS + q, rows, stride=WORD_ROWS), :] = (
            words[:, q * LANES:(q + 1) * LANES])


def _load_words(ref, rows, first=0):
    return jnp.concatenate(
        [ref[pl.ds(first * WORD_ROWS + q, rows, stride=WORD_ROWS), :]
         for q in range(WORD_ROWS)], axis=1)


_DISPATCH_BUFS = 3
_TILES_PER_STEP = 2


def _dispatch_kernel(seg_ref, dst_ref, nch_ref, ntot_ref, nmax_ref, gap_dst_ref,
                     gap_nch_ref, misc_ref, route_ref, h_ref, xs_ref, buf_ref,
                     zero_ref, sem, zsem):
    t = MOE_TILE
    last = N_MOE_TILES - 1
    n_used = misc_ref[0]

    def make_copy(which):
        def mk(s, d, tokens):
            return pltpu.make_async_copy(_token_rows(buf_ref.at[which], s, tokens),
                                         _token_rows(xs_ref, d, tokens), sem.at[which])
        return mk

    def zero_copy(s, d, tokens):
        del s
        return pltpu.make_async_copy(_token_rows(zero_ref, 0, tokens),
                                     _token_rows(xs_ref, d, tokens), zsem)

    @pl.when(pl.program_id(0) == 0)
    def _():
        zero_ref[...] = jnp.zeros_like(zero_ref)

        def per_expert(e, carry):
            d0 = gap_dst_ref[e]
            code = gap_nch_ref[e]
            n = lax.shift_right_logical(code, _REM_SHIFT)

            def per_chunk(cidx, c2):
                _start_copy(zero_copy, 0, d0, cidx * CHUNK, CHUNK)
                return c2

            lax.fori_loop(0, n, per_chunk, 0)
            _start_tail(zero_copy, 0, d0, n, code & (_REM_UNITS - 1))
            return carry

        lax.fori_loop(0, N_EXPERTS, per_expert, 0)

        def per_block(b, carry):
            zero_copy(0, b * EXPERT_BLOCK, EXPERT_BLOCK).start()
            return carry

        lax.fori_loop(n_used, N_EXPERT_BLOCKS, per_block, 0)

    def sort_tile(j):
        k = pl.program_id(0) * _TILES_PER_STEP + j
        slot = k % _DISPATCH_BUFS
        prev = (k + _DISPATCH_BUFS - 1) % _DISPATCH_BUFS
        prev2 = (k + _DISPATCH_BUFS - 2) % _DISPATCH_BUFS
        tile_prev = jnp.maximum(k - 1, 0)
        _segment_copies_inline(tile_prev, k > 0, seg_ref, dst_ref, nch_ref,
                               make_copy(prev))

        route = route_ref[:, j * t:(j + 1) * t]
        pos1 = route[0:1, :].astype(I32)
        pos2 = route[1:2, :].astype(I32)
        r = lax.broadcasted_iota(I32, (SORT_ROWS, t), 0)
        perm = jnp.where((r == pos1) | (r == pos2), 1.0, 0.0).astype(BF16)
        _store_words(buf_ref.at[slot],
                     _pack_pairs(_dot(perm, h_ref[j * t:(j + 1) * t, :])))

        @pl.when((k > 0) & (nmax_ref[tile_prev] > _INLINE_CHUNKS))
        def _():
            _segment_copies_loop(tile_prev, _INLINE_CHUNKS, False, seg_ref, dst_ref,
                                 nch_ref, make_copy(prev))

        @pl.when(k > 1)
        def _():
            _wait_copies(ntot_ref[jnp.maximum(k - 2, 0)], make_copy(prev2))

        @pl.when(k == last)
        def _():
            _segment_copies_loop(k, 0, True, seg_ref, dst_ref, nch_ref, make_copy(slot))
            _wait_copies(ntot_ref[last - 1], make_copy(prev))
            _wait_copies(ntot_ref[last], make_copy(slot))
            _wait_copies(misc_ref[1], zero_copy)
            _wait_each(N_EXPERT_BLOCKS - n_used, lambda: zero_copy(0, 0, EXPERT_BLOCK))

    for j in range(_TILES_PER_STEP):
        sort_tile(j)


def _dispatch(plan, route, h):
    t = MOE_TILE * _TILES_PER_STEP
    return pl.pallas_call(
        _dispatch_kernel,
        out_shape=jax.ShapeDtypeStruct((SORTED_ROWS * WORD_ROWS, LANES), I32),
        grid_spec=pltpu.PrefetchScalarGridSpec(
            num_scalar_prefetch=8,
            grid=(N_MOE_TILES // _TILES_PER_STEP,),
            in_specs=[
                pl.BlockSpec((SUBLANES, t), lambda i, *_: (0, i)),
                pl.BlockSpec((t, D_MODEL), lambda i, *_: (i, 0)),
            ],
            out_specs=pl.BlockSpec(memory_space=pl.ANY),
            scratch_shapes=[
                pltpu.VMEM((_DISPATCH_BUFS, SORT_ROWS * WORD_ROWS, LANES), I32),
                pltpu.VMEM((EXPERT_BLOCK * WORD_ROWS, LANES), I32),
                pltpu.SemaphoreType.DMA((_DISPATCH_BUFS,)),
                pltpu.SemaphoreType.DMA,
            ],
        ),
        compiler_params=_cparams(("arbitrary",)),
        name="moe_dispatch",
    )(plan["seg"], plan["dst"], plan["nch"], plan["ntot"], plan["nmax"],
      plan["gap_dst"], plan["gap_nch"], plan["misc"], route, h)


_WEIGHT_DMA_PRIORITY = 1


def _expert_kernel(blk_e_ref, first_ref, next_ref, misc_ref, x_ref, wg_hbm, wu_hbm,
                   wd_hbm, y_ref, wg_f, wu_f, wd_f, wg_b, wu_b, wd_b, sem, *, layer):
    blk0 = pl.program_id(0) * _BLOCKS_PER_STEP
    blk1 = blk0 + 1
    n_used = misc_ref[0]

    def weight_copies(e):
        return (pltpu.make_async_copy(wg_hbm.at[layer, e], wg_f, sem.at[0]),
                pltpu.make_async_copy(wu_hbm.at[layer, e], wu_f, sem.at[1]),
                pltpu.make_async_copy(wd_hbm.at[layer, e], wd_f, sem.at[2]))

    def take_weights(e):
        for cp in weight_copies(e):
            cp.wait()
        wg_b[...] = wg_f[...].astype(BF16)
        wu_b[...] = wu_f[...].astype(BF16)
        wd_b[...] = wd_f[...].astype(BF16)

        @pl.when(next_ref[e] >= 0)
        def _():
            for cp in weight_copies(next_ref[e]):
                cp.start(priority=_WEIGHT_DMA_PRIORITY)

    def run_rows(first, rows):
        x_lo, x_hi = _unpack_pairs(_load_words(x_ref, rows, first))
        x = jnp.concatenate([x_lo, x_hi], axis=1)
        g = _dot(x, wg_b[...])
        u = _dot(x, wu_b[...])
        a = (g * _sigmoid(g) * u).astype(BF16)
        y = _dot(a, wd_b[...])
        _store_words(y_ref, _pack_pairs(y.astype(BF16).astype(F32)), first)

    @pl.when(blk0 == 0)
    def _():
        for cp in weight_copies(blk_e_ref[0]):
            cp.start(priority=_WEIGHT_DMA_PRIORITY)

    @pl.when(blk0 < n_used)
    def _():
        e0 = blk_e_ref[blk0]
        e1 = blk_e_ref[blk1]
        pl.when(blk0 == first_ref[e0])(functools.partial(take_weights, e0))

        @pl.when(e0 == e1)
        def _():
            run_rows(0, _BLOCKS_PER_STEP * EXPERT_BLOCK)

        @pl.when(e0 != e1)
        def _():
            run_rows(0, EXPERT_BLOCK)
            take_weights(e1)
            run_rows(EXPERT_BLOCK, EXPERT_BLOCK)

    @pl.when(blk0 >= n_used)
    def _():
        y_ref[...] = jnp.zeros_like(y_ref)


def _experts(plan, xs, layer, w_gate, w_up, w_down):
    step_rows = _BLOCKS_PER_STEP * EXPERT_BLOCK * WORD_ROWS
    row_map = lambda i, be, fi, nx, misc: (
        jnp.minimum(i, (misc[0] - 1) // _BLOCKS_PER_STEP), 0)
    out_map = lambda i, be, fi, nx, misc: (i, 0)
    return pl.pallas_call(
        functools.partial(_expert_kernel, layer=layer),
        out_shape=jax.ShapeDtypeStruct((SORTED_ROWS * WORD_ROWS, LANES), I32),
        grid_spec=pltpu.PrefetchScalarGridSpec(
            num_scalar_prefetch=4,
            grid=(N_EXPERT_BLOCKS // _BLOCKS_PER_STEP,),
            in_specs=[
                pl.BlockSpec((step_rows, LANES), row_map),
                pl.BlockSpec(memory_space=pl.ANY),
                pl.BlockSpec(memory_space=pl.ANY),
                pl.BlockSpec(memory_space=pl.ANY),
            ],
            out_specs=pl.BlockSpec((step_rows, LANES), out_map),
            scratch_shapes=[
                pltpu.VMEM((D_MODEL, D_FF), F32),
                pltpu.VMEM((D_MODEL, D_FF), F32),
                pltpu.VMEM((D_FF, D_MODEL), F32),
                pltpu.VMEM((D_MODEL, D_FF), BF16),
                pltpu.VMEM((D_MODEL, D_FF), BF16),
                pltpu.VMEM((D_FF, D_MODEL), BF16),
                pltpu.SemaphoreType.DMA((3,)),
            ],
        ),
        compiler_params=_cparams(("arbitrary",)),
        name="moe_experts",
    )(plan["blk_e"], plan["first_blk"], plan["next_e"], plan["misc"], xs,
      w_gate, w_up, w_down)


_COMBINE_BUFS = 3


def _combine_kernel(seg_ref, dst_ref, nch_ref, ntot_ref, nmax_ref, route_ref, x_ref,
                    mod_ref, fg_ref, ys_ref, out_ref, *scratch, final_norm):
    bufs, sem = scratch[:_COMBINE_BUFS], scratch[_COMBINE_BUFS]
    t = MOE_TILE
    ahead = _COMBINE_BUFS - 1

    def make_copy(which):
        def mk(s, d, tokens):
            return pltpu.make_async_copy(_token_rows(ys_ref, d, tokens),
                                         _token_rows(bufs[which], s, tokens),
                                         sem.at[which])
        return mk

    @pl.when(pl.program_id(0) == 0)
    def _():
        for buf in bufs:
            buf[...] = jnp.zeros_like(buf)
        for k in range(_COMBINE_BUFS - 1):
            _segment_copies_loop(k, 0, True, seg_ref, dst_ref, nch_ref, make_copy(k))

    def unsort_tile(j, k, cur):
        nxt = (cur + ahead) % _COMBINE_BUFS
        tile_next = jnp.minimum(k + ahead, N_MOE_TILES - 1)
        has_next = k + ahead < N_MOE_TILES
        rows = pl.ds(j * t, t)
        _wait_copies(ntot_ref[k], make_copy(cur))
        _segment_copies_inline(tile_next, has_next, seg_ref, dst_ref, nch_ref,
                               make_copy(nxt))
        route = route_ref[:, j * t:(j + 1) * t]
        route_t = jnp.concatenate(
            [route, jnp.zeros((LANES - SUBLANES, t), F32)], axis=0).T
        pos1 = route_t[:, 0:1].astype(I32)
        pos2 = route_t[:, 1:2].astype(I32)
        w1 = route_t[:, 2:3]
        w2 = route_t[:, 3:4]
        c = lax.broadcasted_iota(I32, (t, SORT_ROWS), 1)
        unsort = (jnp.where(c == pos1, w1, 0.0)
                  + jnp.where(c == pos2, w2, 0.0)).astype(BF16)
        y_lo, y_hi = _unpack_pairs(_load_words(bufs[cur], SORT_ROWS))
        moe = jnp.concatenate([_dot(unsort, y_lo), _dot(unsort, y_hi)], axis=1)
        gate = mod_ref[0][:, 2 * D_MODEL:3 * D_MODEL]
        x_new = x_ref[rows, :] + gate * moe
        if final_norm:
            ms = jnp.mean(x_new * x_new, axis=-1, keepdims=True)
            x_new = (x_new * lax.rsqrt(ms + EPS)) * fg_ref[...]
        out_ref[rows, :] = x_new

        @pl.when(has_next & (nmax_ref[tile_next] > _INLINE_CHUNKS))
        def _():
            _segment_copies_loop(tile_next, _INLINE_CHUNKS, False, seg_ref, dst_ref,
                                 nch_ref, make_copy(nxt))

    for j in range(_TILES_PER_STEP):
        k = pl.program_id(0) * _TILES_PER_STEP + j
        for cur in range(_COMBINE_BUFS):
            pl.when(k % _COMBINE_BUFS == cur)(functools.partial(unsort_tile, j, k, cur))


def _combine(plan, route, x, mod, final_g, ys, final_norm):
    t = MOE_TILE * _TILES_PER_STEP
    tiles_per_seq = SEQ // t
    return pl.pallas_call(
        functools.partial(_combine_kernel, final_norm=final_norm),
        out_shape=jax.ShapeDtypeStruct((N_TOK, D_MODEL), F32),
        grid_spec=pltpu.PrefetchScalarGridSpec(
            num_scalar_prefetch=5,
            grid=(N_MOE_TILES // _TILES_PER_STEP,),
            in_specs=[
                pl.BlockSpec((SUBLANES, t), lambda i, *_: (0, i)),
                pl.BlockSpec((t, D_MODEL), lambda i, *_: (i, 0)),
                pl.BlockSpec((1, 1, 3 * D_MODEL), lambda i, *_: (i // tiles_per_seq, 0, 0)),
                pl.BlockSpec((1, D_MODEL), lambda i, *_: (0, 0)),
                pl.BlockSpec(memory_space=pl.ANY),
            ],
            out_specs=pl.BlockSpec((t, D_MODEL), lambda i, *_: (i, 0)),
            scratch_shapes=(
                [pltpu.VMEM((SORT_ROWS * WORD_ROWS, LANES), I32)] * _COMBINE_BUFS
                + [pltpu.SemaphoreType.DMA((_COMBINE_BUFS,))]),
        ),
        compiler_params=_cparams(("arbitrary",)),
        name="moe_combine",
    )(plan["seg"], plan["dst"], plan["nch"], plan["ntot"], plan["nmax"], route, x,
      mod, final_g.reshape(1, D_MODEL).astype(F32), ys)


def _moe_layer(x, norm_g, mod, final_g, w_group, b_group, w_expert, b_expert,
               layer, w_gate, w_up, w_down, final_norm):
    h, route, cnt = _route(x, norm_g, mod, w_group, b_group, w_expert, b_expert)
    plan = _dispatch_plan(cnt[:, :, 0])
    xs = _dispatch(plan, route, h)
    ys = _experts(plan, xs, layer, w_gate, w_up, w_down)
    return _combine(plan, route, x, mod, final_g, ys, final_norm)


def kernel(x, c, ada_w, ada_b, norm_g, final_g, m_w_in, m_b_gates, m_norm_g, m_w_out, s_w_in, s_conv_w, s_w_out, r_w_group, r_b_group, r_w_expert, r_b_expert, e_w_gate, e_w_up, e_w_down):
    mods = _ada_mods(c, ada_w, ada_b)
    xt = x.reshape(N_TOK, D_MODEL)
    for i in range(DEPTH):
        mod_mix = mods[2 * i].reshape(BATCH, 1, 3 * D_MODEL)
        mod_ffn = mods[2 * i + 1].reshape(BATCH, 1, 3 * D_MODEL)
        j = i // 2
        if i % 2 == 0:
            qt, k, vt, ot, gates, gatest = _mlstm_in(xt, norm_g[i, 0], mod_mix,
                                                     m_w_in[j], m_b_gates[j])
            xt = _mlstm_rec(xt, mod_mix, qt, k, vt, ot, gates, gatest,
                            m_norm_g[j], m_w_out[j])
        else:
            xt = _conv_layer(xt, norm_g[i, 0], mod_mix, s_w_in[j], s_conv_w[j],
                             s_w_out[j])
        xt = _moe_layer(xt, norm_g[i, 1], mod_ffn, final_g, r_w_group[i],
                        r_b_group[i], r_w_expert[i], r_b_expert[i], i, e_w_gate,
                        e_w_up, e_w_down, final_norm=(i == DEPTH - 1))
    return xt.reshape(BATCH, SEQ, D_MODEL)
```

```python
import functools

import jax
import jax.numpy as jnp
from jax import lax
from jax.experimental import pallas as pl
from jax.experimental.pallas import tpu as pltpu

F32 = jnp.float32
BF16 = jnp.bfloat16
I32 = jnp.int32

D_MODEL = 1024
BATCH = 8
SEQ = 2048
DEPTH = 4
N_TOK = BATCH * SEQ
M_HEADS = 4
M_DK = 128
M_DV = 256
M_QK = M_HEADS * M_DK
M_V = M_HEADS * M_DV
CONV_K = 3
N_GROUPS = 4
E_PER_GROUP = 8
N_EXPERTS = N_GROUPS * E_PER_GROUP
TOP_K = 2
D_FF = 512
EPS = 1e-6

SUBLANES = 8
LANES = 128

ROW_TILE = 1024
MLSTM_CHUNK = 256
MOE_TILE = 256
EXPERT_BLOCK = 256
WORD_ROWS = (D_MODEL // 2) // LANES
SEG_ALIGN = SUBLANES // WORD_ROWS
CHUNK = 8
_MAX_TILE_SORTED = MOE_TILE * TOP_K + N_EXPERTS * (SEG_ALIGN - 1)
SORT_ROWS = -(-_MAX_TILE_SORTED // LANES) * LANES
BF16_SUBLANES = 2 * SUBLANES
DISPATCH_ROWS = -(-_MAX_TILE_SORTED // BF16_SUBLANES) * BF16_SUBLANES
N_MOE_TILES = N_TOK // MOE_TILE
_MAX_SORTED = (N_MOE_TILES * (MOE_TILE * TOP_K + N_EXPERTS * (SEG_ALIGN - 1))
               + N_EXPERTS * (EXPERT_BLOCK - 1))
_BLOCKS_PER_STEP = 2
N_EXPERT_BLOCKS = (-(-_MAX_SORTED // (EXPERT_BLOCK * _BLOCKS_PER_STEP))
                   * _BLOCKS_PER_STEP)
SORTED_ROWS = N_EXPERT_BLOCKS * EXPERT_BLOCK

VMEM_LIMIT = 48 * 1024 * 1024


def _cparams(sem):
    return pltpu.CompilerParams(dimension_semantics=sem,
                                vmem_limit_bytes=VMEM_LIMIT)


def _dot(a, b):
    return jnp.dot(a, b, preferred_element_type=F32)


def _dot_nt(a, b):
    return lax.dot_general(a, b, (((1,), (1,)), ((), ())),
                           preferred_element_type=F32)


def _split3(x):
    hi = x.astype(BF16)
    r1 = x - hi.astype(F32)
    mid = r1.astype(BF16)
    lo = (r1 - mid.astype(F32)).astype(BF16)
    return hi, mid, lo


def _dot_sel_left(sel, x):
    hi, mid, lo = _split3(x)
    return _dot(sel, hi) + _dot(sel, mid) + _dot(sel, lo)


def _dot_sel_right(x, sel):
    hi, mid, lo = _split3(x)
    return _dot(hi, sel) + _dot(mid, sel) + _dot(lo, sel)


def _sigmoid(x):
    return 1.0 / (1.0 + jnp.exp(-x))


def _rms_mod(x, g, mod):
    ms = jnp.mean(x * x, axis=-1, keepdims=True)
    y = (x * lax.rsqrt(ms + EPS)) * g
    return y * (1.0 + mod[:, D_MODEL:2 * D_MODEL]) + mod[:, 0:D_MODEL]


def _ada_kernel(c_ref, w_ref, b_ref, o_ref):
    c = c_ref[...]
    s = (c * _sigmoid(c)).astype(BF16)
    o_ref[0] = _dot(s, w_ref[0].astype(BF16)) + b_ref[0]


def _ada_mods(c, ada_w, ada_b):
    n_pairs = DEPTH * 2
    w = ada_w.reshape(n_pairs, D_MODEL, 3 * D_MODEL)
    b = ada_b.reshape(n_pairs, 1, 3 * D_MODEL)
    col = D_MODEL
    return pl.pallas_call(
        _ada_kernel,
        out_shape=jax.ShapeDtypeStruct((n_pairs, BATCH, 3 * D_MODEL), F32),
        grid=(n_pairs, 3 * D_MODEL // col),
        in_specs=[
            pl.BlockSpec((BATCH, D_MODEL), lambda p, j: (0, 0)),
            pl.BlockSpec((1, D_MODEL, col), lambda p, j: (p, 0, j)),
            pl.BlockSpec((1, 1, col), lambda p, j: (p, 0, j)),
        ],
        out_specs=pl.BlockSpec((1, BATCH, col), lambda p, j: (p, 0, j)),
        compiler_params=_cparams(("arbitrary", "arbitrary")),
        name="ada_mods",
    )(c, w, b)


def _mlstm_in_kernel(x_ref, g_ref, mod_ref, wqt_ref, wk_ref, wvt_ref, wot_ref,
                     wg_ref, bg_ref, qt_ref, k_ref, vt_ref, ot_ref, gates_ref,
                     gatest_ref):
    h = _rms_mod(x_ref[...], g_ref[...], mod_ref[0])
    hb = h.astype(BF16)
    qt_ref[...] = (_dot_nt(wqt_ref[...], hb) * (M_DK ** -0.5)).astype(BF16)
    k_ref[...] = _dot(hb, wk_ref[...]).astype(BF16)
    vt_ref[...] = _dot_nt(wvt_ref[...], hb).astype(BF16)
    ot_ref[...] = _dot_nt(wot_ref[...], hb).astype(BF16)
    g = _dot(hb, wg_ref[...]) + bg_ref[...]
    log_sig = jnp.minimum(g, 0.0) - jnp.log(1.0 + jnp.exp(-jnp.abs(g)))
    lane = lax.broadcasted_iota(I32, g.shape, 1)
    gg = jnp.where(lane < M_HEADS, g, log_sig)
    gates_ref[...] = gg
    gatest_ref[...] = gg.T[0:SUBLANES, :]


def _mlstm_in(x, norm_g, mod, w_in, b_gates):
    t = ROW_TILE
    tiles_per_seq = SEQ // t
    wqt = w_in[:, 0:M_QK].T.astype(BF16)
    wk = w_in[:, M_QK:2 * M_QK].astype(BF16)
    wvt = w_in[:, 2 * M_QK:2 * M_QK + M_V].T.astype(BF16)
    wot = w_in[:, 2 * M_QK + M_V:2 * M_QK + 2 * M_V].T.astype(BF16)
    n_gate = 2 * M_HEADS
    wg = jnp.pad(w_in[:, 2 * M_QK + 2 * M_V:], ((0, 0), (0, LANES - n_gate))).astype(BF16)
    bg = jnp.pad(b_gates.astype(F32), (0, LANES - n_gate)).reshape(1, LANES)
    full = lambda shape: pl.BlockSpec(shape, lambda i: (0, 0))
    return pl.pallas_call(
        _mlstm_in_kernel,
        out_shape=(
            jax.ShapeDtypeStruct((M_QK, N_TOK), BF16),
            jax.ShapeDtypeStruct((N_TOK, M_QK), BF16),
            jax.ShapeDtypeStruct((M_V, N_TOK), BF16),
            jax.ShapeDtypeStruct((M_V, N_TOK), BF16),
            jax.ShapeDtypeStruct((N_TOK, LANES), F32),
            jax.ShapeDtypeStruct((SUBLANES, N_TOK), F32),
        ),
        grid=(N_TOK // t,),
        in_specs=[
            pl.BlockSpec((t, D_MODEL), lambda i: (i, 0)),
            full((1, D_MODEL)),
            pl.BlockSpec((1, 1, 3 * D_MODEL), lambda i: (i // tiles_per_seq, 0, 0)),
            full((M_QK, D_MODEL)),
            full((D_MODEL, M_QK)),
            full((M_V, D_MODEL)),
            full((M_V, D_MODEL)),
            full((D_MODEL, LANES)),
            full((1, LANES)),
        ],
        out_specs=(
            pl.BlockSpec((M_QK, t), lambda i: (0, i)),
            pl.BlockSpec((t, M_QK), lambda i: (i, 0)),
            pl.BlockSpec((M_V, t), lambda i: (0, i)),
            pl.BlockSpec((M_V, t), lambda i: (0, i)),
            pl.BlockSpec((t, LANES), lambda i: (i, 0)),
            pl.BlockSpec((SUBLANES, t), lambda i: (0, i)),
        ),
        compiler_params=_cparams(("arbitrary",)),
        name="mlstm_in",
    )(x, norm_g.reshape(1, D_MODEL), mod, wqt, wk, wvt, wot, wg, bg)


def _mlstm_rec_kernel(qt_ref, k_ref, vt_ref, ot_ref, gates_ref, gatest_ref,
                      x_ref, mod_ref, ng_ref, wout_ref, out_ref, ct_ref, m_ref):
    L = MLSTM_CHUNK

    @pl.when(pl.program_id(1) == 0)
    def _():
        ct_ref[...] = jnp.zeros_like(ct_ref)
        m_ref[...] = jnp.zeros_like(m_ref)

    gates = gates_ref[...]
    gatest = gatest_ref[...]
    row = lax.broadcasted_iota(I32, (L, L), 0)
    col = lax.broadcasted_iota(I32, (L, L), 1)
    tri_low = jnp.where(row >= col, 1.0, 0.0).astype(BF16)
    tri_up = jnp.where(row <= col, 1.0, 0.0).astype(BF16)
    cum_cols = _dot_sel_left(tri_low, gates)
    cum_rows = _dot_sel_right(gatest, tri_up)

    col_term = gates - pltpu.roll(cum_cols, LANES - M_HEADS, axis=1)
    head_lane = lax.broadcasted_iota(I32, col_term.shape, 1) < M_HEADS
    pieces = [jnp.where(head_lane, p.astype(F32), 0.0) for p in _split3(col_term)]
    col_pieces = (pieces[0] + pltpu.roll(pieces[1], M_HEADS, axis=1)
                  + pltpu.roll(pieces[2], 2 * M_HEADS, axis=1)).astype(BF16)
    ng_wide = jnp.concatenate([ng_ref[...]] * (L // LANES), axis=1)

    heads = range(M_HEADS)
    hl = M_HEADS * L
    qts = [qt_ref[h * M_DK:(h + 1) * M_DK, :] for h in heads]
    khs = [k_ref[:, h * M_DK:(h + 1) * M_DK] for h in heads]
    m_old = m_ref[...]
    ig_all = jnp.concatenate([gatest[h:h + 1, :] for h in heads], axis=1)
    bcum_all = jnp.concatenate(
        [cum_rows[M_HEADS + h:M_HEADS + h + 1, :] for h in heads], axis=1)
    m_prev_all = jnp.concatenate(
        [jnp.broadcast_to(m_old[h:h + 1, 0:1], (1, L)) for h in heads], axis=1)

    sel_row = lax.broadcasted_iota(I32, (LANES, hl), 0)
    sel_col = lax.broadcasted_iota(I32, (LANES, hl), 1)
    sel_head = lax.shift_right_logical(sel_col, L.bit_length() - 1)
    pick = jnp.where((sel_row == sel_head) | (sel_row == M_HEADS + sel_head)
                     | (sel_row == 2 * M_HEADS + sel_head), 1.0, 0.0).astype(BF16)
    src = lax.broadcasted_iota(I32, (L, hl), 0)
    tgt = lax.broadcasted_iota(I32, (L, hl), 1) & (L - 1)
    dlog = jnp.where(src <= tgt, _dot(col_pieces, pick) + bcum_all, -jnp.inf)
    inter_log = bcum_all + m_prev_all
    m_t_all = jnp.maximum(inter_log, jnp.max(dlog, axis=0, keepdims=True))
    w_intra = jnp.exp(dlog - m_t_all)
    w_inter_all = jnp.exp(inter_log - m_t_all)
    floor_all = jnp.exp(-m_t_all)
    scores_all = jnp.concatenate([_dot(khs[h], qts[h]) for h in heads], axis=1) * w_intra
    score_sum = jnp.sum(scores_all, axis=0, keepdims=True)
    scores_b = scores_all.astype(BF16)

    hs = []
    for h in heads:
        lanes = slice(h * L, (h + 1) * L)
        qt, kh = qts[h], khs[h]
        vt = vt_ref[h * M_DV:(h + 1) * M_DV, :]
        state = ct_ref[h]
        ig_row = ig_all[:, lanes]
        bcum_row = bcum_all[:, lanes]
        m_prev = m_old[h:h + 1, 0:1]
        w_inter = w_inter_all[:, lanes]
        q_state = _dot(state.astype(BF16), qt)
        num = _dot(vt, scores_b[:, lanes]) + w_inter * q_state[0:M_DV, :]
        den = score_sum[:, lanes] + w_inter * q_state[M_DV:M_DV + 1, :]
        h_out = num / jnp.maximum(jnp.abs(den), floor_all[:, lanes])

        b_last = bcum_row[:, L - 1:L]
        log_src = b_last - bcum_row + ig_row
        m_new = jnp.maximum(b_last + m_prev,
                            jnp.max(log_src, axis=1, keepdims=True))
        w_src = jnp.exp(log_src - m_new)
        decay = jnp.exp(b_last + m_prev - m_new)
        vt_w = jnp.concatenate(
            [(vt.astype(F32) * w_src).astype(BF16),
             jnp.broadcast_to(w_src, (SUBLANES, L)).astype(BF16)], axis=0)
        ct_ref[h] = decay * state + _dot(vt_w, kh)
        m_ref[h:h + 1, :] = jnp.broadcast_to(m_new, (1, LANES))

        hn = h_out * lax.rsqrt(jnp.mean(h_out * h_out, axis=0, keepdims=True) + EPS)
        og = ot_ref[h * M_DV:(h + 1) * M_DV, :].astype(F32)
        hs.append((hn * ng_wide[h * M_DV:(h + 1) * M_DV, :] * _sigmoid(og)).astype(BF16))

    hs_t = jnp.concatenate(hs, axis=0)
    y = lax.dot_general(hs_t, wout_ref[0], (((0,), (0,)), ((), ())),
                        preferred_element_type=F32)
    gate = mod_ref[0][:, 2 * D_MODEL:3 * D_MODEL]
    out_ref[...] = x_ref[...] + gate * y


def _mlstm_rec(x, mod, qt, k, vt, ot, gates, gatest, m_norm_g, w_out_all, layer):
    L = MLSTM_CHUNK
    nc = SEQ // L
    rows = lambda width: pl.BlockSpec((L, width), lambda b, j: (b * nc + j, 0))
    cols = lambda height: pl.BlockSpec((height, L), lambda b, j: (0, b * nc + j))
    ng = jnp.broadcast_to(m_norm_g.astype(F32).reshape(M_V, 1), (M_V, LANES))
    return pl.pallas_call(
        _mlstm_rec_kernel,
        out_shape=jax.ShapeDtypeStruct((N_TOK, D_MODEL), F32),
        grid=(BATCH, nc),
        in_specs=[
            cols(M_QK), rows(M_QK), cols(M_V), cols(M_V), rows(LANES),
            cols(SUBLANES), rows(D_MODEL),
            pl.BlockSpec((1, 1, 3 * D_MODEL), lambda b, j: (b, 0, 0)),
            pl.BlockSpec((M_V, LANES), lambda b, j: (0, 0)),
            pl.BlockSpec((1, M_V, D_MODEL), lambda b, j: (layer, 0, 0)),
        ],
        out_specs=rows(D_MODEL),
        scratch_shapes=[
            pltpu.VMEM((M_HEADS, M_DV + SUBLANES, M_DK), F32),
            pltpu.VMEM((SUBLANES, LANES), F32),
        ],
        compiler_params=_cparams(("arbitrary", "arbitrary")),
        name="mlstm_rec",
    )(qt, k, vt, ot, gates, gatest, x, mod, ng, w_out_all)


_CONV_COLS = 256


def _conv_kernel(x_ref, g_ref, mod_ref, win_ref, cw_ref, wout_ref, out_ref,
                 carry_ref, z_ref):
    t = ROW_TILE
    tiles_per_seq = SEQ // t

    @pl.when(pl.program_id(0) % tiles_per_seq == 0)
    def _():
        carry_ref[...] = jnp.zeros_like(carry_ref)

    x = x_ref[...]
    mod = mod_ref[0]
    hb = _rms_mod(x, g_ref[...], mod).astype(BF16)
    row = lax.broadcasted_iota(I32, (t, _CONV_COLS), 0)
    for j in range(D_MODEL // _CONV_COLS):
        lo, hi = j * _CONV_COLS, (j + 1) * _CONV_COLS
        b_gate = _dot(hb, win_ref[0, :, lo:hi])
        c_gate = _dot(hb, win_ref[0, :, D_MODEL + lo:D_MODEL + hi])
        xb = _dot(hb, win_ref[0, :, 2 * D_MODEL + lo:2 * D_MODEL + hi])
        u = c_gate * xb
        prev1 = carry_ref[SUBLANES - 1:SUBLANES, lo:hi]
        prev2 = carry_ref[SUBLANES - 2:SUBLANES - 1, lo:hi]
        u1 = jnp.where(row == 0, prev1, pltpu.roll(u, 1, axis=0))
        u2 = jnp.where(row == 0, prev2,
                       jnp.where(row == 1, prev1, pltpu.roll(u, 2, axis=0)))
        y = (cw_ref[0:1, lo:hi] * u2 + cw_ref[1:2, lo:hi] * u1
             + cw_ref[2:3, lo:hi] * u)
        z_ref[:, lo:hi] = (b_gate * y).astype(BF16)
        carry_ref[:, lo:hi] = u[t - SUBLANES:t, :]
    gate = mod[:, 2 * D_MODEL:3 * D_MODEL]
    out_ref[...] = x + gate * _dot(z_ref[...], wout_ref[0])


def _conv_layer(x, norm_g, mod, w_in_all, conv_w, w_out_all, layer):
    t = ROW_TILE
    tiles_per_seq = SEQ // t
    full = lambda shape: pl.BlockSpec(shape, lambda i: (0, 0))
    cw = jnp.pad(conv_w.astype(F32), ((0, SUBLANES - CONV_K), (0, 0)))
    return pl.pallas_call(
        _conv_kernel,
        out_shape=jax.ShapeDtypeStruct((N_TOK, D_MODEL), F32),
        grid=(N_TOK // t,),
        in_specs=[
            pl.BlockSpec((t, D_MODEL), lambda i: (i, 0)),
            full((1, D_MODEL)),
            pl.BlockSpec((1, 1, 3 * D_MODEL), lambda i: (i // tiles_per_seq, 0, 0)),
            pl.BlockSpec((1, D_MODEL, 3 * D_MODEL), lambda i: (layer, 0, 0)),
            full((SUBLANES, D_MODEL)),
            pl.BlockSpec((1, D_MODEL, D_MODEL), lambda i: (layer, 0, 0)),
        ],
        out_specs=pl.BlockSpec((t, D_MODEL), lambda i: (i, 0)),
        scratch_shapes=[
            pltpu.VMEM((SUBLANES, D_MODEL), F32),
            pltpu.VMEM((t, D_MODEL), BF16),
        ],
        compiler_params=_cparams(("arbitrary",)),
        name="conv_layer",
    )(x, norm_g.reshape(1, D_MODEL), mod, w_in_all, cw, w_out_all)


_ROUTE_ROWS = LANES
_EXPERT_ROW0 = SUBLANES


_ROUTE_TILES = 4
_TILE_SHIFT = MOE_TILE.bit_length() - 1
_LANE_SHIFT = LANES.bit_length() - 1


def _route_kernel(x_ref, g_ref, mod_ref, wr_hi_ref, wr_lo_ref, br_ref,
                  h_ref, route_ref, cnt_ref):
    h, route, cnt = _route_tiles(x_ref[...], g_ref[...], mod_ref[0],
                                 wr_hi_ref[...], wr_lo_ref[...], br_ref[...])
    h_ref[...] = h
    route_ref[...] = route
    for j in range(_ROUTE_TILES):
        cnt_ref[j] = cnt[:, j * LANES:(j + 1) * LANES]


def _route_tiles(x, g, mod, wr_hi, wr_lo, br):
    t = MOE_TILE * _ROUTE_TILES
    h = _rms_mod(x, g, mod)
    h_hi = h.astype(BF16)
    h_lo = (h - h_hi.astype(F32)).astype(BF16)
    logits = (_dot_nt(wr_hi, h_hi) + _dot_nt(wr_hi, h_lo)
              + _dot_nt(wr_lo, h_hi)) + br

    sub = lax.broadcasted_iota(I32, (SUBLANES, t), 0)
    neg_inf = -jnp.inf
    gl = jnp.where(sub < N_GROUPS, logits[0:SUBLANES, :], neg_inf)
    gmax = jnp.max(gl, axis=0, keepdims=True)
    g_sel = jnp.min(jnp.where(gl == gmax, sub, SUBLANES), axis=0, keepdims=True)
    p_sel = 1.0 / jnp.sum(jnp.exp(gl - gmax), axis=0, keepdims=True)

    e_sel = jnp.zeros((E_PER_GROUP, t), F32)
    for g in range(N_GROUPS):
        r0 = _EXPERT_ROW0 + g * E_PER_GROUP
        e_sel = jnp.where(g_sel == g, logits[r0:r0 + E_PER_GROUP, :], e_sel)
    v1 = jnp.max(e_sel, axis=0, keepdims=True)
    i1 = jnp.min(jnp.where(e_sel == v1, sub, SUBLANES), axis=0, keepdims=True)
    e_rest = jnp.where(sub == i1, neg_inf, e_sel)
    v2 = jnp.max(e_rest, axis=0, keepdims=True)
    i2 = jnp.min(jnp.where(e_rest == v2, sub, SUBLANES), axis=0, keepdims=True)
    ratio = jnp.exp(v2 - v1)
    w1 = p_sel / (1.0 + ratio)
    w2 = p_sel * ratio / (1.0 + ratio)
    eid1 = g_sel * E_PER_GROUP + i1
    eid2 = g_sel * E_PER_GROUP + i2

    erow = lax.broadcasted_iota(I32, (N_EXPERTS, t), 0)
    m1 = erow == eid1
    m2 = erow == eid2
    member = jnp.where(m1 | m2, 1.0, 0.0)
    r = lax.broadcasted_iota(I32, (t, t), 0)
    c = lax.broadcasted_iota(I32, (t, t), 1)
    same_tile = (lax.shift_right_logical(r, _TILE_SHIFT)
                 == lax.shift_right_logical(c, _TILE_SHIFT))
    earlier = jnp.where((r < c) & same_tile, 1.0, 0.0).astype(BF16)
    member_b = member.astype(BF16)
    rank = _dot(member_b, earlier)
    tr = lax.broadcasted_iota(I32, (t, _ROUTE_TILES * LANES), 0)
    tc = lax.broadcasted_iota(I32, (t, _ROUTE_TILES * LANES), 1)
    in_tile = jnp.where(lax.shift_right_logical(tr, _TILE_SHIFT)
                        == lax.shift_right_logical(tc, _LANE_SHIFT), 1.0, 0.0)
    cnt = _dot(member_b, in_tile.astype(BF16))
    cnt_pad = jnp.floor((cnt + (SEG_ALIGN - 1.0)) * (1.0 / SEG_ALIGN)) * SEG_ALIGN
    er = lax.broadcasted_iota(I32, (N_EXPERTS, N_EXPERTS), 0)
    ec = lax.broadcasted_iota(I32, (N_EXPERTS, N_EXPERTS), 1)
    before = jnp.where(er > ec, 1.0, 0.0).astype(BF16)
    seg_start = _dot(before, cnt_pad.astype(BF16))
    seg_start_tok = jnp.concatenate(
        [seg_start[:, j * LANES:(j + 1) * LANES]
         for j in range(_ROUTE_TILES) for _ in range(MOE_TILE // LANES)], axis=1)
    pos = seg_start_tok + rank
    pos1 = jnp.sum(jnp.where(m1, pos, 0.0), axis=0, keepdims=True)
    pos2 = jnp.sum(jnp.where(m2, pos, 0.0), axis=0, keepdims=True)

    out = jnp.zeros((SUBLANES, t), F32)
    for k, val in enumerate((pos1, pos2, w1, w2)):
        out = jnp.where(sub == k, val, out)
    return h_hi, out, cnt


def _route(x, norm_g, mod, w_group, b_group, w_expert, b_expert):
    t = MOE_TILE * _ROUTE_TILES
    tiles_per_seq = SEQ // t
    wr = jnp.zeros((_ROUTE_ROWS, D_MODEL), F32)
    wr = wr.at[0:N_GROUPS].set(w_group.T.astype(F32))
    wr = wr.at[_EXPERT_ROW0:_EXPERT_ROW0 + N_EXPERTS].set(w_expert.T.astype(F32))
    wr_hi = wr.astype(BF16)
    wr_lo = (wr - wr_hi.astype(F32)).astype(BF16)
    br = jnp.zeros((_ROUTE_ROWS,), F32)
    br = br.at[0:N_GROUPS].set(b_group.astype(F32))
    br = br.at[_EXPERT_ROW0:_EXPERT_ROW0 + N_EXPERTS].set(b_expert.astype(F32))
    full = lambda shape: pl.BlockSpec(shape, lambda i: (0, 0))
    return pl.pallas_call(
        _route_kernel,
        out_shape=(
            jax.ShapeDtypeStruct((N_TOK, D_MODEL), BF16),
            jax.ShapeDtypeStruct((SUBLANES, N_TOK), F32),
            jax.ShapeDtypeStruct((N_MOE_TILES, N_EXPERTS, LANES), F32),
        ),
        grid=(N_MOE_TILES // _ROUTE_TILES,),
        in_specs=[
            pl.BlockSpec((t, D_MODEL), lambda i: (i, 0)),
            full((1, D_MODEL)),
            pl.BlockSpec((1, 1, 3 * D_MODEL), lambda i: (i // tiles_per_seq, 0, 0)),
            full((_ROUTE_ROWS, D_MODEL)),
            full((_ROUTE_ROWS, D_MODEL)),
            full((_ROUTE_ROWS, 1)),
        ],
        out_specs=(
            pl.BlockSpec((t, D_MODEL), lambda i: (i, 0)),
            pl.BlockSpec((SUBLANES, t), lambda i: (0, i)),
            pl.BlockSpec((_ROUTE_TILES, N_EXPERTS, LANES), lambda i: (i, 0, 0)),
        ),
        compiler_params=_cparams(("arbitrary",)),
        name="moe_route",
    )(x, norm_g.reshape(1, D_MODEL), mod, wr_hi, wr_lo, br.reshape(_ROUTE_ROWS, 1))


_REM_UNITS = CHUNK // SEG_ALIGN
_REM_SHIFT = _REM_UNITS.bit_length() - 1


def _chunk_code(tokens):
    return (tokens // CHUNK) * _REM_UNITS + (tokens % CHUNK) // SEG_ALIGN


def _dispatch_plan(cnt):
    cnt = cnt.astype(I32)
    cnt_pad = (cnt + SEG_ALIGN - 1) // SEG_ALIGN * SEG_ALIGN
    seg = jnp.cumsum(cnt_pad, axis=1) - cnt_pad
    tot = jnp.sum(cnt_pad, axis=0)
    ptot = (tot + EXPERT_BLOCK - 1) // EXPERT_BLOCK * EXPERT_BLOCK
    pend = jnp.cumsum(ptot)
    gbase = pend - ptot
    dst = gbase[None, :] + jnp.cumsum(cnt_pad, axis=0) - cnt_pad
    nch = _chunk_code(cnt_pad)
    n_used = (pend[-1] // EXPERT_BLOCK).astype(I32)
    blk = jnp.arange(N_EXPERT_BLOCKS, dtype=I32)
    blk_start = jnp.minimum(blk, n_used - 1) * EXPERT_BLOCK
    blk_e = jnp.sum((pend[None, :] <= blk_start[:, None]).astype(I32), axis=1)
    blk_e = jnp.minimum(blk_e, N_EXPERTS - 1).astype(I32)
    gap_dst = gbase + tot
    gap_nch = _chunk_code(ptot - tot)
    misc = jnp.stack([n_used, jnp.sum(ptot - tot) // SEG_ALIGN]).astype(I32)
    first_blk = gbase // EXPERT_BLOCK
    ids = jnp.arange(N_EXPERTS, dtype=I32)
    later = (ids[None, :] > ids[:, None]) & (ptot[None, :] > 0)
    next_e = jnp.min(jnp.where(later, ids[None, :], N_EXPERTS), axis=1)
    next_e = jnp.where(next_e == N_EXPERTS, -1, next_e)
    return dict(seg=seg.reshape(-1).astype(I32), dst=dst.reshape(-1).astype(I32),
                nch=nch.reshape(-1).astype(I32),
                ntot=(jnp.sum(cnt_pad, axis=1) // SEG_ALIGN).astype(I32),
                nmax=(jnp.max(cnt_pad, axis=1) // CHUNK).astype(I32), blk_e=blk_e,
                first_blk=first_blk.astype(I32), next_e=next_e.astype(I32),
                gap_dst=gap_dst.astype(I32), gap_nch=gap_nch.astype(I32), misc=misc)


_INLINE_CHUNKS = 3


def _start_copy(make_copy, s0, d0, offset, tokens, priority=0):
    s = pl.multiple_of(s0 + offset, SEG_ALIGN)
    d = pl.multiple_of(d0 + offset, SEG_ALIGN)
    make_copy(s, d, tokens).start(priority=priority)


def _start_tail(make_copy, s0, d0, n_full, rem, priority=0):
    half, quarter = CHUNK // 2, CHUNK // 4
    tail = n_full * CHUNK
    has_half = (rem & 2) != 0
    pl.when(has_half)(functools.partial(_start_copy, make_copy, s0, d0, tail, half,
                                        priority))
    tail2 = tail + jnp.where(has_half, half, 0)
    pl.when((rem & 1) != 0)(functools.partial(_start_copy, make_copy, s0, d0, tail2,
                                              quarter, priority))


def _segment_copies_inline(tile, live, seg_ref, dst_ref, nch_ref, make_copy):
    for e in range(N_EXPERTS):
        idx = tile * N_EXPERTS + e
        code = jnp.where(live, nch_ref[idx], 0)
        n = lax.shift_right_logical(code, _REM_SHIFT)
        s0 = seg_ref[idx]
        d0 = dst_ref[idx]
        for cidx in range(_INLINE_CHUNKS):
            pl.when(cidx < n)(functools.partial(_start_copy, make_copy, s0, d0,
                                                cidx * CHUNK, CHUNK, (e + cidx) % 2))
        _start_tail(make_copy, s0, d0, n, code & (_REM_UNITS - 1), e % 2)


def _segment_copies_loop(tile, first, with_tail, seg_ref, dst_ref, nch_ref, make_copy):
    def per_expert(e, carry):
        idx = tile * N_EXPERTS + e
        code = nch_ref[idx]
        n = lax.shift_right_logical(code, _REM_SHIFT)
        s0 = seg_ref[idx]
        d0 = dst_ref[idx]

        def per_chunk(cidx, c2):
            _start_copy(make_copy, s0, d0, cidx * CHUNK, CHUNK)
            return c2

        lax.fori_loop(first, jnp.maximum(n, first), per_chunk, 0)
        if with_tail:
            _start_tail(make_copy, s0, d0, n, code & (_REM_UNITS - 1))
        return carry

    lax.fori_loop(0, N_EXPERTS, per_expert, 0)


_WAIT_GROUP = 64


def _wait_each(n, make_wait):
    def body(_, carry):
        make_wait().wait()
        return carry
    lax.fori_loop(0, n, body, 0)


def _wait_copies(units, make_copy):
    _wait_each(units // _WAIT_GROUP, lambda: make_copy(0, 0, _WAIT_GROUP * SEG_ALIGN))
    _wait_each(units % _WAIT_GROUP, lambda: make_copy(0, 0, SEG_ALIGN))


_HALF = D_MODEL // 2
_BF16_BITS = 16
_HI_MASK = -(1 << _BF16_BITS)


def _pack_pairs(x):
    lo = lax.shift_right_logical(lax.bitcast_convert_type(x[:, 0:_HALF], I32),
                                 _BF16_BITS)
    hi = lax.bitcast_convert_type(x[:, _HALF:D_MODEL], I32) & _HI_MASK
    return lo | hi


def _unpack_pairs(w):
    lo = lax.bitcast_convert_type(lax.shift_left(w, _BF16_BITS), F32).astype(BF16)
    hi = lax.bitcast_convert_type(w & _HI_MASK, F32).astype(BF16)
    return lo, hi


def _token_rows(ref, tok, tokens):
    start = pl.multiple_of(tok * WORD_ROWS, SUBLANES)
    return ref.at[pl.ds(start, tokens * WORD_ROWS)]


def _store_words(ref, words, first=0):
    rows = words.shape[0]
    for q in range(WORD_ROWS):
        ref[pl.ds(first * WORD_ROWS + q, rows, stride=WORD_ROWS), :] = (
            words[:, q * LANES:(q + 1) * LANES])


def _load_words(ref, rows, first=0):
    return jnp.concatenate(
        [ref[pl.ds(first * WORD_ROWS + q, rows, stride=WORD_ROWS), :]
         for q in range(WORD_ROWS)], axis=1)


_DISPATCH_BUFS = 3
_TILES_PER_STEP = 2


def _dispatch_kernel(seg_ref, dst_ref, nch_ref, ntot_ref, nmax_ref, gap_dst_ref,
                     gap_nch_ref, misc_ref, route_ref, h_ref, xs_ref, buf_ref,
                     zero_ref, sem, zsem):
    t = MOE_TILE
    last = N_MOE_TILES - 1
    n_used = misc_ref[0]

    def make_copy(which):
        def mk(s, d, tokens):
            return pltpu.make_async_copy(_token_rows(buf_ref.at[which], s, tokens),
                                         _token_rows(xs_ref, d, tokens), sem.at[which])
        return mk

    def zero_copy(s, d, tokens):
        del s
        return pltpu.make_async_copy(_token_rows(zero_ref, 0, tokens),
                                     _token_rows(xs_ref, d, tokens), zsem)

    @pl.when(pl.program_id(0) == 0)
    def _():
        zero_ref[...] = jnp.zeros_like(zero_ref)

        def per_expert(e, carry):
            d0 = gap_dst_ref[e]
            code = gap_nch_ref[e]
            n = lax.shift_right_logical(code, _REM_SHIFT)

            def per_chunk(cidx, c2):
                _start_copy(zero_copy, 0, d0, cidx * CHUNK, CHUNK)
                return c2

            lax.fori_loop(0, n, per_chunk, 0)
            _start_tail(zero_copy, 0, d0, n, code & (_REM_UNITS - 1))
            return carry

        lax.fori_loop(0, N_EXPERTS, per_expert, 0)

        def per_block(b, carry):
            zero_copy(0, b * EXPERT_BLOCK, EXPERT_BLOCK).start()
            return carry

        lax.fori_loop(n_used, N_EXPERT_BLOCKS, per_block, 0)

    def sort_tile(j):
        k = pl.program_id(0) * _TILES_PER_STEP + j
        slot = k % _DISPATCH_BUFS
        prev = (k + _DISPATCH_BUFS - 1) % _DISPATCH_BUFS
        prev2 = (k + _DISPATCH_BUFS - 2) % _DISPATCH_BUFS
        tile_prev = jnp.maximum(k - 1, 0)
        _segment_copies_inline(tile_prev, k > 0, seg_ref, dst_ref, nch_ref,
                               make_copy(prev))

        route = route_ref[:, j * t:(j + 1) * t]
        pos1 = route[0:1, :].astype(I32)
        pos2 = route[1:2, :].astype(I32)
        r = lax.broadcasted_iota(I32, (DISPATCH_ROWS, t), 0)
        perm = jnp.where((r == pos1) | (r == pos2), 1.0, 0.0).astype(BF16)
        _store_words(buf_ref.at[slot],
                     _pack_pairs(_dot(perm, h_ref[j * t:(j + 1) * t, :])))

        @pl.when((k > 0) & (nmax_ref[tile_prev] > _INLINE_CHUNKS))
        def _():
            _segment_copies_loop(tile_prev, _INLINE_CHUNKS, False, seg_ref, dst_ref,
                                 nch_ref, make_copy(prev))

        @pl.when(k > 1)
        def _():
            _wait_copies(ntot_ref[jnp.maximum(k - 2, 0)], make_copy(prev2))

        @pl.when(k == last)
        def _():
            _segment_copies_loop(k, 0, True, seg_ref, dst_ref, nch_ref, make_copy(slot))
            _wait_copies(ntot_ref[last - 1], make_copy(prev))
            _wait_copies(ntot_ref[last], make_copy(slot))
            _wait_copies(misc_ref[1], zero_copy)
            _wait_each(N_EXPERT_BLOCKS - n_used, lambda: zero_copy(0, 0, EXPERT_BLOCK))

    for j in range(_TILES_PER_STEP):
        sort_tile(j)


def _dispatch(plan, route, h):
    t = MOE_TILE * _TILES_PER_STEP
    return pl.pallas_call(
        _dispatch_kernel,
        out_shape=jax.ShapeDtypeStruct((SORTED_ROWS * WORD_ROWS, LANES), I32),
        grid_spec=pltpu.PrefetchScalarGridSpec(
            num_scalar_prefetch=8,
            grid=(N_MOE_TILES // _TILES_PER_STEP,),
            in_specs=[
                pl.BlockSpec((SUBLANES, t), lambda i, *_: (0, i)),
                pl.BlockSpec((t, D_MODEL), lambda i, *_: (i, 0)),
            ],
            out_specs=pl.BlockSpec(memory_space=pl.ANY),
            scratch_shapes=[
                pltpu.VMEM((_DISPATCH_BUFS, DISPATCH_ROWS * WORD_ROWS, LANES), I32),
                pltpu.VMEM((EXPERT_BLOCK * WORD_ROWS, LANES), I32),
                pltpu.SemaphoreType.DMA((_DISPATCH_BUFS,)),
                pltpu.SemaphoreType.DMA,
            ],
        ),
        compiler_params=_cparams(("arbitrary",)),
        name="moe_dispatch",
    )(plan["seg"], plan["dst"], plan["nch"], plan["ntot"], plan["nmax"],
      plan["gap_dst"], plan["gap_nch"], plan["misc"], route, h)


_WEIGHT_DMA_PRIORITY = 1


def _expert_kernel(blk_e_ref, first_ref, next_ref, misc_ref, x_ref, wg_hbm, wu_hbm,
                   wd_hbm, y_ref, wg_f, wu_f, wd_f, wg_b, wu_b, wd_b, sem, *, layer):
    blk0 = pl.program_id(0) * _BLOCKS_PER_STEP
    blk1 = blk0 + 1
    n_used = misc_ref[0]

    def weight_copies(e):
        return (pltpu.make_async_copy(wg_hbm.at[layer, e], wg_f, sem.at[0]),
                pltpu.make_async_copy(wu_hbm.at[layer, e], wu_f, sem.at[1]),
                pltpu.make_async_copy(wd_hbm.at[layer, e], wd_f, sem.at[2]))

    def take_weights(e):
        for cp in weight_copies(e):
            cp.wait()
        wg_b[...] = wg_f[...].astype(BF16)
        wu_b[...] = wu_f[...].astype(BF16)
        wd_b[...] = wd_f[...].astype(BF16)

        @pl.when(next_ref[e] >= 0)
        def _():
            for cp in weight_copies(next_ref[e]):
                cp.start(priority=_WEIGHT_DMA_PRIORITY)

    def run_rows(first, rows):
        x_lo, x_hi = _unpack_pairs(_load_words(x_ref, rows, first))
        x = jnp.concatenate([x_lo, x_hi], axis=1)
        g = _dot(x, wg_b[...])
        u = _dot(x, wu_b[...])
        a = (g * _sigmoid(g) * u).astype(BF16)
        y = _dot(a, wd_b[...])
        _store_words(y_ref, _pack_pairs(y.astype(BF16).astype(F32)), first)

    @pl.when(blk0 == 0)
    def _():
        for cp in weight_copies(blk_e_ref[0]):
            cp.start(priority=_WEIGHT_DMA_PRIORITY)

    @pl.when(blk0 < n_used)
    def _():
        e0 = blk_e_ref[blk0]
        e1 = blk_e_ref[blk1]
        pl.when(blk0 == first_ref[e0])(functools.partial(take_weights, e0))

        @pl.when(e0 == e1)
        def _():
            run_rows(0, _BLOCKS_PER_STEP * EXPERT_BLOCK)

        @pl.when(e0 != e1)
        def _():
            run_rows(0, EXPERT_BLOCK)
            take_weights(e1)
            run_rows(EXPERT_BLOCK, EXPERT_BLOCK)

    @pl.when(blk0 >= n_used)
    def _():
        y_ref[...] = jnp.zeros_like(y_ref)


def _experts(plan, xs, layer, w_gate, w_up, w_down):
    step_rows = _BLOCKS_PER_STEP * EXPERT_BLOCK * WORD_ROWS
    row_map = lambda i, be, fi, nx, misc: (
        jnp.minimum(i, (misc[0] - 1) // _BLOCKS_PER_STEP), 0)
    out_map = lambda i, be, fi, nx, misc: (i, 0)
    return pl.pallas_call(
        functools.partial(_expert_kernel, layer=layer),
        out_shape=jax.ShapeDtypeStruct((SORTED_ROWS * WORD_ROWS, LANES), I32),
        grid_spec=pltpu.PrefetchScalarGridSpec(
            num_scalar_prefetch=4,
            grid=(N_EXPERT_BLOCKS // _BLOCKS_PER_STEP,),
            in_specs=[
                pl.BlockSpec((step_rows, LANES), row_map),
                pl.BlockSpec(memory_space=pl.ANY),
                pl.BlockSpec(memory_space=pl.ANY),
                pl.BlockSpec(memory_space=pl.ANY),
            ],
            out_specs=pl.BlockSpec((step_rows, LANES), out_map),
            scratch_shapes=[
                pltpu.VMEM((D_MODEL, D_FF), F32),
                pltpu.VMEM((D_MODEL, D_FF), F32),
                pltpu.VMEM((D_FF, D_MODEL), F32),
                pltpu.VMEM((D_MODEL, D_FF), BF16),
                pltpu.VMEM((D_MODEL, D_FF), BF16),
                pltpu.VMEM((D_FF, D_MODEL), BF16),
                pltpu.SemaphoreType.DMA((3,)),
            ],
        ),
        compiler_params=_cparams(("arbitrary",)),
        name="moe_experts",
    )(plan["blk_e"], plan["first_blk"], plan["next_e"], plan["misc"], xs,
      w_gate, w_up, w_down)


_COMBINE_BUFS = 3


def _combine_kernel(seg_ref, dst_ref, nch_ref, ntot_ref, nmax_ref, route_ref, x_ref,
                    mod_ref, fg_ref, ys_ref, out_ref, *scratch, final_norm):
    bufs, sem = scratch[:_COMBINE_BUFS], scratch[_COMBINE_BUFS]
    t = MOE_TILE
    ahead = _COMBINE_BUFS - 1

    def make_copy(which):
        def mk(s, d, tokens):
            return pltpu.make_async_copy(_token_rows(ys_ref, d, tokens),
                                         _token_rows(bufs[which], s, tokens),
                                         sem.at[which])
        return mk

    @pl.when(pl.program_id(0) == 0)
    def _():
        for buf in bufs:
            buf[...] = jnp.zeros_like(buf)
        for k in range(_COMBINE_BUFS - 1):
            _segment_copies_loop(k, 0, True, seg_ref, dst_ref, nch_ref, make_copy(k))

    def unsort_tile(j, k, cur):
        nxt = (cur + ahead) % _COMBINE_BUFS
        tile_next = jnp.minimum(k + ahead, N_MOE_TILES - 1)
        has_next = k + ahead < N_MOE_TILES
        rows = pl.ds(j * t, t)
        _wait_copies(ntot_ref[k], make_copy(cur))
        _segment_copies_inline(tile_next, has_next, seg_ref, dst_ref, nch_ref,
                               make_copy(nxt))
        route = route_ref[:, j * t:(j + 1) * t]
        route_t = jnp.concatenate(
            [route, jnp.zeros((LANES - SUBLANES, t), F32)], axis=0).T
        pos1 = route_t[:, 0:1].astype(I32)
        pos2 = route_t[:, 1:2].astype(I32)
        w1 = route_t[:, 2:3]
        w2 = route_t[:, 3:4]
        c = lax.broadcasted_iota(I32, (t, SORT_ROWS), 1)
        unsort = (jnp.where(c == pos1, w1, 0.0)
                  + jnp.where(c == pos2, w2, 0.0)).astype(BF16)
        y_lo, y_hi = _unpack_pairs(_load_words(bufs[cur], SORT_ROWS))
        moe = jnp.concatenate([_dot(unsort, y_lo), _dot(unsort, y_hi)], axis=1)
        gate = mod_ref[0][:, 2 * D_MODEL:3 * D_MODEL]
        x_new = x_ref[rows, :] + gate * moe
        if final_norm:
            ms = jnp.mean(x_new * x_new, axis=-1, keepdims=True)
            x_new = (x_new * lax.rsqrt(ms + EPS)) * fg_ref[...]
        out_ref[rows, :] = x_new

        @pl.when(has_next & (nmax_ref[tile_next] > _INLINE_CHUNKS))
        def _():
            _segment_copies_loop(tile_next, _INLINE_CHUNKS, False, seg_ref, dst_ref,
                                 nch_ref, make_copy(nxt))

    for j in range(_TILES_PER_STEP):
        k = pl.program_id(0) * _TILES_PER_STEP + j
        for cur in range(_COMBINE_BUFS):
            pl.when(k % _COMBINE_BUFS == cur)(functools.partial(unsort_tile, j, k, cur))


def _combine(plan, route, x, mod, final_g, ys, final_norm):
    t = MOE_TILE * _TILES_PER_STEP
    tiles_per_seq = SEQ // t
    return pl.pallas_call(
        functools.partial(_combine_kernel, final_norm=final_norm),
        out_shape=jax.ShapeDtypeStruct((N_TOK, D_MODEL), F32),
        grid_spec=pltpu.PrefetchScalarGridSpec(
            num_scalar_prefetch=5,
            grid=(N_MOE_TILES // _TILES_PER_STEP,),
            in_specs=[
                pl.BlockSpec((SUBLANES, t), lambda i, *_: (0, i)),
                pl.BlockSpec((t, D_MODEL), lambda i, *_: (i, 0)),
                pl.BlockSpec((1, 1, 3 * D_MODEL), lambda i, *_: (i // tiles_per_seq, 0, 0)),
                pl.BlockSpec((1, D_MODEL), lambda i, *_: (0, 0)),
                pl.BlockSpec(memory_space=pl.ANY),
            ],
            out_specs=pl.BlockSpec((t, D_MODEL), lambda i, *_: (i, 0)),
            scratch_shapes=(
                [pltpu.VMEM((SORT_ROWS * WORD_ROWS, LANES), I32)] * _COMBINE_BUFS
                + [pltpu.SemaphoreType.DMA((_COMBINE_BUFS,))]),
        ),
        compiler_params=_cparams(("arbitrary",)),
        name="moe_combine",
    )(plan["seg"], plan["dst"], plan["nch"], plan["ntot"], plan["nmax"], route, x,
      mod, final_g.reshape(1, D_MODEL).astype(F32), ys)


def _moe_layer(x, norm_g, mod, final_g, w_group, b_group, w_expert, b_expert,
               layer, w_gate, w_up, w_down, final_norm):
    h, route, cnt = _route(x, norm_g, mod, w_group, b_group, w_expert, b_expert)
    plan = _dispatch_plan(cnt[:, :, 0])
    xs = _dispatch(plan, route, h)
    ys = _experts(plan, xs, layer, w_gate, w_up, w_down)
    return _combine(plan, route, x, mod, final_g, ys, final_norm)


def kernel(x, c, ada_w, ada_b, norm_g, final_g, m_w_in, m_b_gates, m_norm_g, m_w_out, s_w_in, s_conv_w, s_w_out, r_w_group, r_b_group, r_w_expert, r_b_expert, e_w_gate, e_w_up, e_w_down):
    mods = _ada_mods(c, ada_w, ada_b)
    m_w_in_b, m_w_out_b = m_w_in.astype(BF16), m_w_out.astype(BF16)
    s_w_in_b, s_w_out_b = s_w_in.astype(BF16), s_w_out.astype(BF16)
    xt = x.reshape(N_TOK, D_MODEL)
    for i in range(DEPTH):
        mod_mix = mods[2 * i].reshape(BATCH, 1, 3 * D_MODEL)
        mod_ffn = mods[2 * i + 1].reshape(BATCH, 1, 3 * D_MODEL)
        j = i // 2
        if i % 2 == 0:
            qt, k, vt, ot, gates, gatest = _mlstm_in(xt, norm_g[i, 0], mod_mix,
                                                     m_w_in_b[j], m_b_gates[j])
            xt = _mlstm_rec(xt, mod_mix, qt, k, vt, ot, gates, gatest,
                            m_norm_g[j], m_w_out_b, j)
        else:
            xt = _conv_layer(xt, norm_g[i, 0], mod_mix, s_w_in_b, s_conv_w[j],
                             s_w_out_b, j)
        xt = _moe_layer(xt, norm_g[i, 1], mod_ffn, final_g, r_w_group[i],
                        r_b_group[i], r_w_expert[i], r_b_expert[i], i, e_w_gate,
                        e_w_up, e_w_down, final_norm=(i == DEPTH - 1))
    return xt.reshape(BATCH, SEQ, D_MODEL)
```

```python
import functools

import jax
import jax.numpy as jnp
from jax import lax
from jax.experimental import pallas as pl
from jax.experimental.pallas import tpu as pltpu

F32 = jnp.float32
BF16 = jnp.bfloat16
I32 = jnp.int32

D_MODEL = 1024
BATCH = 8
SEQ = 2048
DEPTH = 4
N_TOK = BATCH * SEQ
M_HEADS = 4
M_DK = 128
M_DV = 256
M_QK = M_HEADS * M_DK
M_V = M_HEADS * M_DV
CONV_K = 3
N_GROUPS = 4
E_PER_GROUP = 8
N_EXPERTS = N_GROUPS * E_PER_GROUP
TOP_K = 2
D_FF = 512
EPS = 1e-6

SUBLANES = 8
LANES = 128

ROW_TILE = 1024
MLSTM_CHUNK = 256
MOE_TILE = 256
EXPERT_BLOCK = 256
WORD_ROWS = (D_MODEL // 2) // LANES
SEG_ALIGN = SUBLANES // WORD_ROWS
CHUNK = 16
_MAX_TILE_SORTED = MOE_TILE * TOP_K + N_EXPERTS * (SEG_ALIGN - 1)
SORT_ROWS = -(-_MAX_TILE_SORTED // LANES) * LANES
BF16_SUBLANES = 2 * SUBLANES
DISPATCH_ROWS = -(-_MAX_TILE_SORTED // BF16_SUBLANES) * BF16_SUBLANES
N_MOE_TILES = N_TOK // MOE_TILE
_MAX_SORTED = (N_MOE_TILES * (MOE_TILE * TOP_K + N_EXPERTS * (SEG_ALIGN - 1))
               + N_EXPERTS * (EXPERT_BLOCK - 1))
_BLOCKS_PER_STEP = 2
N_EXPERT_BLOCKS = (-(-_MAX_SORTED // (EXPERT_BLOCK * _BLOCKS_PER_STEP))
                   * _BLOCKS_PER_STEP)
SORTED_ROWS = N_EXPERT_BLOCKS * EXPERT_BLOCK

VMEM_LIMIT = 48 * 1024 * 1024


def _cparams(sem):
    return pltpu.CompilerParams(dimension_semantics=sem,
                                vmem_limit_bytes=VMEM_LIMIT)


def _dot(a, b):
    return jnp.dot(a, b, preferred_element_type=F32)


def _dot_nt(a, b):
    return lax.dot_general(a, b, (((1,), (1,)), ((), ())),
                           preferred_element_type=F32)


def _split3(x):
    hi = x.astype(BF16)
    r1 = x - hi.astype(F32)
    mid = r1.astype(BF16)
    lo = (r1 - mid.astype(F32)).astype(BF16)
    return hi, mid, lo


def _dot_sel_left(sel, x):
    hi, mid, lo = _split3(x)
    return _dot(sel, hi) + _dot(sel, mid) + _dot(sel, lo)


def _dot_sel_right(x, sel):
    hi, mid, lo = _split3(x)
    return _dot(hi, sel) + _dot(mid, sel) + _dot(lo, sel)


def _sigmoid(x):
    return 1.0 / (1.0 + jnp.exp(-x))


def _rms_mod(x, g, mod):
    ms = jnp.mean(x * x, axis=-1, keepdims=True)
    y = (x * lax.rsqrt(ms + EPS)) * g
    return y * (1.0 + mod[:, D_MODEL:2 * D_MODEL]) + mod[:, 0:D_MODEL]


def _ada_kernel(c_ref, w_ref, b_ref, o_ref):
    c = c_ref[...]
    s = (c * _sigmoid(c)).astype(BF16)
    o_ref[0] = _dot(s, w_ref[0].astype(BF16)) + b_ref[0]


def _ada_mods(c, ada_w, ada_b):
    n_pairs = DEPTH * 2
    w = ada_w.reshape(n_pairs, D_MODEL, 3 * D_MODEL)
    b = ada_b.reshape(n_pairs, 1, 3 * D_MODEL)
    col = D_MODEL
    return pl.pallas_call(
        _ada_kernel,
        out_shape=jax.ShapeDtypeStruct((n_pairs, BATCH, 3 * D_MODEL), F32),
        grid=(n_pairs, 3 * D_MODEL // col),
        in_specs=[
            pl.BlockSpec((BATCH, D_MODEL), lambda p, j: (0, 0)),
            pl.BlockSpec((1, D_MODEL, col), lambda p, j: (p, 0, j)),
            pl.BlockSpec((1, 1, col), lambda p, j: (p, 0, j)),
        ],
        out_specs=pl.BlockSpec((1, BATCH, col), lambda p, j: (p, 0, j)),
        compiler_params=_cparams(("arbitrary", "arbitrary")),
        name="ada_mods",
    )(c, w, b)


def _mlstm_in_kernel(x_ref, g_ref, mod_ref, wqt_ref, wk_ref, wvt_ref, wot_ref,
                     wg_ref, bg_ref, qt_ref, k_ref, vt_ref, ot_ref, gates_ref,
                     gatest_ref):
    h = _rms_mod(x_ref[...], g_ref[...], mod_ref[0])
    hb = h.astype(BF16)
    qt_ref[...] = (_dot_nt(wqt_ref[...], hb) * (M_DK ** -0.5)).astype(BF16)
    k_ref[...] = _dot(hb, wk_ref[...]).astype(BF16)
    vt_ref[...] = _dot_nt(wvt_ref[...], hb).astype(BF16)
    ot_ref[...] = _dot_nt(wot_ref[...], hb).astype(BF16)
    g = _dot(hb, wg_ref[...]) + bg_ref[...]
    log_sig = jnp.minimum(g, 0.0) - jnp.log(1.0 + jnp.exp(-jnp.abs(g)))
    lane = lax.broadcasted_iota(I32, g.shape, 1)
    gg = jnp.where(lane < M_HEADS, g, log_sig)
    gates_ref[...] = gg
    gatest_ref[...] = gg.T[0:SUBLANES, :]


def _mlstm_in(x, norm_g, mod, w_in, b_gates):
    t = ROW_TILE
    tiles_per_seq = SEQ // t
    wqt = w_in[:, 0:M_QK].T.astype(BF16)
    wk = w_in[:, M_QK:2 * M_QK].astype(BF16)
    wvt = w_in[:, 2 * M_QK:2 * M_QK + M_V].T.astype(BF16)
    wot = w_in[:, 2 * M_QK + M_V:2 * M_QK + 2 * M_V].T.astype(BF16)
    n_gate = 2 * M_HEADS
    wg = jnp.pad(w_in[:, 2 * M_QK + 2 * M_V:], ((0, 0), (0, LANES - n_gate))).astype(BF16)
    bg = jnp.pad(b_gates.astype(F32), (0, LANES - n_gate)).reshape(1, LANES)
    full = lambda shape: pl.BlockSpec(shape, lambda i: (0, 0))
    return pl.pallas_call(
        _mlstm_in_kernel,
        out_shape=(
            jax.ShapeDtypeStruct((M_QK, N_TOK), BF16),
            jax.ShapeDtypeStruct((N_TOK, M_QK), BF16),
            jax.ShapeDtypeStruct((M_V, N_TOK), BF16),
            jax.ShapeDtypeStruct((M_V, N_TOK), BF16),
            jax.ShapeDtypeStruct((N_TOK, LANES), F32),
            jax.ShapeDtypeStruct((SUBLANES, N_TOK), F32),
        ),
        grid=(N_TOK // t,),
        in_specs=[
            pl.BlockSpec((t, D_MODEL), lambda i: (i, 0)),
            full((1, D_MODEL)),
            pl.BlockSpec((1, 1, 3 * D_MODEL), lambda i: (i // tiles_per_seq, 0, 0)),
            full((M_QK, D_MODEL)),
            full((D_MODEL, M_QK)),
            full((M_V, D_MODEL)),
            full((M_V, D_MODEL)),
            full((D_MODEL, LANES)),
            full((1, LANES)),
        ],
        out_specs=(
            pl.BlockSpec((M_QK, t), lambda i: (0, i)),
            pl.BlockSpec((t, M_QK), lambda i: (i, 0)),
            pl.BlockSpec((M_V, t), lambda i: (0, i)),
            pl.BlockSpec((M_V, t), lambda i: (0, i)),
            pl.BlockSpec((t, LANES), lambda i: (i, 0)),
            pl.BlockSpec((SUBLANES, t), lambda i: (0, i)),
        ),
        compiler_params=_cparams(("arbitrary",)),
        name="mlstm_in",
    )(x, norm_g.reshape(1, D_MODEL), mod, wqt, wk, wvt, wot, wg, bg)


def _mlstm_rec_kernel(qt_ref, k_ref, vt_ref, ot_ref, gates_ref, gatest_ref,
                      x_ref, mod_ref, ng_ref, wout_ref, out_ref, ct_ref, m_ref):
    L = MLSTM_CHUNK

    @pl.when(pl.program_id(1) == 0)
    def _():
        ct_ref[...] = jnp.zeros_like(ct_ref)
        m_ref[...] = jnp.zeros_like(m_ref)

    gates = gates_ref[...]
    gatest = gatest_ref[...]
    row = lax.broadcasted_iota(I32, (L, L), 0)
    col = lax.broadcasted_iota(I32, (L, L), 1)
    tri_low = jnp.where(row >= col, 1.0, 0.0).astype(BF16)
    tri_up = jnp.where(row <= col, 1.0, 0.0).astype(BF16)
    cum_cols = _dot_sel_left(tri_low, gates)
    cum_rows = _dot_sel_right(gatest, tri_up)

    col_term = gates - pltpu.roll(cum_cols, LANES - M_HEADS, axis=1)
    head_lane = lax.broadcasted_iota(I32, col_term.shape, 1) < M_HEADS
    pieces = [jnp.where(head_lane, p.astype(F32), 0.0) for p in _split3(col_term)]
    col_pieces = (pieces[0] + pltpu.roll(pieces[1], M_HEADS, axis=1)
                  + pltpu.roll(pieces[2], 2 * M_HEADS, axis=1)).astype(BF16)
    ng_wide = jnp.concatenate([ng_ref[...]] * (L // LANES), axis=1)

    heads = range(M_HEADS)
    hl = M_HEADS * L
    qts = [qt_ref[h * M_DK:(h + 1) * M_DK, :] for h in heads]
    khs = [k_ref[:, h * M_DK:(h + 1) * M_DK] for h in heads]
    m_old = m_ref[...]
    ig_all = jnp.concatenate([gatest[h:h + 1, :] for h in heads], axis=1)
    bcum_all = jnp.concatenate(
        [cum_rows[M_HEADS + h:M_HEADS + h + 1, :] for h in heads], axis=1)
    m_prev_all = jnp.concatenate(
        [jnp.broadcast_to(m_old[h:h + 1, 0:1], (1, L)) for h in heads], axis=1)

    sel_row = lax.broadcasted_iota(I32, (LANES, hl), 0)
    sel_col = lax.broadcasted_iota(I32, (LANES, hl), 1)
    sel_head = lax.shift_right_logical(sel_col, L.bit_length() - 1)
    pick = jnp.where((sel_row == sel_head) | (sel_row == M_HEADS + sel_head)
                     | (sel_row == 2 * M_HEADS + sel_head), 1.0, 0.0).astype(BF16)
    src = lax.broadcasted_iota(I32, (L, hl), 0)
    tgt = lax.broadcasted_iota(I32, (L, hl), 1) & (L - 1)
    dlog = jnp.where(src <= tgt, _dot(col_pieces, pick) + bcum_all, -jnp.inf)
    inter_log = bcum_all + m_prev_all
    m_t_all = jnp.maximum(inter_log, jnp.max(dlog, axis=0, keepdims=True))
    w_intra = jnp.exp(dlog - m_t_all)
    w_inter_all = jnp.exp(inter_log - m_t_all)
    floor_all = jnp.exp(-m_t_all)
    scores_all = jnp.concatenate([_dot(khs[h], qts[h]) for h in heads], axis=1) * w_intra
    score_sum = jnp.sum(scores_all, axis=0, keepdims=True)
    scores_b = scores_all.astype(BF16)

    hs = []
    for h in heads:
        lanes = slice(h * L, (h + 1) * L)
        qt, kh = qts[h], khs[h]
        vt = vt_ref[h * M_DV:(h + 1) * M_DV, :]
        state = ct_ref[h]
        ig_row = ig_all[:, lanes]
        bcum_row = bcum_all[:, lanes]
        m_prev = m_old[h:h + 1, 0:1]
        w_inter = w_inter_all[:, lanes]
        q_state = _dot(state.astype(BF16), qt)
        num = _dot(vt, scores_b[:, lanes]) + w_inter * q_state[0:M_DV, :]
        den = score_sum[:, lanes] + w_inter * q_state[M_DV:M_DV + 1, :]
        h_out = num / jnp.maximum(jnp.abs(den), floor_all[:, lanes])

        b_last = bcum_row[:, L - 1:L]
        log_src = b_last - bcum_row + ig_row
        m_new = jnp.maximum(b_last + m_prev,
                            jnp.max(log_src, axis=1, keepdims=True))
        w_src = jnp.exp(log_src - m_new)
        decay = jnp.exp(b_last + m_prev - m_new)
        vt_w = jnp.concatenate(
            [(vt.astype(F32) * w_src).astype(BF16),
             jnp.broadcast_to(w_src, (SUBLANES, L)).astype(BF16)], axis=0)
        ct_ref[h] = decay * state + _dot(vt_w, kh)
        m_ref[h:h + 1, :] = jnp.broadcast_to(m_new, (1, LANES))

        hn = h_out * lax.rsqrt(jnp.mean(h_out * h_out, axis=0, keepdims=True) + EPS)
        og = ot_ref[h * M_DV:(h + 1) * M_DV, :].astype(F32)
        hs.append((hn * ng_wide[h * M_DV:(h + 1) * M_DV, :] * _sigmoid(og)).astype(BF16))

    hs_t = jnp.concatenate(hs, axis=0)
    y = lax.dot_general(hs_t, wout_ref[0], (((0,), (0,)), ((), ())),
                        preferred_element_type=F32)
    gate = mod_ref[0][:, 2 * D_MODEL:3 * D_MODEL]
    out_ref[...] = x_ref[...] + gate * y


def _mlstm_rec(x, mod, qt, k, vt, ot, gates, gatest, m_norm_g, w_out_all, layer):
    L = MLSTM_CHUNK
    nc = SEQ // L
    rows = lambda width: pl.BlockSpec((L, width), lambda b, j: (b * nc + j, 0))
    cols = lambda height: pl.BlockSpec((height, L), lambda b, j: (0, b * nc + j))
    ng = jnp.broadcast_to(m_norm_g.astype(F32).reshape(M_V, 1), (M_V, LANES))
    return pl.pallas_call(
        _mlstm_rec_kernel,
        out_shape=jax.ShapeDtypeStruct((N_TOK, D_MODEL), F32),
        grid=(BATCH, nc),
        in_specs=[
            cols(M_QK), rows(M_QK), cols(M_V), cols(M_V), rows(LANES),
            cols(SUBLANES), rows(D_MODEL),
            pl.BlockSpec((1, 1, 3 * D_MODEL), lambda b, j: (b, 0, 0)),
            pl.BlockSpec((M_V, LANES), lambda b, j: (0, 0)),
            pl.BlockSpec((1, M_V, D_MODEL), lambda b, j: (layer, 0, 0)),
        ],
        out_specs=rows(D_MODEL),
        scratch_shapes=[
            pltpu.VMEM((M_HEADS, M_DV + SUBLANES, M_DK), F32),
            pltpu.VMEM((SUBLANES, LANES), F32),
        ],
        compiler_params=_cparams(("arbitrary", "arbitrary")),
        name="mlstm_rec",
    )(qt, k, vt, ot, gates, gatest, x, mod, ng, w_out_all)


_CONV_COLS = 256


def _conv_kernel(x_ref, g_ref, mod_ref, win_ref, cw_ref, wout_ref, out_ref,
                 carry_ref, z_ref):
    t = ROW_TILE
    tiles_per_seq = SEQ // t

    @pl.when(pl.program_id(0) % tiles_per_seq == 0)
    def _():
        carry_ref[...] = jnp.zeros_like(carry_ref)

    x = x_ref[...]
    mod = mod_ref[0]
    hb = _rms_mod(x, g_ref[...], mod).astype(BF16)
    row = lax.broadcasted_iota(I32, (t, _CONV_COLS), 0)
    for j in range(D_MODEL // _CONV_COLS):
        lo, hi = j * _CONV_COLS, (j + 1) * _CONV_COLS
        b_gate = _dot(hb, win_ref[0, :, lo:hi])
        c_gate = _dot(hb, win_ref[0, :, D_MODEL + lo:D_MODEL + hi])
        xb = _dot(hb, win_ref[0, :, 2 * D_MODEL + lo:2 * D_MODEL + hi])
        u = c_gate * xb
        prev1 = carry_ref[SUBLANES - 1:SUBLANES, lo:hi]
        prev2 = carry_ref[SUBLANES - 2:SUBLANES - 1, lo:hi]
        u1 = jnp.where(row == 0, prev1, pltpu.roll(u, 1, axis=0))
        u2 = jnp.where(row == 0, prev2,
                       jnp.where(row == 1, prev1, pltpu.roll(u, 2, axis=0)))
        y = (cw_ref[0:1, lo:hi] * u2 + cw_ref[1:2, lo:hi] * u1
             + cw_ref[2:3, lo:hi] * u)
        z_ref[:, lo:hi] = (b_gate * y).astype(BF16)
        carry_ref[:, lo:hi] = u[t - SUBLANES:t, :]
    gate = mod[:, 2 * D_MODEL:3 * D_MODEL]
    out_ref[...] = x + gate * _dot(z_ref[...], wout_ref[0])


def _conv_layer(x, norm_g, mod, w_in_all, conv_w, w_out_all, layer):
    t = ROW_TILE
    tiles_per_seq = SEQ // t
    full = lambda shape: pl.BlockSpec(shape, lambda i: (0, 0))
    cw = jnp.pad(conv_w.astype(F32), ((0, SUBLANES - CONV_K), (0, 0)))
    return pl.pallas_call(
        _conv_kernel,
        out_shape=jax.ShapeDtypeStruct((N_TOK, D_MODEL), F32),
        grid=(N_TOK // t,),
        in_specs=[
            pl.BlockSpec((t, D_MODEL), lambda i: (i, 0)),
            full((1, D_MODEL)),
            pl.BlockSpec((1, 1, 3 * D_MODEL), lambda i: (i // tiles_per_seq, 0, 0)),
            pl.BlockSpec((1, D_MODEL, 3 * D_MODEL), lambda i: (layer, 0, 0)),
            full((SUBLANES, D_MODEL)),
            pl.BlockSpec((1, D_MODEL, D_MODEL), lambda i: (layer, 0, 0)),
        ],
        out_specs=pl.BlockSpec((t, D_MODEL), lambda i: (i, 0)),
        scratch_shapes=[
            pltpu.VMEM((SUBLANES, D_MODEL), F32),
            pltpu.VMEM((t, D_MODEL), BF16),
        ],
        compiler_params=_cparams(("arbitrary",)),
        name="conv_layer",
    )(x, norm_g.reshape(1, D_MODEL), mod, w_in_all, cw, w_out_all)


_ROUTE_ROWS = LANES
_EXPERT_ROW0 = SUBLANES


_ROUTE_TILES = 4
_TILE_SHIFT = MOE_TILE.bit_length() - 1
_LANE_SHIFT = LANES.bit_length() - 1


def _route_kernel(x_ref, g_ref, mod_ref, wr_hi_ref, wr_lo_ref, br_ref,
                  h_ref, route_ref, cnt_ref):
    h, route, cnt = _route_tiles(x_ref[...], g_ref[...], mod_ref[0],
                                 wr_hi_ref[...], wr_lo_ref[...], br_ref[...])
    h_ref[...] = h
    route_ref[...] = route
    for j in range(_ROUTE_TILES):
        cnt_ref[j] = cnt[:, j * LANES:(j + 1) * LANES]


def _route_tiles(x, g, mod, wr_hi, wr_lo, br):
    t = MOE_TILE * _ROUTE_TILES
    h = _rms_mod(x, g, mod)
    h_hi = h.astype(BF16)
    h_lo = (h - h_hi.astype(F32)).astype(BF16)
    logits = (_dot_nt(wr_hi, h_hi) + _dot_nt(wr_hi, h_lo)
              + _dot_nt(wr_lo, h_hi)) + br

    sub = lax.broadcasted_iota(I32, (SUBLANES, t), 0)
    neg_inf = -jnp.inf
    gl = jnp.where(sub < N_GROUPS, logits[0:SUBLANES, :], neg_inf)
    gmax = jnp.max(gl, axis=0, keepdims=True)
    g_sel = jnp.min(jnp.where(gl == gmax, sub, SUBLANES), axis=0, keepdims=True)
    p_sel = 1.0 / jnp.sum(jnp.exp(gl - gmax), axis=0, keepdims=True)

    e_sel = jnp.zeros((E_PER_GROUP, t), F32)
    for g in range(N_GROUPS):
        r0 = _EXPERT_ROW0 + g * E_PER_GROUP
        e_sel = jnp.where(g_sel == g, logits[r0:r0 + E_PER_GROUP, :], e_sel)
    v1 = jnp.max(e_sel, axis=0, keepdims=True)
    i1 = jnp.min(jnp.where(e_sel == v1, sub, SUBLANES), axis=0, keepdims=True)
    e_rest = jnp.where(sub == i1, neg_inf, e_sel)
    v2 = jnp.max(e_rest, axis=0, keepdims=True)
    i2 = jnp.min(jnp.where(e_rest == v2, sub, SUBLANES), axis=0, keepdims=True)
    ratio = jnp.exp(v2 - v1)
    w1 = p_sel / (1.0 + ratio)
    w2 = p_sel * ratio / (1.0 + ratio)
    eid1 = g_sel * E_PER_GROUP + i1
    eid2 = g_sel * E_PER_GROUP + i2

    erow = lax.broadcasted_iota(I32, (N_EXPERTS, t), 0)
    m1 = erow == eid1
    m2 = erow == eid2
    member = jnp.where(m1 | m2, 1.0, 0.0)
    r = lax.broadcasted_iota(I32, (t, t), 0)
    c = lax.broadcasted_iota(I32, (t, t), 1)
    same_tile = (lax.shift_right_logical(r, _TILE_SHIFT)
                 == lax.shift_right_logical(c, _TILE_SHIFT))
    earlier = jnp.where((r < c) & same_tile, 1.0, 0.0).astype(BF16)
    member_b = member.astype(BF16)
    rank = _dot(member_b, earlier)
    tr = lax.broadcasted_iota(I32, (t, _ROUTE_TILES * LANES), 0)
    tc = lax.broadcasted_iota(I32, (t, _ROUTE_TILES * LANES), 1)
    in_tile = jnp.where(lax.shift_right_logical(tr, _TILE_SHIFT)
                        == lax.shift_right_logical(tc, _LANE_SHIFT), 1.0, 0.0)
    cnt = _dot(member_b, in_tile.astype(BF16))
    cnt_pad = jnp.floor((cnt + (SEG_ALIGN - 1.0)) * (1.0 / SEG_ALIGN)) * SEG_ALIGN
    er = lax.broadcasted_iota(I32, (N_EXPERTS, N_EXPERTS), 0)
    ec = lax.broadcasted_iota(I32, (N_EXPERTS, N_EXPERTS), 1)
    before = jnp.where(er > ec, 1.0, 0.0).astype(BF16)
    seg_start = _dot(before, cnt_pad.astype(BF16))
    seg_start_tok = jnp.concatenate(
        [seg_start[:, j * LANES:(j + 1) * LANES]
         for j in range(_ROUTE_TILES) for _ in range(MOE_TILE // LANES)], axis=1)
    pos = seg_start_tok + rank
    pos1 = jnp.sum(jnp.where(m1, pos, 0.0), axis=0, keepdims=True)
    pos2 = jnp.sum(jnp.where(m2, pos, 0.0), axis=0, keepdims=True)

    out = jnp.zeros((SUBLANES, t), F32)
    for k, val in enumerate((pos1, pos2, w1, w2)):
        out = jnp.where(sub == k, val, out)
    return h_hi, out, cnt


def _route(x, norm_g, mod, w_group, b_group, w_expert, b_expert):
    t = MOE_TILE * _ROUTE_TILES
    tiles_per_seq = SEQ // t
    wr = jnp.zeros((_ROUTE_ROWS, D_MODEL), F32)
    wr = wr.at[0:N_GROUPS].set(w_group.T.astype(F32))
    wr = wr.at[_EXPERT_ROW0:_EXPERT_ROW0 + N_EXPERTS].set(w_expert.T.astype(F32))
    wr_hi = wr.astype(BF16)
    wr_lo = (wr - wr_hi.astype(F32)).astype(BF16)
    br = jnp.zeros((_ROUTE_ROWS,), F32)
    br = br.at[0:N_GROUPS].set(b_group.astype(F32))
    br = br.at[_EXPERT_ROW0:_EXPERT_ROW0 + N_EXPERTS].set(b_expert.astype(F32))
    full = lambda shape: pl.BlockSpec(shape, lambda i: (0, 0))
    return pl.pallas_call(
        _route_kernel,
        out_shape=(
            jax.ShapeDtypeStruct((N_TOK, D_MODEL), BF16),
            jax.ShapeDtypeStruct((SUBLANES, N_TOK), F32),
            jax.ShapeDtypeStruct((N_MOE_TILES, N_EXPERTS, LANES), F32),
        ),
        grid=(N_MOE_TILES // _ROUTE_TILES,),
        in_specs=[
            pl.BlockSpec((t, D_MODEL), lambda i: (i, 0)),
            full((1, D_MODEL)),
            pl.BlockSpec((1, 1, 3 * D_MODEL), lambda i: (i // tiles_per_seq, 0, 0)),
            full((_ROUTE_ROWS, D_MODEL)),
            full((_ROUTE_ROWS, D_MODEL)),
            full((_ROUTE_ROWS, 1)),
        ],
        out_specs=(
            pl.BlockSpec((t, D_MODEL), lambda i: (i, 0)),
            pl.BlockSpec((SUBLANES, t), lambda i: (0, i)),
            pl.BlockSpec((_ROUTE_TILES, N_EXPERTS, LANES), lambda i: (i, 0, 0)),
        ),
        compiler_params=_cparams(("arbitrary",)),
        name="moe_route",
    )(x, norm_g.reshape(1, D_MODEL), mod, wr_hi, wr_lo, br.reshape(_ROUTE_ROWS, 1))


_REM_UNITS = CHUNK // SEG_ALIGN
_REM_SHIFT = _REM_UNITS.bit_length() - 1


def _chunk_code(tokens):
    return (tokens // CHUNK) * _REM_UNITS + (tokens % CHUNK) // SEG_ALIGN


def _dispatch_plan(cnt):
    cnt = cnt.astype(I32)
    cnt_pad = (cnt + SEG_ALIGN - 1) // SEG_ALIGN * SEG_ALIGN
    seg = jnp.cumsum(cnt_pad, axis=1) - cnt_pad
    tot = jnp.sum(cnt_pad, axis=0)
    ptot = (tot + EXPERT_BLOCK - 1) // EXPERT_BLOCK * EXPERT_BLOCK
    pend = jnp.cumsum(ptot)
    gbase = pend - ptot
    dst = gbase[None, :] + jnp.cumsum(cnt_pad, axis=0) - cnt_pad
    nch = _chunk_code(cnt_pad)
    n_used = (pend[-1] // EXPERT_BLOCK).astype(I32)
    blk = jnp.arange(N_EXPERT_BLOCKS, dtype=I32)
    blk_start = jnp.minimum(blk, n_used - 1) * EXPERT_BLOCK
    blk_e = jnp.sum((pend[None, :] <= blk_start[:, None]).astype(I32), axis=1)
    blk_e = jnp.minimum(blk_e, N_EXPERTS - 1).astype(I32)
    gap_dst = gbase + tot
    gap_nch = _chunk_code(ptot - tot)
    misc = jnp.stack([n_used, jnp.sum(ptot - tot) // SEG_ALIGN]).astype(I32)
    first_blk = gbase // EXPERT_BLOCK
    ids = jnp.arange(N_EXPERTS, dtype=I32)
    later = (ids[None, :] > ids[:, None]) & (ptot[None, :] > 0)
    next_e = jnp.min(jnp.where(later, ids[None, :], N_EXPERTS), axis=1)
    next_e = jnp.where(next_e == N_EXPERTS, -1, next_e)
    return dict(seg=seg.reshape(-1).astype(I32), dst=dst.reshape(-1).astype(I32),
                nch=nch.reshape(-1).astype(I32),
                ntot=(jnp.sum(cnt_pad, axis=1) // SEG_ALIGN).astype(I32),
                nmax=(jnp.max(cnt_pad, axis=1) // CHUNK).astype(I32), blk_e=blk_e,
                first_blk=first_blk.astype(I32), next_e=next_e.astype(I32),
                gap_dst=gap_dst.astype(I32), gap_nch=gap_nch.astype(I32), misc=misc)


_INLINE_CHUNKS = 1


def _start_copy(make_copy, s0, d0, offset, tokens, priority=0):
    s = pl.multiple_of(s0 + offset, SEG_ALIGN)
    d = pl.multiple_of(d0 + offset, SEG_ALIGN)
    make_copy(s, d, tokens).start(priority=priority)


def _start_tail(make_copy, s0, d0, n_full, rem, priority=0):
    tail = n_full * CHUNK
    for bit in reversed(range(_REM_SHIFT)):
        units = 1 << bit
        present = (rem & units) != 0
        pl.when(present)(functools.partial(_start_copy, make_copy, s0, d0, tail,
                                           units * SEG_ALIGN, priority))
        tail = tail + jnp.where(present, units * SEG_ALIGN, 0)


def _segment_copies_inline(tile, live, seg_ref, dst_ref, nch_ref, make_copy):
    for e in range(N_EXPERTS):
        idx = tile * N_EXPERTS + e
        code = jnp.where(live, nch_ref[idx], 0)
        n = lax.shift_right_logical(code, _REM_SHIFT)
        s0 = seg_ref[idx]
        d0 = dst_ref[idx]
        for cidx in range(_INLINE_CHUNKS):
            pl.when(cidx < n)(functools.partial(_start_copy, make_copy, s0, d0,
                                                cidx * CHUNK, CHUNK, (e + cidx) % 2))
        _start_tail(make_copy, s0, d0, n, code & (_REM_UNITS - 1), e % 2)


def _segment_copies_loop(tile, first, with_tail, seg_ref, dst_ref, nch_ref, make_copy):
    def per_expert(e, carry):
        idx = tile * N_EXPERTS + e
        code = nch_ref[idx]
        n = lax.shift_right_logical(code, _REM_SHIFT)
        s0 = seg_ref[idx]
        d0 = dst_ref[idx]

        def per_chunk(cidx, c2):
            _start_copy(make_copy, s0, d0, cidx * CHUNK, CHUNK)
            return c2

        lax.fori_loop(first, jnp.maximum(n, first), per_chunk, 0)
        if with_tail:
            _start_tail(make_copy, s0, d0, n, code & (_REM_UNITS - 1))
        return carry

    lax.fori_loop(0, N_EXPERTS, per_expert, 0)


_WAIT_GROUP = 64


def _wait_each(n, make_wait):
    def body(_, carry):
        make_wait().wait()
        return carry
    lax.fori_loop(0, n, body, 0)


def _wait_copies(units, make_copy):
    _wait_each(units // _WAIT_GROUP, lambda: make_copy(0, 0, _WAIT_GROUP * SEG_ALIGN))
    _wait_each(units % _WAIT_GROUP, lambda: make_copy(0, 0, SEG_ALIGN))


_HALF = D_MODEL // 2
_BF16_BITS = 16
_HI_MASK = -(1 << _BF16_BITS)


def _pack_pairs(x):
    lo = lax.shift_right_logical(lax.bitcast_convert_type(x[:, 0:_HALF], I32),
                                 _BF16_BITS)
    hi = lax.bitcast_convert_type(x[:, _HALF:D_MODEL], I32) & _HI_MASK
    return lo | hi


def _unpack_pairs(w):
    lo = lax.bitcast_convert_type(lax.shift_left(w, _BF16_BITS), F32).astype(BF16)
    hi = lax.bitcast_convert_type(w & _HI_MASK, F32).astype(BF16)
    return lo, hi


def _token_rows(ref, tok, tokens):
    start = pl.multiple_of(tok * WORD_ROWS, SUBLANES)
    return ref.at[pl.ds(start, tokens * WORD_ROWS)]


def _store_words(ref, words, first=0):
    rows = words.shape[0]
    for q in range(WORD_ROWS):
        ref[pl.ds(first * WORD_ROWS + q, rows, stride=WORD_ROWS), :] = (
            words[:, q * LANES:(q + 1) * LANES])


def _load_words(ref, rows, first=0):
    return jnp.concatenate(
        [ref[pl.ds(first * WORD_ROWS + q, rows, stride=WORD_ROWS), :]
         for q in range(WORD_ROWS)], axis=1)


_DISPATCH_BUFS = 3
_TILES_PER_STEP = 2


def _dispatch_kernel(seg_ref, dst_ref, nch_ref, ntot_ref, nmax_ref, gap_dst_ref,
                     gap_nch_ref, misc_ref, route_ref, h_ref, xs_ref, buf_ref,
                     zero_ref, sem, zsem):
    t = MOE_TILE
    last = N_MOE_TILES - 1
    n_used = misc_ref[0]

    def make_copy(which):
        def mk(s, d, tokens):
            return pltpu.make_async_copy(_token_rows(buf_ref.at[which], s, tokens),
                                         _token_rows(xs_ref, d, tokens), sem.at[which])
        return mk

    def zero_copy(s, d, tokens):
        del s
        return pltpu.make_async_copy(_token_rows(zero_ref, 0, tokens),
                                     _token_rows(xs_ref, d, tokens), zsem)

    @pl.when(pl.program_id(0) == 0)
    def _():
        zero_ref[...] = jnp.zeros_like(zero_ref)

        def per_expert(e, carry):
            d0 = gap_dst_ref[e]
            code = gap_nch_ref[e]
            n = lax.shift_right_logical(code, _REM_SHIFT)

            def per_chunk(cidx, c2):
                _start_copy(zero_copy, 0, d0, cidx * CHUNK, CHUNK)
                return c2

            lax.fori_loop(0, n, per_chunk, 0)
            _start_tail(zero_copy, 0, d0, n, code & (_REM_UNITS - 1))
            return carry

        lax.fori_loop(0, N_EXPERTS, per_expert, 0)

        def per_block(b, carry):
            zero_copy(0, b * EXPERT_BLOCK, EXPERT_BLOCK).start()
            return carry

        lax.fori_loop(n_used, N_EXPERT_BLOCKS, per_block, 0)

    def sort_tile(j):
        k = pl.program_id(0) * _TILES_PER_STEP + j
        slot = k % _DISPATCH_BUFS
        prev = (k + _DISPATCH_BUFS - 1) % _DISPATCH_BUFS
        prev2 = (k + _DISPATCH_BUFS - 2) % _DISPATCH_BUFS
        tile_prev = jnp.maximum(k - 1, 0)
        _segment_copies_inline(tile_prev, k > 0, seg_ref, dst_ref, nch_ref,
                               make_copy(prev))

        route = route_ref[:, j * t:(j + 1) * t]
        pos1 = route[0:1, :].astype(I32)
        pos2 = route[1:2, :].astype(I32)
        r = lax.broadcasted_iota(I32, (DISPATCH_ROWS, t), 0)
        perm = jnp.where((r == pos1) | (r == pos2), 1.0, 0.0).astype(BF16)
        _store_words(buf_ref.at[slot],
                     _pack_pairs(_dot(perm, h_ref[j * t:(j + 1) * t, :])))

        @pl.when((k > 0) & (nmax_ref[tile_prev] > _INLINE_CHUNKS))
        def _():
            _segment_copies_loop(tile_prev, _INLINE_CHUNKS, False, seg_ref, dst_ref,
                                 nch_ref, make_copy(prev))

        @pl.when(k > 1)
        def _():
            _wait_copies(ntot_ref[jnp.maximum(k - 2, 0)], make_copy(prev2))

        @pl.when(k == last)
        def _():
            _segment_copies_loop(k, 0, True, seg_ref, dst_ref, nch_ref, make_copy(slot))
            _wait_copies(ntot_ref[last - 1], make_copy(prev))
            _wait_copies(ntot_ref[last], make_copy(slot))
            _wait_copies(misc_ref[1], zero_copy)
            _wait_each(N_EXPERT_BLOCKS - n_used, lambda: zero_copy(0, 0, EXPERT_BLOCK))

    for j in range(_TILES_PER_STEP):
        sort_tile(j)


def _dispatch(plan, route, h):
    t = MOE_TILE * _TILES_PER_STEP
    return pl.pallas_call(
        _dispatch_kernel,
        out_shape=jax.ShapeDtypeStruct((SORTED_ROWS * WORD_ROWS, LANES), I32),
        grid_spec=pltpu.PrefetchScalarGridSpec(
            num_scalar_prefetch=8,
            grid=(N_MOE_TILES // _TILES_PER_STEP,),
            in_specs=[
                pl.BlockSpec((SUBLANES, t), lambda i, *_: (0, i)),
                pl.BlockSpec((t, D_MODEL), lambda i, *_: (i, 0)),
            ],
            out_specs=pl.BlockSpec(memory_space=pl.ANY),
            scratch_shapes=[
                pltpu.VMEM((_DISPATCH_BUFS, DISPATCH_ROWS * WORD_ROWS, LANES), I32),
                pltpu.VMEM((EXPERT_BLOCK * WORD_ROWS, LANES), I32),
                pltpu.SemaphoreType.DMA((_DISPATCH_BUFS,)),
                pltpu.SemaphoreType.DMA,
            ],
        ),
        compiler_params=_cparams(("arbitrary",)),
        name="moe_dispatch",
    )(plan["seg"], plan["dst"], plan["nch"], plan["ntot"], plan["nmax"],
      plan["gap_dst"], plan["gap_nch"], plan["misc"], route, h)


_WEIGHT_DMA_PRIORITY = 1


def _expert_kernel(blk_e_ref, first_ref, next_ref, misc_ref, x_ref, wg_hbm, wu_hbm,
                   wd_hbm, y_ref, wg_f, wu_f, wd_f, wg_b, wu_b, wd_b, sem, *, layer):
    blk0 = pl.program_id(0) * _BLOCKS_PER_STEP
    blk1 = blk0 + 1
    n_used = misc_ref[0]

    def weight_copies(e):
        return (pltpu.make_async_copy(wg_hbm.at[layer, e], wg_f, sem.at[0]),
                pltpu.make_async_copy(wu_hbm.at[layer, e], wu_f, sem.at[1]),
                pltpu.make_async_copy(wd_hbm.at[layer, e], wd_f, sem.at[2]))

    def take_weights(e):
        for cp in weight_copies(e):
            cp.wait()
        wg_b[...] = wg_f[...].astype(BF16)
        wu_b[...] = wu_f[...].astype(BF16)
        wd_b[...] = wd_f[...].astype(BF16)

        @pl.when(next_ref[e] >= 0)
        def _():
            for cp in weight_copies(next_ref[e]):
                cp.start(priority=_WEIGHT_DMA_PRIORITY)

    def run_rows(first, rows):
        x_lo, x_hi = _unpack_pairs(_load_words(x_ref, rows, first))
        x = jnp.concatenate([x_lo, x_hi], axis=1)
        g = _dot(x, wg_b[...])
        u = _dot(x, wu_b[...])
        a = (g * _sigmoid(g) * u).astype(BF16)
        y = _dot(a, wd_b[...])
        _store_words(y_ref, _pack_pairs(y.astype(BF16).astype(F32)), first)

    @pl.when(blk0 == 0)
    def _():
        for cp in weight_copies(blk_e_ref[0]):
            cp.start(priority=_WEIGHT_DMA_PRIORITY)

    @pl.when(blk0 < n_used)
    def _():
        e0 = blk_e_ref[blk0]
        e1 = blk_e_ref[blk1]
        pl.when(blk0 == first_ref[e0])(functools.partial(take_weights, e0))

        @pl.when(e0 == e1)
        def _():
            run_rows(0, _BLOCKS_PER_STEP * EXPERT_BLOCK)

        @pl.when(e0 != e1)
        def _():
            run_rows(0, EXPERT_BLOCK)
            take_weights(e1)
            run_rows(EXPERT_BLOCK, EXPERT_BLOCK)

    @pl.when(blk0 >= n_used)
    def _():
        y_ref[...] = jnp.zeros_like(y_ref)


def _experts(plan, xs, layer, w_gate, w_up, w_down):
    step_rows = _BLOCKS_PER_STEP * EXPERT_BLOCK * WORD_ROWS
    row_map = lambda i, be, fi, nx, misc: (
        jnp.minimum(i, (misc[0] - 1) // _BLOCKS_PER_STEP), 0)
    out_map = lambda i, be, fi, nx, misc: (i, 0)
    return pl.pallas_call(
        functools.partial(_expert_kernel, layer=layer),
        out_shape=jax.ShapeDtypeStruct((SORTED_ROWS * WORD_ROWS, LANES), I32),
        grid_spec=pltpu.PrefetchScalarGridSpec(
            num_scalar_prefetch=4,
            grid=(N_EXPERT_BLOCKS // _BLOCKS_PER_STEP,),
            in_specs=[
                pl.BlockSpec((step_rows, LANES), row_map),
                pl.BlockSpec(memory_space=pl.ANY),
                pl.BlockSpec(memory_space=pl.ANY),
                pl.BlockSpec(memory_space=pl.ANY),
            ],
            out_specs=pl.BlockSpec((step_rows, LANES), out_map),
            scratch_shapes=[
                pltpu.VMEM((D_MODEL, D_FF), F32),
                pltpu.VMEM((D_MODEL, D_FF), F32),
                pltpu.VMEM((D_FF, D_MODEL), F32),
                pltpu.VMEM((D_MODEL, D_FF), BF16),
                pltpu.VMEM((D_MODEL, D_FF), BF16),
                pltpu.VMEM((D_FF, D_MODEL), BF16),
                pltpu.SemaphoreType.DMA((3,)),
            ],
        ),
        compiler_params=_cparams(("arbitrary",)),
        name="moe_experts",
    )(plan["blk_e"], plan["first_blk"], plan["next_e"], plan["misc"], xs,
      w_gate, w_up, w_down)


_COMBINE_BUFS = 3


def _combine_kernel(seg_ref, dst_ref, nch_ref, ntot_ref, nmax_ref, route_ref, x_ref,
                    mod_ref, fg_ref, ys_ref, out_ref, *scratch, final_norm):
    bufs, sem = scratch[:_COMBINE_BUFS], scratch[_COMBINE_BUFS]
    t = MOE_TILE
    ahead = _COMBINE_BUFS - 1

    def make_copy(which):
        def mk(s, d, tokens):
            return pltpu.make_async_copy(_token_rows(ys_ref, d, tokens),
                                         _token_rows(bufs[which], s, tokens),
                                         sem.at[which])
        return mk

    @pl.when(pl.program_id(0) == 0)
    def _():
        for buf in bufs:
            buf[...] = jnp.zeros_like(buf)
        for k in range(_COMBINE_BUFS - 1):
            _segment_copies_loop(k, 0, True, seg_ref, dst_ref, nch_ref, make_copy(k))

    def unsort_tile(j, k, cur):
        nxt = (cur + ahead) % _COMBINE_BUFS
        tile_next = jnp.minimum(k + ahead, N_MOE_TILES - 1)
        has_next = k + ahead < N_MOE_TILES
        rows = pl.ds(j * t, t)
        _wait_copies(ntot_ref[k], make_copy(cur))
        _segment_copies_inline(tile_next, has_next, seg_ref, dst_ref, nch_ref,
                               make_copy(nxt))
        route = route_ref[:, j * t:(j + 1) * t]
        route_t = jnp.concatenate(
            [route, jnp.zeros((LANES - SUBLANES, t), F32)], axis=0).T
        pos1 = route_t[:, 0:1].astype(I32)
        pos2 = route_t[:, 1:2].astype(I32)
        w1 = route_t[:, 2:3]
        w2 = route_t[:, 3:4]
        c = lax.broadcasted_iota(I32, (t, SORT_ROWS), 1)
        unsort = (jnp.where(c == pos1, w1, 0.0)
                  + jnp.where(c == pos2, w2, 0.0)).astype(BF16)
        y_lo, y_hi = _unpack_pairs(_load_words(bufs[cur], SORT_ROWS))
        moe = jnp.concatenate([_dot(unsort, y_lo), _dot(unsort, y_hi)], axis=1)
        gate = mod_ref[0][:, 2 * D_MODEL:3 * D_MODEL]
        x_new = x_ref[rows, :] + gate * moe
        if final_norm:
            ms = jnp.mean(x_new * x_new, axis=-1, keepdims=True)
            x_new = (x_new * lax.rsqrt(ms + EPS)) * fg_ref[...]
        out_ref[rows, :] = x_new

        @pl.when(has_next & (nmax_ref[tile_next] > _INLINE_CHUNKS))
        def _():
            _segment_copies_loop(tile_next, _INLINE_CHUNKS, False, seg_ref, dst_ref,
                                 nch_ref, make_copy(nxt))

    for j in range(_TILES_PER_STEP):
        k = pl.program_id(0) * _TILES_PER_STEP + j
        for cur in range(_COMBINE_BUFS):
            pl.when(k % _COMBINE_BUFS == cur)(functools.partial(unsort_tile, j, k, cur))


def _combine(plan, route, x, mod, final_g, ys, final_norm):
    t = MOE_TILE * _TILES_PER_STEP
    tiles_per_seq = SEQ // t
    return pl.pallas_call(
        functools.partial(_combine_kernel, final_norm=final_norm),
        out_shape=jax.ShapeDtypeStruct((N_TOK, D_MODEL), F32),
        grid_spec=pltpu.PrefetchScalarGridSpec(
            num_scalar_prefetch=5,
            grid=(N_MOE_TILES // _TILES_PER_STEP,),
            in_specs=[
                pl.BlockSpec((SUBLANES, t), lambda i, *_: (0, i)),
                pl.BlockSpec((t, D_MODEL), lambda i, *_: (i, 0)),
                pl.BlockSpec((1, 1, 3 * D_MODEL), lambda i, *_: (i // tiles_per_seq, 0, 0)),
                pl.BlockSpec((1, D_MODEL), lambda i, *_: (0, 0)),
                pl.BlockSpec(memory_space=pl.ANY),
            ],
            out_specs=pl.BlockSpec((t, D_MODEL), lambda i, *_: (i, 0)),
            scratch_shapes=(
                [pltpu.VMEM((SORT_ROWS * WORD_ROWS, LANES), I32)] * _COMBINE_BUFS
                + [pltpu.SemaphoreType.DMA((_COMBINE_BUFS,))]),
        ),
        compiler_params=_cparams(("arbitrary",)),
        name="moe_combine",
    )(plan["seg"], plan["dst"], plan["nch"], plan["ntot"], plan["nmax"], route, x,
      mod, final_g.reshape(1, D_MODEL).astype(F32), ys)


def _moe_layer(x, norm_g, mod, final_g, w_group, b_group, w_expert, b_expert,
               layer, w_gate, w_up, w_down, final_norm):
    h, route, cnt = _route(x, norm_g, mod, w_group, b_group, w_expert, b_expert)
    plan = _dispatch_plan(cnt[:, :, 0])
    xs = _dispatch(plan, route, h)
    ys = _experts(plan, xs, layer, w_gate, w_up, w_down)
    return _combine(plan, route, x, mod, final_g, ys, final_norm)


def kernel(x, c, ada_w, ada_b, norm_g, final_g, m_w_in, m_b_gates, m_norm_g, m_w_out, s_w_in, s_conv_w, s_w_out, r_w_group, r_b_group, r_w_expert, r_b_expert, e_w_gate, e_w_up, e_w_down):
    mods = _ada_mods(c, ada_w, ada_b)
    m_w_in_b, m_w_out_b = m_w_in.astype(BF16), m_w_out.astype(BF16)
    s_w_in_b, s_w_out_b = s_w_in.astype(BF16), s_w_out.astype(BF16)
    xt = x.reshape(N_TOK, D_MODEL)
    for i in range(DEPTH):
        mod_mix = mods[2 * i].reshape(BATCH, 1, 3 * D_MODEL)
        mod_ffn = mods[2 * i + 1].reshape(BATCH, 1, 3 * D_MODEL)
        j = i // 2
        if i % 2 == 0:
            qt, k, vt, ot, gates, gatest = _mlstm_in(xt, norm_g[i, 0], mod_mix,
                                                     m_w_in_b[j], m_b_gates[j])
            xt = _mlstm_rec(xt, mod_mix, qt, k, vt, ot, gates, gatest,
                            m_norm_g[j], m_w_out_b, j)
        else:
            xt = _conv_layer(xt, norm_g[i, 0], mod_mix, s_w_in_b, s_conv_w[j],
                             s_w_out_b, j)
        xt = _moe_layer(xt, norm_g[i, 1], mod_ffn, final_g, r_w_group[i],
                        r_b_group[i], r_w_expert[i], r_b_expert[i], i, e_w_gate,
                        e_w_up, e_w_down, final_norm=(i == DEPTH - 1))
    return xt.reshape(BATCH, SEQ, D_MODEL)
```

```python
import functools

import jax
import jax.numpy as jnp
from jax import lax
from jax.experimental import pallas as pl
from jax.experimental.pallas import tpu as pltpu

F32 = jnp.float32
BF16 = jnp.bfloat16
I32 = jnp.int32

D_MODEL = 1024
BATCH = 8
SEQ = 2048
DEPTH = 4
N_TOK = BATCH * SEQ
M_HEADS = 4
M_DK = 128
M_DV = 256
M_QK = M_HEADS * M_DK
M_V = M_HEADS * M_DV
CONV_K = 3
N_GROUPS = 4
E_PER_GROUP = 8
N_EXPERTS = N_GROUPS * E_PER_GROUP
TOP_K = 2
D_FF = 512
EPS = 1e-6

SUBLANES = 8
LANES = 128

ROW_TILE = 1024
MLSTM_CHUNK = 256
MOE_TILE = 256
EXPERT_BLOCK = 256
WORD_ROWS = (D_MODEL // 2) // LANES
SEG_ALIGN = SUBLANES // WORD_ROWS
CHUNK = 8
_MAX_TILE_SORTED = MOE_TILE * TOP_K + N_EXPERTS * (SEG_ALIGN - 1)
SORT_ROWS = -(-_MAX_TILE_SORTED // LANES) * LANES
BF16_SUBLANES = 2 * SUBLANES
DISPATCH_ROWS = -(-_MAX_TILE_SORTED // BF16_SUBLANES) * BF16_SUBLANES
N_MOE_TILES = N_TOK // MOE_TILE
_MAX_SORTED = (N_MOE_TILES * (MOE_TILE * TOP_K + N_EXPERTS * (SEG_ALIGN - 1))
               + N_EXPERTS * (EXPERT_BLOCK - 1))
_BLOCKS_PER_STEP = 2
N_EXPERT_BLOCKS = (-(-_MAX_SORTED // (EXPERT_BLOCK * _BLOCKS_PER_STEP))
                   * _BLOCKS_PER_STEP)
SORTED_ROWS = N_EXPERT_BLOCKS * EXPERT_BLOCK

VMEM_LIMIT = 48 * 1024 * 1024


def _cparams(sem):
    return pltpu.CompilerParams(dimension_semantics=sem,
                                vmem_limit_bytes=VMEM_LIMIT)


def _dot(a, b):
    return jnp.dot(a, b, preferred_element_type=F32)


def _dot_nt(a, b):
    return lax.dot_general(a, b, (((1,), (1,)), ((), ())),
                           preferred_element_type=F32)


def _split3(x):
    hi = x.astype(BF16)
    r1 = x - hi.astype(F32)
    mid = r1.astype(BF16)
    lo = (r1 - mid.astype(F32)).astype(BF16)
    return hi, mid, lo


def _dot_sel_left(sel, x):
    hi, mid, lo = _split3(x)
    return _dot(sel, hi) + _dot(sel, mid) + _dot(sel, lo)


def _dot_sel_right(x, sel):
    hi, mid, lo = _split3(x)
    return _dot(hi, sel) + _dot(mid, sel) + _dot(lo, sel)


def _sigmoid(x):
    return 1.0 / (1.0 + jnp.exp(-x))


def _rms_mod(x, g, mod):
    ms = jnp.mean(x * x, axis=-1, keepdims=True)
    y = (x * lax.rsqrt(ms + EPS)) * g
    return y * (1.0 + mod[:, D_MODEL:2 * D_MODEL]) + mod[:, 0:D_MODEL]


def _ada_kernel(c_ref, w_ref, b_ref, o_ref):
    c = c_ref[...]
    s = (c * _sigmoid(c)).astype(BF16)
    o_ref[0] = _dot(s, w_ref[0].astype(BF16)) + b_ref[0]


def _ada_mods(c, ada_w, ada_b):
    n_pairs = DEPTH * 2
    w = ada_w.reshape(n_pairs, D_MODEL, 3 * D_MODEL)
    b = ada_b.reshape(n_pairs, 1, 3 * D_MODEL)
    col = D_MODEL
    return pl.pallas_call(
        _ada_kernel,
        out_shape=jax.ShapeDtypeStruct((n_pairs, BATCH, 3 * D_MODEL), F32),
        grid=(n_pairs, 3 * D_MODEL // col),
        in_specs=[
            pl.BlockSpec((BATCH, D_MODEL), lambda p, j: (0, 0)),
            pl.BlockSpec((1, D_MODEL, col), lambda p, j: (p, 0, j)),
            pl.BlockSpec((1, 1, col), lambda p, j: (p, 0, j)),
        ],
        out_specs=pl.BlockSpec((1, BATCH, col), lambda p, j: (p, 0, j)),
        compiler_params=_cparams(("arbitrary", "arbitrary")),
        name="ada_mods",
    )(c, w, b)


def _mlstm_in_kernel(x_ref, g_ref, mod_ref, wqt_ref, wk_ref, wvt_ref, wot_ref,
                     wg_ref, bg_ref, qt_ref, k_ref, vt_ref, ot_ref, gates_ref,
                     gatest_ref):
    h = _rms_mod(x_ref[...], g_ref[...], mod_ref[0])
    hb = h.astype(BF16)
    qt_ref[...] = (_dot_nt(wqt_ref[...], hb) * (M_DK ** -0.5)).astype(BF16)
    k_ref[...] = _dot(hb, wk_ref[...]).astype(BF16)
    vt_ref[...] = _dot_nt(wvt_ref[...], hb).astype(BF16)
    ot_ref[...] = _dot_nt(wot_ref[...], hb).astype(BF16)
    g = _dot(hb, wg_ref[...]) + bg_ref[...]
    log_sig = jnp.minimum(g, 0.0) - jnp.log(1.0 + jnp.exp(-jnp.abs(g)))
    lane = lax.broadcasted_iota(I32, g.shape, 1)
    gg = jnp.where(lane < M_HEADS, g, log_sig)
    gates_ref[...] = gg
    gatest_ref[...] = gg.T[0:SUBLANES, :]


def _mlstm_in(x, norm_g, mod, w_in, b_gates):
    t = ROW_TILE
    tiles_per_seq = SEQ // t
    wqt = w_in[:, 0:M_QK].T.astype(BF16)
    wk = w_in[:, M_QK:2 * M_QK].astype(BF16)
    wvt = w_in[:, 2 * M_QK:2 * M_QK + M_V].T.astype(BF16)
    wot = w_in[:, 2 * M_QK + M_V:2 * M_QK + 2 * M_V].T.astype(BF16)
    n_gate = 2 * M_HEADS
    wg = jnp.pad(w_in[:, 2 * M_QK + 2 * M_V:], ((0, 0), (0, LANES - n_gate))).astype(BF16)
    bg = jnp.pad(b_gates.astype(F32), (0, LANES - n_gate)).reshape(1, LANES)
    full = lambda shape: pl.BlockSpec(shape, lambda i: (0, 0))
    return pl.pallas_call(
        _mlstm_in_kernel,
        out_shape=(
            jax.ShapeDtypeStruct((M_QK, N_TOK), BF16),
            jax.ShapeDtypeStruct((N_TOK, M_QK), BF16),
            jax.ShapeDtypeStruct((M_V, N_TOK), BF16),
            jax.ShapeDtypeStruct((M_V, N_TOK), BF16),
            jax.ShapeDtypeStruct((N_TOK, LANES), F32),
            jax.ShapeDtypeStruct((SUBLANES, N_TOK), F32),
        ),
        grid=(N_TOK // t,),
        in_specs=[
            pl.BlockSpec((t, D_MODEL), lambda i: (i, 0)),
            full((1, D_MODEL)),
            pl.BlockSpec((1, 1, 3 * D_MODEL), lambda i: (i // tiles_per_seq, 0, 0)),
            full((M_QK, D_MODEL)),
            full((D_MODEL, M_QK)),
            full((M_V, D_MODEL)),
            full((M_V, D_MODEL)),
            full((D_MODEL, LANES)),
            full((1, LANES)),
        ],
        out_specs=(
            pl.BlockSpec((M_QK, t), lambda i: (0, i)),
            pl.BlockSpec((t, M_QK), lambda i: (i, 0)),
            pl.BlockSpec((M_V, t), lambda i: (0, i)),
            pl.BlockSpec((M_V, t), lambda i: (0, i)),
            pl.BlockSpec((t, LANES), lambda i: (i, 0)),
            pl.BlockSpec((SUBLANES, t), lambda i: (0, i)),
        ),
        compiler_params=_cparams(("arbitrary",)),
        name="mlstm_in",
    )(x, norm_g.reshape(1, D_MODEL), mod, wqt, wk, wvt, wot, wg, bg)


def _mlstm_rec_kernel(qt_ref, k_ref, vt_ref, ot_ref, gates_ref, gatest_ref,
                      x_ref, mod_ref, ng_ref, wout_ref, out_ref, ct_ref, m_ref):
    L = MLSTM_CHUNK

    @pl.when(pl.program_id(1) == 0)
    def _():
        ct_ref[...] = jnp.zeros_like(ct_ref)
        m_ref[...] = jnp.zeros_like(m_ref)

    gates = gates_ref[...]
    gatest = gatest_ref[...]
    row = lax.broadcasted_iota(I32, (L, L), 0)
    col = lax.broadcasted_iota(I32, (L, L), 1)
    tri_low = jnp.where(row >= col, 1.0, 0.0).astype(BF16)
    tri_up = jnp.where(row <= col, 1.0, 0.0).astype(BF16)
    cum_cols = _dot_sel_left(tri_low, gates)
    cum_rows = _dot_sel_right(gatest, tri_up)

    col_term = gates - pltpu.roll(cum_cols, LANES - M_HEADS, axis=1)
    head_lane = lax.broadcasted_iota(I32, col_term.shape, 1) < M_HEADS
    pieces = [jnp.where(head_lane, p.astype(F32), 0.0) for p in _split3(col_term)]
    col_pieces = (pieces[0] + pltpu.roll(pieces[1], M_HEADS, axis=1)
                  + pltpu.roll(pieces[2], 2 * M_HEADS, axis=1)).astype(BF16)
    ng_wide = jnp.concatenate([ng_ref[...]] * (L // LANES), axis=1)

    heads = range(M_HEADS)
    hl = M_HEADS * L
    qts = [qt_ref[h * M_DK:(h + 1) * M_DK, :] for h in heads]
    khs = [k_ref[:, h * M_DK:(h + 1) * M_DK] for h in heads]
    m_old = m_ref[...]
    ig_all = jnp.concatenate([gatest[h:h + 1, :] for h in heads], axis=1)
    bcum_all = jnp.concatenate(
        [cum_rows[M_HEADS + h:M_HEADS + h + 1, :] for h in heads], axis=1)
    m_prev_all = jnp.concatenate(
        [jnp.broadcast_to(m_old[h:h + 1, 0:1], (1, L)) for h in heads], axis=1)

    sel_row = lax.broadcasted_iota(I32, (LANES, hl), 0)
    sel_col = lax.broadcasted_iota(I32, (LANES, hl), 1)
    sel_head = lax.shift_right_logical(sel_col, L.bit_length() - 1)
    pick = jnp.where((sel_row == sel_head) | (sel_row == M_HEADS + sel_head)
                     | (sel_row == 2 * M_HEADS + sel_head), 1.0, 0.0).astype(BF16)
    src = lax.broadcasted_iota(I32, (L, hl), 0)
    tgt = lax.broadcasted_iota(I32, (L, hl), 1) & (L - 1)
    dlog = jnp.where(src <= tgt, _dot(col_pieces, pick) + bcum_all, -jnp.inf)
    inter_log = bcum_all + m_prev_all
    m_t_all = jnp.maximum(inter_log, jnp.max(dlog, axis=0, keepdims=True))
    w_intra = jnp.exp(dlog - m_t_all)
    w_inter_all = jnp.exp(inter_log - m_t_all)
    floor_all = jnp.exp(-m_t_all)
    scores_all = jnp.concatenate([_dot(khs[h], qts[h]) for h in heads], axis=1) * w_intra
    score_sum = jnp.sum(scores_all, axis=0, keepdims=True)
    scores_b = scores_all.astype(BF16)

    hs = []
    for h in heads:
        lanes = slice(h * L, (h + 1) * L)
        qt, kh = qts[h], khs[h]
        vt = vt_ref[h * M_DV:(h + 1) * M_DV, :]
        state = ct_ref[h]
        ig_row = ig_all[:, lanes]
        bcum_row = bcum_all[:, lanes]
        m_prev = m_old[h:h + 1, 0:1]
        w_inter = w_inter_all[:, lanes]
        q_state = _dot(state.astype(BF16), qt)
        num = _dot(vt, scores_b[:, lanes]) + w_inter * q_state[0:M_DV, :]
        den = score_sum[:, lanes] + w_inter * q_state[M_DV:M_DV + 1, :]
        h_out = num / jnp.maximum(jnp.abs(den), floor_all[:, lanes])

        b_last = bcum_row[:, L - 1:L]
        log_src = b_last - bcum_row + ig_row
        m_new = jnp.maximum(b_last + m_prev,
                            jnp.max(log_src, axis=1, keepdims=True))
        w_src = jnp.exp(log_src - m_new)
        decay = jnp.exp(b_last + m_prev - m_new)
        vt_w = jnp.concatenate(
            [(vt.astype(F32) * w_src).astype(BF16),
             jnp.broadcast_to(w_src, (SUBLANES, L)).astype(BF16)], axis=0)
        ct_ref[h] = decay * state + _dot(vt_w, kh)
        m_ref[h:h + 1, :] = jnp.broadcast_to(m_new, (1, LANES))

        hn = h_out * lax.rsqrt(jnp.mean(h_out * h_out, axis=0, keepdims=True) + EPS)
        og = ot_ref[h * M_DV:(h + 1) * M_DV, :].astype(F32)
        hs.append((hn * ng_wide[h * M_DV:(h + 1) * M_DV, :] * _sigmoid(og)).astype(BF16))

    hs_t = jnp.concatenate(hs, axis=0)
    y = lax.dot_general(hs_t, wout_ref[0], (((0,), (0,)), ((), ())),
                        preferred_element_type=F32)
    gate = mod_ref[0][:, 2 * D_MODEL:3 * D_MODEL]
    out_ref[...] = x_ref[...] + gate * y


def _mlstm_rec(x, mod, qt, k, vt, ot, gates, gatest, m_norm_g, w_out_all, layer):
    L = MLSTM_CHUNK
    nc = SEQ // L
    rows = lambda width: pl.BlockSpec((L, width), lambda b, j: (b * nc + j, 0))
    cols = lambda height: pl.BlockSpec((height, L), lambda b, j: (0, b * nc + j))
    ng = jnp.broadcast_to(m_norm_g.astype(F32).reshape(M_V, 1), (M_V, LANES))
    return pl.pallas_call(
        _mlstm_rec_kernel,
        out_shape=jax.ShapeDtypeStruct((N_TOK, D_MODEL), F32),
        grid=(BATCH, nc),
        in_specs=[
            cols(M_QK), rows(M_QK), cols(M_V), cols(M_V), rows(LANES),
            cols(SUBLANES), rows(D_MODEL),
            pl.BlockSpec((1, 1, 3 * D_MODEL), lambda b, j: (b, 0, 0)),
            pl.BlockSpec((M_V, LANES), lambda b, j: (0, 0)),
            pl.BlockSpec((1, M_V, D_MODEL), lambda b, j: (layer, 0, 0)),
        ],
        out_specs=rows(D_MODEL),
        scratch_shapes=[
            pltpu.VMEM((M_HEADS, M_DV + SUBLANES, M_DK), F32),
            pltpu.VMEM((SUBLANES, LANES), F32),
        ],
        compiler_params=_cparams(("arbitrary", "arbitrary")),
        name="mlstm_rec",
    )(qt, k, vt, ot, gates, gatest, x, mod, ng, w_out_all)


_CONV_COLS = 256


def _conv_kernel(x_ref, g_ref, mod_ref, win_ref, cw_ref, wout_ref, out_ref,
                 carry_ref, z_ref):
    t = ROW_TILE
    tiles_per_seq = SEQ // t

    @pl.when(pl.program_id(0) % tiles_per_seq == 0)
    def _():
        carry_ref[...] = jnp.zeros_like(carry_ref)

    x = x_ref[...]
    mod = mod_ref[0]
    hb = _rms_mod(x, g_ref[...], mod).astype(BF16)
    row = lax.broadcasted_iota(I32, (t, _CONV_COLS), 0)
    for j in range(D_MODEL // _CONV_COLS):
        lo, hi = j * _CONV_COLS, (j + 1) * _CONV_COLS
        b_gate = _dot(hb, win_ref[0, :, lo:hi])
        c_gate = _dot(hb, win_ref[0, :, D_MODEL + lo:D_MODEL + hi])
        xb = _dot(hb, win_ref[0, :, 2 * D_MODEL + lo:2 * D_MODEL + hi])
        u = c_gate * xb
        prev1 = carry_ref[SUBLANES - 1:SUBLANES, lo:hi]
        prev2 = carry_ref[SUBLANES - 2:SUBLANES - 1, lo:hi]
        u1 = jnp.where(row == 0, prev1, pltpu.roll(u, 1, axis=0))
        u2 = jnp.where(row == 0, prev2,
                       jnp.where(row == 1, prev1, pltpu.roll(u, 2, axis=0)))
        y = (cw_ref[0:1, lo:hi] * u2 + cw_ref[1:2, lo:hi] * u1
             + cw_ref[2:3, lo:hi] * u)
        z_ref[:, lo:hi] = (b_gate * y).astype(BF16)
        carry_ref[:, lo:hi] = u[t - SUBLANES:t, :]
    gate = mod[:, 2 * D_MODEL:3 * D_MODEL]
    out_ref[...] = x + gate * _dot(z_ref[...], wout_ref[0])


def _conv_layer(x, norm_g, mod, w_in_all, conv_w, w_out_all, layer):
    t = ROW_TILE
    tiles_per_seq = SEQ // t
    full = lambda shape: pl.BlockSpec(shape, lambda i: (0, 0))
    cw = jnp.pad(conv_w.astype(F32), ((0, SUBLANES - CONV_K), (0, 0)))
    return pl.pallas_call(
        _conv_kernel,
        out_shape=jax.ShapeDtypeStruct((N_TOK, D_MODEL), F32),
        grid=(N_TOK // t,),
        in_specs=[
            pl.BlockSpec((t, D_MODEL), lambda i: (i, 0)),
            full((1, D_MODEL)),
            pl.BlockSpec((1, 1, 3 * D_MODEL), lambda i: (i // tiles_per_seq, 0, 0)),
            pl.BlockSpec((1, D_MODEL, 3 * D_MODEL), lambda i: (layer, 0, 0)),
            full((SUBLANES, D_MODEL)),
            pl.BlockSpec((1, D_MODEL, D_MODEL), lambda i: (layer, 0, 0)),
        ],
        out_specs=pl.BlockSpec((t, D_MODEL), lambda i: (i, 0)),
        scratch_shapes=[
            pltpu.VMEM((SUBLANES, D_MODEL), F32),
            pltpu.VMEM((t, D_MODEL), BF16),
        ],
        compiler_params=_cparams(("arbitrary",)),
        name="conv_layer",
    )(x, norm_g.reshape(1, D_MODEL), mod, w_in_all, cw, w_out_all)


_ROUTE_ROWS = LANES
_EXPERT_ROW0 = SUBLANES


_ROUTE_TILES = 4
_TILE_SHIFT = MOE_TILE.bit_length() - 1
_LANE_SHIFT = LANES.bit_length() - 1


def _route_kernel(x_ref, g_ref, mod_ref, wr_hi_ref, wr_lo_ref, br_ref,
                  h_ref, route_ref, cnt_ref):
    h, route, cnt = _route_tiles(x_ref[...], g_ref[...], mod_ref[0],
                                 wr_hi_ref[...], wr_lo_ref[...], br_ref[...])
    h_ref[...] = h
    route_ref[...] = route
    for j in range(_ROUTE_TILES):
        cnt_ref[j] = cnt[:, j * LANES:(j + 1) * LANES]


def _route_tiles(x, g, mod, wr_hi, wr_lo, br):
    t = MOE_TILE * _ROUTE_TILES
    h = _rms_mod(x, g, mod)
    h_hi = h.astype(BF16)
    h_lo = (h - h_hi.astype(F32)).astype(BF16)
    logits = (_dot_nt(wr_hi, h_hi) + _dot_nt(wr_hi, h_lo)
              + _dot_nt(wr_lo, h_hi)) + br

    sub = lax.broadcasted_iota(I32, (SUBLANES, t), 0)
    neg_inf = -jnp.inf
    gl = jnp.where(sub < N_GROUPS, logits[0:SUBLANES, :], neg_inf)
    gmax = jnp.max(gl, axis=0, keepdims=True)
    g_sel = jnp.min(jnp.where(gl == gmax, sub, SUBLANES), axis=0, keepdims=True)
    p_sel = 1.0 / jnp.sum(jnp.exp(gl - gmax), axis=0, keepdims=True)

    e_sel = jnp.zeros((E_PER_GROUP, t), F32)
    for g in range(N_GROUPS):
        r0 = _EXPERT_ROW0 + g * E_PER_GROUP
        e_sel = jnp.where(g_sel == g, logits[r0:r0 + E_PER_GROUP, :], e_sel)
    v1 = jnp.max(e_sel, axis=0, keepdims=True)
    i1 = jnp.min(jnp.where(e_sel == v1, sub, SUBLANES), axis=0, keepdims=True)
    e_rest = jnp.where(sub == i1, neg_inf, e_sel)
    v2 = jnp.max(e_rest, axis=0, keepdims=True)
    i2 = jnp.min(jnp.where(e_rest == v2, sub, SUBLANES), axis=0, keepdims=True)
    ratio = jnp.exp(v2 - v1)
    w1 = p_sel / (1.0 + ratio)
    w2 = p_sel * ratio / (1.0 + ratio)
    eid1 = g_sel * E_PER_GROUP + i1
    eid2 = g_sel * E_PER_GROUP + i2

    erow = lax.broadcasted_iota(I32, (N_EXPERTS, t), 0)
    m1 = erow == eid1
    m2 = erow == eid2
    member = jnp.where(m1 | m2, 1.0, 0.0)
    r = lax.broadcasted_iota(I32, (t, t), 0)
    c = lax.broadcasted_iota(I32, (t, t), 1)
    same_tile = (lax.shift_right_logical(r, _TILE_SHIFT)
                 == lax.shift_right_logical(c, _TILE_SHIFT))
    earlier = jnp.where((r < c) & same_tile, 1.0, 0.0).astype(BF16)
    member_b = member.astype(BF16)
    rank = _dot(member_b, earlier)
    tr = lax.broadcasted_iota(I32, (t, _ROUTE_TILES * LANES), 0)
    tc = lax.broadcasted_iota(I32, (t, _ROUTE_TILES * LANES), 1)
    in_tile = jnp.where(lax.shift_right_logical(tr, _TILE_SHIFT)
                        == lax.shift_right_logical(tc, _LANE_SHIFT), 1.0, 0.0)
    cnt = _dot(member_b, in_tile.astype(BF16))
    cnt_pad = jnp.floor((cnt + (SEG_ALIGN - 1.0)) * (1.0 / SEG_ALIGN)) * SEG_ALIGN
    er = lax.broadcasted_iota(I32, (N_EXPERTS, N_EXPERTS), 0)
    ec = lax.broadcasted_iota(I32, (N_EXPERTS, N_EXPERTS), 1)
    before = jnp.where(er > ec, 1.0, 0.0).astype(BF16)
    seg_start = _dot(before, cnt_pad.astype(BF16))
    seg_start_tok = jnp.concatenate(
        [seg_start[:, j * LANES:(j + 1) * LANES]
         for j in range(_ROUTE_TILES) for _ in range(MOE_TILE // LANES)], axis=1)
    pos = seg_start_tok + rank
    pos1 = jnp.sum(jnp.where(m1, pos, 0.0), axis=0, keepdims=True)
    pos2 = jnp.sum(jnp.where(m2, pos, 0.0), axis=0, keepdims=True)

    out = jnp.zeros((SUBLANES, t), F32)
    for k, val in enumerate((pos1, pos2, w1, w2)):
        out = jnp.where(sub == k, val, out)
    return h_hi, out, cnt


def _route(x, norm_g, mod, w_group, b_group, w_expert, b_expert):
    t = MOE_TILE * _ROUTE_TILES
    tiles_per_seq = SEQ // t
    wr = jnp.zeros((_ROUTE_ROWS, D_MODEL), F32)
    wr = wr.at[0:N_GROUPS].set(w_group.T.astype(F32))
    wr = wr.at[_EXPERT_ROW0:_EXPERT_ROW0 + N_EXPERTS].set(w_expert.T.astype(F32))
    wr_hi = wr.astype(BF16)
    wr_lo = (wr - wr_hi.astype(F32)).astype(BF16)
    br = jnp.zeros((_ROUTE_ROWS,), F32)
    br = br.at[0:N_GROUPS].set(b_group.astype(F32))
    br = br.at[_EXPERT_ROW0:_EXPERT_ROW0 + N_EXPERTS].set(b_expert.astype(F32))
    full = lambda shape: pl.BlockSpec(shape, lambda i: (0, 0))
    return pl.pallas_call(
        _route_kernel,
        out_shape=(
            jax.ShapeDtypeStruct((N_TOK, D_MODEL), BF16),
            jax.ShapeDtypeStruct((SUBLANES, N_TOK), F32),
            jax.ShapeDtypeStruct((N_MOE_TILES, N_EXPERTS, LANES), F32),
        ),
        grid=(N_MOE_TILES // _ROUTE_TILES,),
        in_specs=[
            pl.BlockSpec((t, D_MODEL), lambda i: (i, 0)),
            full((1, D_MODEL)),
            pl.BlockSpec((1, 1, 3 * D_MODEL), lambda i: (i // tiles_per_seq, 0, 0)),
            full((_ROUTE_ROWS, D_MODEL)),
            full((_ROUTE_ROWS, D_MODEL)),
            full((_ROUTE_ROWS, 1)),
        ],
        out_specs=(
            pl.BlockSpec((t, D_MODEL), lambda i: (i, 0)),
            pl.BlockSpec((SUBLANES, t), lambda i: (0, i)),
            pl.BlockSpec((_ROUTE_TILES, N_EXPERTS, LANES), lambda i: (i, 0, 0)),
        ),
        compiler_params=_cparams(("arbitrary",)),
        name="moe_route",
    )(x, norm_g.reshape(1, D_MODEL), mod, wr_hi, wr_lo, br.reshape(_ROUTE_ROWS, 1))


_REM_UNITS = CHUNK // SEG_ALIGN
_REM_SHIFT = _REM_UNITS.bit_length() - 1


def _chunk_code(tokens):
    return (tokens // CHUNK) * _REM_UNITS + (tokens % CHUNK) // SEG_ALIGN


def _dispatch_plan(cnt):
    cnt = cnt.astype(I32)
    cnt_pad = (cnt + SEG_ALIGN - 1) // SEG_ALIGN * SEG_ALIGN
    seg = jnp.cumsum(cnt_pad, axis=1) - cnt_pad
    tot = jnp.sum(cnt_pad, axis=0)
    ptot = (tot + EXPERT_BLOCK - 1) // EXPERT_BLOCK * EXPERT_BLOCK
    pend = jnp.cumsum(ptot)
    gbase = pend - ptot
    dst = gbase[None, :] + jnp.cumsum(cnt_pad, axis=0) - cnt_pad
    nch = _chunk_code(cnt_pad)
    n_used = (pend[-1] // EXPERT_BLOCK).astype(I32)
    blk = jnp.arange(N_EXPERT_BLOCKS, dtype=I32)
    blk_start = jnp.minimum(blk, n_used - 1) * EXPERT_BLOCK
    blk_e = jnp.sum((pend[None, :] <= blk_start[:, None]).astype(I32), axis=1)
    blk_e = jnp.minimum(blk_e, N_EXPERTS - 1).astype(I32)
    gap_dst = gbase + tot
    gap_nch = _chunk_code(ptot - tot)
    misc = jnp.stack([n_used, jnp.sum(ptot - tot) // SEG_ALIGN]).astype(I32)
    first_blk = gbase // EXPERT_BLOCK
    ids = jnp.arange(N_EXPERTS, dtype=I32)
    later = (ids[None, :] > ids[:, None]) & (ptot[None, :] > 0)
    next_e = jnp.min(jnp.where(later, ids[None, :], N_EXPERTS), axis=1)
    next_e = jnp.where(next_e == N_EXPERTS, -1, next_e)
    return dict(seg=seg.reshape(-1).astype(I32), dst=dst.reshape(-1).astype(I32),
                nch=nch.reshape(-1).astype(I32),
                ntot=(jnp.sum(cnt_pad, axis=1) // SEG_ALIGN).astype(I32),
                nmax=(jnp.max(cnt_pad, axis=1) // CHUNK).astype(I32), blk_e=blk_e,
                first_blk=first_blk.astype(I32), next_e=next_e.astype(I32),
                gap_dst=gap_dst.astype(I32), gap_nch=gap_nch.astype(I32), misc=misc)


_INLINE_CHUNKS = 3


def _start_copy(make_copy, s0, d0, offset, tokens, priority=0):
    s = pl.multiple_of(s0 + offset, SEG_ALIGN)
    d = pl.multiple_of(d0 + offset, SEG_ALIGN)
    make_copy(s, d, tokens).start(priority=priority)


def _start_tail(make_copy, s0, d0, n_full, rem, priority=0):
    half, quarter = CHUNK // 2, CHUNK // 4
    tail = n_full * CHUNK
    has_half = (rem & 2) != 0
    pl.when(has_half)(functools.partial(_start_copy, make_copy, s0, d0, tail, half,
                                        priority))
    tail2 = tail + jnp.where(has_half, half, 0)
    pl.when((rem & 1) != 0)(functools.partial(_start_copy, make_copy, s0, d0, tail2,
                                              quarter, priority))


def _segment_copies_inline(tile, live, seg_ref, dst_ref, nch_ref, make_copy):
    for e in range(N_EXPERTS):
        idx = tile * N_EXPERTS + e
        code = jnp.where(live, nch_ref[idx], 0)
        n = lax.shift_right_logical(code, _REM_SHIFT)
        s0 = seg_ref[idx]
        d0 = dst_ref[idx]
        for cidx in range(_INLINE_CHUNKS):
            pl.when(cidx < n)(functools.partial(_start_copy, make_copy, s0, d0,
                                                cidx * CHUNK, CHUNK, (e + cidx) % 2))
        _start_tail(make_copy, s0, d0, n, code & (_REM_UNITS - 1), e % 2)


def _segment_copies_loop(tile, first, with_tail, seg_ref, dst_ref, nch_ref, make_copy):
    def per_expert(e, carry):
        idx = tile * N_EXPERTS + e
        code = nch_ref[idx]
        n = lax.shift_right_logical(code, _REM_SHIFT)
        s0 = seg_ref[idx]
        d0 = dst_ref[idx]

        def per_chunk(cidx, c2):
            _start_copy(make_copy, s0, d0, cidx * CHUNK, CHUNK)
            return c2

        lax.fori_loop(first, jnp.maximum(n, first), per_chunk, 0)
        if with_tail:
            _start_tail(make_copy, s0, d0, n, code & (_REM_UNITS - 1))
        return carry

    lax.fori_loop(0, N_EXPERTS, per_expert, 0)


_WAIT_GROUP = 64


def _wait_each(n, make_wait):
    def body(_, carry):
        make_wait().wait()
        return carry
    lax.fori_loop(0, n, body, 0)


def _wait_copies(units, make_copy):
    _wait_each(units // _WAIT_GROUP, lambda: make_copy(0, 0, _WAIT_GROUP * SEG_ALIGN))
    _wait_each(units % _WAIT_GROUP, lambda: make_copy(0, 0, SEG_ALIGN))


_HALF = D_MODEL // 2
_BF16_BITS = 16
_HI_MASK = -(1 << _BF16_BITS)


def _pack_pairs(x):
    lo = lax.shift_right_logical(lax.bitcast_convert_type(x[:, 0:_HALF], I32),
                                 _BF16_BITS)
    hi = lax.bitcast_convert_type(x[:, _HALF:D_MODEL], I32) & _HI_MASK
    return lo | hi


def _unpack_pairs(w):
    lo = lax.bitcast_convert_type(lax.shift_left(w, _BF16_BITS), F32).astype(BF16)
    hi = lax.bitcast_convert_type(w & _HI_MASK, F32).astype(BF16)
    return lo, hi


def _token_rows(ref, tok, tokens):
    start = pl.multiple_of(tok * WORD_ROWS, SUBLANES)
    return ref.at[pl.ds(start, tokens * WORD_ROWS)]


def _store_words(ref, words, first=0):
    rows = words.shape[0]
    for q in range(WORD_ROWS):
        ref[pl.ds(first * WORD_ROWS + q, rows, stride=WORD_ROWS), :] = (
            words[:, q * LANES:(q + 1) * LANES])


def _load_words(ref, rows, first=0):
    return jnp.concatenate(
        [ref[pl.ds(first * WORD_ROWS + q, rows, stride=WORD_ROWS), :]
         for q in range(WORD_ROWS)], axis=1)


_DISPATCH_BUFS = 3
_TILES_PER_STEP = 2


def _dispatch_kernel(seg_ref, dst_ref, nch_ref, ntot_ref, nmax_ref, gap_dst_ref,
                     gap_nch_ref, misc_ref, route_ref, h_ref, xs_ref, buf_ref,
                     zero_ref, sem, zsem):
    t = MOE_TILE
    last = N_MOE_TILES - 1
    n_used = misc_ref[0]

    def make_copy(which):
        def mk(s, d, tokens):
            return pltpu.make_async_copy(_token_rows(buf_ref.at[which], s, tokens),
                                         _token_rows(xs_ref, d, tokens), sem.at[which])
        return mk

    def zero_copy(s, d, tokens):
        del s
        return pltpu.make_async_copy(_token_rows(zero_ref, 0, tokens),
                                     _token_rows(xs_ref, d, tokens), zsem)

    @pl.when(pl.program_id(0) == 0)
    def _():
        zero_ref[...] = jnp.zeros_like(zero_ref)

        def per_expert(e, carry):
            d0 = gap_dst_ref[e]
            code = gap_nch_ref[e]
            n = lax.shift_right_logical(code, _REM_SHIFT)

            def per_chunk(cidx, c2):
                _start_copy(zero_copy, 0, d0, cidx * CHUNK, CHUNK)
                return c2

            lax.fori_loop(0, n, per_chunk, 0)
            _start_tail(zero_copy, 0, d0, n, code & (_REM_UNITS - 1))
            return carry

        lax.fori_loop(0, N_EXPERTS, per_expert, 0)

        def per_block(b, carry):
            zero_copy(0, b * EXPERT_BLOCK, EXPERT_BLOCK).start()
            return carry

        lax.fori_loop(n_used, N_EXPERT_BLOCKS, per_block, 0)

    def sort_tile(j):
        k = pl.program_id(0) * _TILES_PER_STEP + j
        slot = k % _DISPATCH_BUFS
        prev = (k + _DISPATCH_BUFS - 1) % _DISPATCH_BUFS
        prev2 = (k + _DISPATCH_BUFS - 2) % _DISPATCH_BUFS
        tile_prev = jnp.maximum(k - 1, 0)
        _segment_copies_inline(tile_prev, k > 0, seg_ref, dst_ref, nch_ref,
                               make_copy(prev))

        route = route_ref[:, j * t:(j + 1) * t]
        pos1 = route[0:1, :].astype(I32)
        pos2 = route[1:2, :].astype(I32)
        r = lax.broadcasted_iota(I32, (DISPATCH_ROWS, t), 0)
        perm = jnp.where((r == pos1) | (r == pos2), 1.0, 0.0).astype(BF16)
        _store_words(buf_ref.at[slot],
                     _pack_pairs(_dot(perm, h_ref[j * t:(j + 1) * t, :])))

        @pl.when((k > 0) & (nmax_ref[tile_prev] > _INLINE_CHUNKS))
        def _():
            _segment_copies_loop(tile_prev, _INLINE_CHUNKS, False, seg_ref, dst_ref,
                                 nch_ref, make_copy(prev))

        @pl.when(k > 1)
        def _():
            _wait_copies(ntot_ref[jnp.maximum(k - 2, 0)], make_copy(prev2))

        @pl.when(k == last)
        def _():
            _segment_copies_loop(k, 0, True, seg_ref, dst_ref, nch_ref, make_copy(slot))
            _wait_copies(ntot_ref[last - 1], make_copy(prev))
            _wait_copies(ntot_ref[last], make_copy(slot))
            _wait_copies(misc_ref[1], zero_copy)
            _wait_each(N_EXPERT_BLOCKS - n_used, lambda: zero_copy(0, 0, EXPERT_BLOCK))

    for j in range(_TILES_PER_STEP):
        sort_tile(j)


def _dispatch(plan, route, h):
    t = MOE_TILE * _TILES_PER_STEP
    return pl.pallas_call(
        _dispatch_kernel,
        out_shape=jax.ShapeDtypeStruct((SORTED_ROWS * WORD_ROWS, LANES), I32),
        grid_spec=pltpu.PrefetchScalarGridSpec(
            num_scalar_prefetch=8,
            grid=(N_MOE_TILES // _TILES_PER_STEP,),
            in_specs=[
                pl.BlockSpec((SUBLANES, t), lambda i, *_: (0, i)),
                pl.BlockSpec((t, D_MODEL), lambda i, *_: (i, 0)),
            ],
            out_specs=pl.BlockSpec(memory_space=pl.ANY),
            scratch_shapes=[
                pltpu.VMEM((_DISPATCH_BUFS, DISPATCH_ROWS * WORD_ROWS, LANES), I32),
                pltpu.VMEM((EXPERT_BLOCK * WORD_ROWS, LANES), I32),
                pltpu.SemaphoreType.DMA((_DISPATCH_BUFS,)),
                pltpu.SemaphoreType.DMA,
            ],
        ),
        compiler_params=_cparams(("arbitrary",)),
        name="moe_dispatch",
    )(plan["seg"], plan["dst"], plan["nch"], plan["ntot"], plan["nmax"],
      plan["gap_dst"], plan["gap_nch"], plan["misc"], route, h)


_WEIGHT_DMA_PRIORITY = 1


def _expert_kernel(blk_e_ref, first_ref, next_ref, misc_ref, x_ref, wg_hbm, wu_hbm,
                   wd_hbm, y_ref, wg_f, wu_f, wd_f, wg_b, wu_b, wd_b, sem, *, layer):
    blk0 = pl.program_id(0) * _BLOCKS_PER_STEP
    blk1 = blk0 + 1
    n_used = misc_ref[0]

    def weight_copies(e):
        return (pltpu.make_async_copy(wg_hbm.at[layer, e], wg_f, sem.at[0]),
                pltpu.make_async_copy(wu_hbm.at[layer, e], wu_f, sem.at[1]),
                pltpu.make_async_copy(wd_hbm.at[layer, e], wd_f, sem.at[2]))

    def take_weights(e):
        for cp in weight_copies(e):
            cp.wait()
        wg_b[...] = wg_f[...].astype(BF16)
        wu_b[...] = wu_f[...].astype(BF16)
        wd_b[...] = wd_f[...].astype(BF16)

        @pl.when(next_ref[e] >= 0)
        def _():
            for cp in weight_copies(next_ref[e]):
                cp.start(priority=_WEIGHT_DMA_PRIORITY)

    def run_rows(first, rows):
        x_lo, x_hi = _unpack_pairs(_load_words(x_ref, rows, first))
        x = jnp.concatenate([x_lo, x_hi], axis=1)
        g = _dot(x, wg_b[...])
        u = _dot(x, wu_b[...])
        a = (g * _sigmoid(g) * u).astype(BF16)
        y = _dot(a, wd_b[...])
        _store_words(y_ref, _pack_pairs(y.astype(BF16).astype(F32)), first)

    @pl.when(blk0 == 0)
    def _():
        for cp in weight_copies(blk_e_ref[0]):
            cp.start(priority=_WEIGHT_DMA_PRIORITY)

    @pl.when(blk0 < n_used)
    def _():
        e0 = blk_e_ref[blk0]
        e1 = blk_e_ref[blk1]
        pl.when(blk0 == first_ref[e0])(functools.partial(take_weights, e0))

        @pl.when(e0 == e1)
        def _():
            run_rows(0, _BLOCKS_PER_STEP * EXPERT_BLOCK)

        @pl.when(e0 != e1)
        def _():
            run_rows(0, EXPERT_BLOCK)
            take_weights(e1)
            run_rows(EXPERT_BLOCK, EXPERT_BLOCK)

    @pl.when(blk0 >= n_used)
    def _():
        y_ref[...] = jnp.zeros_like(y_ref)


def _experts(plan, xs, layer, w_gate, w_up, w_down):
    step_rows = _BLOCKS_PER_STEP * EXPERT_BLOCK * WORD_ROWS
    row_map = lambda i, be, fi, nx, misc: (
        jnp.minimum(i, (misc[0] - 1) // _BLOCKS_PER_STEP), 0)
    out_map = lambda i, be, fi, nx, misc: (i, 0)
    return pl.pallas_call(
        functools.partial(_expert_kernel, layer=layer),
        out_shape=jax.ShapeDtypeStruct((SORTED_ROWS * WORD_ROWS, LANES), I32),
        grid_spec=pltpu.PrefetchScalarGridSpec(
            num_scalar_prefetch=4,
            grid=(N_EXPERT_BLOCKS // _BLOCKS_PER_STEP,),
            in_specs=[
                pl.BlockSpec((step_rows, LANES), row_map),
                pl.BlockSpec(memory_space=pl.ANY),
                pl.BlockSpec(memory_space=pl.ANY),
                pl.BlockSpec(memory_space=pl.ANY),
            ],
            out_specs=pl.BlockSpec((step_rows, LANES), out_map),
            scratch_shapes=[
                pltpu.VMEM((D_MODEL, D_FF), F32),
                pltpu.VMEM((D_MODEL, D_FF), F32),
                pltpu.VMEM((D_FF, D_MODEL), F32),
                pltpu.VMEM((D_MODEL, D_FF), BF16),
                pltpu.VMEM((D_MODEL, D_FF), BF16),
                pltpu.VMEM((D_FF, D_MODEL), BF16),
                pltpu.SemaphoreType.DMA((3,)),
            ],
        ),
        compiler_params=_cparams(("arbitrary",)),
        name="moe_experts",
    )(plan["blk_e"], plan["first_blk"], plan["next_e"], plan["misc"], xs,
      w_gate, w_up, w_down)


_COMBINE_BUFS = 3


def _combine_kernel(seg_ref, dst_ref, nch_ref, ntot_ref, nmax_ref, route_ref, x_ref,
                    mod_ref, fg_ref, ys_ref, out_ref, *scratch, final_norm):
    bufs, sem = scratch[:_COMBINE_BUFS], scratch[_COMBINE_BUFS]
    t = MOE_TILE
    ahead = _COMBINE_BUFS - 1

    def make_copy(which):
        def mk(s, d, tokens):
            return pltpu.make_async_copy(_token_rows(ys_ref, d, tokens),
                                         _token_rows(bufs[which], s, tokens),
                                         sem.at[which])
        return mk

    @pl.when(pl.program_id(0) == 0)
    def _():
        for buf in bufs:
            buf[...] = jnp.zeros_like(buf)
        for k in range(_COMBINE_BUFS - 1):
            _segment_copies_loop(k, 0, True, seg_ref, dst_ref, nch_ref, make_copy(k))

    def unsort_tile(j, k, cur):
        nxt = (cur + ahead) % _COMBINE_BUFS
        tile_next = jnp.minimum(k + ahead, N_MOE_TILES - 1)
        has_next = k + ahead < N_MOE_TILES
        rows = pl.ds(j * t, t)
        _wait_copies(ntot_ref[k], make_copy(cur))
        _segment_copies_inline(tile_next, has_next, seg_ref, dst_ref, nch_ref,
                               make_copy(nxt))
        route = route_ref[:, j * t:(j + 1) * t]
        route_t = jnp.concatenate(
            [route, jnp.zeros((LANES - SUBLANES, t), F32)], axis=0).T
        pos1 = route_t[:, 0:1].astype(I32)
        pos2 = route_t[:, 1:2].astype(I32)
        w1 = route_t[:, 2:3]
        w2 = route_t[:, 3:4]
        c = lax.broadcasted_iota(I32, (t, SORT_ROWS), 1)
        unsort = (jnp.where(c == pos1, w1, 0.0)
                  + jnp.where(c == pos2, w2, 0.0)).astype(BF16)
        y_lo, y_hi = _unpack_pairs(_load_words(bufs[cur], SORT_ROWS))
        moe = jnp.concatenate([_dot(unsort, y_lo), _dot(unsort, y_hi)], axis=1)
        gate = mod_ref[0][:, 2 * D_MODEL:3 * D_MODEL]
        x_new = x_ref[rows, :] + gate * moe
        if final_norm:
            ms = jnp.mean(x_new * x_new, axis=-1, keepdims=True)
            x_new = (x_new * lax.rsqrt(ms + EPS)) * fg_ref[...]
        out_ref[rows, :] = x_new

        @pl.when(has_next & (nmax_ref[tile_next] > _INLINE_CHUNKS))
        def _():
            _segment_copies_loop(tile_next, _INLINE_CHUNKS, False, seg_ref, dst_ref,
                                 nch_ref, make_copy(nxt))

    for j in range(_TILES_PER_STEP):
        k = pl.program_id(0) * _TILES_PER_STEP + j
        for cur in range(_COMBINE_BUFS):
            pl.when(k % _COMBINE_BUFS == cur)(functools.partial(unsort_tile, j, k, cur))


def _combine(plan, route, x, mod, final_g, ys, final_norm):
    t = MOE_TILE * _TILES_PER_STEP
    tiles_per_seq = SEQ // t
    return pl.pallas_call(
        functools.partial(_combine_kernel, final_norm=final_norm),
        out_shape=jax.ShapeDtypeStruct((N_TOK, D_MODEL), F32),
        grid_spec=pltpu.PrefetchScalarGridSpec(
            num_scalar_prefetch=5,
            grid=(N_MOE_TILES // _TILES_PER_STEP,),
            in_specs=[
                pl.BlockSpec((SUBLANES, t), lambda i, *_: (0, i)),
                pl.BlockSpec((t, D_MODEL), lambda i, *_: (i, 0)),
                pl.BlockSpec((1, 1, 3 * D_MODEL), lambda i, *_: (i // tiles_per_seq, 0, 0)),
                pl.BlockSpec((1, D_MODEL), lambda i, *_: (0, 0)),
                pl.BlockSpec(memory_space=pl.ANY),
            ],
            out_specs=pl.BlockSpec((t, D_MODEL), lambda i, *_: (i, 0)),
            scratch_shapes=(
                [pltpu.VMEM((SORT_ROWS * WORD_ROWS, LANES), I32)] * _COMBINE_BUFS
                + [pltpu.SemaphoreType.DMA((_COMBINE_BUFS,))]),
        ),
        compiler_params=_cparams(("arbitrary",)),
        name="moe_combine",
    )(plan["seg"], plan["dst"], plan["nch"], plan["ntot"], plan["nmax"], route, x,
      mod, final_g.reshape(1, D_MODEL).astype(F32), ys)


def _moe_layer(x, norm_g, mod, final_g, w_group, b_group, w_expert, b_expert,
               layer, w_gate, w_up, w_down, final_norm):
    h, route, cnt = _route(x, norm_g, mod, w_group, b_group, w_expert, b_expert)
    plan = _dispatch_plan(cnt[:, :, 0])
    xs = _dispatch(plan, route, h)
    ys = _experts(plan, xs, layer, w_gate, w_up, w_down)
    return _combine(plan, route, x, mod, final_g, ys, final_norm)


def kernel(x, c, ada_w, ada_b, norm_g, final_g, m_w_in, m_b_gates, m_norm_g, m_w_out, s_w_in, s_conv_w, s_w_out, r_w_group, r_b_group, r_w_expert, r_b_expert, e_w_gate, e_w_up, e_w_down):
    mods = _ada_mods(c, ada_w, ada_b)
    m_w_out_b = m_w_out.astype(BF16)
    s_w_in_b, s_w_out_b = s_w_in.astype(BF16), s_w_out.astype(BF16)
    xt = x.reshape(N_TOK, D_MODEL)
    for i in range(DEPTH):
        mod_mix = mods[2 * i].reshape(BATCH, 1, 3 * D_MODEL)
        mod_ffn = mods[2 * i + 1].reshape(BATCH, 1, 3 * D_MODEL)
        j = i // 2
        if i % 2 == 0:
            qt, k, vt, ot, gates, gatest = _mlstm_in(xt, norm_g[i, 0], mod_mix,
                                                     m_w_in[j], m_b_gates[j])
            xt = _mlstm_rec(xt, mod_mix, qt, k, vt, ot, gates, gatest,
                            m_norm_g[j], m_w_out_b, j)
        else:
            xt = _conv_layer(xt, norm_g[i, 0], mod_mix, s_w_in_b, s_conv_w[j],
                             s_w_out_b, j)
        xt = _moe_layer(xt, norm_g[i, 1], mod_ffn, final_g, r_w_group[i],
                        r_b_group[i], r_w_expert[i], r_b_expert[i], i, e_w_gate,
                        e_w_up, e_w_down, final_norm=(i == DEPTH - 1))
    return xt.reshape(BATCH, SEQ, D_MODEL)
```

```python
import functools

import jax
import jax.numpy as jnp
from jax import lax
from jax.experimental import pallas as pl
from jax.experimental.pallas import tpu as pltpu

F32 = jnp.float32
BF16 = jnp.bfloat16
I32 = jnp.int32

D_MODEL = 1024
BATCH = 8
SEQ = 2048
DEPTH = 4
N_TOK = BATCH * SEQ
M_HEADS = 4
M_DK = 128
M_DV = 256
M_QK = M_HEADS * M_DK
M_V = M_HEADS * M_DV
CONV_K = 3
N_GROUPS = 4
E_PER_GROUP = 8
N_EXPERTS = N_GROUPS * E_PER_GROUP
TOP_K = 2
D_FF = 512
EPS = 1e-6

SUBLANES = 8
LANES = 128

ROW_TILE = 1024
MLSTM_CHUNK = 256
MOE_TILE = 256
EXPERT_BLOCK = 256
WORD_ROWS = (D_MODEL // 2) // LANES
SEG_ALIGN = SUBLANES // WORD_ROWS
CHUNK = 8
_MAX_TILE_SORTED = MOE_TILE * TOP_K + N_EXPERTS * (SEG_ALIGN - 1)
SORT_ROWS = -(-_MAX_TILE_SORTED // LANES) * LANES
BF16_SUBLANES = 2 * SUBLANES
DISPATCH_ROWS = -(-_MAX_TILE_SORTED // BF16_SUBLANES) * BF16_SUBLANES
N_MOE_TILES = N_TOK // MOE_TILE
_MAX_SORTED = (N_MOE_TILES * (MOE_TILE * TOP_K + N_EXPERTS * (SEG_ALIGN - 1))
               + N_EXPERTS * (EXPERT_BLOCK - 1))
_BLOCKS_PER_STEP = 2
N_EXPERT_BLOCKS = (-(-_MAX_SORTED // (EXPERT_BLOCK * _BLOCKS_PER_STEP))
                   * _BLOCKS_PER_STEP)
SORTED_ROWS = N_EXPERT_BLOCKS * EXPERT_BLOCK

VMEM_LIMIT = 48 * 1024 * 1024


def _cparams(sem):
    return pltpu.CompilerParams(dimension_semantics=sem,
                                vmem_limit_bytes=VMEM_LIMIT)


def _dot(a, b):
    return jnp.dot(a, b, preferred_element_type=F32)


def _dot_nt(a, b):
    return lax.dot_general(a, b, (((1,), (1,)), ((), ())),
                           preferred_element_type=F32)


def _split3(x):
    hi = x.astype(BF16)
    r1 = x - hi.astype(F32)
    mid = r1.astype(BF16)
    lo = (r1 - mid.astype(F32)).astype(BF16)
    return hi, mid, lo


def _dot_sel_left(sel, x):
    hi, mid, lo = _split3(x)
    return _dot(sel, hi) + _dot(sel, mid) + _dot(sel, lo)


def _dot_sel_right(x, sel):
    hi, mid, lo = _split3(x)
    return _dot(hi, sel) + _dot(mid, sel) + _dot(lo, sel)


def _sigmoid(x):
    return 1.0 / (1.0 + jnp.exp(-x))


def _rms_mod(x, g, mod):
    ms = jnp.mean(x * x, axis=-1, keepdims=True)
    y = (x * lax.rsqrt(ms + EPS)) * g
    return y * (1.0 + mod[:, D_MODEL:2 * D_MODEL]) + mod[:, 0:D_MODEL]


def _ada_kernel(c_ref, w_ref, b_ref, o_ref):
    c = c_ref[...]
    s = (c * _sigmoid(c)).astype(BF16)
    o_ref[0] = _dot(s, w_ref[0].astype(BF16)) + b_ref[0]


def _ada_mods(c, ada_w, ada_b):
    n_pairs = DEPTH * 2
    w = ada_w.reshape(n_pairs, D_MODEL, 3 * D_MODEL)
    b = ada_b.reshape(n_pairs, 1, 3 * D_MODEL)
    col = D_MODEL
    return pl.pallas_call(
        _ada_kernel,
        out_shape=jax.ShapeDtypeStruct((n_pairs, BATCH, 3 * D_MODEL), F32),
        grid=(n_pairs, 3 * D_MODEL // col),
        in_specs=[
            pl.BlockSpec((BATCH, D_MODEL), lambda p, j: (0, 0)),
            pl.BlockSpec((1, D_MODEL, col), lambda p, j: (p, 0, j)),
            pl.BlockSpec((1, 1, col), lambda p, j: (p, 0, j)),
        ],
        out_specs=pl.BlockSpec((1, BATCH, col), lambda p, j: (p, 0, j)),
        compiler_params=_cparams(("arbitrary", "arbitrary")),
        name="ada_mods",
    )(c, w, b)


def _mlstm_in_kernel(x_ref, g_ref, mod_ref, wqt_ref, wk_ref, wvt_ref, wot_ref,
                     wg_ref, bg_ref, qt_ref, k_ref, vt_ref, ot_ref, gates_ref,
                     gatest_ref):
    h = _rms_mod(x_ref[...], g_ref[...], mod_ref[0])
    hb = h.astype(BF16)
    qt_ref[...] = (_dot_nt(wqt_ref[...], hb) * (M_DK ** -0.5)).astype(BF16)
    k_ref[...] = _dot(hb, wk_ref[...]).astype(BF16)
    vt_ref[...] = _dot_nt(wvt_ref[...], hb).astype(BF16)
    ot_ref[...] = _dot_nt(wot_ref[...], hb).astype(BF16)
    g = _dot(hb, wg_ref[...]) + bg_ref[...]
    log_sig = jnp.minimum(g, 0.0) - jnp.log(1.0 + jnp.exp(-jnp.abs(g)))
    lane = lax.broadcasted_iota(I32, g.shape, 1)
    gg = jnp.where(lane < M_HEADS, g, log_sig)
    gates_ref[...] = gg
    gatest_ref[...] = gg.T[0:SUBLANES, :]


def _mlstm_in(x, norm_g, mod, w_in, b_gates):
    t = ROW_TILE
    tiles_per_seq = SEQ // t
    wqt = w_in[:, 0:M_QK].T.astype(BF16)
    wk = w_in[:, M_QK:2 * M_QK].astype(BF16)
    wvt = w_in[:, 2 * M_QK:2 * M_QK + M_V].T.astype(BF16)
    wot = w_in[:, 2 * M_QK + M_V:2 * M_QK + 2 * M_V].T.astype(BF16)
    n_gate = 2 * M_HEADS
    wg = jnp.pad(w_in[:, 2 * M_QK + 2 * M_V:], ((0, 0), (0, LANES - n_gate))).astype(BF16)
    bg = jnp.pad(b_gates.astype(F32), (0, LANES - n_gate)).reshape(1, LANES)
    full = lambda shape: pl.BlockSpec(shape, lambda i: (0, 0))
    return pl.pallas_call(
        _mlstm_in_kernel,
        out_shape=(
            jax.ShapeDtypeStruct((M_QK, N_TOK), BF16),
            jax.ShapeDtypeStruct((N_TOK, M_QK), BF16),
            jax.ShapeDtypeStruct((M_V, N_TOK), BF16),
            jax.ShapeDtypeStruct((M_V, N_TOK), BF16),
            jax.ShapeDtypeStruct((N_TOK, LANES), F32),
            jax.ShapeDtypeStruct((SUBLANES, N_TOK), F32),
        ),
        grid=(N_TOK // t,),
        in_specs=[
            pl.BlockSpec((t, D_MODEL), lambda i: (i, 0)),
            full((1, D_MODEL)),
            pl.BlockSpec((1, 1, 3 * D_MODEL), lambda i: (i // tiles_per_seq, 0, 0)),
            full((M_QK, D_MODEL)),
            full((D_MODEL, M_QK)),
            full((M_V, D_MODEL)),
            full((M_V, D_MODEL)),
            full((D_MODEL, LANES)),
            full((1, LANES)),
        ],
        out_specs=(
            pl.BlockSpec((M_QK, t), lambda i: (0, i)),
            pl.BlockSpec((t, M_QK), lambda i: (i, 0)),
            pl.BlockSpec((M_V, t), lambda i: (0, i)),
            pl.BlockSpec((M_V, t), lambda i: (0, i)),
            pl.BlockSpec((t, LANES), lambda i: (i, 0)),
            pl.BlockSpec((SUBLANES, t), lambda i: (0, i)),
        ),
        compiler_params=_cparams(("arbitrary",)),
        name="mlstm_in",
    )(x, norm_g.reshape(1, D_MODEL), mod, wqt, wk, wvt, wot, wg, bg)


def _mlstm_rec_kernel(qt_ref, k_ref, vt_ref, ot_ref, gates_ref, gatest_ref,
                      x_ref, mod_ref, ng_ref, wout_ref, out_ref, ct_ref, m_ref):
    L = MLSTM_CHUNK

    @pl.when(pl.program_id(1) == 0)
    def _():
        ct_ref[...] = jnp.zeros_like(ct_ref)
        m_ref[...] = jnp.zeros_like(m_ref)

    gates = gates_ref[...]
    gatest = gatest_ref[...]
    row = lax.broadcasted_iota(I32, (L, L), 0)
    col = lax.broadcasted_iota(I32, (L, L), 1)
    tri_low = jnp.where(row >= col, 1.0, 0.0).astype(BF16)
    tri_up = jnp.where(row <= col, 1.0, 0.0).astype(BF16)
    cum_cols = _dot_sel_left(tri_low, gates)
    cum_rows = _dot_sel_right(gatest, tri_up)

    col_term = gates - pltpu.roll(cum_cols, LANES - M_HEADS, axis=1)
    head_lane = lax.broadcasted_iota(I32, col_term.shape, 1) < M_HEADS
    pieces = [jnp.where(head_lane, p.astype(F32), 0.0) for p in _split3(col_term)]
    col_pieces = (pieces[0] + pltpu.roll(pieces[1], M_HEADS, axis=1)
                  + pltpu.roll(pieces[2], 2 * M_HEADS, axis=1)).astype(BF16)
    ng_wide = jnp.concatenate([ng_ref[...]] * (L // LANES), axis=1)

    heads = range(M_HEADS)
    hl = M_HEADS * L
    qts = [qt_ref[h * M_DK:(h + 1) * M_DK, :] for h in heads]
    khs = [k_ref[:, h * M_DK:(h + 1) * M_DK] for h in heads]
    m_old = m_ref[...]
    ig_all = jnp.concatenate([gatest[h:h + 1, :] for h in heads], axis=1)
    bcum_all = jnp.concatenate(
        [cum_rows[M_HEADS + h:M_HEADS + h + 1, :] for h in heads], axis=1)
    m_prev_all = jnp.concatenate(
        [jnp.broadcast_to(m_old[h:h + 1, 0:1], (1, L)) for h in heads], axis=1)

    sel_row = lax.broadcasted_iota(I32, (LANES, hl), 0)
    sel_col = lax.broadcasted_iota(I32, (LANES, hl), 1)
    sel_head = lax.shift_right_logical(sel_col, L.bit_length() - 1)
    pick = jnp.where((sel_row == sel_head) | (sel_row == M_HEADS + sel_head)
                     | (sel_row == 2 * M_HEADS + sel_head), 1.0, 0.0).astype(BF16)
    src = lax.broadcasted_iota(I32, (L, hl), 0)
    tgt = lax.broadcasted_iota(I32, (L, hl), 1) & (L - 1)
    dlog = jnp.where(src <= tgt, _dot(col_pieces, pick) + bcum_all, -jnp.inf)
    inter_log = bcum_all + m_prev_all
    m_t_all = jnp.maximum(inter_log, jnp.max(dlog, axis=0, keepdims=True))
    w_intra = jnp.exp(dlog - m_t_all)
    w_inter_all = jnp.exp(inter_log - m_t_all)
    floor_all = jnp.exp(-m_t_all)
    scores_all = jnp.concatenate([_dot(khs[h], qts[h]) for h in heads], axis=1) * w_intra
    score_sum = jnp.sum(scores_all, axis=0, keepdims=True)
    scores_b = scores_all.astype(BF16)

    hs = []
    for h in heads:
        lanes = slice(h * L, (h + 1) * L)
        qt, kh = qts[h], khs[h]
        vt = vt_ref[h * M_DV:(h + 1) * M_DV, :]
        state = ct_ref[h]
        ig_row = ig_all[:, lanes]
        bcum_row = bcum_all[:, lanes]
        m_prev = m_old[h:h + 1, 0:1]
        w_inter = w_inter_all[:, lanes]
        q_state = _dot(state.astype(BF16), qt)
        num = _dot(vt, scores_b[:, lanes]) + w_inter * q_state[0:M_DV, :]
        den = score_sum[:, lanes] + w_inter * q_state[M_DV:M_DV + 1, :]
        h_out = num / jnp.maximum(jnp.abs(den), floor_all[:, lanes])

        b_last = bcum_row[:, L - 1:L]
        log_src = b_last - bcum_row + ig_row
        m_new = jnp.maximum(b_last + m_prev,
                            jnp.max(log_src, axis=1, keepdims=True))
        w_src = jnp.exp(log_src - m_new)
        decay = jnp.exp(b_last + m_prev - m_new)
        vt_w = jnp.concatenate(
            [(vt.astype(F32) * w_src).astype(BF16),
             jnp.broadcast_to(w_src, (SUBLANES, L)).astype(BF16)], axis=0)
        ct_ref[h] = decay * state + _dot(vt_w, kh)
        m_ref[h:h + 1, :] = jnp.broadcast_to(m_new, (1, LANES))

        hn = h_out * lax.rsqrt(jnp.mean(h_out * h_out, axis=0, keepdims=True) + EPS)
        og = ot_ref[h * M_DV:(h + 1) * M_DV, :].astype(F32)
        hs.append((hn * ng_wide[h * M_DV:(h + 1) * M_DV, :] * _sigmoid(og)).astype(BF16))

    hs_t = jnp.concatenate(hs, axis=0)
    y = lax.dot_general(hs_t, wout_ref[0], (((0,), (0,)), ((), ())),
                        preferred_element_type=F32)
    gate = mod_ref[0][:, 2 * D_MODEL:3 * D_MODEL]
    out_ref[...] = x_ref[...] + gate * y


def _mlstm_rec(x, mod, qt, k, vt, ot, gates, gatest, m_norm_g, w_out_all, layer):
    L = MLSTM_CHUNK
    nc = SEQ // L
    rows = lambda width: pl.BlockSpec((L, width), lambda b, j: (b * nc + j, 0))
    cols = lambda height: pl.BlockSpec((height, L), lambda b, j: (0, b * nc + j))
    ng = jnp.broadcast_to(m_norm_g.astype(F32).reshape(M_V, 1), (M_V, LANES))
    return pl.pallas_call(
        _mlstm_rec_kernel,
        out_shape=jax.ShapeDtypeStruct((N_TOK, D_MODEL), F32),
        grid=(BATCH, nc),
        in_specs=[
            cols(M_QK), rows(M_QK), cols(M_V), cols(M_V), rows(LANES),
            cols(SUBLANES), rows(D_MODEL),
            pl.BlockSpec((1, 1, 3 * D_MODEL), lambda b, j: (b, 0, 0)),
            pl.BlockSpec((M_V, LANES), lambda b, j: (0, 0)),
            pl.BlockSpec((1, M_V, D_MODEL), lambda b, j: (layer, 0, 0)),
        ],
        out_specs=rows(D_MODEL),
        scratch_shapes=[
            pltpu.VMEM((M_HEADS, M_DV + SUBLANES, M_DK), F32),
            pltpu.VMEM((SUBLANES, LANES), F32),
        ],
        compiler_params=_cparams(("arbitrary", "arbitrary")),
        name="mlstm_rec",
    )(qt, k, vt, ot, gates, gatest, x, mod, ng, w_out_all)


_CONV_COLS = 256


def _conv_kernel(x_ref, g_ref, mod_ref, win_ref, cw_ref, wout_ref, out_ref,
                 carry_ref, z_ref):
    t = ROW_TILE
    tiles_per_seq = SEQ // t

    @pl.when(pl.program_id(0) % tiles_per_seq == 0)
    def _():
        carry_ref[...] = jnp.zeros_like(carry_ref)

    x = x_ref[...]
    mod = mod_ref[0]
    hb = _rms_mod(x, g_ref[...], mod).astype(BF16)
    row = lax.broadcasted_iota(I32, (t, _CONV_COLS), 0)
    for j in range(D_MODEL // _CONV_COLS):
        lo, hi = j * _CONV_COLS, (j + 1) * _CONV_COLS
        b_gate = _dot(hb, win_ref[0, :, lo:hi])
        c_gate = _dot(hb, win_ref[0, :, D_MODEL + lo:D_MODEL + hi])
        xb = _dot(hb, win_ref[0, :, 2 * D_MODEL + lo:2 * D_MODEL + hi])
        u = c_gate * xb
        prev1 = carry_ref[SUBLANES - 1:SUBLANES, lo:hi]
        prev2 = carry_ref[SUBLANES - 2:SUBLANES - 1, lo:hi]
        u1 = jnp.where(row == 0, prev1, pltpu.roll(u, 1, axis=0))
        u2 = jnp.where(row == 0, prev2,
                       jnp.where(row == 1, prev1, pltpu.roll(u, 2, axis=0)))
        y = (cw_ref[0:1, lo:hi] * u2 + cw_ref[1:2, lo:hi] * u1
             + cw_ref[2:3, lo:hi] * u)
        z_ref[:, lo:hi] = (b_gate * y).astype(BF16)
        carry_ref[:, lo:hi] = u[t - SUBLANES:t, :]
    gate = mod[:, 2 * D_MODEL:3 * D_MODEL]
    out_ref[...] = x + gate * _dot(z_ref[...], wout_ref[0])


def _conv_layer(x, norm_g, mod, w_in_all, conv_w, w_out_all, layer):
    t = ROW_TILE
    tiles_per_seq = SEQ // t
    full = lambda shape: pl.BlockSpec(shape, lambda i: (0, 0))
    cw = jnp.pad(conv_w.astype(F32), ((0, SUBLANES - CONV_K), (0, 0)))
    return pl.pallas_call(
        _conv_kernel,
        out_shape=jax.ShapeDtypeStruct((N_TOK, D_MODEL), F32),
        grid=(N_TOK // t,),
        in_specs=[
            pl.BlockSpec((t, D_MODEL), lambda i: (i, 0)),
            full((1, D_MODEL)),
            pl.BlockSpec((1, 1, 3 * D_MODEL), lambda i: (i // tiles_per_seq, 0, 0)),
            pl.BlockSpec((1, D_MODEL, 3 * D_MODEL), lambda i: (layer, 0, 0)),
            full((SUBLANES, D_MODEL)),
            pl.BlockSpec((1, D_MODEL, D_MODEL), lambda i: (layer, 0, 0)),
        ],
        out_specs=pl.BlockSpec((t, D_MODEL), lambda i: (i, 0)),
        scratch_shapes=[
            pltpu.VMEM((SUBLANES, D_MODEL), F32),
            pltpu.VMEM((t, D_MODEL), BF16),
        ],
        compiler_params=_cparams(("arbitrary",)),
        name="conv_layer",
    )(x, norm_g.reshape(1, D_MODEL), mod, w_in_all, cw, w_out_all)


_ROUTE_ROWS = LANES
_EXPERT_ROW0 = SUBLANES


_ROUTE_TILES = 4
_TILE_SHIFT = MOE_TILE.bit_length() - 1
_LANE_SHIFT = LANES.bit_length() - 1


def _route_kernel(x_ref, g_ref, mod_ref, wr_hi_ref, wr_lo_ref, br_ref,
                  h_ref, route_ref, cnt_ref):
    h, route, cnt = _route_tiles(x_ref[...], g_ref[...], mod_ref[0],
                                 wr_hi_ref[...], wr_lo_ref[...], br_ref[...])
    h_ref[...] = h
    route_ref[...] = route
    for j in range(_ROUTE_TILES):
        cnt_ref[j] = cnt[:, j * LANES:(j + 1) * LANES]


def _route_tiles(x, g, mod, wr_hi, wr_lo, br):
    t = MOE_TILE * _ROUTE_TILES
    h = _rms_mod(x, g, mod)
    h_hi = h.astype(BF16)
    h_lo = (h - h_hi.astype(F32)).astype(BF16)
    logits = (_dot_nt(wr_hi, h_hi) + _dot_nt(wr_hi, h_lo)
              + _dot_nt(wr_lo, h_hi)) + br

    sub = lax.broadcasted_iota(I32, (SUBLANES, t), 0)
    neg_inf = -jnp.inf
    gl = jnp.where(sub < N_GROUPS, logits[0:SUBLANES, :], neg_inf)
    gmax = jnp.max(gl, axis=0, keepdims=True)
    g_sel = jnp.min(jnp.where(gl == gmax, sub, SUBLANES), axis=0, keepdims=True)
    p_sel = 1.0 / jnp.sum(jnp.exp(gl - gmax), axis=0, keepdims=True)

    e_sel = jnp.zeros((E_PER_GROUP, t), F32)
    for g in range(N_GROUPS):
        r0 = _EXPERT_ROW0 + g * E_PER_GROUP
        e_sel = jnp.where(g_sel == g, logits[r0:r0 + E_PER_GROUP, :], e_sel)
    v1 = jnp.max(e_sel, axis=0, keepdims=True)
    i1 = jnp.min(jnp.where(e_sel == v1, sub, SUBLANES), axis=0, keepdims=True)
    e_rest = jnp.where(sub == i1, neg_inf, e_sel)
    v2 = jnp.max(e_rest, axis=0, keepdims=True)
    i2 = jnp.min(jnp.where(e_rest == v2, sub, SUBLANES), axis=0, keepdims=True)
    ratio = jnp.exp(v2 - v1)
    w1 = p_sel / (1.0 + ratio)
    w2 = p_sel * ratio / (1.0 + ratio)
    eid1 = g_sel * E_PER_GROUP + i1
    eid2 = g_sel * E_PER_GROUP + i2

    erow = lax.broadcasted_iota(I32, (N_EXPERTS, t), 0)
    m1 = erow == eid1
    m2 = erow == eid2
    member = jnp.where(m1 | m2, 1.0, 0.0)
    r = lax.broadcasted_iota(I32, (t, t), 0)
    c = lax.broadcasted_iota(I32, (t, t), 1)
    same_tile = (lax.shift_right_logical(r, _TILE_SHIFT)
                 == lax.shift_right_logical(c, _TILE_SHIFT))
    earlier = jnp.where((r < c) & same_tile, 1.0, 0.0).astype(BF16)
    member_b = member.astype(BF16)
    rank = _dot(member_b, earlier)
    tr = lax.broadcasted_iota(I32, (t, _ROUTE_TILES * LANES), 0)
    tc = lax.broadcasted_iota(I32, (t, _ROUTE_TILES * LANES), 1)
    in_tile = jnp.where(lax.shift_right_logical(tr, _TILE_SHIFT)
                        == lax.shift_right_logical(tc, _LANE_SHIFT), 1.0, 0.0)
    cnt = _dot(member_b, in_tile.astype(BF16))
    cnt_pad = jnp.floor((cnt + (SEG_ALIGN - 1.0)) * (1.0 / SEG_ALIGN)) * SEG_ALIGN
    er = lax.broadcasted_iota(I32, (N_EXPERTS, N_EXPERTS), 0)
    ec = lax.broadcasted_iota(I32, (N_EXPERTS, N_EXPERTS), 1)
    before = jnp.where(er > ec, 1.0, 0.0).astype(BF16)
    seg_start = _dot(before, cnt_pad.astype(BF16))
    seg_start_tok = jnp.concatenate(
        [seg_start[:, j * LANES:(j + 1) * LANES]
         for j in range(_ROUTE_TILES) for _ in range(MOE_TILE // LANES)], axis=1)
    pos = seg_start_tok + rank
    pos1 = jnp.sum(jnp.where(m1, pos, 0.0), axis=0, keepdims=True)
    pos2 = jnp.sum(jnp.where(m2, pos, 0.0), axis=0, keepdims=True)

    out = jnp.zeros((SUBLANES, t), F32)
    for k, val in enumerate((pos1, pos2, w1, w2)):
        out = jnp.where(sub == k, val, out)
    return h_hi, out, cnt


def _route(x, norm_g, mod, w_group, b_group, w_expert, b_expert):
    t = MOE_TILE * _ROUTE_TILES
    tiles_per_seq = SEQ // t
    wr = jnp.zeros((_ROUTE_ROWS, D_MODEL), F32)
    wr = wr.at[0:N_GROUPS].set(w_group.T.astype(F32))
    wr = wr.at[_EXPERT_ROW0:_EXPERT_ROW0 + N_EXPERTS].set(w_expert.T.astype(F32))
    wr_hi = wr.astype(BF16)
    wr_lo = (wr - wr_hi.astype(F32)).astype(BF16)
    br = jnp.zeros((_ROUTE_ROWS,), F32)
    br = br.at[0:N_GROUPS].set(b_group.astype(F32))
    br = br.at[_EXPERT_ROW0:_EXPERT_ROW0 + N_EXPERTS].set(b_expert.astype(F32))
    full = lambda shape: pl.BlockSpec(shape, lambda i: (0, 0))
    return pl.pallas_call(
        _route_kernel,
        out_shape=(
            jax.ShapeDtypeStruct((N_TOK, D_MODEL), BF16),
            jax.ShapeDtypeStruct((SUBLANES, N_TOK), F32),
            jax.ShapeDtypeStruct((N_MOE_TILES, N_EXPERTS, LANES), F32),
        ),
        grid=(N_MOE_TILES // _ROUTE_TILES,),
        in_specs=[
            pl.BlockSpec((t, D_MODEL), lambda i: (i, 0)),
            full((1, D_MODEL)),
            pl.BlockSpec((1, 1, 3 * D_MODEL), lambda i: (i // tiles_per_seq, 0, 0)),
            full((_ROUTE_ROWS, D_MODEL)),
            full((_ROUTE_ROWS, D_MODEL)),
            full((_ROUTE_ROWS, 1)),
        ],
        out_specs=(
            pl.BlockSpec((t, D_MODEL), lambda i: (i, 0)),
            pl.BlockSpec((SUBLANES, t), lambda i: (0, i)),
            pl.BlockSpec((_ROUTE_TILES, N_EXPERTS, LANES), lambda i: (i, 0, 0)),
        ),
        compiler_params=_cparams(("arbitrary",)),
        name="moe_route",
    )(x, norm_g.reshape(1, D_MODEL), mod, wr_hi, wr_lo, br.reshape(_ROUTE_ROWS, 1))


_REM_UNITS = CHUNK // SEG_ALIGN
_REM_SHIFT = _REM_UNITS.bit_length() - 1


def _chunk_code(tokens):
    return (tokens // CHUNK) * _REM_UNITS + (tokens % CHUNK) // SEG_ALIGN


def _dispatch_plan(cnt):
    cnt = cnt.astype(I32)
    cnt_pad = (cnt + SEG_ALIGN - 1) // SEG_ALIGN * SEG_ALIGN
    seg = jnp.cumsum(cnt_pad, axis=1) - cnt_pad
    tot = jnp.sum(cnt_pad, axis=0)
    ptot = (tot + EXPERT_BLOCK - 1) // EXPERT_BLOCK * EXPERT_BLOCK
    pend = jnp.cumsum(ptot)
    gbase = pend - ptot
    dst = gbase[None, :] + jnp.cumsum(cnt_pad, axis=0) - cnt_pad
    nch = _chunk_code(cnt_pad)
    n_used = (pend[-1] // EXPERT_BLOCK).astype(I32)
    blk = jnp.arange(N_EXPERT_BLOCKS, dtype=I32)
    blk_start = jnp.minimum(blk, n_used - 1) * EXPERT_BLOCK
    blk_e = jnp.sum((pend[None, :] <= blk_start[:, None]).astype(I32), axis=1)
    blk_e = jnp.minimum(blk_e, N_EXPERTS - 1).astype(I32)
    gap_dst = gbase + tot
    gap_nch = _chunk_code(ptot - tot)
    misc = jnp.stack([n_used, jnp.sum(ptot - tot) // SEG_ALIGN]).astype(I32)
    first_blk = gbase // EXPERT_BLOCK
    ids = jnp.arange(N_EXPERTS, dtype=I32)
    later = (ids[None, :] > ids[:, None]) & (ptot[None, :] > 0)
    next_e = jnp.min(jnp.where(later, ids[None, :], N_EXPERTS), axis=1)
    next_e = jnp.where(next_e == N_EXPERTS, -1, next_e)
    return dict(seg=seg.reshape(-1).astype(I32), dst=dst.reshape(-1).astype(I32),
                nch=nch.reshape(-1).astype(I32),
                ntot=(jnp.sum(cnt_pad, axis=1) // SEG_ALIGN).astype(I32),
                nmax=(jnp.max(cnt_pad, axis=1) // CHUNK).astype(I32), blk_e=blk_e,
                first_blk=first_blk.astype(I32), next_e=next_e.astype(I32),
                gap_dst=gap_dst.astype(I32), gap_nch=gap_nch.astype(I32), misc=misc)


_INLINE_CHUNKS = 3


def _start_copy(make_copy, s0, d0, offset, tokens, priority=0):
    s = pl.multiple_of(s0 + offset, SEG_ALIGN)
    d = pl.multiple_of(d0 + offset, SEG_ALIGN)
    make_copy(s, d, tokens).start(priority=priority)


def _start_tail(make_copy, s0, d0, n_full, rem, priority=0):
    half, quarter = CHUNK // 2, CHUNK // 4
    tail = n_full * CHUNK
    has_half = (rem & 2) != 0
    pl.when(has_half)(functools.partial(_start_copy, make_copy, s0, d0, tail, half,
                                        priority))
    tail2 = tail + jnp.where(has_half, half, 0)
    pl.when((rem & 1) != 0)(functools.partial(_start_copy, make_copy, s0, d0, tail2,
                                              quarter, priority))


def _segment_copies_inline(tile, live, seg_ref, dst_ref, nch_ref, make_copy):
    for e in range(N_EXPERTS):
        idx = tile * N_EXPERTS + e
        code = jnp.where(live, nch_ref[idx], 0)
        n = lax.shift_right_logical(code, _REM_SHIFT)
        s0 = seg_ref[idx]
        d0 = dst_ref[idx]
        for cidx in range(_INLINE_CHUNKS):
            pl.when(cidx < n)(functools.partial(_start_copy, make_copy, s0, d0,
                                                cidx * CHUNK, CHUNK, (e + cidx) % 2))
        _start_tail(make_copy, s0, d0, n, code & (_REM_UNITS - 1), e % 2)


def _segment_copies_loop(tile, first, with_tail, seg_ref, dst_ref, nch_ref, make_copy):
    def per_expert(e, carry):
        idx = tile * N_EXPERTS + e
        code = nch_ref[idx]
        n = lax.shift_right_logical(code, _REM_SHIFT)
        s0 = seg_ref[idx]
        d0 = dst_ref[idx]

        def per_chunk(cidx, c2):
            _start_copy(make_copy, s0, d0, cidx * CHUNK, CHUNK)
            return c2

        lax.fori_loop(first, jnp.maximum(n, first), per_chunk, 0)
        if with_tail:
            _start_tail(make_copy, s0, d0, n, code & (_REM_UNITS - 1))
        return carry

    lax.fori_loop(0, N_EXPERTS, per_expert, 0)


_WAIT_GROUP = 64


def _wait_each(n, make_wait):
    def body(_, carry):
        make_wait().wait()
        return carry
    lax.fori_loop(0, n, body, 0)


def _wait_copies(units, make_copy):
    _wait_each(units // _WAIT_GROUP, lambda: make_copy(0, 0, _WAIT_GROUP * SEG_ALIGN))
    _wait_each(units % _WAIT_GROUP, lambda: make_copy(0, 0, SEG_ALIGN))


_HALF = D_MODEL // 2
_BF16_BITS = 16
_HI_MASK = -(1 << _BF16_BITS)


def _pack_pairs(x):
    lo = lax.shift_right_logical(lax.bitcast_convert_type(x[:, 0:_HALF], I32),
                                 _BF16_BITS)
    hi = lax.bitcast_convert_type(x[:, _HALF:D_MODEL], I32) & _HI_MASK
    return lo | hi


def _unpack_pairs(w):
    lo = lax.bitcast_convert_type(lax.shift_left(w, _BF16_BITS), F32).astype(BF16)
    hi = lax.bitcast_convert_type(w & _HI_MASK, F32).astype(BF16)
    return lo, hi


def _token_rows(ref, tok, tokens):
    start = pl.multiple_of(tok * WORD_ROWS, SUBLANES)
    return ref.at[pl.ds(start, tokens * WORD_ROWS)]


def _store_words(ref, words, first=0):
    rows = words.shape[0]
    for q in range(WORD_ROWS):
        ref[pl.ds(first * WORD_ROWS + q, rows, stride=WORD_ROWS), :] = (
            words[:, q * LANES:(q + 1) * LANES])


def _load_words(ref, rows, first=0):
    return jnp.concatenate(
        [ref[pl.ds(first * WORD_ROWS + q, rows, stride=WORD_ROWS), :]
         for q in range(WORD_ROWS)], axis=1)


_DISPATCH_BUFS = 3
_TILES_PER_STEP = 2


def _dispatch_kernel(seg_ref, dst_ref, nch_ref, ntot_ref, nmax_ref, gap_dst_ref,
                     gap_nch_ref, misc_ref, route_ref, h_ref, xs_ref, buf_ref,
                     zero_ref, sem, zsem):
    t = MOE_TILE
    last = N_MOE_TILES - 1
    n_used = misc_ref[0]

    def make_copy(which):
        def mk(s, d, tokens):
            return pltpu.make_async_copy(_token_rows(buf_ref.at[which], s, tokens),
                                         _token_rows(xs_ref, d, tokens), sem.at[which])
        return mk

    def zero_copy(s, d, tokens):
        del s
        return pltpu.make_async_copy(_token_rows(zero_ref, 0, tokens),
                                     _token_rows(xs_ref, d, tokens), zsem)

    @pl.when(pl.program_id(0) == 0)
    def _():
        zero_ref[...] = jnp.zeros_like(zero_ref)

        def per_expert(e, carry):
            d0 = gap_dst_ref[e]
            code = gap_nch_ref[e]
            n = lax.shift_right_logical(code, _REM_SHIFT)

            def per_chunk(cidx, c2):
                _start_copy(zero_copy, 0, d0, cidx * CHUNK, CHUNK)
                return c2

            lax.fori_loop(0, n, per_chunk, 0)
            _start_tail(zero_copy, 0, d0, n, code & (_REM_UNITS - 1))
            return carry

        lax.fori_loop(0, N_EXPERTS, per_expert, 0)

        def per_block(b, carry):
            zero_copy(0, b * EXPERT_BLOCK, EXPERT_BLOCK).start()
            return carry

        lax.fori_loop(n_used, N_EXPERT_BLOCKS, per_block, 0)

    def sort_tile(j):
        k = pl.program_id(0) * _TILES_PER_STEP + j
        slot = k % _DISPATCH_BUFS
        prev = (k + _DISPATCH_BUFS - 1) % _DISPATCH_BUFS
        prev2 = (k + _DISPATCH_BUFS - 2) % _DISPATCH_BUFS
        tile_prev = jnp.maximum(k - 1, 0)
        _segment_copies_inline(tile_prev, k > 0, seg_ref, dst_ref, nch_ref,
                               make_copy(prev))

        route = route_ref[:, j * t:(j + 1) * t]
        pos1 = route[0:1, :].astype(I32)
        pos2 = route[1:2, :].astype(I32)
        r = lax.broadcasted_iota(I32, (DISPATCH_ROWS, t), 0)
        perm = jnp.where((r == pos1) | (r == pos2), 1.0, 0.0).astype(BF16)
        _store_words(buf_ref.at[slot],
                     _pack_pairs(_dot(perm, h_ref[j * t:(j + 1) * t, :])))

        @pl.when((k > 0) & (nmax_ref[tile_prev] > _INLINE_CHUNKS))
        def _():
            _segment_copies_loop(tile_prev, _INLINE_CHUNKS, False, seg_ref, dst_ref,
                                 nch_ref, make_copy(prev))

        @pl.when(k > 1)
        def _():
            _wait_copies(ntot_ref[jnp.maximum(k - 2, 0)], make_copy(prev2))

        @pl.when(k == last)
        def _():
            _segment_copies_loop(k, 0, True, seg_ref, dst_ref, nch_ref, make_copy(slot))
            _wait_copies(ntot_ref[last - 1], make_copy(prev))
            _wait_copies(ntot_ref[last], make_copy(slot))
            _wait_copies(misc_ref[1], zero_copy)
            _wait_each(N_EXPERT_BLOCKS - n_used, lambda: zero_copy(0, 0, EXPERT_BLOCK))

    for j in range(_TILES_PER_STEP):
        sort_tile(j)


def _dispatch(plan, route, h):
    t = MOE_TILE * _TILES_PER_STEP
    return pl.pallas_call(
        _dispatch_kernel,
        out_shape=jax.ShapeDtypeStruct((SORTED_ROWS * WORD_ROWS, LANES), I32),
        grid_spec=pltpu.PrefetchScalarGridSpec(
            num_scalar_prefetch=8,
            grid=(N_MOE_TILES // _TILES_PER_STEP,),
            in_specs=[
                pl.BlockSpec((SUBLANES, t), lambda i, *_: (0, i)),
                pl.BlockSpec((t, D_MODEL), lambda i, *_: (i, 0)),
            ],
            out_specs=pl.BlockSpec(memory_space=pl.ANY),
            scratch_shapes=[
                pltpu.VMEM((_DISPATCH_BUFS, DISPATCH_ROWS * WORD_ROWS, LANES), I32),
                pltpu.VMEM((EXPERT_BLOCK * WORD_ROWS, LANES), I32),
                pltpu.SemaphoreType.DMA((_DISPATCH_BUFS,)),
                pltpu.SemaphoreType.DMA,
            ],
        ),
        compiler_params=_cparams(("arbitrary",)),
        name="moe_dispatch",
    )(plan["seg"], plan["dst"], plan["nch"], plan["ntot"], plan["nmax"],
      plan["gap_dst"], plan["gap_nch"], plan["misc"], route, h)


_WEIGHT_DMA_PRIORITY = 1


def _expert_kernel(blk_e_ref, first_ref, next_ref, misc_ref, x_ref, wg_hbm, wu_hbm,
                   wd_hbm, y_ref, wg_f, wu_f, wd_f, wg_b, wu_b, wd_b, sem, *, layer):
    blk0 = pl.program_id(0) * _BLOCKS_PER_STEP
    blk1 = blk0 + 1
    n_used = misc_ref[0]

    def weight_copies(e):
        return (pltpu.make_async_copy(wg_hbm.at[layer, e], wg_f, sem.at[0]),
                pltpu.make_async_copy(wu_hbm.at[layer, e], wu_f, sem.at[1]),
                pltpu.make_async_copy(wd_hbm.at[layer, e], wd_f, sem.at[2]))

    def take_weights(e):
        for cp in weight_copies(e):
            cp.wait()
        wg_b[...] = wg_f[...].astype(BF16)
        wu_b[...] = wu_f[...].astype(BF16)
        wd_b[...] = wd_f[...].astype(BF16)

        @pl.when(next_ref[e] >= 0)
        def _():
            for cp in weight_copies(next_ref[e]):
                cp.start(priority=_WEIGHT_DMA_PRIORITY)

    def run_rows(first, rows):
        x_lo, x_hi = _unpack_pairs(_load_words(x_ref, rows, first))
        x = jnp.concatenate([x_lo, x_hi], axis=1)
        g = _dot(x, wg_b[...])
        u = _dot(x, wu_b[...])
        a = (g * _sigmoid(g) * u).astype(BF16)
        y = _dot(a, wd_b[...])
        _store_words(y_ref, _pack_pairs(y.astype(BF16).astype(F32)), first)

    @pl.when(blk0 == 0)
    def _():
        for cp in weight_copies(blk_e_ref[0]):
            cp.start(priority=_WEIGHT_DMA_PRIORITY)

    @pl.when(blk0 < n_used)
    def _():
        e0 = blk_e_ref[blk0]
        e1 = blk_e_ref[blk1]
        pl.when(blk0 == first_ref[e0])(functools.partial(take_weights, e0))

        @pl.when(e0 == e1)
        def _():
            run_rows(0, _BLOCKS_PER_STEP * EXPERT_BLOCK)

        @pl.when(e0 != e1)
        def _():
            run_rows(0, EXPERT_BLOCK)
            take_weights(e1)
            run_rows(EXPERT_BLOCK, EXPERT_BLOCK)

    @pl.when(blk0 >= n_used)
    def _():
        y_ref[...] = jnp.zeros_like(y_ref)


def _experts(plan, xs, layer, w_gate, w_up, w_down):
    step_rows = _BLOCKS_PER_STEP * EXPERT_BLOCK * WORD_ROWS
    row_map = lambda i, be, fi, nx, misc: (
        jnp.minimum(i, (misc[0] - 1) // _BLOCKS_PER_STEP), 0)
    out_map = lambda i, be, fi, nx, misc: (i, 0)
    return pl.pallas_call(
        functools.partial(_expert_kernel, layer=layer),
        out_shape=jax.ShapeDtypeStruct((SORTED_ROWS * WORD_ROWS, LANES), I32),
        grid_spec=pltpu.PrefetchScalarGridSpec(
            num_scalar_prefetch=4,
            grid=(N_EXPERT_BLOCKS // _BLOCKS_PER_STEP,),
            in_specs=[
                pl.BlockSpec((step_rows, LANES), row_map),
                pl.BlockSpec(memory_space=pl.ANY),
                pl.BlockSpec(memory_space=pl.ANY),
                pl.BlockSpec(memory_space=pl.ANY),
            ],
            out_specs=pl.BlockSpec((step_rows, LANES), out_map),
            scratch_shapes=[
                pltpu.VMEM((D_MODEL, D_FF), F32),
                pltpu.VMEM((D_MODEL, D_FF), F32),
                pltpu.VMEM((D_FF, D_MODEL), F32),
                pltpu.VMEM((D_MODEL, D_FF), BF16),
                pltpu.VMEM((D_MODEL, D_FF), BF16),
                pltpu.VMEM((D_FF, D_MODEL), BF16),
                pltpu.SemaphoreType.DMA((3,)),
            ],
        ),
        compiler_params=_cparams(("arbitrary",)),
        name="moe_experts",
    )(plan["blk_e"], plan["first_blk"], plan["next_e"], plan["misc"], xs,
      w_gate, w_up, w_down)


_COMBINE_BUFS = 3
_X_BUFS = 3


def _combine_kernel(seg_ref, dst_ref, nch_ref, ntot_ref, nmax_ref, route_ref, x_ref,
                    mod_ref, fg_ref, ys_ref, out_ref, *scratch, final_norm):
    bufs, sem = scratch[:_COMBINE_BUFS], scratch[_COMBINE_BUFS]
    xbuf, xsem = scratch[_COMBINE_BUFS + 1], scratch[_COMBINE_BUFS + 2]
    t = MOE_TILE
    ahead = _COMBINE_BUFS - 1

    step = pl.program_id(0)
    n_steps = N_MOE_TILES // _TILES_PER_STEP
    step_rows = t * _TILES_PER_STEP

    def x_copy(s, slot):
        start = pl.multiple_of(s * step_rows, step_rows)
        return pltpu.make_async_copy(x_ref.at[pl.ds(start, step_rows)], xbuf.at[slot],
                                     xsem.at[slot])

    @pl.when(step == 0)
    def _():
        for s in range(_X_BUFS - 1):
            x_copy(s, s).start()

    @pl.when(step + _X_BUFS - 1 < n_steps)
    def _():
        x_copy(step + _X_BUFS - 1, (step + _X_BUFS - 1) % _X_BUFS).start()

    x_slot = step % _X_BUFS
    x_copy(step, x_slot).wait()

    def make_copy(which):
        def mk(s, d, tokens):
            return pltpu.make_async_copy(_token_rows(ys_ref, d, tokens),
                                         _token_rows(bufs[which], s, tokens),
                                         sem.at[which])
        return mk

    @pl.when(pl.program_id(0) == 0)
    def _():
        for buf in bufs:
            buf[...] = jnp.zeros_like(buf)
        for k in range(_COMBINE_BUFS - 1):
            _segment_copies_loop(k, 0, True, seg_ref, dst_ref, nch_ref, make_copy(k))

    def unsort_tile(j, k, cur):
        nxt = (cur + ahead) % _COMBINE_BUFS
        tile_next = jnp.minimum(k + ahead, N_MOE_TILES - 1)
        has_next = k + ahead < N_MOE_TILES
        rows = pl.ds(j * t, t)
        _wait_copies(ntot_ref[k], make_copy(cur))
        _segment_copies_inline(tile_next, has_next, seg_ref, dst_ref, nch_ref,
                               make_copy(nxt))
        route = route_ref[:, j * t:(j + 1) * t]
        route_t = jnp.concatenate(
            [route, jnp.zeros((LANES - SUBLANES, t), F32)], axis=0).T
        pos1 = route_t[:, 0:1].astype(I32)
        pos2 = route_t[:, 1:2].astype(I32)
        w1 = route_t[:, 2:3]
        w2 = route_t[:, 3:4]
        c = lax.broadcasted_iota(I32, (t, SORT_ROWS), 1)
        unsort = (jnp.where(c == pos1, w1, 0.0)
                  + jnp.where(c == pos2, w2, 0.0)).astype(BF16)
        y_lo, y_hi = _unpack_pairs(_load_words(bufs[cur], SORT_ROWS))
        moe = jnp.concatenate([_dot(unsort, y_lo), _dot(unsort, y_hi)], axis=1)
        gate = mod_ref[0][:, 2 * D_MODEL:3 * D_MODEL]
        x_new = xbuf[x_slot, rows, :] + gate * moe
        if final_norm:
            ms = jnp.mean(x_new * x_new, axis=-1, keepdims=True)
            x_new = (x_new * lax.rsqrt(ms + EPS)) * fg_ref[...]
        out_ref[rows, :] = x_new

        @pl.when(has_next & (nmax_ref[tile_next] > _INLINE_CHUNKS))
        def _():
            _segment_copies_loop(tile_next, _INLINE_CHUNKS, False, seg_ref, dst_ref,
                                 nch_ref, make_copy(nxt))

    for j in range(_TILES_PER_STEP):
        k = pl.program_id(0) * _TILES_PER_STEP + j
        for cur in range(_COMBINE_BUFS):
            pl.when(k % _COMBINE_BUFS == cur)(functools.partial(unsort_tile, j, k, cur))


def _combine(plan, route, x, mod, final_g, ys, final_norm):
    t = MOE_TILE * _TILES_PER_STEP
    tiles_per_seq = SEQ // t
    return pl.pallas_call(
        functools.partial(_combine_kernel, final_norm=final_norm),
        out_shape=jax.ShapeDtypeStruct((N_TOK, D_MODEL), F32),
        grid_spec=pltpu.PrefetchScalarGridSpec(
            num_scalar_prefetch=5,
            grid=(N_MOE_TILES // _TILES_PER_STEP,),
            in_specs=[
                pl.BlockSpec((SUBLANES, t), lambda i, *_: (0, i)),
                pl.BlockSpec(memory_space=pl.ANY),
                pl.BlockSpec((1, 1, 3 * D_MODEL), lambda i, *_: (i // tiles_per_seq, 0, 0)),
                pl.BlockSpec((1, D_MODEL), lambda i, *_: (0, 0)),
                pl.BlockSpec(memory_space=pl.ANY),
            ],
            out_specs=pl.BlockSpec((t, D_MODEL), lambda i, *_: (i, 0)),
            scratch_shapes=(
                [pltpu.VMEM((SORT_ROWS * WORD_ROWS, LANES), I32)] * _COMBINE_BUFS
                + [pltpu.SemaphoreType.DMA((_COMBINE_BUFS,)),
                   pltpu.VMEM((_X_BUFS, t, D_MODEL), F32),
                   pltpu.SemaphoreType.DMA((_X_BUFS,))]),
        ),
        compiler_params=_cparams(("arbitrary",)),
        name="moe_combine",
    )(plan["seg"], plan["dst"], plan["nch"], plan["ntot"], plan["nmax"], route, x,
      mod, final_g.reshape(1, D_MODEL).astype(F32), ys)


def _moe_layer(x, norm_g, mod, final_g, w_group, b_group, w_expert, b_expert,
               layer, w_gate, w_up, w_down, final_norm):
    h, route, cnt = _route(x, norm_g, mod, w_group, b_group, w_expert, b_expert)
    plan = _dispatch_plan(cnt[:, :, 0])
    xs = _dispatch(plan, route, h)
    ys = _experts(plan, xs, layer, w_gate, w_up, w_down)
    return _combine(plan, route, x, mod, final_g, ys, final_norm)


def kernel(x, c, ada_w, ada_b, norm_g, final_g, m_w_in, m_b_gates, m_norm_g, m_w_out, s_w_in, s_conv_w, s_w_out, r_w_group, r_b_group, r_w_expert, r_b_expert, e_w_gate, e_w_up, e_w_down):
    mods = _ada_mods(c, ada_w, ada_b)
    m_w_in_b, m_w_out_b = m_w_in.astype(BF16), m_w_out.astype(BF16)
    s_w_in_b, s_w_out_b = s_w_in.astype(BF16), s_w_out.astype(BF16)
    xt = x.reshape(N_TOK, D_MODEL)
    for i in range(DEPTH):
        mod_mix = mods[2 * i].reshape(BATCH, 1, 3 * D_MODEL)
        mod_ffn = mods[2 * i + 1].reshape(BATCH, 1, 3 * D_MODEL)
        j = i // 2
        if i % 2 == 0:
            qt, k, vt, ot, gates, gatest = _mlstm_in(xt, norm_g[i, 0], mod_mix,
                                                     m_w_in_b[j], m_b_gates[j])
            xt = _mlstm_rec(xt, mod_mix, qt, k, vt, ot, gates, gatest,
                            m_norm_g[j], m_w_out_b, j)
        else:
            xt = _conv_layer(xt, norm_g[i, 0], mod_mix, s_w_in_b, s_conv_w[j],
                             s_w_out_b, j)
        xt = _moe_layer(xt, norm_g[i, 1], mod_ffn, final_g, r_w_group[i],
                        r_b_group[i], r_w_expert[i], r_b_expert[i], i, e_w_gate,
                        e_w_up, e_w_down, final_norm=(i == DEPTH - 1))
    return xt.reshape(BATCH, SEQ, D_MODEL)
```
